```python
import math
import jax, jax.numpy as jnp
from jax import lax
import numpy as np

D_MODEL = 1024
BATCH = 8
SEQ = 2048
DEPTH = 4
DEC_BATCH = 128
DEC_SEQ = 4
PAST_LEN = 16384
PAGE_SIZE = 128

N_EVEN = (DEPTH + 1) // 2
N_ODD = DEPTH // 2
W_A = D_MODEL // 2
A_BLOCKS = 8
A_BLOCK = W_A // A_BLOCKS
A_CONV = 4
LRU_C = 8.0
W_B = D_MODEL // 2
B_GROUPS = 8
B_CONV = 3
C_HEADS = 4
C_DK = 64
C_DV = 128
W_CK = C_HEADS * C_DK
W_CV = C_HEADS * C_DV
RET_CHUNK = 128
ROPE_BASE = 10000.0
D_GROUPS = 4
D_CHUNK = 128
W_D = D_MODEL // 2
D_GROUP = W_D // D_GROUPS
EVEN_IN = 2 * W_A + 3 * W_B
ODD_IN = 2 * W_CK + 2 * W_CV + 2 * W_D
MIX_OUT = W_A + W_B
D_FF = -(-8 * D_MODEL // (3 * 256)) * 256
ALPHA = (2 * DEPTH) ** 0.25
BETA = (8 * DEPTH) ** -0.25
LN_EPS = 1e-5

kernel_name = "hybrid_rglru_conv_retention_gmlp_decoder_step"


def layer_norm(x, g, b):
    xf = x.astype(jnp.float32)
    mu = jnp.mean(xf, -1, keepdims=True)
    var = jnp.mean(jnp.square(xf - mu), -1, keepdims=True)
    return ((xf - mu) * lax.rsqrt(var + LN_EPS) * g.astype(jnp.float32) + b.astype(jnp.float32)).astype(x.dtype)


def causal_dwconv(x, prefix, w):
    K = w.shape[0]
    T = x.shape[1]
    xp = jnp.concatenate([prefix.astype(x.dtype), x], axis=1)
    y = sum(w[j] * xp[:, j:j + T] for j in range(K))
    return y, xp[:, xp.shape[1] - (K - 1):]


def rg_lru(x, h0, wa, ba, wx, bx, lam):
    N, T, _ = x.shape
    xb = x.reshape(N, T, A_BLOCKS, A_BLOCK)
    r = jax.nn.sigmoid((jnp.einsum('ntgi,gij->ntgj', xb, wa).reshape(N, T, W_A) + ba).astype(jnp.float32))
    i = jax.nn.sigmoid((jnp.einsum('ntgi,gij->ntgj', xb, wx).reshape(N, T, W_A) + bx).astype(jnp.float32))
    log_a = -LRU_C * r * jax.nn.softplus(-lam.astype(jnp.float32))
    a = jnp.exp(log_a)
    mult = jnp.sqrt(-jnp.expm1(2.0 * log_a))
    if h0 is None:
        mult = mult.at[:, 0].set(1.0)
    b = mult * (i * x.astype(jnp.float32))
    if h0 is not None:
        b = b.at[:, 0].add(a[:, 0] * h0.astype(jnp.float32))

    def combine(left, right):
        a1, b1 = left
        a2, b2 = right
        return a1 * a2, a2 * b1 + b2

    _, h = lax.associative_scan(combine, (a, b), axis=1)
    return h.astype(x.dtype), h[:, -1]


def mixer_ab(x, conv_a_buf, h0, sconv_buf, w_in, conv_a_w, conv_a_b, lru_wa, lru_ba, lru_wx, lru_bx, lru_lam, conv_b_w, w_out):
    proj = x @ w_in
    xa, ga, bg, cg, xb = jnp.split(proj, [W_A, 2 * W_A, 2 * W_A + W_B, 2 * W_A + 2 * W_B], axis=-1)
    xa_c, new_conv_a = causal_dwconv(xa, conv_a_buf, conv_a_w)
    h, h_last = rg_lru(xa_c + conv_a_b, h0, lru_wa, lru_ba, lru_wx, lru_bx, lru_lam)
    ya = h * jax.nn.gelu(ga)
    zb, new_sconv = causal_dwconv(cg * xb, sconv_buf, conv_b_w)
    yb = bg * zb
    y = jnp.concatenate([ya, yb], axis=-1) @ w_out
    return y, new_conv_a, h_last, new_sconv


def rope(x, pos):
    half = x.shape[-1] // 2
    freq = ROPE_BASE ** (-jnp.arange(half, dtype=jnp.float32) / half)
    ang = pos.astype(jnp.float32)[:, None] * freq
    cos = jnp.cos(ang)[None, :, None, :]
    sin = jnp.sin(ang)[None, :, None, :]
    xf = x.astype(jnp.float32)
    x1, x2 = xf[..., :half], xf[..., half:]
    return jnp.concatenate([x1 * cos - x2 * sin, x1 * sin + x2 * cos], axis=-1)


def retention_chunk(q, k, v, S0):
    L = q.shape[1]
    log_g = jnp.log1p(-jnp.exp2(-5.0 - jnp.arange(C_HEADS, dtype=jnp.float32)))
    idx = jnp.arange(L, dtype=jnp.float32)
    rel = idx[:, None] - idx[None, :]
    decay = jnp.where(rel >= 0, jnp.exp(log_g[:, None, None] * jnp.maximum(rel, 0.0)), 0.0)
    scores = jnp.einsum('nlhd,nmhd->nhlm', q, k) * decay
    intra = jnp.einsum('nhlm,nmhe->nlhe', scores, v)
    cross = jnp.einsum('nlhd,nhde->nlhe', q, S0) * jnp.exp(log_g[None, :] * (idx[:, None] + 1.0))[None, :, :, None]
    k_dec = k * jnp.exp(log_g[None, :] * (L - 1.0 - idx[:, None]))[None, :, :, None]
    S_new = jnp.exp(log_g * L)[None, :, None, None] * S0 + jnp.einsum('nlhd,nlhe->nhde', k_dec, v)
    return intra + cross, S_new


def retention(q, k, v, S0):
    N, T = q.shape[:2]
    if T <= RET_CHUNK:
        return retention_chunk(q, k, v, S0)
    nc = T // RET_CHUNK

    def to_chunks(a):
        return jnp.moveaxis(a.reshape(N, nc, RET_CHUNK, *a.shape[2:]), 1, 0)

    def step(S, qkv):
        o, S_next = retention_chunk(qkv[0], qkv[1], qkv[2], S)
        return S_next, o

    S_fin, o = lax.scan(step, S0, (to_chunks(q), to_chunks(k), to_chunks(v)))
    return jnp.moveaxis(o, 0, 1).reshape(N, T, C_HEADS, C_DV), S_fin


def mixer_cd(x, S0, pos, w_in, ret_gn_g, sp_w, sp_b, gm_ln_g, gm_ln_b, w_out):
    N, T, _ = x.shape
    proj = x @ w_in
    q, k, v, g, u, vd = jnp.split(proj, [W_CK, 2 * W_CK, 2 * W_CK + W_CV, 2 * W_CK + 2 * W_CV, 2 * W_CK + 2 * W_CV + W_D], axis=-1)
    qh = rope(q.reshape(N, T, C_HEADS, C_DK), pos)
    kh = rope(k.reshape(N, T, C_HEADS, C_DK), pos) * (C_DK ** -0.5)
    vh = v.reshape(N, T, C_HEADS, C_DV).astype(jnp.float32)
    o, S_new = retention(qh, kh, vh, S0.astype(jnp.float32))
    mu = jnp.mean(o, -1, keepdims=True)
    var = jnp.mean(jnp.square(o - mu), -1, keepdims=True)
    o = ((o - mu) * lax.rsqrt(var + LN_EPS)).reshape(N, T, W_CV) * ret_gn_g.astype(jnp.float32)
    y_c = (jax.nn.silu(g.astype(jnp.float32)) * o).astype(x.dtype)
    u = jax.nn.gelu(u)
    vd = layer_norm(jax.nn.gelu(vd), gm_ln_g, gm_ln_b)
    Lc = min(T, D_CHUNK)
    nc = T // Lc
    w_s = jnp.tril(sp_w[:, :Lc, :Lc])
    vc = vd.reshape(N, nc, Lc, D_GROUPS, D_GROUP)
    s = jnp.einsum('gts,ncsgd->nctgd', w_s, vc) + jnp.swapaxes(sp_b[:, :Lc], 0, 1)[:, :, None]
    y_d = (u * s.reshape(N, T, W_D)).astype(x.dtype)
    y = jnp.concatenate([y_c, y_d], axis=-1) @ w_out
    return y, S_new, vd


def swiglu(x, w1, w3, w2):
    return (jax.nn.silu(x @ w1) * (x @ w3)) @ w2


def setup_inputs(seed: int = 0) -> dict:
    key = jax.random.key(seed)
    ks = iter(jax.random.split(key, 40))
    nrm = lambda shape, s: jax.random.normal(next(ks), shape, jnp.float32) * s
    u = jax.random.uniform(next(ks), (N_EVEN, W_A), jnp.float32, 0.9, 0.999)
    sg = u ** (1.0 / LRU_C)
    lru_lam = jnp.log(sg) - jnp.log1p(-sg)
    return {
        "x_prompt": nrm((BATCH, SEQ, D_MODEL), 1.0),
        "x_sample": nrm((DEC_BATCH, DEC_SEQ, D_MODEL), 1.0),
        "state_lru_h": nrm((DEC_BATCH, N_EVEN, W_A), 0.5),
        "state_lru_conv": nrm((DEC_BATCH, N_EVEN, A_CONV - 1, W_A), 1.0),
        "state_sconv": nrm((DEC_BATCH, N_EVEN, B_CONV - 1, W_B), 1.0),
        "state_ret": nrm((DEC_BATCH, N_ODD, C_HEADS, C_DK, C_DV), 1.0),
        "w_in_even": nrm((N_EVEN, D_MODEL, EVEN_IN), D_MODEL ** -0.5),
        "conv_a_w": nrm((N_EVEN, A_CONV, W_A), A_CONV ** -0.5),
        "conv_a_b": nrm((N_EVEN, W_A), 0.01),
        "lru_wa": nrm((N_EVEN, A_BLOCKS, A_BLOCK, A_BLOCK), A_BLOCK ** -0.5),
        "lru_ba": nrm((N_EVEN, W_A), 0.01),
        "lru_wx": nrm((N_EVEN, A_BLOCKS, A_BLOCK, A_BLOCK), A_BLOCK ** -0.5),
        "lru_bx": nrm((N_EVEN, W_A), 0.01),
        "lru_lam": lru_lam,
        "conv_b_w": nrm((N_EVEN, B_CONV, W_B), B_CONV ** -0.5),
        "w_out_even": nrm((N_EVEN, MIX_OUT, D_MODEL), BETA * MIX_OUT ** -0.5),
        "w_in_odd": nrm((N_ODD, D_MODEL, ODD_IN), D_MODEL ** -0.5),
        "ret_gn_g": 1.0 + nrm((N_ODD, W_CV), 0.01),
        "sp_w": nrm((N_ODD, D_GROUPS, D_CHUNK, D_CHUNK), D_CHUNK ** -0.5),
        "sp_b": nrm((N_ODD, D_GROUPS, D_CHUNK), 0.01),
        "gm_ln_g": 1.0 + nrm((N_ODD, W_D), 0.01),
        "gm_ln_b": nrm((N_ODD, W_D), 0.01),
        "w_out_odd": nrm((N_ODD, W_CV + W_D, D_MODEL), BETA * (W_CV + W_D) ** -0.5),
        "ffn_w1": nrm((DEPTH, D_MODEL, D_FF), D_MODEL ** -0.5),
        "ffn_w3": nrm((DEPTH, D_MODEL, D_FF), D_MODEL ** -0.5),
        "ffn_w2": nrm((DEPTH, D_FF, D_MODEL), BETA * D_FF ** -0.5),
        "ln1_g": 1.0 + nrm((DEPTH, D_MODEL), 0.01),
        "ln1_b": nrm((DEPTH, D_MODEL), 0.01),
        "ln2_g": 1.0 + nrm((DEPTH, D_MODEL), 0.01),
        "ln2_b": nrm((DEPTH, D_MODEL), 0.01),
    }


def reference(x_prompt, x_sample, state_lru_h, state_lru_conv, state_sconv, state_ret,
              w_in_even, conv_a_w, conv_a_b, lru_wa, lru_ba, lru_wx, lru_bx, lru_lam, conv_b_w, w_out_even,
              w_in_odd, ret_gn_g, sp_w, sp_b, gm_ln_g, gm_ln_b, w_out_odd,
              ffn_w1, ffn_w3, ffn_w2, ln1_g, ln1_b, ln2_g, ln2_b):
    xp, xs = x_prompt, x_sample
    pos_p = jnp.arange(SEQ, dtype=jnp.int32)
    pos_s = PAST_LEN + jnp.arange(DEC_SEQ, dtype=jnp.int32)
    h_p, h_s, ca_p, ca_s, sc_p, sc_s, r_p, r_s, v_s = [], [], [], [], [], [], [], [], []
    for l in range(DEPTH):
        if l % 2 == 0:
            e = l // 2
            pa = (w_in_even[e], conv_a_w[e], conv_a_b[e], lru_wa[e], lru_ba[e], lru_wx[e], lru_bx[e], lru_lam[e], conv_b_w[e], w_out_even[e])
            mp, cap, hlp, scp = mixer_ab(xp, jnp.zeros((BATCH, A_CONV - 1, W_A), xp.dtype), None,
                                         jnp.zeros((BATCH, B_CONV - 1, W_B), xp.dtype), *pa)
            ms, cas, hls, scs = mixer_ab(xs, state_lru_conv[:, e], state_lru_h[:, e], state_sconv[:, e], *pa)
            h_p.append(hlp.astype(xp.dtype)); h_s.append(hls.astype(state_lru_h.dtype))
            ca_p.append(cap); ca_s.append(cas.astype(state_lru_conv.dtype))
            sc_p.append(scp); sc_s.append(scs.astype(state_sconv.dtype))
        else:
            o = l // 2
            po = (w_in_odd[o], ret_gn_g[o], sp_w[o], sp_b[o], gm_ln_g[o], gm_ln_b[o], w_out_odd[o])
            mp, rp, _ = mixer_cd(xp, jnp.zeros((BATCH, C_HEADS, C_DK, C_DV), jnp.float32), pos_p, *po)
            ms, rs, vs = mixer_cd(xs, state_ret[:, o], pos_s, *po)
            r_p.append(rp.astype(xp.dtype)); r_s.append(rs.astype(state_ret.dtype))
            v_s.append(vs)
        xp = layer_norm(ALPHA * xp + mp, ln1_g[l], ln1_b[l])
        xs = layer_norm(ALPHA * xs + ms, ln1_g[l], ln1_b[l])
        xp = layer_norm(ALPHA * xp + swiglu(xp, ffn_w1[l], ffn_w3[l], ffn_w2[l]), ln2_g[l], ln2_b[l])
        xs = layer_norm(ALPHA * xs + swiglu(xs, ffn_w1[l], ffn_w3[l], ffn_w2[l]), ln2_g[l], ln2_b[l])
    lru_h_prompt = jnp.stack(h_p, axis=1)
    lru_h_sample = jnp.stack(h_s, axis=1)
    lru_conv_prompt = jnp.stack(ca_p, axis=1)
    lru_conv_sample = jnp.stack(ca_s, axis=1)
    sconv_prompt = jnp.stack(sc_p, axis=1)
    sconv_sample = jnp.stack(sc_s, axis=1)
    ret_prompt = jnp.stack(r_p, axis=1)
    ret_sample = jnp.stack(r_s, axis=1)
    chunk_v_sample = jnp.stack(v_s, axis=1)
    return (xp, xs, lru_h_prompt, lru_h_sample, lru_conv_prompt, lru_conv_sample,
            sconv_prompt, sconv_sample, ret_prompt, ret_sample, chunk_v_sample)
```

```python
import functools

import jax
import jax.numpy as jnp
from jax import lax
from jax.experimental import pallas as pl
from jax.experimental.pallas import tpu as pltpu

F32 = jnp.float32
BF16 = jnp.bfloat16

D_MODEL = 1024
BATCH = 8
SEQ = 2048
DEPTH = 4
DEC_BATCH = 128
DEC_SEQ = 4
PAST_LEN = 16384
W_A = 512
A_BLOCKS = 8
A_BLOCK = 64
A_CONV = 4
LRU_C = 8.0
W_B = 512
B_CONV = 3
C_HEADS = 4
C_DK = 64
C_DV = 128
W_CK = C_HEADS * C_DK
W_CV = C_HEADS * C_DV
RET_CHUNK = 128
ROPE_BASE = 10000.0
D_GROUPS = 4
D_CHUNK = 128
W_D = 512
D_GROUP = W_D // D_GROUPS
EVEN_IN = 2 * W_A + 3 * W_B
ODD_IN = 2 * W_CK + 2 * W_CV + 2 * W_D
D_FF = 2816
ALPHA = (2 * DEPTH) ** 0.25
LN_EPS = 1e-5

SUBLANES = 8
LANES = 128
MIB = 1024 * 1024

ROW_TILE = 512
EVEN_TT = 128
EVEN_PITCH = EVEN_TT + SUBLANES
SAMPLE_NB = 16


def _params(sem, vmem_mib):
    return pltpu.CompilerParams(dimension_semantics=sem, vmem_limit_bytes=vmem_mib * MIB)


def _const_spec(shape):
    nd = len(shape)
    return pl.BlockSpec(shape, lambda *_: (0,) * nd, pipeline_mode=pl.Buffered(1))


def _layer_norm(x, g, b):
    mu = jnp.mean(x, -1, keepdims=True)
    xc = x - mu
    var = jnp.mean(xc * xc, -1, keepdims=True)
    return xc * lax.rsqrt(var + LN_EPS) * g + b


def _softplus(x):
    return jnp.maximum(x, 0.0) + jnp.log1p(jnp.exp(-jnp.abs(x)))


def _dot(a, b):
    return jnp.dot(a, b, preferred_element_type=F32)


def _proj_kernel(x_ref, w_ref, o_ref):
    o_ref[...] = _dot(x_ref[...].astype(BF16), w_ref[...])


def _in_proj(x2d, w):
    m, k = x2d.shape
    n = w.shape[1]
    tm = min(ROW_TILE, m)
    return pl.pallas_call(
        _proj_kernel,
        grid=(m // tm,),
        in_specs=[pl.BlockSpec((tm, k), lambda i: (i, 0)), _const_spec((k, n))],
        out_specs=pl.BlockSpec((tm, n), lambda i: (i, 0)),
        out_shape=jax.ShapeDtypeStruct((m, n), F32),
        compiler_params=_params(("parallel",), 40),
        name="in_proj",
    )(x2d, w)


def _dense_kernel(y_ref, x_ref, wo_ref, g1_ref, b1_ref, w1_ref, w3_ref, w2_ref, g2_ref, b2_ref, o_ref):
    mix = _dot(y_ref[...], wo_ref[...])
    x1 = _layer_norm(ALPHA * x_ref[...] + mix, g1_ref[...], b1_ref[...])
    xb = x1.astype(BF16)
    h = jax.nn.silu(_dot(xb, w1_ref[...])) * _dot(xb, w3_ref[...])
    f = _dot(h.astype(BF16), w2_ref[...])
    o_ref[...] = _layer_norm(ALPHA * x1 + f, g2_ref[...], b2_ref[...])


def _dense_block(ycat, x2d, wo, g1, b1, w1, w3, w2, g2, b2):
    m = x2d.shape[0]
    tm = min(ROW_TILE, m)
    row = lambda i: (i, 0)
    return pl.pallas_call(
        _dense_kernel,
        grid=(m // tm,),
        in_specs=[
            pl.BlockSpec((tm, D_MODEL), row),
            pl.BlockSpec((tm, D_MODEL), row),
            _const_spec((D_MODEL, D_MODEL)),
            _const_spec((1, D_MODEL)),
            _const_spec((1, D_MODEL)),
            _const_spec((D_MODEL, D_FF)),
            _const_spec((D_MODEL, D_FF)),
            _const_spec((D_FF, D_MODEL)),
            _const_spec((1, D_MODEL)),
            _const_spec((1, D_MODEL)),
        ],
        out_specs=pl.BlockSpec((tm, D_MODEL), row),
        out_shape=jax.ShapeDtypeStruct((m, D_MODEL), F32),
        compiler_params=_params(("parallel",), 56),
        name="dense_block",
    )(ycat, x2d, wo, g1, b1, w1, w3, w2, g2, b2)


def _lru_gates(xc, wblk_ref, ba, bx, sp):
    half = W_A // 2
    xb = xc.astype(BF16)
    pre0 = _dot(xb[:, :half], wblk_ref[0])
    pre1 = _dot(xb[:, half:], wblk_ref[1])
    r = jax.nn.sigmoid(jnp.concatenate([pre0[:, :half], pre1[:, :half]], axis=1) + ba)
    i = jax.nn.sigmoid(jnp.concatenate([pre0[:, half:], pre1[:, half:]], axis=1) + bx)
    log_a = (-LRU_C) * r * sp
    a = jnp.exp(log_a)
    mult = jnp.sqrt((1.0 + a * a) * jnp.tanh(-log_a))
    return a, mult, i


def _even_prompt_kernel(proj_ref, caw_ref, cab_ref, wblk_ref, ba_ref, bx_ref, lam_ref, cbw_ref,
                        y_ref, hlast_ref, ca_out_ref, sc_out_ref,
                        xa_ext, cb_ext, a_s, b_s, h_c):
    tt, pitch = EVEN_TT, EVEN_PITCH
    nslab = W_A // LANES
    j = pl.program_id(0)

    @pl.when(j == 0)
    def _():
        xa_ext[:, 0:SUBLANES, :] = jnp.zeros((BATCH, SUBLANES, W_A), F32)
        cb_ext[:, 0:SUBLANES, :] = jnp.zeros((BATCH, SUBLANES, W_B), F32)
        h_c[...] = jnp.zeros(h_c.shape, F32)

    @pl.when(j > 0)
    def _():
        xa_ext[:, 0:SUBLANES, :] = xa_ext[:, tt:tt + SUBLANES, :]
        cb_ext[:, 0:SUBLANES, :] = cb_ext[:, tt:tt + SUBLANES, :]

    sp = _softplus(-lam_ref[...])
    ba = ba_ref[...]
    bx = bx_ref[...]
    cab = cab_ref[...]
    first_row = (lax.broadcasted_iota(jnp.int32, (tt, 1), 0) == 0) & (j == 0)

    def stage(b, carry):
        xa = proj_ref[b, :, 0:W_A]
        xa_ext[b, SUBLANES:SUBLANES + tt, :] = xa
        xc = cab + caw_ref[A_CONV - 1:A_CONV, :] * xa
        for k in range(A_CONV - 1):
            off = SUBLANES - (A_CONV - 1) + k
            xc = xc + caw_ref[k:k + 1, :] * xa_ext[b, pl.ds(off, tt), :]
        a, mult, gate = _lru_gates(xc, wblk_ref, ba, bx, sp)
        mult = jnp.where(first_row, 1.0, mult)
        bb = mult * (gate * xc)
        row0 = pl.multiple_of(b * pitch, SUBLANES)
        for s in range(nslab):
            a_s[s, pl.ds(row0, tt), :] = a[:, s * LANES:(s + 1) * LANES]
            b_s[s, pl.ds(row0, tt), :] = bb[:, s * LANES:(s + 1) * LANES]
        cb_ext[b, SUBLANES:SUBLANES + tt, :] = proj_ref[b, :, 2 * W_A + W_B:2 * W_A + 2 * W_B] * proj_ref[b, :, 2 * W_A + 2 * W_B:]
        return carry

    lax.fori_loop(0, BATCH, stage, 0)

    def step(t, hs):
        out = []
        for s in range(nslab):
            rows = pl.ds(t, BATCH, stride=pitch)
            hn = a_s[s, rows, :] * hs[s] + b_s[s, rows, :]
            b_s[s, rows, :] = hn
            out.append(hn)
        return tuple(out)

    hs = lax.fori_loop(0, tt, step, tuple(h_c[s] for s in range(nslab)), unroll=4)
    for s in range(nslab):
        h_c[s] = hs[s]

    def emit(b, carry):
        row0 = pl.multiple_of(b * pitch, SUBLANES)
        h = jnp.concatenate([b_s[s, pl.ds(row0, tt), :] for s in range(nslab)], axis=1)
        ya = h * jax.nn.gelu(proj_ref[b, :, W_A:2 * W_A])
        zb = cbw_ref[B_CONV - 1:B_CONV, :] * cb_ext[b, SUBLANES:SUBLANES + tt, :]
        for k in range(B_CONV - 1):
            off = SUBLANES - (B_CONV - 1) + k
            zb = zb + cbw_ref[k:k + 1, :] * cb_ext[b, pl.ds(off, tt), :]
        yb = proj_ref[b, :, 2 * W_A:2 * W_A + W_B] * zb
        y_ref[b, :, 0:W_A] = ya.astype(BF16)
        y_ref[b, :, W_A:] = yb.astype(BF16)
        return carry

    lax.fori_loop(0, BATCH, emit, 0)

    @pl.when(j == pl.num_programs(0) - 1)
    def _():
        hlast_ref[...] = jnp.concatenate(list(hs), axis=1)
        ca_out_ref[...] = xa_ext[:, tt:tt + SUBLANES, :]
        sc_out_ref[...] = cb_ext[:, tt:tt + SUBLANES, :]


def _even_prompt(proj, caw, cab, wblk, ba, bx, lam, cbw):
    tt, pitch = EVEN_TT, EVEN_PITCH
    nslab = W_A // LANES
    return pl.pallas_call(
        _even_prompt_kernel,
        grid=(SEQ // tt,),
        in_specs=[
            pl.BlockSpec((BATCH, tt, EVEN_IN), lambda j: (0, j, 0)),
            _const_spec((A_CONV, W_A)),
            _const_spec((1, W_A)),
            _const_spec((2, W_A // 2, W_A)),
            _const_spec((1, W_A)),
            _const_spec((1, W_A)),
            _const_spec((1, W_A)),
            _const_spec((B_CONV, W_B)),
        ],
        out_specs=[
            pl.BlockSpec((BATCH, tt, D_MODEL), lambda j: (0, j, 0)),
            pl.BlockSpec((BATCH, W_A), lambda j: (0, 0)),
            pl.BlockSpec((BATCH, SUBLANES, W_A), lambda j: (0, 0, 0)),
            pl.BlockSpec((BATCH, SUBLANES, W_B), lambda j: (0, 0, 0)),
        ],
        out_shape=[
            jax.ShapeDtypeStruct((BATCH, SEQ, D_MODEL), BF16),
            jax.ShapeDtypeStruct((BATCH, W_A), F32),
            jax.ShapeDtypeStruct((BATCH, SUBLANES, W_A), F32),
            jax.ShapeDtypeStruct((BATCH, SUBLANES, W_B), F32),
        ],
        scratch_shapes=[
            pltpu.VMEM((BATCH, tt + SUBLANES, W_A), F32),
            pltpu.VMEM((BATCH, tt + SUBLANES, W_B), F32),
            pltpu.VMEM((nslab, BATCH * pitch, LANES), F32),
            pltpu.VMEM((nslab, BATCH * pitch, LANES), F32),
            pltpu.VMEM((nslab, BATCH, LANES), F32),
        ],
        compiler_params=_params(("arbitrary",), 56),
        name="even_prompt",
    )(proj, caw, cab, wblk, ba, bx, lam, cbw)


def _even_sample_kernel(proj_ref, ca_ref, sc_ref, h0_ref, caw_ref, cab_ref, wblk_ref, ba_ref, bx_ref,
                        lam_ref, cbw_ref, y_ref, hlast_ref, ca_out_ref, sc_out_ref):
    sp = _softplus(-lam_ref[...])
    rows_a = [ca_ref[k] for k in range(A_CONV - 1)] + [proj_ref[l, :, 0:W_A] for l in range(DEC_SEQ)]
    xc = []
    for l in range(DEC_SEQ):
        acc = cab_ref[...] + caw_ref[0:1, :] * rows_a[l]
        for k in range(1, A_CONV):
            acc = acc + caw_ref[k:k + 1, :] * rows_a[l + k]
        xc.append(acc)
    xc_all = jnp.concatenate(xc, axis=0)
    a, mult, gate = _lru_gates(xc_all, wblk_ref, ba_ref[...], bx_ref[...], sp)
    bb = mult * (gate * xc_all)
    h = h0_ref[...]
    rows_b = [sc_ref[k] for k in range(B_CONV - 1)]
    for l in range(DEC_SEQ):
        sl = slice(l * DEC_BATCH, (l + 1) * DEC_BATCH)
        h = a[sl] * h + bb[sl]
        y_ref[l, :, 0:W_A] = (h * jax.nn.gelu(proj_ref[l, :, W_A:2 * W_A])).astype(BF16)
        rows_b.append(proj_ref[l, :, 2 * W_A + W_B:2 * W_A + 2 * W_B] * proj_ref[l, :, 2 * W_A + 2 * W_B:])
    hlast_ref[...] = h
    for l in range(DEC_SEQ):
        zb = cbw_ref[0:1, :] * rows_b[l]
        for k in range(1, B_CONV):
            zb = zb + cbw_ref[k:k + 1, :] * rows_b[l + k]
        y_ref[l, :, W_A:] = (proj_ref[l, :, 2 * W_A:2 * W_A + W_B] * zb).astype(BF16)
    for k in range(A_CONV - 1):
        ca_out_ref[k] = rows_a[DEC_SEQ + k]
    for k in range(B_CONV - 1):
        sc_out_ref[k] = rows_b[DEC_SEQ + k]


def _even_sample(proj_tm, ca_tm, sc_tm, h0, caw, cab, wblk, ba, bx, lam, cbw):
    return pl.pallas_call(
        _even_sample_kernel,
        out_shape=[
            jax.ShapeDtypeStruct((DEC_SEQ, DEC_BATCH, D_MODEL), BF16),
            jax.ShapeDtypeStruct((DEC_BATCH, W_A), F32),
            jax.ShapeDtypeStruct((A_CONV - 1, DEC_BATCH, W_A), F32),
            jax.ShapeDtypeStruct((B_CONV - 1, DEC_BATCH, W_B), F32),
        ],
        compiler_params=pltpu.CompilerParams(vmem_limit_bytes=40 * MIB),
        name="even_sample",
    )(proj_tm, ca_tm, sc_tm, h0, caw, cab, wblk, ba, bx, lam, cbw)


def _rope(x, cos, sin_signed):
    half = C_DK // 2
    ax = x.ndim - 1
    pieces = []
    for p in range(W_CK // LANES):
        xv = x[..., p * LANES:(p + 1) * LANES]
        lane = lax.broadcasted_iota(jnp.int32, xv.shape, ax)
        partner = jnp.where((lane % C_DK) < half, pltpu.roll(xv, LANES - half, axis=ax), pltpu.roll(xv, half, axis=ax))
        pieces.append(xv * cos[..., p * LANES:(p + 1) * LANES] + partner * sin_signed[..., p * LANES:(p + 1) * LANES])
    return jnp.concatenate(pieces, axis=ax)


def _group_norm(o):
    mu = jnp.mean(o, -1, keepdims=True)
    oc = o - mu
    var = jnp.mean(oc * oc, -1, keepdims=True)
    return oc * lax.rsqrt(var + LN_EPS)


def _odd_prompt_kernel(proj_ref, cos_ref, sin_ref, dec_ref, cs_ref, kd_ref, sdec_ref, gn_ref,
                       spw_ref, spb_ref, lng_ref, lnb_ref, y_ref, s_out_ref, s_c):
    c = pl.program_id(1)

    @pl.when(c == 0)
    def _():
        s_c[...] = jnp.zeros(s_c.shape, F32)

    q = _rope(proj_ref[0, :, 0:W_CK], cos_ref[...], sin_ref[...])
    k = _rope(proj_ref[0, :, W_CK:2 * W_CK], cos_ref[...], sin_ref[...]) * (C_DK ** -0.5)
    kd = k * kd_ref[...]
    v0 = 2 * W_CK
    g0 = v0 + W_CV
    for h in range(C_HEADS):
        qh = q[:, h * C_DK:(h + 1) * C_DK].astype(BF16)
        kh = k[:, h * C_DK:(h + 1) * C_DK].astype(BF16)
        kdh = kd[:, h * C_DK:(h + 1) * C_DK].astype(BF16)
        vh = proj_ref[0, :, v0 + h * C_DV:v0 + (h + 1) * C_DV].astype(BF16)
        s_prev = s_c[h]
        scores = lax.dot_general(qh, kh, (((1,), (1,)), ((), ())), preferred_element_type=F32) * dec_ref[h]
        o = _dot(scores.astype(BF16), vh) + _dot(qh, s_prev.astype(BF16)) * cs_ref[:, h * C_DV:(h + 1) * C_DV]
        s_c[h] = sdec_ref[h] * s_prev + lax.dot_general(kdh, vh, (((0,), (0,)), ((), ())), preferred_element_type=F32)
        gate = jax.nn.silu(proj_ref[0, :, g0 + h * C_DV:g0 + (h + 1) * C_DV])
        y_ref[0, :, h * C_DV:(h + 1) * C_DV] = (gate * (_group_norm(o) * gn_ref[:, h * C_DV:(h + 1) * C_DV])).astype(BF16)

    u0 = g0 + W_CV
    u = jax.nn.gelu(proj_ref[0, :, u0:u0 + W_D])
    vd = _layer_norm(jax.nn.gelu(proj_ref[0, :, u0 + W_D:]), lng_ref[...], lnb_ref[...]).astype(BF16)
    ri = lax.broadcasted_iota(jnp.int32, (D_CHUNK, D_CHUNK), 0)
    ci = lax.broadcasted_iota(jnp.int32, (D_CHUNK, D_CHUNK), 1)
    for gi in range(D_GROUPS):
        w = jnp.where(ri >= ci, spw_ref[gi], 0.0).astype(BF16)
        s = _dot(w, vd[:, gi * D_GROUP:(gi + 1) * D_GROUP]) + spb_ref[:, gi:gi + 1]
        y_ref[0, :, W_CV + gi * D_GROUP:W_CV + (gi + 1) * D_GROUP] = (u[:, gi * D_GROUP:(gi + 1) * D_GROUP] * s).astype(BF16)

    @pl.when(c == pl.num_programs(1) - 1)
    def _():
        s_out_ref[0] = s_c[...]


def _odd_prompt(proj, cos, sin, dec, cs, kd, sdec, gn, spw, spb_t, lng, lnb):
    nchunk = SEQ // RET_CHUNK
    return pl.pallas_call(
        _odd_prompt_kernel,
        grid=(BATCH, nchunk),
        in_specs=[
            pl.BlockSpec((1, RET_CHUNK, ODD_IN), lambda b, c: (b, c, 0)),
            pl.BlockSpec((RET_CHUNK, W_CK), lambda b, c: (c, 0)),
            pl.BlockSpec((RET_CHUNK, W_CK), lambda b, c: (c, 0)),
            _const_spec((C_HEADS, RET_CHUNK, RET_CHUNK)),
            _const_spec((RET_CHUNK, W_CV)),
            _const_spec((RET_CHUNK, W_CK)),
            _const_spec((C_HEADS, C_DK, C_DV)),
            _const_spec((1, W_CV)),
            _const_spec((D_GROUPS, D_CHUNK, D_CHUNK)),
            _const_spec((D_CHUNK, D_GROUPS)),
            _const_spec((1, W_D)),
            _const_spec((1, W_D)),
        ],
        out_specs=[
            pl.BlockSpec((1, RET_CHUNK, D_MODEL), lambda b, c: (b, c, 0)),
            pl.BlockSpec((1, C_HEADS, C_DK, C_DV), lambda b, c: (b, 0, 0, 0)),
        ],
        out_shape=[
            jax.ShapeDtypeStruct((BATCH, SEQ, D_MODEL), BF16),
            jax.ShapeDtypeStruct((BATCH, C_HEADS, C_DK, C_DV), F32),
        ],
        scratch_shapes=[pltpu.VMEM((C_HEADS, C_DK, C_DV), F32)],
        compiler_params=_params(("parallel", "arbitrary"), 40),
        name="odd_prompt",
    )(proj, cos, sin, dec, cs, kd, sdec, gn, spw, spb_t, lng, lnb)


def _odd_sample_kernel(proj_ref, s0_ref, cos_ref, sin_ref, dec_ref, cs_ref, kd_ref, sdec_ref, gn_ref,
                       spw_ref, spb_ref, lng_ref, lnb_ref, y_ref, s_out_ref, vd_ref):
    q = _rope(proj_ref[:, :, 0:W_CK], cos_ref[...], sin_ref[...])
    k = _rope(proj_ref[:, :, W_CK:2 * W_CK], cos_ref[...], sin_ref[...]) * (C_DK ** -0.5)
    kd = k * kd_ref[...]
    v0 = 2 * W_CK
    g0 = v0 + W_CV
    for h in range(C_HEADS):
        qh = q[:, :, h * C_DK:(h + 1) * C_DK].astype(BF16)
        kh = k[:, :, h * C_DK:(h + 1) * C_DK].astype(BF16)
        kdh = kd[:, :, h * C_DK:(h + 1) * C_DK].astype(BF16)
        vh = proj_ref[:, :, v0 + h * C_DV:v0 + (h + 1) * C_DV].astype(BF16)
        s_prev = s0_ref[:, h]
        scores = jnp.einsum('nld,nmd->nlm', qh, kh, preferred_element_type=F32) * dec_ref[h]
        o = (jnp.einsum('nlm,nme->nle', scores.astype(BF16), vh, preferred_element_type=F32)
             + jnp.einsum('nld,nde->nle', qh, s_prev.astype(BF16), preferred_element_type=F32)
             * cs_ref[:, h * C_DV:(h + 1) * C_DV])
        s_out_ref[:, h] = sdec_ref[h] * s_prev + jnp.einsum('nld,nle->nde', kdh, vh, preferred_element_type=F32)
        gate = jax.nn.silu(proj_ref[:, :, g0 + h * C_DV:g0 + (h + 1) * C_DV])
        y_ref[:, :, h * C_DV:(h + 1) * C_DV] = (gate * (_group_norm(o) * gn_ref[:, h * C_DV:(h + 1) * C_DV])).astype(BF16)

    u0 = g0 + W_CV
    u = jax.nn.gelu(proj_ref[:, :, u0:u0 + W_D])
    vd = _layer_norm(jax.nn.gelu(proj_ref[:, :, u0 + W_D:]), lng_ref[...], lnb_ref[...])
    vd_ref[...] = vd
    s = spb_ref[...] + spw_ref[0] * vd[:, 0:1, :]
    for m in range(1, DEC_SEQ):
        s = s + spw_ref[m] * vd[:, m:m + 1, :]
    y_ref[:, :, W_CV:] = (u * s).astype(BF16)


def _odd_sample(proj, s0, cos, sin, dec, cs, kd, sdec, gn, spw_rows, spb_rows, lng, lnb):
    nb = SAMPLE_NB
    seq3 = lambda i: (i, 0, 0)
    return pl.pallas_call(
        _odd_sample_kernel,
        grid=(DEC_BATCH // nb,),
        in_specs=[
            pl.BlockSpec((nb, DEC_SEQ, ODD_IN), seq3),
            pl.BlockSpec((nb, C_HEADS, C_DK, C_DV), lambda i: (i, 0, 0, 0)),
            _const_spec((DEC_SEQ, W_CK)),
            _const_spec((DEC_SEQ, W_CK)),
            _const_spec((C_HEADS, DEC_SEQ, DEC_SEQ)),
            _const_spec((DEC_SEQ, W_CV)),
            _const_spec((DEC_SEQ, W_CK)),
            _const_spec((C_HEADS, C_DK, C_DV)),
            _const_spec((1, W_CV)),
            _const_spec((DEC_SEQ, DEC_SEQ, W_D)),
            _const_spec((DEC_SEQ, W_D)),
            _const_spec((1, W_D)),
            _const_spec((1, W_D)),
        ],
        out_specs=[
            pl.BlockSpec((nb, DEC_SEQ, D_MODEL), seq3),
            pl.BlockSpec((nb, C_HEADS, C_DK, C_DV), lambda i: (i, 0, 0, 0)),
            pl.BlockSpec((nb, DEC_SEQ, W_D), seq3),
        ],
        out_shape=[
            jax.ShapeDtypeStruct((DEC_BATCH, DEC_SEQ, D_MODEL), BF16),
            jax.ShapeDtypeStruct((DEC_BATCH, C_HEADS, C_DK, C_DV), F32),
            jax.ShapeDtypeStruct((DEC_BATCH, DEC_SEQ, W_D), F32),
        ],
        compiler_params=_params(("parallel",), 40),
        name="odd_sample",
    )(proj, s0, cos, sin, dec, cs, kd, sdec, gn, spw_rows, spb_rows, lng, lnb)


def _rope_tables(pos):
    half = C_DK // 2
    freq = ROPE_BASE ** (-jnp.arange(half, dtype=F32) / half)
    ang = pos.astype(F32)[:, None] * freq
    cos, sin = jnp.cos(ang), jnp.sin(ang)
    cos_l = jnp.tile(jnp.concatenate([cos, cos], axis=-1), (1, C_HEADS))
    sin_l = jnp.tile(jnp.concatenate([-sin, sin], axis=-1), (1, C_HEADS))
    return cos_l, sin_l


def _retention_tables(length):
    log_g = jnp.log1p(-jnp.exp2(-5.0 - jnp.arange(C_HEADS, dtype=F32)))
    idx = jnp.arange(length, dtype=F32)
    rel = idx[:, None] - idx[None, :]
    decay = jnp.where(rel >= 0, jnp.exp(log_g[:, None, None] * jnp.maximum(rel, 0.0)), 0.0)
    cross = jnp.exp(log_g[None, :] * (idx[:, None] + 1.0))
    kdec = jnp.exp(log_g[None, :] * (length - 1.0 - idx[:, None]))
    sdec = jnp.exp(log_g * length)
    return (decay,
            jnp.repeat(cross, C_DV, axis=1),
            jnp.repeat(kdec, C_DK, axis=1),
            jnp.broadcast_to(sdec[:, None, None], (C_HEADS, C_DK, C_DV)))


def _block_diag_halves(wa, wx):
    nb = A_BLOCKS // 2
    eye = jnp.eye(nb, dtype=wa.dtype)

    def bd(w):
        return (eye[:, None, :, None] * w[:, :, None, :]).reshape(nb * A_BLOCK, nb * A_BLOCK)

    halves = [jnp.concatenate([bd(wa[hh * nb:(hh + 1) * nb]), bd(wx[hh * nb:(hh + 1) * nb])], axis=1) for hh in range(2)]
    return jnp.stack(halves).astype(BF16)


def kernel(x_prompt, x_sample, state_lru_h, state_lru_conv, state_sconv, state_ret, w_in_even, conv_a_w, conv_a_b, lru_wa, lru_ba, lru_wx, lru_bx, lru_lam, conv_b_w, w_out_even, w_in_odd, ret_gn_g, sp_w, sp_b, gm_ln_g, gm_ln_b, w_out_odd, ffn_w1, ffn_w3, ffn_w2, ln1_g, ln1_b, ln2_g, ln2_b):
    row = lambda v: v.reshape(1, -1)
    xp = x_prompt.reshape(BATCH * SEQ, D_MODEL)
    xs = jnp.swapaxes(x_sample, 0, 1).reshape(DEC_SEQ * DEC_BATCH, D_MODEL)

    cos_p, sin_p = _rope_tables(jnp.arange(SEQ, dtype=jnp.int32))
    cos_s, sin_s = _rope_tables(PAST_LEN + jnp.arange(DEC_SEQ, dtype=jnp.int32))
    tab_p = _retention_tables(RET_CHUNK)
    tab_s = _retention_tables(DEC_SEQ)

    h_p, h_s, ca_p, ca_s, sc_p, sc_s, r_p, r_s, v_s = [], [], [], [], [], [], [], [], []
    for l in range(DEPTH):
        if l % 2 == 0:
            e = l // 2
            w_in = w_in_even[e].astype(BF16)
            w_out = w_out_even[e].astype(BF16)
            wblk = _block_diag_halves(lru_wa[e], lru_wx[e])
            pe = (conv_a_w[e], row(conv_a_b[e]), wblk, row(lru_ba[e]), row(lru_bx[e]), row(lru_lam[e]), conv_b_w[e])
            proj_p = _in_proj(xp, w_in).reshape(BATCH, SEQ, EVEN_IN)
            yp, hlp, cap, scp = _even_prompt(proj_p, *pe)
            proj_s = _in_proj(xs, w_in).reshape(DEC_SEQ, DEC_BATCH, EVEN_IN)
            ys, hls, cas, scs = _even_sample(
                proj_s, jnp.swapaxes(state_lru_conv[:, e], 0, 1), jnp.swapaxes(state_sconv[:, e], 0, 1),
                state_lru_h[:, e], *pe)
            h_p.append(hlp)
            h_s.append(hls)
            ca_p.append(cap[:, SUBLANES - (A_CONV - 1):])
            ca_s.append(jnp.swapaxes(cas, 0, 1))
            sc_p.append(scp[:, SUBLANES - (B_CONV - 1):])
            sc_s.append(jnp.swapaxes(scs, 0, 1))
            yp = yp.reshape(BATCH * SEQ, D_MODEL)
            ys = ys.reshape(DEC_SEQ * DEC_BATCH, D_MODEL)
        else:
            o = l // 2
            w_in = w_in_odd[o].astype(BF16)
            w_out = w_out_odd[o].astype(BF16)
            tril = jnp.tril(sp_w[o, :, :DEC_SEQ, :DEC_SEQ])
            spw_rows = jnp.repeat(jnp.transpose(tril, (2, 1, 0)), D_GROUP, axis=2)
            spb_rows = jnp.repeat(jnp.swapaxes(sp_b[o, :, :DEC_SEQ], 0, 1), D_GROUP, axis=1)
            po = (row(ret_gn_g[o]),)
            ln = (row(gm_ln_g[o]), row(gm_ln_b[o]))
            proj_p = _in_proj(xp, w_in).reshape(BATCH, SEQ, ODD_IN)
            yp, rp = _odd_prompt(proj_p, cos_p, sin_p, *tab_p, *po, sp_w[o], jnp.swapaxes(sp_b[o], 0, 1), *ln)
            proj_s = _in_proj(xs, w_in).reshape(DEC_BATCH, DEC_SEQ, ODD_IN)
            ys, rs, vs = _odd_sample(proj_s, state_ret[:, o], cos_s, sin_s, *tab_s, *po, spw_rows, spb_rows, *ln)
            r_p.append(rp)
            r_s.append(rs)
            v_s.append(vs)
            yp = yp.reshape(BATCH * SEQ, D_MODEL)
            ys = ys.reshape(DEC_BATCH * DEC_SEQ, D_MODEL)
        dense_w = (w_out, row(ln1_g[l]), row(ln1_b[l]), ffn_w1[l].astype(BF16), ffn_w3[l].astype(BF16),
                   ffn_w2[l].astype(BF16), row(ln2_g[l]), row(ln2_b[l]))
        xp = _dense_block(yp, xp, *dense_w)
        xs = _dense_block(ys, xs, *dense_w)
        if l + 1 < DEPTH:
            if l % 2 == 0:
                xs = jnp.swapaxes(xs.reshape(DEC_SEQ, DEC_BATCH, D_MODEL), 0, 1).reshape(DEC_BATCH * DEC_SEQ, D_MODEL)
            else:
                xs = jnp.swapaxes(xs.reshape(DEC_BATCH, DEC_SEQ, D_MODEL), 0, 1).reshape(DEC_SEQ * DEC_BATCH, D_MODEL)

    return (xp.reshape(BATCH, SEQ, D_MODEL), xs.reshape(DEC_BATCH, DEC_SEQ, D_MODEL),
            jnp.stack(h_p, axis=1), jnp.stack(h_s, axis=1),
            jnp.stack(ca_p, axis=1), jnp.stack(ca_s, axis=1),
            jnp.stack(sc_p, axis=1), jnp.stack(sc_s, axis=1),
            jnp.stack(r_p, axis=1), jnp.stack(r_s, axis=1),
            jnp.stack(v_s, axis=1))
```

```python
import jax
import jax.numpy as jnp
from jax import lax
from jax.experimental import pallas as pl
from jax.experimental.pallas import tpu as pltpu

F32 = jnp.float32
BF16 = jnp.bfloat16

D_MODEL = 1024
BATCH = 8
SEQ = 2048
DEPTH = 4
DEC_BATCH = 128
DEC_SEQ = 4
PAST_LEN = 16384
N_ODD = DEPTH // 2
W_A = 512
A_BLOCKS = 8
A_BLOCK = 64
A_CONV = 4
LRU_C = 8.0
W_B = 512
B_CONV = 3
C_HEADS = 4
C_DK = 64
C_DV = 128
W_CK = C_HEADS * C_DK
W_CV = C_HEADS * C_DV
RET_CHUNK = 128
ROPE_BASE = 10000.0
D_GROUPS = 4
D_CHUNK = 128
W_D = 512
D_GROUP = W_D // D_GROUPS
EVEN_IN = 2 * W_A + 3 * W_B
ODD_IN = 2 * W_CK + 2 * W_CV + 2 * W_D
D_FF = 2816
ALPHA = (2 * DEPTH) ** 0.25
LN_EPS = 1e-5

SUBLANES = 8
LANES = 128
MIB = 1024 * 1024

ROW_TILE = 512
EVEN_TT = 128
EVEN_PITCH = EVEN_TT + SUBLANES
SAMPLE_NB = 16


def _params(sem, vmem_mib):
    return pltpu.CompilerParams(dimension_semantics=sem, vmem_limit_bytes=vmem_mib * MIB)


def _const_spec(shape, layer=None):
    nd = len(shape)
    if layer is None:
        return pl.BlockSpec(shape, lambda *_: (0,) * nd, pipeline_mode=pl.Buffered(1))
    return pl.BlockSpec((None,) + tuple(shape), lambda *_: (layer,) + (0,) * nd, pipeline_mode=pl.Buffered(1))


def _layer_norm(x, g, b):
    mu = jnp.mean(x, -1, keepdims=True)
    xc = x - mu
    var = jnp.mean(xc * xc, -1, keepdims=True)
    return xc * lax.rsqrt(var + LN_EPS) * g + b


def _softplus(x):
    return jnp.maximum(x, 0.0) + jnp.log1p(jnp.exp(-jnp.abs(x)))


def _dot(a, b):
    return jnp.dot(a, b, preferred_element_type=F32)


def _proj_kernel(x_ref, w_ref, o_ref):
    o_ref[...] = _dot(x_ref[...].astype(BF16), w_ref[...])


def _in_proj(x2d, w_stack, layer):
    m, k = x2d.shape
    n = w_stack.shape[2]
    tm = min(ROW_TILE, m)
    return pl.pallas_call(
        _proj_kernel,
        grid=(m // tm,),
        in_specs=[pl.BlockSpec((tm, k), lambda i: (i, 0)), _const_spec((k, n), layer)],
        out_specs=pl.BlockSpec((tm, n), lambda i: (i, 0)),
        out_shape=jax.ShapeDtypeStruct((m, n), F32),
        compiler_params=_params(("parallel",), 40),
        name="in_proj",
    )(x2d, w_stack)


def _dense_kernel(y_ref, x_ref, wo_ref, g1_ref, b1_ref, w1_ref, w3_ref, w2_ref, g2_ref, b2_ref, o_ref):
    mix = _dot(y_ref[...], wo_ref[...])
    x1 = _layer_norm(ALPHA * x_ref[...] + mix, g1_ref[...], b1_ref[...])
    xb = x1.astype(BF16)
    h = jax.nn.silu(_dot(xb, w1_ref[...])) * _dot(xb, w3_ref[...])
    f = _dot(h.astype(BF16), w2_ref[...])
    o_ref[...] = _layer_norm(ALPHA * x1 + f, g2_ref[...], b2_ref[...])


def _dense_block(ycat, x2d, wo_stack, wo_layer, layer, g1, b1, w1, w3, w2, g2, b2):
    m = x2d.shape[0]
    tm = min(ROW_TILE, m)
    row = lambda i: (i, 0)
    return pl.pallas_call(
        _dense_kernel,
        grid=(m // tm,),
        in_specs=[
            pl.BlockSpec((tm, D_MODEL), row),
            pl.BlockSpec((tm, D_MODEL), row),
            _const_spec((D_MODEL, D_MODEL), wo_layer),
            _const_spec((1, D_MODEL), layer),
            _const_spec((1, D_MODEL), layer),
            _const_spec((D_MODEL, D_FF), layer),
            _const_spec((D_MODEL, D_FF), layer),
            _const_spec((D_FF, D_MODEL), layer),
            _const_spec((1, D_MODEL), layer),
            _const_spec((1, D_MODEL), layer),
        ],
        out_specs=pl.BlockSpec((tm, D_MODEL), row),
        out_shape=jax.ShapeDtypeStruct((m, D_MODEL), F32),
        compiler_params=_params(("parallel",), 56),
        name="dense_block",
    )(ycat, x2d, wo_stack, g1, b1, w1, w3, w2, g2, b2)


def _lru_gates(xc, wblk_ref, ba, bx, sp):
    half = W_A // 2
    xb = xc.astype(BF16)
    pre0 = _dot(xb[:, :half], wblk_ref[0])
    pre1 = _dot(xb[:, half:], wblk_ref[1])
    r = jax.nn.sigmoid(jnp.concatenate([pre0[:, :half], pre1[:, :half]], axis=1) + ba)
    i = jax.nn.sigmoid(jnp.concatenate([pre0[:, half:], pre1[:, half:]], axis=1) + bx)
    log_a = (-LRU_C) * r * sp
    a = jnp.exp(log_a)
    mult = jnp.sqrt((1.0 + a * a) * jnp.tanh(-log_a))
    return a, mult, i


def _even_prompt_kernel(x_ref, w_ref, caw_ref, cab_ref, wblk_ref, ba_ref, bx_ref, lam_ref, cbw_ref,
                        y_ref, hlast_ref, ca_out_ref, sc_out_ref,
                        p0, p1, xa_ext, cb_ext, a_s, b_s, g_s, h_c):
    tt, pitch = EVEN_TT, EVEN_PITCH
    nslab = W_A // LANES
    j = pl.program_id(0)
    bufs = (p0, p1)

    def project(b):
        bufs[b % 2][...] = _dot(x_ref[b].astype(BF16), w_ref[...])

    @pl.when(j == 0)
    def _():
        xa_ext[:, 0:SUBLANES, :] = jnp.zeros((BATCH, SUBLANES, W_A), F32)
        cb_ext[:, 0:SUBLANES, :] = jnp.zeros((BATCH, SUBLANES, W_B), F32)
        h_c[...] = jnp.zeros(h_c.shape, F32)

    @pl.when(j > 0)
    def _():
        xa_ext[:, 0:SUBLANES, :] = xa_ext[:, tt:tt + SUBLANES, :]
        cb_ext[:, 0:SUBLANES, :] = cb_ext[:, tt:tt + SUBLANES, :]

    sp = _softplus(-lam_ref[...])
    ba = ba_ref[...]
    bx = bx_ref[...]
    cab = cab_ref[...]
    first_row = (lax.broadcasted_iota(jnp.int32, (tt, 1), 0) == 0) & (j == 0)

    def stage(b):
        p_ref = bufs[b % 2]
        xa = p_ref[:, 0:W_A]
        xa_ext[b, SUBLANES:SUBLANES + tt, :] = xa
        xc = cab + caw_ref[A_CONV - 1:A_CONV, :] * xa
        for k in range(A_CONV - 1):
            off = SUBLANES - (A_CONV - 1) + k
            xc = xc + caw_ref[k:k + 1, :] * xa_ext[b, pl.ds(off, tt), :]
        a, mult, gate = _lru_gates(xc, wblk_ref, ba, bx, sp)
        mult = jnp.where(first_row, 1.0, mult)
        bb = mult * (gate * xc)
        for s in range(nslab):
            a_s[s, b * pitch:b * pitch + tt, :] = a[:, s * LANES:(s + 1) * LANES]
            b_s[s, b * pitch:b * pitch + tt, :] = bb[:, s * LANES:(s + 1) * LANES]
        g_s[b] = jax.nn.gelu(p_ref[:, W_A:2 * W_A])
        cb = p_ref[:, 2 * W_A + W_B:2 * W_A + 2 * W_B] * p_ref[:, 2 * W_A + 2 * W_B:]
        cb_ext[b, SUBLANES:SUBLANES + tt, :] = cb
        zb = cbw_ref[B_CONV - 1:B_CONV, :] * cb
        for k in range(B_CONV - 1):
            off = SUBLANES - (B_CONV - 1) + k
            zb = zb + cbw_ref[k:k + 1, :] * cb_ext[b, pl.ds(off, tt), :]
        y_ref[b, :, W_A:] = (p_ref[:, 2 * W_A:2 * W_A + W_B] * zb).astype(BF16)

    project(0)
    for b in range(BATCH):
        if b + 1 < BATCH:
            project(b + 1)
        stage(b)

    def step(t, hs):
        out = []
        for s in range(nslab):
            rows = pl.ds(t, BATCH, stride=pitch)
            hn = a_s[s, rows, :] * hs[s] + b_s[s, rows, :]
            b_s[s, rows, :] = hn
            out.append(hn)
        return tuple(out)

    hs = lax.fori_loop(0, tt, step, tuple(h_c[s] for s in range(nslab)), unroll=4)
    for s in range(nslab):
        h_c[s] = hs[s]

    def emit(b, carry):
        row0 = pl.multiple_of(b * pitch, SUBLANES)
        h = jnp.concatenate([b_s[s, pl.ds(row0, tt), :] for s in range(nslab)], axis=1)
        y_ref[b, :, 0:W_A] = (h * g_s[b]).astype(BF16)
        return carry

    lax.fori_loop(0, BATCH, emit, 0)

    @pl.when(j == pl.num_programs(0) - 1)
    def _():
        hlast_ref[...] = jnp.concatenate(list(hs), axis=1)
        ca_out_ref[...] = xa_ext[:, tt:tt + SUBLANES, :]
        sc_out_ref[...] = cb_ext[:, tt:tt + SUBLANES, :]


def _even_param_specs(e):
    return [
        _const_spec((A_CONV, W_A), e),
        _const_spec((1, W_A), e),
        _const_spec((2, W_A // 2, W_A), e),
        _const_spec((1, W_A), e),
        _const_spec((1, W_A), e),
        _const_spec((1, W_A), e),
        _const_spec((B_CONV, W_B), e),
    ]


def _even_prompt(x3d, w_stack, e, params):
    tt, pitch = EVEN_TT, EVEN_PITCH
    nslab = W_A // LANES
    return pl.pallas_call(
        _even_prompt_kernel,
        grid=(SEQ // tt,),
        in_specs=[pl.BlockSpec((BATCH, tt, D_MODEL), lambda j: (0, j, 0)),
                  _const_spec((D_MODEL, EVEN_IN), e)] + _even_param_specs(e),
        out_specs=[
            pl.BlockSpec((BATCH, tt, D_MODEL), lambda j: (0, j, 0)),
            pl.BlockSpec((BATCH, W_A), lambda j: (0, 0)),
            pl.BlockSpec((BATCH, SUBLANES, W_A), lambda j: (0, 0, 0)),
            pl.BlockSpec((BATCH, SUBLANES, W_B), lambda j: (0, 0, 0)),
        ],
        out_shape=[
            jax.ShapeDtypeStruct((BATCH, SEQ, D_MODEL), BF16),
            jax.ShapeDtypeStruct((BATCH, W_A), F32),
            jax.ShapeDtypeStruct((BATCH, SUBLANES, W_A), F32),
            jax.ShapeDtypeStruct((BATCH, SUBLANES, W_B), F32),
        ],
        scratch_shapes=[
            pltpu.VMEM((tt, EVEN_IN), F32),
            pltpu.VMEM((tt, EVEN_IN), F32),
            pltpu.VMEM((BATCH, tt + SUBLANES, W_A), F32),
            pltpu.VMEM((BATCH, tt + SUBLANES, W_B), F32),
            pltpu.VMEM((nslab, BATCH * pitch, LANES), F32),
            pltpu.VMEM((nslab, BATCH * pitch, LANES), F32),
            pltpu.VMEM((BATCH, tt, W_A), F32),
            pltpu.VMEM((nslab, BATCH, LANES), F32),
        ],
        compiler_params=_params(("arbitrary",), 48),
        name="even_prompt",
    )(x3d, w_stack, *params)


def _even_sample_kernel(proj_ref, ca_ref, sc_ref, h0_ref, caw_ref, cab_ref, wblk_ref, ba_ref, bx_ref,
                        lam_ref, cbw_ref, y_ref, hlast_ref, ca_out_ref, sc_out_ref):
    sp = _softplus(-lam_ref[...])
    rows_a = [ca_ref[k] for k in range(A_CONV - 1)] + [proj_ref[l, :, 0:W_A] for l in range(DEC_SEQ)]
    xc = []
    for l in range(DEC_SEQ):
        acc = cab_ref[...] + caw_ref[0:1, :] * rows_a[l]
        for k in range(1, A_CONV):
            acc = acc + caw_ref[k:k + 1, :] * rows_a[l + k]
        xc.append(acc)
    xc_all = jnp.concatenate(xc, axis=0)
    a, mult, gate = _lru_gates(xc_all, wblk_ref, ba_ref[...], bx_ref[...], sp)
    bb = mult * (gate * xc_all)
    h = h0_ref[...]
    rows_b = [sc_ref[k] for k in range(B_CONV - 1)]
    for l in range(DEC_SEQ):
        sl = slice(l * DEC_BATCH, (l + 1) * DEC_BATCH)
        h = a[sl] * h + bb[sl]
        y_ref[l, :, 0:W_A] = (h * jax.nn.gelu(proj_ref[l, :, W_A:2 * W_A])).astype(BF16)
        rows_b.append(proj_ref[l, :, 2 * W_A + W_B:2 * W_A + 2 * W_B] * proj_ref[l, :, 2 * W_A + 2 * W_B:])
    hlast_ref[...] = h
    for l in range(DEC_SEQ):
        zb = cbw_ref[0:1, :] * rows_b[l]
        for k in range(1, B_CONV):
            zb = zb + cbw_ref[k:k + 1, :] * rows_b[l + k]
        y_ref[l, :, W_A:] = (proj_ref[l, :, 2 * W_A:2 * W_A + W_B] * zb).astype(BF16)
    for k in range(A_CONV - 1):
        ca_out_ref[k] = rows_a[DEC_SEQ + k]
    for k in range(B_CONV - 1):
        sc_out_ref[k] = rows_b[DEC_SEQ + k]


def _even_sample(proj_tm, ca_tm, sc_tm, h0_tm, e, params):
    whole = lambda shape: pl.BlockSpec(shape, lambda i: (0,) * len(shape))
    return pl.pallas_call(
        _even_sample_kernel,
        grid=(1,),
        in_specs=[
            whole((DEC_SEQ, DEC_BATCH, EVEN_IN)),
            _const_spec((A_CONV - 1, DEC_BATCH, W_A), e),
            _const_spec((B_CONV - 1, DEC_BATCH, W_B), e),
            _const_spec((DEC_BATCH, W_A), e),
        ] + _even_param_specs(e),
        out_specs=[
            whole((DEC_SEQ, DEC_BATCH, D_MODEL)),
            whole((DEC_BATCH, W_A)),
            whole((A_CONV - 1, DEC_BATCH, W_A)),
            whole((B_CONV - 1, DEC_BATCH, W_B)),
        ],
        out_shape=[
            jax.ShapeDtypeStruct((DEC_SEQ, DEC_BATCH, D_MODEL), BF16),
            jax.ShapeDtypeStruct((DEC_BATCH, W_A), F32),
            jax.ShapeDtypeStruct((A_CONV - 1, DEC_BATCH, W_A), F32),
            jax.ShapeDtypeStruct((B_CONV - 1, DEC_BATCH, W_B), F32),
        ],
        compiler_params=_params(("arbitrary",), 40),
        name="even_sample",
    )(proj_tm, ca_tm, sc_tm, h0_tm, *params)


def _rope(x, cos, sin_signed):
    half = C_DK // 2
    ax = x.ndim - 1
    pieces = []
    for p in range(W_CK // LANES):
        xv = x[..., p * LANES:(p + 1) * LANES]
        lane = lax.broadcasted_iota(jnp.int32, xv.shape, ax)
        partner = jnp.where((lane % C_DK) < half, pltpu.roll(xv, LANES - half, axis=ax), pltpu.roll(xv, half, axis=ax))
        pieces.append(xv * cos[..., p * LANES:(p + 1) * LANES] + partner * sin_signed[..., p * LANES:(p + 1) * LANES])
    return jnp.concatenate(pieces, axis=ax)


def _group_norm(o):
    mu = jnp.mean(o, -1, keepdims=True)
    oc = o - mu
    var = jnp.mean(oc * oc, -1, keepdims=True)
    return oc * lax.rsqrt(var + LN_EPS)


def _odd_chunk(p_ref, b, cos_ref, sin_ref, dec_ref, cs_ref, kd_ref, sdec_ref, gn_ref,
               spw_ref, spb_ref, lng_ref, lnb_ref, y_ref, s_c):
    q = _rope(p_ref[:, 0:W_CK], cos_ref[...], sin_ref[...])
    k = _rope(p_ref[:, W_CK:2 * W_CK], cos_ref[...], sin_ref[...]) * (C_DK ** -0.5)
    kd = k * kd_ref[...]
    v0 = 2 * W_CK
    g0 = v0 + W_CV
    for h in range(C_HEADS):
        qh = q[:, h * C_DK:(h + 1) * C_DK].astype(BF16)
        kh = k[:, h * C_DK:(h + 1) * C_DK].astype(BF16)
        kdh = kd[:, h * C_DK:(h + 1) * C_DK].astype(BF16)
        vh = p_ref[:, v0 + h * C_DV:v0 + (h + 1) * C_DV].astype(BF16)
        s_prev = s_c[b, h]
        scores = lax.dot_general(qh, kh, (((1,), (1,)), ((), ())), preferred_element_type=F32) * dec_ref[h]
        o = _dot(scores.astype(BF16), vh) + _dot(qh, s_prev.astype(BF16)) * cs_ref[:, h * C_DV:(h + 1) * C_DV]
        s_c[b, h] = sdec_ref[h] * s_prev + lax.dot_general(kdh, vh, (((0,), (0,)), ((), ())), preferred_element_type=F32)
        gate = jax.nn.silu(p_ref[:, g0 + h * C_DV:g0 + (h + 1) * C_DV])
        y_ref[b, :, h * C_DV:(h + 1) * C_DV] = (gate * (_group_norm(o) * gn_ref[:, h * C_DV:(h + 1) * C_DV])).astype(BF16)

    u0 = g0 + W_CV
    u = jax.nn.gelu(p_ref[:, u0:u0 + W_D])
    vd = _layer_norm(jax.nn.gelu(p_ref[:, u0 + W_D:]), lng_ref[...], lnb_ref[...]).astype(BF16)
    ri = lax.broadcasted_iota(jnp.int32, (D_CHUNK, D_CHUNK), 0)
    ci = lax.broadcasted_iota(jnp.int32, (D_CHUNK, D_CHUNK), 1)
    for gi in range(D_GROUPS):
        w = jnp.where(ri >= ci, spw_ref[gi], 0.0).astype(BF16)
        s = _dot(w, vd[:, gi * D_GROUP:(gi + 1) * D_GROUP]) + spb_ref[:, gi:gi + 1]
        y_ref[b, :, W_CV + gi * D_GROUP:W_CV + (gi + 1) * D_GROUP] = (u[:, gi * D_GROUP:(gi + 1) * D_GROUP] * s).astype(BF16)


def _odd_prompt_kernel(x_ref, w_ref, cos_ref, sin_ref, dec_ref, cs_ref, kd_ref, sdec_ref, gn_ref,
                       spw_ref, spb_ref, lng_ref, lnb_ref, y_ref, s_out_ref, p0, p1, s_c):
    c = pl.program_id(0)

    @pl.when(c == 0)
    def _():
        s_c[...] = jnp.zeros(s_c.shape, F32)

    bufs = (p0, p1)

    def project(b):
        bufs[b % 2][...] = _dot(x_ref[b].astype(BF16), w_ref[...])

    project(0)
    for b in range(BATCH):
        if b + 1 < BATCH:
            project(b + 1)
        _odd_chunk(bufs[b % 2], b, cos_ref, sin_ref, dec_ref, cs_ref, kd_ref, sdec_ref, gn_ref,
                   spw_ref, spb_ref, lng_ref, lnb_ref, y_ref, s_c)

    @pl.when(c == pl.num_programs(0) - 1)
    def _():
        s_out_ref[...] = s_c[...]


def _odd_prompt(x3d, w_stack, o, cos, sin, tabs, gn, spw, spb_t, lng, lnb):
    nchunk = SEQ // RET_CHUNK
    dec, cs, kd, sdec = tabs
    return pl.pallas_call(
        _odd_prompt_kernel,
        grid=(nchunk,),
        in_specs=[
            pl.BlockSpec((BATCH, RET_CHUNK, D_MODEL), lambda c: (0, c, 0)),
            _const_spec((D_MODEL, ODD_IN), o),
            pl.BlockSpec((RET_CHUNK, W_CK), lambda c: (c, 0)),
            pl.BlockSpec((RET_CHUNK, W_CK), lambda c: (c, 0)),
            _const_spec((C_HEADS, RET_CHUNK, RET_CHUNK)),
            _const_spec((RET_CHUNK, W_CV)),
            _const_spec((RET_CHUNK, W_CK)),
            _const_spec((C_HEADS, C_DK, C_DV)),
            _const_spec((1, W_CV), o),
            _const_spec((D_GROUPS, D_CHUNK, D_CHUNK), o),
            _const_spec((D_CHUNK, D_GROUPS), o),
            _const_spec((1, W_D), o),
            _const_spec((1, W_D), o),
        ],
        out_specs=[
            pl.BlockSpec((BATCH, RET_CHUNK, D_MODEL), lambda c: (0, c, 0)),
            pl.BlockSpec((BATCH, C_HEADS, C_DK, C_DV), lambda c: (0, 0, 0, 0)),
        ],
        out_shape=[
            jax.ShapeDtypeStruct((BATCH, SEQ, D_MODEL), BF16),
            jax.ShapeDtypeStruct((BATCH, C_HEADS, C_DK, C_DV), F32),
        ],
        scratch_shapes=[
            pltpu.VMEM((RET_CHUNK, ODD_IN), F32),
            pltpu.VMEM((RET_CHUNK, ODD_IN), F32),
            pltpu.VMEM((BATCH, C_HEADS, C_DK, C_DV), F32),
        ],
        compiler_params=_params(("arbitrary",), 48),
        name="odd_prompt",
    )(x3d, w_stack, cos, sin, dec, cs, kd, sdec, gn, spw, spb_t, lng, lnb)


def _odd_sample_kernel(proj_ref, s0_ref, cos_ref, sin_ref, dec_ref, cs_ref, kd_ref, sdec_ref, gn_ref,
                       spw_ref, spb_ref, lng_ref, lnb_ref, *rest):
    y_ref, s_out_ref, vd_ref = rest[-3:]
    if len(rest) > 3:
        s_out_ref[:, 0] = rest[0][...]
        s_out_ref = s_out_ref.at[:, 1]
    q = _rope(proj_ref[:, :, 0:W_CK], cos_ref[...], sin_ref[...])
    k = _rope(proj_ref[:, :, W_CK:2 * W_CK], cos_ref[...], sin_ref[...]) * (C_DK ** -0.5)
    kd = k * kd_ref[...]
    v0 = 2 * W_CK
    g0 = v0 + W_CV
    for h in range(C_HEADS):
        qh = q[:, :, h * C_DK:(h + 1) * C_DK].astype(BF16)
        kh = k[:, :, h * C_DK:(h + 1) * C_DK].astype(BF16)
        kdh = kd[:, :, h * C_DK:(h + 1) * C_DK].astype(BF16)
        vh = proj_ref[:, :, v0 + h * C_DV:v0 + (h + 1) * C_DV].astype(BF16)
        s_prev = s0_ref[:, h]
        scores = jnp.einsum('nld,nmd->nlm', qh, kh, preferred_element_type=F32) * dec_ref[h]
        o = (jnp.einsum('nlm,nme->nle', scores.astype(BF16), vh, preferred_element_type=F32)
             + jnp.einsum('nld,nde->nle', qh, s_prev.astype(BF16), preferred_element_type=F32)
             * cs_ref[:, h * C_DV:(h + 1) * C_DV])
        s_out_ref[:, h] = sdec_ref[h] * s_prev + jnp.einsum('nld,nle->nde', kdh, vh, preferred_element_type=F32)
        gate = jax.nn.silu(proj_ref[:, :, g0 + h * C_DV:g0 + (h + 1) * C_DV])
        y_ref[:, :, h * C_DV:(h + 1) * C_DV] = (gate * (_group_norm(o) * gn_ref[:, h * C_DV:(h + 1) * C_DV])).astype(BF16)

    u0 = g0 + W_CV
    u = jax.nn.gelu(proj_ref[:, :, u0:u0 + W_D])
    vd = _layer_norm(jax.nn.gelu(proj_ref[:, :, u0 + W_D:]), lng_ref[...], lnb_ref[...])
    vd_ref[...] = vd
    s = spb_ref[...] + spw_ref[0] * vd[:, 0:1, :]
    for m in range(1, DEC_SEQ):
        s = s + spw_ref[m] * vd[:, m:m + 1, :]
    y_ref[:, :, W_CV:] = (u * s).astype(BF16)


def _odd_sample(proj, state_ret, o, cos, sin, tabs, gn, spw_rows, spb_rows, lng, lnb, prev_states):
    nb = SAMPLE_NB
    dec, cs, kd, sdec = tabs
    seq3 = lambda i: (i, 0, 0)
    one_state = pl.BlockSpec((nb, C_HEADS, C_DK, C_DV), lambda i: (i, 0, 0, 0))
    in_specs = [
        pl.BlockSpec((nb, DEC_SEQ, ODD_IN), seq3),
        pl.BlockSpec((nb, None, C_HEADS, C_DK, C_DV), lambda i: (i, o, 0, 0, 0)),
        _const_spec((DEC_SEQ, W_CK)),
        _const_spec((DEC_SEQ, W_CK)),
        _const_spec((C_HEADS, DEC_SEQ, DEC_SEQ)),
        _const_spec((DEC_SEQ, W_CV)),
        _const_spec((DEC_SEQ, W_CK)),
        _const_spec((C_HEADS, C_DK, C_DV)),
        _const_spec((1, W_CV), o),
        _const_spec((DEC_SEQ, DEC_SEQ, W_D), o),
        _const_spec((DEC_SEQ, W_D), o),
        _const_spec((1, W_D), o),
        _const_spec((1, W_D), o),
    ]
    args = [proj, state_ret, cos, sin, dec, cs, kd, sdec, gn, spw_rows, spb_rows, lng, lnb]
    if prev_states is None:
        state_spec = one_state
        state_shape = (DEC_BATCH, C_HEADS, C_DK, C_DV)
    else:
        in_specs.append(one_state)
        args.append(prev_states)
        state_spec = pl.BlockSpec((nb, N_ODD, C_HEADS, C_DK, C_DV), lambda i: (i, 0, 0, 0, 0))
        state_shape = (DEC_BATCH, N_ODD, C_HEADS, C_DK, C_DV)
    return pl.pallas_call(
        _odd_sample_kernel,
        grid=(DEC_BATCH // nb,),
        in_specs=in_specs,
        out_specs=[pl.BlockSpec((nb, DEC_SEQ, D_MODEL), seq3), state_spec, pl.BlockSpec((nb, DEC_SEQ, W_D), seq3)],
        out_shape=[
            jax.ShapeDtypeStruct((DEC_BATCH, DEC_SEQ, D_MODEL), BF16),
            jax.ShapeDtypeStruct(state_shape, F32),
            jax.ShapeDtypeStruct((DEC_BATCH, DEC_SEQ, W_D), F32),
        ],
        compiler_params=_params(("parallel",), 48),
        name="odd_sample",
    )(*args)


def _rope_tables(pos):
    half = C_DK // 2
    freq = ROPE_BASE ** (-jnp.arange(half, dtype=F32) / half)
    ang = pos.astype(F32)[:, None] * freq
    cos, sin = jnp.cos(ang), jnp.sin(ang)
    cos_l = jnp.tile(jnp.concatenate([cos, cos], axis=-1), (1, C_HEADS))
    sin_l = jnp.tile(jnp.concatenate([-sin, sin], axis=-1), (1, C_HEADS))
    return cos_l, sin_l


def _retention_tables(length):
    log_g = jnp.log1p(-jnp.exp2(-5.0 - jnp.arange(C_HEADS, dtype=F32)))
    idx = jnp.arange(length, dtype=F32)
    rel = idx[:, None] - idx[None, :]
    decay = jnp.where(rel >= 0, jnp.exp(log_g[:, None, None] * jnp.maximum(rel, 0.0)), 0.0)
    cross = jnp.exp(log_g[None, :] * (idx[:, None] + 1.0))
    kdec = jnp.exp(log_g[None, :] * (length - 1.0 - idx[:, None]))
    sdec = jnp.exp(log_g * length)
    return (decay,
            jnp.repeat(cross, C_DV, axis=1),
            jnp.repeat(kdec, C_DK, axis=1),
            jnp.broadcast_to(sdec[:, None, None], (C_HEADS, C_DK, C_DV)))


def _block_diag_halves(wa, wx):
    nl = wa.shape[0]
    nb = A_BLOCKS // 2
    eye = jnp.eye(nb, dtype=wa.dtype)

    def bd(w):
        return (eye[:, None, :, None] * w[:, :, :, :, None, :]).reshape(nl, 2, nb * A_BLOCK, nb * A_BLOCK)

    split = lambda w: w.reshape(nl, 2, nb, A_BLOCK, A_BLOCK)
    return jnp.concatenate([bd(split(wa)), bd(split(wx))], axis=-1).astype(BF16)


def kernel(x_prompt, x_sample, state_lru_h, state_lru_conv, state_sconv, state_ret, w_in_even, conv_a_w, conv_a_b, lru_wa, lru_ba, lru_wx, lru_bx, lru_lam, conv_b_w, w_out_even, w_in_odd, ret_gn_g, sp_w, sp_b, gm_ln_g, gm_ln_b, w_out_odd, ffn_w1, ffn_w3, ffn_w2, ln1_g, ln1_b, ln2_g, ln2_b):
    rows = lambda v: v.reshape(v.shape[0], 1, v.shape[1])
    xp = x_prompt.reshape(BATCH * SEQ, D_MODEL)
    xs = jnp.swapaxes(x_sample, 0, 1).reshape(DEC_SEQ * DEC_BATCH, D_MODEL)

    cos_p, sin_p = _rope_tables(jnp.arange(SEQ, dtype=jnp.int32))
    cos_s, sin_s = _rope_tables(PAST_LEN + jnp.arange(DEC_SEQ, dtype=jnp.int32))
    tab_p = _retention_tables(RET_CHUNK)
    tab_s = _retention_tables(DEC_SEQ)

    w_in_even_b, w_out_even_b = w_in_even.astype(BF16), w_out_even.astype(BF16)
    w_in_odd_b, w_out_odd_b = w_in_odd.astype(BF16), w_out_odd.astype(BF16)
    dense_p = (rows(ln1_g), rows(ln1_b), ffn_w1.astype(BF16), ffn_w3.astype(BF16), ffn_w2.astype(BF16),
               rows(ln2_g), rows(ln2_b))
    even_p = (conv_a_w, rows(conv_a_b), _block_diag_halves(lru_wa, lru_wx), rows(lru_ba), rows(lru_bx),
              rows(lru_lam), conv_b_w)
    ca_tm = jnp.transpose(state_lru_conv, (1, 2, 0, 3))
    sc_tm = jnp.transpose(state_sconv, (1, 2, 0, 3))
    h0_tm = jnp.swapaxes(state_lru_h, 0, 1)
    tril = jnp.tril(sp_w[:, :, :DEC_SEQ, :DEC_SEQ])
    spw_rows = jnp.repeat(jnp.transpose(tril, (0, 3, 2, 1)), D_GROUP, axis=3)
    spb_rows = jnp.repeat(jnp.swapaxes(sp_b[:, :, :DEC_SEQ], 1, 2), D_GROUP, axis=2)
    spb_t = jnp.swapaxes(sp_b, 1, 2)
    odd_small = (rows(ret_gn_g),)
    odd_ln = (rows(gm_ln_g), rows(gm_ln_b))

    h_p, h_s, ca_p, ca_s, sc_p, sc_s, r_p, v_s = [], [], [], [], [], [], [], []
    ret_sample = None
    for l in range(DEPTH):
        if l % 2 == 0:
            e = l // 2
            wo_stack, wo_layer = w_out_even_b, e
            yp, hlp, cap, scp = _even_prompt(xp.reshape(BATCH, SEQ, D_MODEL), w_in_even_b, e, even_p)
            proj_s = _in_proj(xs, w_in_even_b, e).reshape(DEC_SEQ, DEC_BATCH, EVEN_IN)
            ys, hls, cas, scs = _even_sample(proj_s, ca_tm, sc_tm, h0_tm, e, even_p)
            h_p.append(hlp)
            h_s.append(hls)
            ca_p.append(cap[:, SUBLANES - (A_CONV - 1):])
            ca_s.append(jnp.swapaxes(cas, 0, 1))
            sc_p.append(scp[:, SUBLANES - (B_CONV - 1):])
            sc_s.append(jnp.swapaxes(scs, 0, 1))
            yp = yp.reshape(BATCH * SEQ, D_MODEL)
            ys = ys.reshape(DEC_SEQ * DEC_BATCH, D_MODEL)
        else:
            o = l // 2
            wo_stack, wo_layer = w_out_odd_b, o
            yp, rp = _odd_prompt(xp.reshape(BATCH, SEQ, D_MODEL), w_in_odd_b, o, cos_p, sin_p, tab_p,
                                 *odd_small, sp_w, spb_t, *odd_ln)
            proj_s = _in_proj(xs, w_in_odd_b, o).reshape(DEC_BATCH, DEC_SEQ, ODD_IN)
            ys, ret_sample, vs = _odd_sample(proj_s, state_ret, o, cos_s, sin_s, tab_s, *odd_small,
                                             spw_rows, spb_rows, *odd_ln, ret_sample)
            r_p.append(rp)
            v_s.append(vs)
            yp = yp.reshape(BATCH * SEQ, D_MODEL)
            ys = ys.reshape(DEC_BATCH * DEC_SEQ, D_MODEL)
        xp = _dense_block(yp, xp, wo_stack, wo_layer, l, *dense_p)
        xs = _dense_block(ys, xs, wo_stack, wo_layer, l, *dense_p)
        if l + 1 < DEPTH:
            if l % 2 == 0:
                xs = jnp.swapaxes(xs.reshape(DEC_SEQ, DEC_BATCH, D_MODEL), 0, 1).reshape(DEC_BATCH * DEC_SEQ, D_MODEL)
            else:
                xs = jnp.swapaxes(xs.reshape(DEC_BATCH, DEC_SEQ, D_MODEL), 0, 1).reshape(DEC_SEQ * DEC_BATCH, D_MODEL)

    return (xp.reshape(BATCH, SEQ, D_MODEL), xs.reshape(DEC_BATCH, DEC_SEQ, D_MODEL),
            jnp.stack(h_p, axis=1), jnp.stack(h_s, axis=1),
            jnp.stack(ca_p, axis=1), jnp.stack(ca_s, axis=1),
            jnp.stack(sc_p, axis=1), jnp.stack(sc_s, axis=1),
            jnp.stack(r_p, axis=1), ret_sample, jnp.stack(v_s, axis=1))
```

```python
import jax
import jax.numpy as jnp
from jax import lax
from jax.experimental import pallas as pl
from jax.experimental.pallas import tpu as pltpu

F32 = jnp.float32
BF16 = jnp.bfloat16

D_MODEL = 1024
BATCH = 8
SEQ = 2048
DEPTH = 4
DEC_BATCH = 128
DEC_SEQ = 4
PAST_LEN = 16384
N_ODD = DEPTH // 2
W_A = 512
A_BLOCKS = 8
A_BLOCK = 64
A_CONV = 4
LRU_C = 8.0
W_B = 512
B_CONV = 3
C_HEADS = 4
C_DK = 64
C_DV = 128
W_CK = C_HEADS * C_DK
W_CV = C_HEADS * C_DV
RET_CHUNK = 128
ROPE_BASE = 10000.0
D_GROUPS = 4
D_CHUNK = 128
W_D = 512
D_GROUP = W_D // D_GROUPS
EVEN_IN = 2 * W_A + 3 * W_B
ODD_IN = 2 * W_CK + 2 * W_CV + 2 * W_D
D_FF = 2816
ALPHA = (2 * DEPTH) ** 0.25
LN_EPS = 1e-5

SUBLANES = 8
LANES = 128
MIB = 1024 * 1024

ROW_TILE = 512
DENSE_SPLIT = 2
PROJ_GROUP = 2
PROJ_COLS = 512
EVEN_TT = 128
EVEN_PITCH = EVEN_TT + SUBLANES
SAMPLE_NB = 16


def _params(sem, vmem_mib):
    return pltpu.CompilerParams(dimension_semantics=sem, vmem_limit_bytes=vmem_mib * MIB)


def _const_spec(shape, layer=None):
    nd = len(shape)
    if layer is None:
        return pl.BlockSpec(shape, lambda *_: (0,) * nd, pipeline_mode=pl.Buffered(1))
    return pl.BlockSpec((None,) + tuple(shape), lambda *_: (layer,) + (0,) * nd, pipeline_mode=pl.Buffered(1))


def _layer_norm(x, g, b):
    mu = jnp.mean(x, -1, keepdims=True)
    xc = x - mu
    var = jnp.mean(xc * xc, -1, keepdims=True)
    return xc * lax.rsqrt(var + LN_EPS) * g + b


def _softplus(x):
    return jnp.maximum(x, 0.0) + jnp.log1p(jnp.exp(-jnp.abs(x)))


def _dot(a, b):
    return jnp.dot(a, b, preferred_element_type=F32)


def _interleave(first, second):
    merged = [((i + 0.25) / len(first), 0, t) for i, t in enumerate(first)]
    merged += [((i + 0.5) / len(second), 1, t) for i, t in enumerate(second)]
    for _, _, thunk in sorted(merged, key=lambda e: e[:2]):
        thunk()


def _proj_kernel(x_ref, w_ref, o_ref):
    o_ref[...] = _dot(x_ref[...].astype(BF16), w_ref[...])


def _in_proj(x2d, w_stack, layer):
    m, k = x2d.shape
    n = w_stack.shape[2]
    tm = min(ROW_TILE, m)
    return pl.pallas_call(
        _proj_kernel,
        grid=(m // tm,),
        in_specs=[pl.BlockSpec((tm, k), lambda i: (i, 0)), _const_spec((k, n), layer)],
        out_specs=pl.BlockSpec((tm, n), lambda i: (i, 0)),
        out_shape=jax.ShapeDtypeStruct((m, n), F32),
        compiler_params=_params(("parallel",), 40),
        name="in_proj",
    )(x2d, w_stack)


def _dense_kernel(yp_ref, xp_ref, ys_ref, xs_ref, wo_ref, g1_ref, b1_ref, w1_ref, w3_ref, w2_ref, g2_ref, b2_ref,
                  op_ref, os_ref):
    n_prompt = pl.num_programs(0) - 1
    weights = (wo_ref, g1_ref, b1_ref, w1_ref, w3_ref, w2_ref, g2_ref, b2_ref)

    @pl.when(pl.program_id(0) < n_prompt)
    def _():
        _dense_rows(yp_ref, xp_ref, *weights, op_ref)

    @pl.when(pl.program_id(0) == n_prompt)
    def _():
        _dense_rows(ys_ref, xs_ref, *weights, os_ref)


def _dense_rows(y_ref, x_ref, wo_ref, g1_ref, b1_ref, w1_ref, w3_ref, w2_ref, g2_ref, b2_ref, o_ref):
    rows = y_ref.shape[0] // DENSE_SPLIT
    sls = [slice(p * rows, (p + 1) * rows) for p in range(DENSE_SPLIT)]
    mix = [_dot(y_ref[sl, :], wo_ref[...]) for sl in sls]
    x1, h = [], []
    for p, sl in enumerate(sls):
        x1.append(_layer_norm(ALPHA * x_ref[sl, :] + mix[p], g1_ref[...], b1_ref[...]))
        xb = x1[p].astype(BF16)
        h.append((jax.nn.silu(_dot(xb, w1_ref[...])) * _dot(xb, w3_ref[...])).astype(BF16))
    for p, sl in enumerate(sls):
        f = _dot(h[p], w2_ref[...])
        o_ref[sl, :] = _layer_norm(ALPHA * x1[p] + f, g2_ref[...], b2_ref[...])


def _dense_block(yp, xp, ys, xs, wo_stack, wo_layer, layer, g1, b1, w1, w3, w2, g2, b2):
    tm = ROW_TILE
    n_prompt = xp.shape[0] // tm
    assert xs.shape[0] == tm and xp.shape[0] % tm == 0
    prompt_row = lambda i: (jnp.minimum(i, n_prompt - 1), 0)
    sample_row = lambda i: (0, 0)
    return pl.pallas_call(
        _dense_kernel,
        grid=(n_prompt + 1,),
        in_specs=[
            pl.BlockSpec((tm, D_MODEL), prompt_row),
            pl.BlockSpec((tm, D_MODEL), prompt_row),
            pl.BlockSpec((tm, D_MODEL), sample_row),
            pl.BlockSpec((tm, D_MODEL), sample_row),
            _const_spec((D_MODEL, D_MODEL), wo_layer),
            _const_spec((1, D_MODEL), layer),
            _const_spec((1, D_MODEL), layer),
            _const_spec((D_MODEL, D_FF), layer),
            _const_spec((D_MODEL, D_FF), layer),
            _const_spec((D_FF, D_MODEL), layer),
            _const_spec((1, D_MODEL), layer),
            _const_spec((1, D_MODEL), layer),
        ],
        out_specs=[pl.BlockSpec((tm, D_MODEL), prompt_row), pl.BlockSpec((tm, D_MODEL), sample_row)],
        out_shape=[jax.ShapeDtypeStruct(xp.shape, F32), jax.ShapeDtypeStruct(xs.shape, F32)],
        compiler_params=_params(("arbitrary",), 58),
        name="dense_block",
    )(yp, xp, ys, xs, wo_stack, g1, b1, w1, w3, w2, g2, b2)


def _lru_gates(xc, wblk_ref, ba, bx, sp):
    half = W_A // 2
    xb = xc.astype(BF16)
    pre0 = _dot(xb[:, :half], wblk_ref[0])
    pre1 = _dot(xb[:, half:], wblk_ref[1])
    r = jax.nn.sigmoid(jnp.concatenate([pre0[:, :half], pre1[:, :half]], axis=1) + ba)
    i = jax.nn.sigmoid(jnp.concatenate([pre0[:, half:], pre1[:, half:]], axis=1) + bx)
    log_a = (-LRU_C) * r * sp
    a = jnp.exp(log_a)
    mult = jnp.sqrt((1.0 + a * a) * jnp.tanh(-log_a))
    return a, mult, i


def _even_prompt_kernel(x_ref, xn_ref, w_ref, caw_ref, cab_ref, wblk_ref, ba_ref, bx_ref, lam_ref, cbw_ref,
                        y_ref, hlast_ref, ca_out_ref, sc_out_ref,
                        xb_s, p0, p1, xa_ext, cb_ext, a_s, b_s, g_s, h_c):
    tt, pitch = EVEN_TT, EVEN_PITCH
    nslab = W_A // LANES
    ngroup = BATCH // PROJ_GROUP
    j = pl.program_id(0)
    bufs = (p0, p1)
    assert ngroup % 2 == 0
    rows_g = PROJ_GROUP * tt

    xb_s[0:BATCH * tt, :] = x_ref[...].reshape(BATCH * tt, D_MODEL).astype(BF16)
    xb_s[BATCH * tt:, :] = xn_ref[...].reshape(rows_g, D_MODEL).astype(BF16)

    def project(g):
        def chunk(c):
            cols = slice(c * PROJ_COLS, (c + 1) * PROJ_COLS)
            bufs[g % 2][:, cols] = _dot(xb_s[g * rows_g:(g + 1) * rows_g, :], w_ref[:, cols])
        return [lambda c=c: chunk(c) for c in range(EVEN_IN // PROJ_COLS)]

    @pl.when(j == 0)
    def _():
        for thunk in project(0):
            thunk()

    @pl.when(j == 0)
    def _():
        xa_ext[:, 0:SUBLANES, :] = jnp.zeros((BATCH, SUBLANES, W_A), F32)
        cb_ext[:, 0:SUBLANES, :] = jnp.zeros((BATCH, SUBLANES, W_B), F32)
        h_c[...] = jnp.zeros(h_c.shape, F32)

    @pl.when(j > 0)
    def _():
        xa_ext[:, 0:SUBLANES, :] = xa_ext[:, tt:tt + SUBLANES, :]
        cb_ext[:, 0:SUBLANES, :] = cb_ext[:, tt:tt + SUBLANES, :]

    sp = _softplus(-lam_ref[...])
    ba = ba_ref[...]
    bx = bx_ref[...]
    cab = cab_ref[...]
    first_row = (lax.broadcasted_iota(jnp.int32, (tt, 1), 0) == 0) & (j == 0)

    def proj_of(b):
        g, r = divmod(b, PROJ_GROUP)
        return bufs[g % 2].at[r * tt:(r + 1) * tt]

    def stage_lru(b):
        p_ref = proj_of(b)
        xa = p_ref[:, 0:W_A]
        xa_ext[b, SUBLANES:SUBLANES + tt, :] = xa
        xc = cab + caw_ref[A_CONV - 1:A_CONV, :] * xa
        for k in range(A_CONV - 1):
            off = SUBLANES - (A_CONV - 1) + k
            xc = xc + caw_ref[k:k + 1, :] * xa_ext[b, pl.ds(off, tt), :]
        a, mult, gate = _lru_gates(xc, wblk_ref, ba, bx, sp)
        mult = jnp.where(first_row, 1.0, mult)
        bb = mult * (gate * xc)
        for s in range(nslab):
            a_s[s, b * pitch:b * pitch + tt, :] = a[:, s * LANES:(s + 1) * LANES]
            b_s[s, b * pitch:b * pitch + tt, :] = bb[:, s * LANES:(s + 1) * LANES]

    def stage_conv(b):
        p_ref = proj_of(b)
        g_s[b] = jax.nn.gelu(p_ref[:, W_A:2 * W_A])
        cb = p_ref[:, 2 * W_A + W_B:2 * W_A + 2 * W_B] * p_ref[:, 2 * W_A + 2 * W_B:]
        cb_ext[b, SUBLANES:SUBLANES + tt, :] = cb
        zb = cbw_ref[B_CONV - 1:B_CONV, :] * cb
        for k in range(B_CONV - 1):
            off = SUBLANES - (B_CONV - 1) + k
            zb = zb + cbw_ref[k:k + 1, :] * cb_ext[b, pl.ds(off, tt), :]
        y_ref[b, :, W_A:] = (p_ref[:, 2 * W_A:2 * W_A + W_B] * zb).astype(BF16)

    hs = [None] * nslab

    def scan(t0, t1):
        for t in range(t0, t1):
            rows = pl.ds(t, BATCH, stride=pitch)
            for s in range(nslab):
                prev = h_c[s] if t == 0 else hs[s]
                hs[s] = a_s[s, rows, :] * prev + b_s[s, rows, :]
                b_s[s, rows, :] = hs[s]

    for g in range(ngroup):
        vector_work = []
        for b in range(g * PROJ_GROUP, (g + 1) * PROJ_GROUP):
            vector_work += [lambda b=b: stage_lru(b), lambda b=b: stage_conv(b)]
        if g + 1 == ngroup:
            nscan = 4
            vector_work += [lambda q=q: scan(q * tt // nscan, (q + 1) * tt // nscan) for q in range(nscan)]
        _interleave(project(g + 1), vector_work)
    for s in range(nslab):
        h_c[s] = hs[s]

    def emit(b, carry):
        row0 = pl.multiple_of(b * pitch, SUBLANES)
        h = jnp.concatenate([b_s[s, pl.ds(row0, tt), :] for s in range(nslab)], axis=1)
        y_ref[b, :, 0:W_A] = (h * g_s[b]).astype(BF16)
        return carry

    lax.fori_loop(0, BATCH, emit, 0)

    @pl.when(j == pl.num_programs(0) - 1)
    def _():
        hlast_ref[...] = jnp.concatenate(list(hs), axis=1)
        ca_out_ref[...] = xa_ext[:, tt:tt + SUBLANES, :]
        sc_out_ref[...] = cb_ext[:, tt:tt + SUBLANES, :]


def _even_param_specs(e):
    return [
        _const_spec((A_CONV, W_A), e),
        _const_spec((1, W_A), e),
        _const_spec((2, W_A // 2, W_A), e),
        _const_spec((1, W_A), e),
        _const_spec((1, W_A), e),
        _const_spec((1, W_A), e),
        _const_spec((B_CONV, W_B), e),
    ]


def _even_prompt(x3d, w_stack, e, params):
    tt, pitch = EVEN_TT, EVEN_PITCH
    nslab = W_A // LANES
    nblock = SEQ // tt
    return pl.pallas_call(
        _even_prompt_kernel,
        grid=(nblock,),
        in_specs=[pl.BlockSpec((BATCH, tt, D_MODEL), lambda j: (0, j, 0)),
                  pl.BlockSpec((PROJ_GROUP, tt, D_MODEL), lambda j: (0, jnp.minimum(j + 1, nblock - 1), 0)),
                  _const_spec((D_MODEL, EVEN_IN), e)] + _even_param_specs(e),
        out_specs=[
            pl.BlockSpec((BATCH, tt, D_MODEL), lambda j: (0, j, 0)),
            pl.BlockSpec((BATCH, W_A), lambda j: (0, 0)),
            pl.BlockSpec((BATCH, SUBLANES, W_A), lambda j: (0, 0, 0)),
            pl.BlockSpec((BATCH, SUBLANES, W_B), lambda j: (0, 0, 0)),
        ],
        out_shape=[
            jax.ShapeDtypeStruct((BATCH, SEQ, D_MODEL), BF16),
            jax.ShapeDtypeStruct((BATCH, W_A), F32),
            jax.ShapeDtypeStruct((BATCH, SUBLANES, W_A), F32),
            jax.ShapeDtypeStruct((BATCH, SUBLANES, W_B), F32),
        ],
        scratch_shapes=[
            pltpu.VMEM(((BATCH + PROJ_GROUP) * tt, D_MODEL), BF16),
            pltpu.VMEM((PROJ_GROUP * tt, EVEN_IN), F32),
            pltpu.VMEM((PROJ_GROUP * tt, EVEN_IN), F32),
            pltpu.VMEM((BATCH, tt + SUBLANES, W_A), F32),
            pltpu.VMEM((BATCH, tt + SUBLANES, W_B), F32),
            pltpu.VMEM((nslab, BATCH * pitch, LANES), F32),
            pltpu.VMEM((nslab, BATCH * pitch, LANES), F32),
            pltpu.VMEM((BATCH, tt, W_A), F32),
            pltpu.VMEM((nslab, BATCH, LANES), F32),
        ],
        compiler_params=_params(("arbitrary",), 48),
        name="even_prompt",
    )(x3d, x3d, w_stack, *params)


def _even_sample_kernel(proj_ref, ca_ref, sc_ref, h0_ref, caw_ref, cab_ref, wblk_ref, ba_ref, bx_ref,
                        lam_ref, cbw_ref, y_ref, hlast_ref, ca_out_ref, sc_out_ref):
    sp = _softplus(-lam_ref[...])
    rows_a = [ca_ref[k] for k in range(A_CONV - 1)] + [proj_ref[l, :, 0:W_A] for l in range(DEC_SEQ)]
    xc = []
    for l in range(DEC_SEQ):
        acc = cab_ref[...] + caw_ref[0:1, :] * rows_a[l]
        for k in range(1, A_CONV):
            acc = acc + caw_ref[k:k + 1, :] * rows_a[l + k]
        xc.append(acc)
    xc_all = jnp.concatenate(xc, axis=0)
    a, mult, gate = _lru_gates(xc_all, wblk_ref, ba_ref[...], bx_ref[...], sp)
    bb = mult * (gate * xc_all)
    h = h0_ref[...]
    rows_b = [sc_ref[k] for k in range(B_CONV - 1)]
    for l in range(DEC_SEQ):
        sl = slice(l * DEC_BATCH, (l + 1) * DEC_BATCH)
        h = a[sl] * h + bb[sl]
        y_ref[l, :, 0:W_A] = (h * jax.nn.gelu(proj_ref[l, :, W_A:2 * W_A])).astype(BF16)
        rows_b.append(proj_ref[l, :, 2 * W_A + W_B:2 * W_A + 2 * W_B] * proj_ref[l, :, 2 * W_A + 2 * W_B:])
    hlast_ref[...] = h
    for l in range(DEC_SEQ):
        zb = cbw_ref[0:1, :] * rows_b[l]
        for k in range(1, B_CONV):
            zb = zb + cbw_ref[k:k + 1, :] * rows_b[l + k]
        y_ref[l, :, W_A:] = (proj_ref[l, :, 2 * W_A:2 * W_A + W_B] * zb).astype(BF16)
    for k in range(A_CONV - 1):
        ca_out_ref[k] = rows_a[DEC_SEQ + k]
    for k in range(B_CONV - 1):
        sc_out_ref[k] = rows_b[DEC_SEQ + k]


def _even_sample(proj_tm, ca_tm, sc_tm, h0_tm, e, params):
    whole = lambda shape: pl.BlockSpec(shape, lambda i: (0,) * len(shape))
    return pl.pallas_call(
        _even_sample_kernel,
        grid=(1,),
        in_specs=[
            whole((DEC_SEQ, DEC_BATCH, EVEN_IN)),
            _const_spec((A_CONV - 1, DEC_BATCH, W_A), e),
            _const_spec((B_CONV - 1, DEC_BATCH, W_B), e),
            _const_spec((DEC_BATCH, W_A), e),
        ] + _even_param_specs(e),
        out_specs=[
            whole((DEC_SEQ, DEC_BATCH, D_MODEL)),
            whole((DEC_BATCH, W_A)),
            whole((A_CONV - 1, DEC_BATCH, W_A)),
            whole((B_CONV - 1, DEC_BATCH, W_B)),
        ],
        out_shape=[
            jax.ShapeDtypeStruct((DEC_SEQ, DEC_BATCH, D_MODEL), BF16),
            jax.ShapeDtypeStruct((DEC_BATCH, W_A), F32),
            jax.ShapeDtypeStruct((A_CONV - 1, DEC_BATCH, W_A), F32),
            jax.ShapeDtypeStruct((B_CONV - 1, DEC_BATCH, W_B), F32),
        ],
        compiler_params=_params(("arbitrary",), 40),
        name="even_sample",
    )(proj_tm, ca_tm, sc_tm, h0_tm, *params)


def _rope(x, cos, sin_signed):
    half = C_DK // 2
    ax = x.ndim - 1
    pieces = []
    for p in range(W_CK // LANES):
        xv = x[..., p * LANES:(p + 1) * LANES]
        lane = lax.broadcasted_iota(jnp.int32, xv.shape, ax)
        partner = jnp.where((lane % C_DK) < half, pltpu.roll(xv, LANES - half, axis=ax), pltpu.roll(xv, half, axis=ax))
        pieces.append(xv * cos[..., p * LANES:(p + 1) * LANES] + partner * sin_signed[..., p * LANES:(p + 1) * LANES])
    return jnp.concatenate(pieces, axis=ax)


def _group_norm(o):
    mu = jnp.mean(o, -1, keepdims=True)
    oc = o - mu
    var = jnp.mean(oc * oc, -1, keepdims=True)
    return oc * lax.rsqrt(var + LN_EPS)


def _odd_chunk(p_ref, b, cos_ref, sin_ref, dec_ref, cs_ref, kd_ref, sdec_ref, gn_ref,
               spw_ref, spb_ref, lng_ref, lnb_ref, y_ref, s_c):
    v0 = 2 * W_CK
    g0 = v0 + W_CV
    u0 = g0 + W_CV
    qk = {}

    def rotary():
        qk["q"] = _rope(p_ref[:, 0:W_CK], cos_ref[...], sin_ref[...])
        qk["k"] = _rope(p_ref[:, W_CK:2 * W_CK], cos_ref[...], sin_ref[...]) * (C_DK ** -0.5)
        qk["kd"] = qk["k"] * kd_ref[...]

    def head(h):
        qh = qk["q"][:, h * C_DK:(h + 1) * C_DK].astype(BF16)
        kh = qk["k"][:, h * C_DK:(h + 1) * C_DK].astype(BF16)
        kdh = qk["kd"][:, h * C_DK:(h + 1) * C_DK].astype(BF16)
        vh = p_ref[:, v0 + h * C_DV:v0 + (h + 1) * C_DV].astype(BF16)
        s_prev = s_c[b, h]
        scores = lax.dot_general(qh, kh, (((1,), (1,)), ((), ())), preferred_element_type=F32) * dec_ref[h]
        o = _dot(scores.astype(BF16), vh) + _dot(qh, s_prev.astype(BF16)) * cs_ref[:, h * C_DV:(h + 1) * C_DV]
        s_c[b, h] = sdec_ref[h] * s_prev + lax.dot_general(kdh, vh, (((0,), (0,)), ((), ())), preferred_element_type=F32)
        gate = jax.nn.silu(p_ref[:, g0 + h * C_DV:g0 + (h + 1) * C_DV])
        y_ref[b, :, h * C_DV:(h + 1) * C_DV] = (gate * (_group_norm(o) * gn_ref[:, h * C_DV:(h + 1) * C_DV])).astype(BF16)

    def gating():
        u = jax.nn.gelu(p_ref[:, u0:u0 + W_D])
        vd = _layer_norm(jax.nn.gelu(p_ref[:, u0 + W_D:]), lng_ref[...], lnb_ref[...]).astype(BF16)
        ri = lax.broadcasted_iota(jnp.int32, (D_CHUNK, D_CHUNK), 0)
        ci = lax.broadcasted_iota(jnp.int32, (D_CHUNK, D_CHUNK), 1)
        for gi in range(D_GROUPS):
            w = jnp.where(ri >= ci, spw_ref[gi], 0.0).astype(BF16)
            s = _dot(w, vd[:, gi * D_GROUP:(gi + 1) * D_GROUP]) + spb_ref[:, gi:gi + 1]
            y_ref[b, :, W_CV + gi * D_GROUP:W_CV + (gi + 1) * D_GROUP] = (u[:, gi * D_GROUP:(gi + 1) * D_GROUP] * s).astype(BF16)

    return [rotary] + [lambda h=h: head(h) for h in range(C_HEADS)] + [gating]


def _odd_prompt_kernel(x_ref, xn_ref, w_ref, cos_ref, sin_ref, dec_ref, cs_ref, kd_ref, sdec_ref, gn_ref,
                       spw_ref, spb_ref, lng_ref, lnb_ref, y_ref, s_out_ref, xb_s, p0, p1, s_c):
    c = pl.program_id(0)

    @pl.when(c == 0)
    def _():
        s_c[...] = jnp.zeros(s_c.shape, F32)

    bufs = (p0, p1)
    ngroup = BATCH // PROJ_GROUP
    assert ngroup % 2 == 0
    rows_g = PROJ_GROUP * RET_CHUNK

    xb_s[0:BATCH * RET_CHUNK, :] = x_ref[...].reshape(BATCH * RET_CHUNK, D_MODEL).astype(BF16)
    xb_s[BATCH * RET_CHUNK:, :] = xn_ref[...].reshape(rows_g, D_MODEL).astype(BF16)

    def project(g):
        def chunk(k):
            cols = slice(k * PROJ_COLS, (k + 1) * PROJ_COLS)
            bufs[g % 2][:, cols] = _dot(xb_s[g * rows_g:(g + 1) * rows_g, :], w_ref[:, cols])
        return [lambda k=k: chunk(k) for k in range(ODD_IN // PROJ_COLS)]

    @pl.when(c == 0)
    def _():
        for thunk in project(0):
            thunk()

    for g in range(ngroup):
        mixing = []
        for r in range(PROJ_GROUP):
            mixing += _odd_chunk(bufs[g % 2].at[r * RET_CHUNK:(r + 1) * RET_CHUNK], g * PROJ_GROUP + r,
                                 cos_ref, sin_ref, dec_ref, cs_ref, kd_ref, sdec_ref, gn_ref,
                                 spw_ref, spb_ref, lng_ref, lnb_ref, y_ref, s_c)
        _interleave(project(g + 1), mixing)

    @pl.when(c == pl.num_programs(0) - 1)
    def _():
        s_out_ref[...] = s_c[...]


def _odd_prompt(x3d, w_stack, o, cos, sin, tabs, gn, spw, spb_t, lng, lnb):
    nchunk = SEQ // RET_CHUNK
    dec, cs, kd, sdec = tabs
    return pl.pallas_call(
        _odd_prompt_kernel,
        grid=(nchunk,),
        in_specs=[
            pl.BlockSpec((BATCH, RET_CHUNK, D_MODEL), lambda c: (0, c, 0)),
            pl.BlockSpec((PROJ_GROUP, RET_CHUNK, D_MODEL), lambda c: (0, jnp.minimum(c + 1, nchunk - 1), 0)),
            _const_spec((D_MODEL, ODD_IN), o),
            pl.BlockSpec((RET_CHUNK, W_CK), lambda c: (c, 0)),
            pl.BlockSpec((RET_CHUNK, W_CK), lambda c: (c, 0)),
            _const_spec((C_HEADS, RET_CHUNK, RET_CHUNK)),
            _const_spec((RET_CHUNK, W_CV)),
            _const_spec((RET_CHUNK, W_CK)),
            _const_spec((C_HEADS, C_DK, C_DV)),
            _const_spec((1, W_CV), o),
            _const_spec((D_GROUPS, D_CHUNK, D_CHUNK), o),
            _const_spec((D_CHUNK, D_GROUPS), o),
            _const_spec((1, W_D), o),
            _const_spec((1, W_D), o),
        ],
        out_specs=[
            pl.BlockSpec((BATCH, RET_CHUNK, D_MODEL), lambda c: (0, c, 0)),
            pl.BlockSpec((BATCH, C_HEADS, C_DK, C_DV), lambda c: (0, 0, 0, 0)),
        ],
        out_shape=[
            jax.ShapeDtypeStruct((BATCH, SEQ, D_MODEL), BF16),
            jax.ShapeDtypeStruct((BATCH, C_HEADS, C_DK, C_DV), F32),
        ],
        scratch_shapes=[
            pltpu.VMEM(((BATCH + PROJ_GROUP) * RET_CHUNK, D_MODEL), BF16),
            pltpu.VMEM((PROJ_GROUP * RET_CHUNK, ODD_IN), F32),
            pltpu.VMEM((PROJ_GROUP * RET_CHUNK, ODD_IN), F32),
            pltpu.VMEM((BATCH, C_HEADS, C_DK, C_DV), F32),
        ],
        compiler_params=_params(("arbitrary",), 48),
        name="odd_prompt",
    )(x3d, x3d, w_stack, cos, sin, dec, cs, kd, sdec, gn, spw, spb_t, lng, lnb)


def _odd_sample_kernel(proj_ref, s0_ref, cos_ref, sin_ref, dec_ref, cs_ref, kd_ref, sdec_ref, gn_ref,
                       spw_ref, spb_ref, lng_ref, lnb_ref, *rest):
    y_ref, s_out_ref, vd_ref = rest[-3:]
    if len(rest) > 3:
        s_out_ref[:, 0] = rest[0][...]
        s_out_ref = s_out_ref.at[:, 1]
    q = _rope(proj_ref[:, :, 0:W_CK], cos_ref[...], sin_ref[...])
    k = _rope(proj_ref[:, :, W_CK:2 * W_CK], cos_ref[...], sin_ref[...]) * (C_DK ** -0.5)
    kd = k * kd_ref[...]
    v0 = 2 * W_CK
    g0 = v0 + W_CV
    for h in range(C_HEADS):
        qh = q[:, :, h * C_DK:(h + 1) * C_DK].astype(BF16)
        kh = k[:, :, h * C_DK:(h + 1) * C_DK].astype(BF16)
        kdh = kd[:, :, h * C_DK:(h + 1) * C_DK].astype(BF16)
        vh = proj_ref[:, :, v0 + h * C_DV:v0 + (h + 1) * C_DV].astype(BF16)
        s_prev = s0_ref[:, h]
        scores = jnp.einsum('nld,nmd->nlm', qh, kh, preferred_element_type=F32) * dec_ref[h]
        o = (jnp.einsum('nlm,nme->nle', scores.astype(BF16), vh, preferred_element_type=F32)
             + jnp.einsum('nld,nde->nle', qh, s_prev.astype(BF16), preferred_element_type=F32)
             * cs_ref[:, h * C_DV:(h + 1) * C_DV])
        s_out_ref[:, h] = sdec_ref[h] * s_prev + jnp.einsum('nld,nle->nde', kdh, vh, preferred_element_type=F32)
        gate = jax.nn.silu(proj_ref[:, :, g0 + h * C_DV:g0 + (h + 1) * C_DV])
        y_ref[:, :, h * C_DV:(h + 1) * C_DV] = (gate * (_group_norm(o) * gn_ref[:, h * C_DV:(h + 1) * C_DV])).astype(BF16)

    u0 = g0 + W_CV
    u = jax.nn.gelu(proj_ref[:, :, u0:u0 + W_D])
    vd = _layer_norm(jax.nn.gelu(proj_ref[:, :, u0 + W_D:]), lng_ref[...], lnb_ref[...])
    vd_ref[...] = vd
    s = spb_ref[...] + spw_ref[0] * vd[:, 0:1, :]
    for m in range(1, DEC_SEQ):
        s = s + spw_ref[m] * vd[:, m:m + 1, :]
    y_ref[:, :, W_CV:] = (u * s).astype(BF16)


def _odd_sample(proj, state_ret, o, cos, sin, tabs, gn, spw_rows, spb_rows, lng, lnb, prev_states):
    nb = SAMPLE_NB
    dec, cs, kd, sdec = tabs
    seq3 = lambda i: (i, 0, 0)
    one_state = pl.BlockSpec((nb, C_HEADS, C_DK, C_DV), lambda i: (i, 0, 0, 0))
    in_specs = [
        pl.BlockSpec((nb, DEC_SEQ, ODD_IN), seq3),
        pl.BlockSpec((nb, None, C_HEADS, C_DK, C_DV), lambda i: (i, o, 0, 0, 0)),
        _const_spec((DEC_SEQ, W_CK)),
        _const_spec((DEC_SEQ, W_CK)),
        _const_spec((C_HEADS, DEC_SEQ, DEC_SEQ)),
        _const_spec((DEC_SEQ, W_CV)),
        _const_spec((DEC_SEQ, W_CK)),
        _const_spec((C_HEADS, C_DK, C_DV)),
        _const_spec((1, W_CV), o),
        _const_spec((DEC_SEQ, DEC_SEQ, W_D), o),
        _const_spec((DEC_SEQ, W_D), o),
        _const_spec((1, W_D), o),
        _const_spec((1, W_D), o),
    ]
    args = [proj, state_ret, cos, sin, dec, cs, kd, sdec, gn, spw_rows, spb_rows, lng, lnb]
    if prev_states is None:
        state_spec = one_state
        state_shape = (DEC_BATCH, C_HEADS, C_DK, C_DV)
    else:
        in_specs.append(one_state)
        args.append(prev_states)
        state_spec = pl.BlockSpec((nb, N_ODD, C_HEADS, C_DK, C_DV), lambda i: (i, 0, 0, 0, 0))
        state_shape = (DEC_BATCH, N_ODD, C_HEADS, C_DK, C_DV)
    return pl.pallas_call(
        _odd_sample_kernel,
        grid=(DEC_BATCH // nb,),
        in_specs=in_specs,
        out_specs=[pl.BlockSpec((nb, DEC_SEQ, D_MODEL), seq3), state_spec, pl.BlockSpec((nb, DEC_SEQ, W_D), seq3)],
        out_shape=[
            jax.ShapeDtypeStruct((DEC_BATCH, DEC_SEQ, D_MODEL), BF16),
            jax.ShapeDtypeStruct(state_shape, F32),
            jax.ShapeDtypeStruct((DEC_BATCH, DEC_SEQ, W_D), F32),
        ],
        compiler_params=_params(("parallel",), 48),
        name="odd_sample",
    )(*args)


def _rope_tables(pos):
    half = C_DK // 2
    freq = ROPE_BASE ** (-jnp.arange(half, dtype=F32) / half)
    ang = pos.astype(F32)[:, None] * freq
    cos, sin = jnp.cos(ang), jnp.sin(ang)
    cos_l = jnp.tile(jnp.concatenate([cos, cos], axis=-1), (1, C_HEADS))
    sin_l = jnp.tile(jnp.concatenate([-sin, sin], axis=-1), (1, C_HEADS))
    return cos_l, sin_l


def _retention_tables(length):
    log_g = jnp.log1p(-jnp.exp2(-5.0 - jnp.arange(C_HEADS, dtype=F32)))
    idx = jnp.arange(length, dtype=F32)
    rel = idx[:, None] - idx[None, :]
    decay = jnp.where(rel >= 0, jnp.exp(log_g[:, None, None] * jnp.maximum(rel, 0.0)), 0.0)
    cross = jnp.exp(log_g[None, :] * (idx[:, None] + 1.0))
    kdec = jnp.exp(log_g[None, :] * (length - 1.0 - idx[:, None]))
    sdec = jnp.exp(log_g * length)
    return (decay,
            jnp.repeat(cross, C_DV, axis=1),
            jnp.repeat(kdec, C_DK, axis=1),
            jnp.broadcast_to(sdec[:, None, None], (C_HEADS, C_DK, C_DV)))


def _block_diag_halves(wa, wx):
    nl = wa.shape[0]
    nb = A_BLOCKS // 2
    eye = jnp.eye(nb, dtype=wa.dtype)

    def bd(w):
        return (eye[:, None, :, None] * w[:, :, :, :, None, :]).reshape(nl, 2, nb * A_BLOCK, nb * A_BLOCK)

    split = lambda w: w.reshape(nl, 2, nb, A_BLOCK, A_BLOCK)
    return jnp.concatenate([bd(split(wa)), bd(split(wx))], axis=-1).astype(BF16)


def kernel(x_prompt, x_sample, state_lru_h, state_lru_conv, state_sconv, state_ret, w_in_even, conv_a_w, conv_a_b, lru_wa, lru_ba, lru_wx, lru_bx, lru_lam, conv_b_w, w_out_even, w_in_odd, ret_gn_g, sp_w, sp_b, gm_ln_g, gm_ln_b, w_out_odd, ffn_w1, ffn_w3, ffn_w2, ln1_g, ln1_b, ln2_g, ln2_b):
    rows = lambda v: v.reshape(v.shape[0], 1, v.shape[1])
    xp = x_prompt.reshape(BATCH * SEQ, D_MODEL)
    xs = jnp.swapaxes(x_sample, 0, 1).reshape(DEC_SEQ * DEC_BATCH, D_MODEL)

    cos_p, sin_p = _rope_tables(jnp.arange(SEQ, dtype=jnp.int32))
    cos_s, sin_s = _rope_tables(PAST_LEN + jnp.arange(DEC_SEQ, dtype=jnp.int32))
    tab_p = _retention_tables(RET_CHUNK)
    tab_s = _retention_tables(DEC_SEQ)

    w_in_even_b, w_out_even_b = w_in_even.astype(BF16), w_out_even.astype(BF16)
    w_in_odd_b, w_out_odd_b = w_in_odd.astype(BF16), w_out_odd.astype(BF16)
    dense_p = (rows(ln1_g), rows(ln1_b), ffn_w1.astype(BF16), ffn_w3.astype(BF16), ffn_w2.astype(BF16),
               rows(ln2_g), rows(ln2_b))
    even_p = (conv_a_w, rows(conv_a_b), _block_diag_halves(lru_wa, lru_wx), rows(lru_ba), rows(lru_bx),
              rows(lru_lam), conv_b_w)
    ca_tm = jnp.transpose(state_lru_conv, (1, 2, 0, 3))
    sc_tm = jnp.transpose(state_sconv, (1, 2, 0, 3))
    h0_tm = jnp.swapaxes(state_lru_h, 0, 1)
    tril = jnp.tril(sp_w[:, :, :DEC_SEQ, :DEC_SEQ])
    spw_rows = jnp.repeat(jnp.transpose(tril, (0, 3, 2, 1)), D_GROUP, axis=3)
    spb_rows = jnp.repeat(jnp.swapaxes(sp_b[:, :, :DEC_SEQ], 1, 2), D_GROUP, axis=2)
    spb_t = jnp.swapaxes(sp_b, 1, 2)
    odd_small = (rows(ret_gn_g),)
    odd_ln = (rows(gm_ln_g), rows(gm_ln_b))

    h_p, h_s, ca_p, ca_s, sc_p, sc_s, r_p, v_s = [], [], [], [], [], [], [], []
    ret_sample = None
    for l in range(DEPTH):
        if l % 2 == 0:
            e = l // 2
            wo_stack, wo_layer = w_out_even_b, e
            yp, hlp, cap, scp = _even_prompt(xp.reshape(BATCH, SEQ, D_MODEL), w_in_even_b, e, even_p)
            proj_s = _in_proj(xs, w_in_even_b, e).reshape(DEC_SEQ, DEC_BATCH, EVEN_IN)
            ys, hls, cas, scs = _even_sample(proj_s, ca_tm, sc_tm, h0_tm, e, even_p)
            h_p.append(hlp)
            h_s.append(hls)
            ca_p.append(cap[:, SUBLANES - (A_CONV - 1):])
            ca_s.append(jnp.swapaxes(cas, 0, 1))
            sc_p.append(scp[:, SUBLANES - (B_CONV - 1):])
            sc_s.append(jnp.swapaxes(scs, 0, 1))
            yp = yp.reshape(BATCH * SEQ, D_MODEL)
            ys = ys.reshape(DEC_SEQ * DEC_BATCH, D_MODEL)
        else:
            o = l // 2
            wo_stack, wo_layer = w_out_odd_b, o
            yp, rp = _odd_prompt(xp.reshape(BATCH, SEQ, D_MODEL), w_in_odd_b, o, cos_p, sin_p, tab_p,
                                 *odd_small, sp_w, spb_t, *odd_ln)
            proj_s = _in_proj(xs, w_in_odd_b, o).reshape(DEC_BATCH, DEC_SEQ, ODD_IN)
            ys, ret_sample, vs = _odd_sample(proj_s, state_ret, o, cos_s, sin_s, tab_s, *odd_small,
                                             spw_rows, spb_rows, *odd_ln, ret_sample)
            r_p.append(rp)
            v_s.append(vs)
            yp = yp.reshape(BATCH * SEQ, D_MODEL)
            ys = ys.reshape(DEC_BATCH * DEC_SEQ, D_MODEL)
        xp, xs = _dense_block(yp, xp, ys, xs, wo_stack, wo_layer, l, *dense_p)
        if l + 1 < DEPTH:
            if l % 2 == 0:
                xs = jnp.swapaxes(xs.reshape(DEC_SEQ, DEC_BATCH, D_MODEL), 0, 1).reshape(DEC_BATCH * DEC_SEQ, D_MODEL)
            else:
                xs = jnp.swapaxes(xs.reshape(DEC_BATCH, DEC_SEQ, D_MODEL), 0, 1).reshape(DEC_SEQ * DEC_BATCH, D_MODEL)

    return (xp.reshape(BATCH, SEQ, D_MODEL), xs.reshape(DEC_BATCH, DEC_SEQ, D_MODEL),
            jnp.stack(h_p, axis=1), jnp.stack(h_s, axis=1),
            jnp.stack(ca_p, axis=1), jnp.stack(ca_s, axis=1),
            jnp.stack(sc_p, axis=1), jnp.stack(sc_s, axis=1),
            jnp.stack(r_p, axis=1), ret_sample, jnp.stack(v_s, axis=1))
```

```python
import jax
import jax.numpy as jnp
from jax import lax
from jax.experimental import pallas as pl
from jax.experimental.pallas import tpu as pltpu

F32 = jnp.float32
BF16 = jnp.bfloat16

D_MODEL = 1024
BATCH = 8
SEQ = 2048
DEPTH = 4
DEC_BATCH = 128
DEC_SEQ = 4
PAST_LEN = 16384
N_ODD = DEPTH // 2
W_A = 512
A_BLOCKS = 8
A_BLOCK = 64
A_CONV = 4
LRU_C = 8.0
W_B = 512
B_CONV = 3
C_HEADS = 4
C_DK = 64
C_DV = 128
W_CK = C_HEADS * C_DK
W_CV = C_HEADS * C_DV
RET_CHUNK = 128
ROPE_BASE = 10000.0
D_GROUPS = 4
D_CHUNK = 128
W_D = 512
D_GROUP = W_D // D_GROUPS
EVEN_IN = 2 * W_A + 3 * W_B
ODD_IN = 2 * W_CK + 2 * W_CV + 2 * W_D
D_FF = 2816
ALPHA = (2 * DEPTH) ** 0.25
LN_EPS = 1e-5

SUBLANES = 8
LANES = 128
MIB = 1024 * 1024

ROW_TILE = 512
DENSE_SPLIT = 2
EVEN_TT = 128
EVEN_PITCH = EVEN_TT + SUBLANES
SAMPLE_NB = 32


def _params(sem, vmem_mib):
    return pltpu.CompilerParams(dimension_semantics=sem, vmem_limit_bytes=vmem_mib * MIB)


def _const_spec(shape, layer=None):
    nd = len(shape)
    if layer is None:
        return pl.BlockSpec(shape, lambda *_: (0,) * nd, pipeline_mode=pl.Buffered(1))
    return pl.BlockSpec((None,) + tuple(shape), lambda *_: (layer,) + (0,) * nd, pipeline_mode=pl.Buffered(1))


def _layer_norm(x, g, b):
    mu = jnp.mean(x, -1, keepdims=True)
    xc = x - mu
    var = jnp.mean(xc * xc, -1, keepdims=True)
    return xc * lax.rsqrt(var + LN_EPS) * g + b


def _softplus(x):
    return jnp.maximum(x, 0.0) + jnp.log1p(jnp.exp(-jnp.abs(x)))


def _dot(a, b):
    return jnp.dot(a, b, preferred_element_type=F32)


def _proj_kernel(x_ref, w_ref, o_ref):
    o_ref[...] = _dot(x_ref[...].astype(BF16), w_ref[...])


def _in_proj(x2d, w_stack, layer):
    m, k = x2d.shape
    n = w_stack.shape[2]
    tm = min(ROW_TILE, m)
    return pl.pallas_call(
        _proj_kernel,
        grid=(m // tm,),
        in_specs=[pl.BlockSpec((tm, k), lambda i: (i, 0)), _const_spec((k, n), layer)],
        out_specs=pl.BlockSpec((tm, n), lambda i: (i, 0)),
        out_shape=jax.ShapeDtypeStruct((m, n), F32),
        compiler_params=_params(("parallel",), 40),
        name="in_proj",
    )(x2d, w_stack)


def _dense_kernel(yp_ref, xp_ref, ys_ref, xs_ref, wo_ref, g1_ref, b1_ref, w1_ref, w3_ref, w2_ref, g2_ref, b2_ref,
                  op_ref, os_ref):
    n_prompt = pl.num_programs(0) - 1
    weights = (wo_ref, g1_ref, b1_ref, w1_ref, w3_ref, w2_ref, g2_ref, b2_ref)

    @pl.when(pl.program_id(0) < n_prompt)
    def _():
        _dense_rows(yp_ref, xp_ref, *weights, op_ref)

    @pl.when(pl.program_id(0) == n_prompt)
    def _():
        _dense_rows(ys_ref, xs_ref, *weights, os_ref)


def _dense_rows(y_ref, x_ref, wo_ref, g1_ref, b1_ref, w1_ref, w3_ref, w2_ref, g2_ref, b2_ref, o_ref):
    nparts = DENSE_SPLIT
    rows = y_ref.shape[0] // nparts
    sls = [slice(p * rows, (p + 1) * rows) for p in range(nparts)]
    mix, x1, xb, h = {}, {}, {}, {}

    def out_proj(p):
        mix[p] = _dot(y_ref[sls[p], :], wo_ref[...])

    def norm1(p):
        x1[p] = _layer_norm(ALPHA * x_ref[sls[p], :] + mix[p], g1_ref[...], b1_ref[...])
        xb[p] = x1[p].astype(BF16)

    def gate_up(p):
        h[p] = (jax.nn.silu(_dot(xb[p], w1_ref[...])) * _dot(xb[p], w3_ref[...])).astype(BF16)

    def down_norm2(p):
        f = _dot(h[p], w2_ref[...])
        o_ref[sls[p], :] = _layer_norm(ALPHA * x1[p] + f, g2_ref[...], b2_ref[...])

    out_proj(0)
    for p in range(nparts):
        norm1(p)
        if p + 1 < nparts:
            out_proj(p + 1)
    for p in range(nparts):
        gate_up(p)
    for p in range(nparts):
        down_norm2(p)


def _dense_block(yp, xp, ys, xs, wo_stack, wo_layer, layer, g1, b1, w1, w3, w2, g2, b2):
    tm = ROW_TILE
    n_prompt = xp.shape[0] // tm
    assert xs.shape[0] == tm and xp.shape[0] % tm == 0
    prompt_row = lambda i: (jnp.minimum(i, n_prompt - 1), 0)
    sample_row = lambda i: (0, 0)
    return pl.pallas_call(
        _dense_kernel,
        grid=(n_prompt + 1,),
        in_specs=[
            pl.BlockSpec((tm, D_MODEL), prompt_row),
            pl.BlockSpec((tm, D_MODEL), prompt_row),
            pl.BlockSpec((tm, D_MODEL), sample_row),
            pl.BlockSpec((tm, D_MODEL), sample_row),
            _const_spec((D_MODEL, D_MODEL), wo_layer),
            _const_spec((1, D_MODEL), layer),
            _const_spec((1, D_MODEL), layer),
            _const_spec((D_MODEL, D_FF), layer),
            _const_spec((D_MODEL, D_FF), layer),
            _const_spec((D_FF, D_MODEL), layer),
            _const_spec((1, D_MODEL), layer),
            _const_spec((1, D_MODEL), layer),
        ],
        out_specs=[pl.BlockSpec((tm, D_MODEL), prompt_row), pl.BlockSpec((tm, D_MODEL), sample_row)],
        out_shape=[jax.ShapeDtypeStruct(xp.shape, F32), jax.ShapeDtypeStruct(xs.shape, F32)],
        compiler_params=_params(("arbitrary",), 58),
        name="dense_block",
    )(yp, xp, ys, xs, wo_stack, g1, b1, w1, w3, w2, g2, b2)


def _lru_gates(xc, wblk_ref, ba, bx, sp):
    half = W_A // 2
    xb = xc.astype(BF16)
    pre0 = _dot(xb[:, :half], wblk_ref[0])
    pre1 = _dot(xb[:, half:], wblk_ref[1])
    r = jax.nn.sigmoid(jnp.concatenate([pre0[:, :half], pre1[:, :half]], axis=1) + ba)
    i = jax.nn.sigmoid(jnp.concatenate([pre0[:, half:], pre1[:, half:]], axis=1) + bx)
    log_a = (-LRU_C) * r * sp
    a = jnp.exp(log_a)
    mult = jnp.sqrt((1.0 + a * a) * jnp.tanh(-log_a))
    return a, mult, i


def _causal_taps(x_slabs, ext, b, w_ref, nrows):
    ktaps = w_ref.shape[0]
    out = []
    for s, x in enumerate(x_slabs):
        lanes = slice(s * LANES, (s + 1) * LANES)
        ext[s, b, SUBLANES:SUBLANES + nrows, :] = x
        acc = w_ref[ktaps - 1:ktaps, lanes] * x
        for k in range(ktaps - 1):
            off = SUBLANES - (ktaps - 1) + k
            acc = acc + w_ref[k:k + 1, lanes] * ext[s, b, off:off + nrows, :]
        out.append(acc)
    return jnp.concatenate(out, axis=1)


def _even_prompt_kernel(x_ref, w_ref, caw_ref, cab_ref, wblk_ref, ba_ref, bx_ref, lam_ref, cbw_ref,
                        y_ref, hlast_ref, ca_out_ref, sc_out_ref,
                        p0, p1, xa_ext, cb_ext, a_s, b_s, g_s, h_c):
    tt, pitch = EVEN_TT, EVEN_PITCH
    nslab = W_A // LANES
    j = pl.program_id(0)
    bufs = (p0, p1)

    def project(b):
        bufs[b % 2][...] = _dot(x_ref[b].astype(BF16), w_ref[...])

    @pl.when(j == 0)
    def _():
        xa_ext[:, :, 0:SUBLANES, :] = jnp.zeros((nslab, BATCH, SUBLANES, LANES), F32)
        cb_ext[:, :, 0:SUBLANES, :] = jnp.zeros((nslab, BATCH, SUBLANES, LANES), F32)
        h_c[...] = jnp.zeros(h_c.shape, F32)

    @pl.when(j > 0)
    def _():
        xa_ext[:, :, 0:SUBLANES, :] = xa_ext[:, :, tt:tt + SUBLANES, :]
        cb_ext[:, :, 0:SUBLANES, :] = cb_ext[:, :, tt:tt + SUBLANES, :]

    sp = _softplus(-lam_ref[...])
    ba = ba_ref[...]
    bx = bx_ref[...]
    cab = cab_ref[...]
    first_row = (lax.broadcasted_iota(jnp.int32, (SUBLANES, 1), 0) == 0) & (j == 0)
    cg0 = 2 * W_A + W_B
    xb0 = 2 * W_A + 2 * W_B

    def stage(b):
        p_ref = bufs[b % 2]
        xc = cab + _causal_taps([p_ref[:, s * LANES:(s + 1) * LANES] for s in range(nslab)], xa_ext, b, caw_ref, tt)
        a, mult, gate = _lru_gates(xc, wblk_ref, ba, bx, sp)
        mult = jnp.concatenate([jnp.where(first_row, 1.0, mult[:SUBLANES]), mult[SUBLANES:]], axis=0)
        bb = mult * (gate * xc)
        for s in range(nslab):
            a_s[s, b * pitch:b * pitch + tt, :] = a[:, s * LANES:(s + 1) * LANES]
            b_s[s, b * pitch:b * pitch + tt, :] = bb[:, s * LANES:(s + 1) * LANES]
        g_s[b] = jax.nn.gelu(p_ref[:, W_A:2 * W_A])
        cb = [p_ref[:, cg0 + s * LANES:cg0 + (s + 1) * LANES] * p_ref[:, xb0 + s * LANES:xb0 + (s + 1) * LANES]
              for s in range(nslab)]
        zb = _causal_taps(cb, cb_ext, b, cbw_ref, tt)
        y_ref[b, :, W_A:] = (p_ref[:, 2 * W_A:2 * W_A + W_B] * zb).astype(BF16)

    project(0)
    for b in range(BATCH):
        if b + 1 < BATCH:
            project(b + 1)
        stage(b)

    def step(t, hs):
        out = []
        for s in range(nslab):
            rows = pl.ds(t, BATCH, stride=pitch)
            hn = a_s[s, rows, :] * hs[s] + b_s[s, rows, :]
            b_s[s, rows, :] = hn
            out.append(hn)
        return tuple(out)

    hs = lax.fori_loop(0, tt, step, tuple(h_c[s] for s in range(nslab)), unroll=4)
    for s in range(nslab):
        h_c[s] = hs[s]

    def emit(b, carry):
        row0 = pl.multiple_of(b * pitch, SUBLANES)
        h = jnp.concatenate([b_s[s, pl.ds(row0, tt), :] for s in range(nslab)], axis=1)
        y_ref[b, :, 0:W_A] = (h * g_s[b]).astype(BF16)
        return carry

    lax.fori_loop(0, BATCH, emit, 0)

    @pl.when(j == pl.num_programs(0) - 1)
    def _():
        hlast_ref[...] = jnp.concatenate(list(hs), axis=1)
        for s in range(nslab):
            ca_out_ref[:, :, s * LANES:(s + 1) * LANES] = xa_ext[s, :, tt:tt + SUBLANES, :]
            sc_out_ref[:, :, s * LANES:(s + 1) * LANES] = cb_ext[s, :, tt:tt + SUBLANES, :]


def _even_param_specs(e):
    return [
        _const_spec((A_CONV, W_A), e),
        _const_spec((1, W_A), e),
        _const_spec((2, W_A // 2, W_A), e),
        _const_spec((1, W_A), e),
        _const_spec((1, W_A), e),
        _const_spec((1, W_A), e),
        _const_spec((B_CONV, W_B), e),
    ]


def _even_prompt(x3d, w_stack, e, params):
    tt, pitch = EVEN_TT, EVEN_PITCH
    nslab = W_A // LANES
    assert W_A == W_B
    return pl.pallas_call(
        _even_prompt_kernel,
        grid=(SEQ // tt,),
        in_specs=[pl.BlockSpec((BATCH, tt, D_MODEL), lambda j: (0, j, 0)),
                  _const_spec((D_MODEL, EVEN_IN), e)] + _even_param_specs(e),
        out_specs=[
            pl.BlockSpec((BATCH, tt, D_MODEL), lambda j: (0, j, 0)),
            pl.BlockSpec((BATCH, W_A), lambda j: (0, 0)),
            pl.BlockSpec((BATCH, SUBLANES, W_A), lambda j: (0, 0, 0)),
            pl.BlockSpec((BATCH, SUBLANES, W_B), lambda j: (0, 0, 0)),
        ],
        out_shape=[
            jax.ShapeDtypeStruct((BATCH, SEQ, D_MODEL), BF16),
            jax.ShapeDtypeStruct((BATCH, W_A), F32),
            jax.ShapeDtypeStruct((BATCH, SUBLANES, W_A), F32),
            jax.ShapeDtypeStruct((BATCH, SUBLANES, W_B), F32),
        ],
        scratch_shapes=[
            pltpu.VMEM((tt, EVEN_IN), F32),
            pltpu.VMEM((tt, EVEN_IN), F32),
            pltpu.VMEM((nslab, BATCH, tt + SUBLANES, LANES), F32),
            pltpu.VMEM((nslab, BATCH, tt + SUBLANES, LANES), F32),
            pltpu.VMEM((nslab, BATCH * pitch, LANES), F32),
            pltpu.VMEM((nslab, BATCH * pitch, LANES), F32),
            pltpu.VMEM((BATCH, tt, W_A), F32),
            pltpu.VMEM((nslab, BATCH, LANES), F32),
        ],
        compiler_params=_params(("arbitrary",), 48),
        name="even_prompt",
    )(x3d, w_stack, *params)


def _even_sample_kernel(proj_ref, ca_ref, sc_ref, h0_ref, caw_ref, cab_ref, wblk_ref, ba_ref, bx_ref,
                        lam_ref, cbw_ref, y_ref, hlast_ref, ca_out_ref, sc_out_ref):
    sp = _softplus(-lam_ref[...])
    rows_a = [ca_ref[k] for k in range(A_CONV - 1)] + [proj_ref[l, :, 0:W_A] for l in range(DEC_SEQ)]
    xc = []
    for l in range(DEC_SEQ):
        acc = cab_ref[...] + caw_ref[0:1, :] * rows_a[l]
        for k in range(1, A_CONV):
            acc = acc + caw_ref[k:k + 1, :] * rows_a[l + k]
        xc.append(acc)
    xc_all = jnp.concatenate(xc, axis=0)
    a, mult, gate = _lru_gates(xc_all, wblk_ref, ba_ref[...], bx_ref[...], sp)
    bb = mult * (gate * xc_all)
    h = h0_ref[...]
    rows_b = [sc_ref[k] for k in range(B_CONV - 1)]
    for l in range(DEC_SEQ):
        sl = slice(l * DEC_BATCH, (l + 1) * DEC_BATCH)
        h = a[sl] * h + bb[sl]
        y_ref[l, :, 0:W_A] = (h * jax.nn.gelu(proj_ref[l, :, W_A:2 * W_A])).astype(BF16)
        rows_b.append(proj_ref[l, :, 2 * W_A + W_B:2 * W_A + 2 * W_B] * proj_ref[l, :, 2 * W_A + 2 * W_B:])
    hlast_ref[...] = h
    for l in range(DEC_SEQ):
        zb = cbw_ref[0:1, :] * rows_b[l]
        for k in range(1, B_CONV):
            zb = zb + cbw_ref[k:k + 1, :] * rows_b[l + k]
        y_ref[l, :, W_A:] = (proj_ref[l, :, 2 * W_A:2 * W_A + W_B] * zb).astype(BF16)
    for k in range(A_CONV - 1):
        ca_out_ref[k] = rows_a[DEC_SEQ + k]
    for k in range(B_CONV - 1):
        sc_out_ref[k] = rows_b[DEC_SEQ + k]


def _even_sample(proj_tm, ca_tm, sc_tm, h0_tm, e, params):
    whole = lambda shape: pl.BlockSpec(shape, lambda i: (0,) * len(shape))
    return pl.pallas_call(
        _even_sample_kernel,
        grid=(1,),
        in_specs=[
            whole((DEC_SEQ, DEC_BATCH, EVEN_IN)),
            _const_spec((A_CONV - 1, DEC_BATCH, W_A), e),
            _const_spec((B_CONV - 1, DEC_BATCH, W_B), e),
            _const_spec((DEC_BATCH, W_A), e),
        ] + _even_param_specs(e),
        out_specs=[
            whole((DEC_SEQ, DEC_BATCH, D_MODEL)),
            whole((DEC_BATCH, W_A)),
            whole((A_CONV - 1, DEC_BATCH, W_A)),
            whole((B_CONV - 1, DEC_BATCH, W_B)),
        ],
        out_shape=[
            jax.ShapeDtypeStruct((DEC_SEQ, DEC_BATCH, D_MODEL), BF16),
            jax.ShapeDtypeStruct((DEC_BATCH, W_A), F32),
            jax.ShapeDtypeStruct((A_CONV - 1, DEC_BATCH, W_A), F32),
            jax.ShapeDtypeStruct((B_CONV - 1, DEC_BATCH, W_B), F32),
        ],
        compiler_params=_params(("arbitrary",), 40),
        name="even_sample",
    )(proj_tm, ca_tm, sc_tm, h0_tm, *params)


def _rope(x, cos, sin_signed):
    half = C_DK // 2
    ax = x.ndim - 1
    pieces = []
    for p in range(W_CK // LANES):
        xv = x[..., p * LANES:(p + 1) * LANES]
        lane = lax.broadcasted_iota(jnp.int32, xv.shape, ax)
        partner = jnp.where((lane % C_DK) < half, pltpu.roll(xv, LANES - half, axis=ax), pltpu.roll(xv, half, axis=ax))
        pieces.append(xv * cos[..., p * LANES:(p + 1) * LANES] + partner * sin_signed[..., p * LANES:(p + 1) * LANES])
    return jnp.concatenate(pieces, axis=ax)


def _group_norm(o):
    mu = jnp.mean(o, -1, keepdims=True)
    oc = o - mu
    var = jnp.mean(oc * oc, -1, keepdims=True)
    return oc * lax.rsqrt(var + LN_EPS)


def _odd_chunk(p_ref, b, cos_ref, sin_ref, dec_ref, cs_ref, kd_ref, sdec_ref, gn_ref,
               spw_ref, spb_ref, lng_ref, lnb_ref, y_ref, s_c):
    q = _rope(p_ref[:, 0:W_CK], cos_ref[...], sin_ref[...])
    k = _rope(p_ref[:, W_CK:2 * W_CK], cos_ref[...], sin_ref[...]) * (C_DK ** -0.5)
    kd = k * kd_ref[...]
    v0 = 2 * W_CK
    g0 = v0 + W_CV
    for h in range(C_HEADS):
        qh = q[:, h * C_DK:(h + 1) * C_DK].astype(BF16)
        kh = k[:, h * C_DK:(h + 1) * C_DK].astype(BF16)
        kdh = kd[:, h * C_DK:(h + 1) * C_DK].astype(BF16)
        vh = p_ref[:, v0 + h * C_DV:v0 + (h + 1) * C_DV].astype(BF16)
        s_prev = s_c[b, h]
        scores = lax.dot_general(qh, kh, (((1,), (1,)), ((), ())), preferred_element_type=F32) * dec_ref[h]
        o = _dot(scores.astype(BF16), vh) + _dot(qh, s_prev.astype(BF16)) * cs_ref[:, h * C_DV:(h + 1) * C_DV]
        s_c[b, h] = sdec_ref[h] * s_prev + lax.dot_general(kdh, vh, (((0,), (0,)), ((), ())), preferred_element_type=F32)
        gate = jax.nn.silu(p_ref[:, g0 + h * C_DV:g0 + (h + 1) * C_DV])
        y_ref[b, :, h * C_DV:(h + 1) * C_DV] = (gate * (_group_norm(o) * gn_ref[:, h * C_DV:(h + 1) * C_DV])).astype(BF16)

    u0 = g0 + W_CV
    u = jax.nn.gelu(p_ref[:, u0:u0 + W_D])
    vd = _layer_norm(jax.nn.gelu(p_ref[:, u0 + W_D:]), lng_ref[...], lnb_ref[...]).astype(BF16)
    ri = lax.broadcasted_iota(jnp.int32, (D_CHUNK, D_CHUNK), 0)
    ci = lax.broadcasted_iota(jnp.int32, (D_CHUNK, D_CHUNK), 1)
    for gi in range(D_GROUPS):
        w = jnp.where(ri >= ci, spw_ref[gi], 0.0).astype(BF16)
        s = _dot(w, vd[:, gi * D_GROUP:(gi + 1) * D_GROUP]) + spb_ref[:, gi:gi + 1]
        y_ref[b, :, W_CV + gi * D_GROUP:W_CV + (gi + 1) * D_GROUP] = (u[:, gi * D_GROUP:(gi + 1) * D_GROUP] * s).astype(BF16)


def _odd_prompt_kernel(x_ref, w_ref, cos_ref, sin_ref, dec_ref, cs_ref, kd_ref, sdec_ref, gn_ref,
                       spw_ref, spb_ref, lng_ref, lnb_ref, y_ref, s_out_ref, p0, p1, s_c):
    c = pl.program_id(0)

    @pl.when(c == 0)
    def _():
        s_c[...] = jnp.zeros(s_c.shape, F32)

    bufs = (p0, p1)

    def project(b):
        bufs[b % 2][...] = _dot(x_ref[b].astype(BF16), w_ref[...])

    project(0)
    for b in range(BATCH):
        if b + 1 < BATCH:
            project(b + 1)
        _odd_chunk(bufs[b % 2], b, cos_ref, sin_ref, dec_ref, cs_ref, kd_ref, sdec_ref, gn_ref,
                   spw_ref, spb_ref, lng_ref, lnb_ref, y_ref, s_c)

    @pl.when(c == pl.num_programs(0) - 1)
    def _():
        s_out_ref[...] = s_c[...]


def _odd_prompt(x3d, w_stack, o, cos, sin, tabs, gn, spw, spb_t, lng, lnb):
    nchunk = SEQ // RET_CHUNK
    dec, cs, kd, sdec = tabs
    return pl.pallas_call(
        _odd_prompt_kernel,
        grid=(nchunk,),
        in_specs=[
            pl.BlockSpec((BATCH, RET_CHUNK, D_MODEL), lambda c: (0, c, 0)),
            _const_spec((D_MODEL, ODD_IN), o),
            pl.BlockSpec((RET_CHUNK, W_CK), lambda c: (c, 0)),
            pl.BlockSpec((RET_CHUNK, W_CK), lambda c: (c, 0)),
            _const_spec((C_HEADS, RET_CHUNK, RET_CHUNK)),
            _const_spec((RET_CHUNK, W_CV)),
            _const_spec((RET_CHUNK, W_CK)),
            _const_spec((C_HEADS, C_DK, C_DV)),
            _const_spec((1, W_CV), o),
            _const_spec((D_GROUPS, D_CHUNK, D_CHUNK), o),
            _const_spec((D_CHUNK, D_GROUPS), o),
            _const_spec((1, W_D), o),
            _const_spec((1, W_D), o),
        ],
        out_specs=[
            pl.BlockSpec((BATCH, RET_CHUNK, D_MODEL), lambda c: (0, c, 0)),
            pl.BlockSpec((BATCH, C_HEADS, C_DK, C_DV), lambda c: (0, 0, 0, 0)),
        ],
        out_shape=[
            jax.ShapeDtypeStruct((BATCH, SEQ, D_MODEL), BF16),
            jax.ShapeDtypeStruct((BATCH, C_HEADS, C_DK, C_DV), F32),
        ],
        scratch_shapes=[
            pltpu.VMEM((RET_CHUNK, ODD_IN), F32),
            pltpu.VMEM((RET_CHUNK, ODD_IN), F32),
            pltpu.VMEM((BATCH, C_HEADS, C_DK, C_DV), F32),
        ],
        compiler_params=_params(("arbitrary",), 48),
        name="odd_prompt",
    )(x3d, w_stack, cos, sin, dec, cs, kd, sdec, gn, spw, spb_t, lng, lnb)


def _odd_sample_kernel(proj_ref, s0_ref, cos_ref, sin_ref, dec_ref, cs_ref, kd_ref, sdec_ref, gn_ref,
                       spw_ref, spb_ref, lng_ref, lnb_ref, *rest):
    y_ref, s_out_ref, vd_ref = rest[-3:]
    if len(rest) > 3:
        s_out_ref[:, 0] = rest[0][...]
        s_out_ref = s_out_ref.at[:, 1]
    q = _rope(proj_ref[:, :, 0:W_CK], cos_ref[...], sin_ref[...])
    k = _rope(proj_ref[:, :, W_CK:2 * W_CK], cos_ref[...], sin_ref[...]) * (C_DK ** -0.5)
    kd = k * kd_ref[...]
    v0 = 2 * W_CK
    g0 = v0 + W_CV
    for h in range(C_HEADS):
        qh = q[:, :, h * C_DK:(h + 1) * C_DK].astype(BF16)
        kh = k[:, :, h * C_DK:(h + 1) * C_DK].astype(BF16)
        kdh = kd[:, :, h * C_DK:(h + 1) * C_DK].astype(BF16)
        vh = proj_ref[:, :, v0 + h * C_DV:v0 + (h + 1) * C_DV].astype(BF16)
        s_prev = s0_ref[:, h]
        scores = jnp.einsum('nld,nmd->nlm', qh, kh, preferred_element_type=F32) * dec_ref[h]
        o = (jnp.einsum('nlm,nme->nle', scores.astype(BF16), vh, preferred_element_type=F32)
             + jnp.einsum('nld,nde->nle', qh, s_prev.astype(BF16), preferred_element_type=F32)
             * cs_ref[:, h * C_DV:(h + 1) * C_DV])
        s_out_ref[:, h] = sdec_ref[h] * s_prev + jnp.einsum('nld,nle->nde', kdh, vh, preferred_element_type=F32)
        gate = jax.nn.silu(proj_ref[:, :, g0 + h * C_DV:g0 + (h + 1) * C_DV])
        y_ref[:, :, h * C_DV:(h + 1) * C_DV] = (gate * (_group_norm(o) * gn_ref[:, h * C_DV:(h + 1) * C_DV])).astype(BF16)

    u0 = g0 + W_CV
    u = jax.nn.gelu(proj_ref[:, :, u0:u0 + W_D])
    vd = _layer_norm(jax.nn.gelu(proj_ref[:, :, u0 + W_D:]), lng_ref[...], lnb_ref[...])
    vd_ref[...] = vd
    s = spb_ref[...] + spw_ref[0] * vd[:, 0:1, :]
    for m in range(1, DEC_SEQ):
        s = s + spw_ref[m] * vd[:, m:m + 1, :]
    y_ref[:, :, W_CV:] = (u * s).astype(BF16)


def _odd_sample(proj, state_ret, o, cos, sin, tabs, gn, spw_rows, spb_rows, lng, lnb, prev_states):
    nb = SAMPLE_NB
    dec, cs, kd, sdec = tabs
    seq3 = lambda i: (i, 0, 0)
    one_state = pl.BlockSpec((nb, C_HEADS, C_DK, C_DV), lambda i: (i, 0, 0, 0))
    in_specs = [
        pl.BlockSpec((nb, DEC_SEQ, ODD_IN), seq3),
        pl.BlockSpec((nb, None, C_HEADS, C_DK, C_DV), lambda i: (i, o, 0, 0, 0)),
        _const_spec((DEC_SEQ, W_CK)),
        _const_spec((DEC_SEQ, W_CK)),
        _const_spec((C_HEADS, DEC_SEQ, DEC_SEQ)),
        _const_spec((DEC_SEQ, W_CV)),
        _const_spec((DEC_SEQ, W_CK)),
        _const_spec((C_HEADS, C_DK, C_DV)),
        _const_spec((1, W_CV), o),
        _const_spec((DEC_SEQ, DEC_SEQ, W_D), o),
        _const_spec((DEC_SEQ, W_D), o),
        _const_spec((1, W_D), o),
        _const_spec((1, W_D), o),
    ]
    args = [proj, state_ret, cos, sin, dec, cs, kd, sdec, gn, spw_rows, spb_rows, lng, lnb]
    if prev_states is None:
        state_spec = one_state
        state_shape = (DEC_BATCH, C_HEADS, C_DK, C_DV)
    else:
        in_specs.append(one_state)
        args.append(prev_states)
        state_spec = pl.BlockSpec((nb, N_ODD, C_HEADS, C_DK, C_DV), lambda i: (i, 0, 0, 0, 0))
        state_shape = (DEC_BATCH, N_ODD, C_HEADS, C_DK, C_DV)
    return pl.pallas_call(
        _odd_sample_kernel,
        grid=(DEC_BATCH // nb,),
        in_specs=in_specs,
        out_specs=[pl.BlockSpec((nb, DEC_SEQ, D_MODEL), seq3), state_spec, pl.BlockSpec((nb, DEC_SEQ, W_D), seq3)],
        out_shape=[
            jax.ShapeDtypeStruct((DEC_BATCH, DEC_SEQ, D_MODEL), BF16),
            jax.ShapeDtypeStruct(state_shape, F32),
            jax.ShapeDtypeStruct((DEC_BATCH, DEC_SEQ, W_D), F32),
        ],
        compiler_params=_params(("parallel",), 48),
        name="odd_sample",
    )(*args)


def _rope_tables(pos):
    half = C_DK // 2
    freq = ROPE_BASE ** (-jnp.arange(half, dtype=F32) / half)
    ang = pos.astype(F32)[:, None] * freq
    cos, sin = jnp.cos(ang), jnp.sin(ang)
    cos_l = jnp.tile(jnp.concatenate([cos, cos], axis=-1), (1, C_HEADS))
    sin_l = jnp.tile(jnp.concatenate([-sin, sin], axis=-1), (1, C_HEADS))
    return cos_l, sin_l


def _retention_tables(length):
    log_g = jnp.log1p(-jnp.exp2(-5.0 - jnp.arange(C_HEADS, dtype=F32)))
    idx = jnp.arange(length, dtype=F32)
    rel = idx[:, None] - idx[None, :]
    decay = jnp.where(rel >= 0, jnp.exp(log_g[:, None, None] * jnp.maximum(rel, 0.0)), 0.0)
    cross = jnp.exp(log_g[None, :] * (idx[:, None] + 1.0))
    kdec = jnp.exp(log_g[None, :] * (length - 1.0 - idx[:, None]))
    sdec = jnp.exp(log_g * length)
    return (decay,
            jnp.repeat(cross, C_DV, axis=1),
            jnp.repeat(kdec, C_DK, axis=1),
            jnp.broadcast_to(sdec[:, None, None], (C_HEADS, C_DK, C_DV)))


def _block_diag_halves(wa, wx):
    nl = wa.shape[0]
    nb = A_BLOCKS // 2
    eye = jnp.eye(nb, dtype=wa.dtype)

    def bd(w):
        return (eye[:, None, :, None] * w[:, :, :, :, None, :]).reshape(nl, 2, nb * A_BLOCK, nb * A_BLOCK)

    split = lambda w: w.reshape(nl, 2, nb, A_BLOCK, A_BLOCK)
    return jnp.concatenate([bd(split(wa)), bd(split(wx))], axis=-1).astype(BF16)


def kernel(x_prompt, x_sample, state_lru_h, state_lru_conv, state_sconv, state_ret, w_in_even, conv_a_w, conv_a_b, lru_wa, lru_ba, lru_wx, lru_bx, lru_lam, conv_b_w, w_out_even, w_in_odd, ret_gn_g, sp_w, sp_b, gm_ln_g, gm_ln_b, w_out_odd, ffn_w1, ffn_w3, ffn_w2, ln1_g, ln1_b, ln2_g, ln2_b):
    rows = lambda v: v.reshape(v.shape[0], 1, v.shape[1])
    xp = x_prompt.reshape(BATCH * SEQ, D_MODEL)
    xs = jnp.swapaxes(x_sample, 0, 1).reshape(DEC_SEQ * DEC_BATCH, D_MODEL)

    cos_p, sin_p = _rope_tables(jnp.arange(SEQ, dtype=jnp.int32))
    cos_s, sin_s = _rope_tables(PAST_LEN + jnp.arange(DEC_SEQ, dtype=jnp.int32))
    tab_p = _retention_tables(RET_CHUNK)
    tab_s = _retention_tables(DEC_SEQ)

    w_in_even_b, w_out_even_b = w_in_even.astype(BF16), w_out_even.astype(BF16)
    w_in_odd_b, w_out_odd_b = w_in_odd.astype(BF16), w_out_odd.astype(BF16)
    dense_p = (rows(ln1_g), rows(ln1_b), ffn_w1.astype(BF16), ffn_w3.astype(BF16), ffn_w2.astype(BF16),
               rows(ln2_g), rows(ln2_b))
    even_p = (conv_a_w, rows(conv_a_b), _block_diag_halves(lru_wa, lru_wx), rows(lru_ba), rows(lru_bx),
              rows(lru_lam), conv_b_w)
    ca_tm = jnp.transpose(state_lru_conv, (1, 2, 0, 3))
    sc_tm = jnp.transpose(state_sconv, (1, 2, 0, 3))
    h0_tm = jnp.swapaxes(state_lru_h, 0, 1)
    tril = jnp.tril(sp_w[:, :, :DEC_SEQ, :DEC_SEQ])
    spw_rows = jnp.repeat(jnp.transpose(tril, (0, 3, 2, 1)), D_GROUP, axis=3)
    spb_rows = jnp.repeat(jnp.swapaxes(sp_b[:, :, :DEC_SEQ], 1, 2), D_GROUP, axis=2)
    spb_t = jnp.swapaxes(sp_b, 1, 2)
    odd_small = (rows(ret_gn_g),)
    odd_ln = (rows(gm_ln_g), rows(gm_ln_b))

    h_p, h_s, ca_p, ca_s, sc_p, sc_s, r_p, v_s = [], [], [], [], [], [], [], []
    ret_sample = None
    for l in range(DEPTH):
        if l % 2 == 0:
            e = l // 2
            wo_stack, wo_layer = w_out_even_b, e
            yp, hlp, cap, scp = _even_prompt(xp.reshape(BATCH, SEQ, D_MODEL), w_in_even_b, e, even_p)
            proj_s = _in_proj(xs, w_in_even_b, e).reshape(DEC_SEQ, DEC_BATCH, EVEN_IN)
            ys, hls, cas, scs = _even_sample(proj_s, ca_tm, sc_tm, h0_tm, e, even_p)
            h_p.append(hlp)
            h_s.append(hls)
            ca_p.append(cap[:, SUBLANES - (A_CONV - 1):])
            ca_s.append(jnp.swapaxes(cas, 0, 1))
            sc_p.append(scp[:, SUBLANES - (B_CONV - 1):])
            sc_s.append(jnp.swapaxes(scs, 0, 1))
            yp = yp.reshape(BATCH * SEQ, D_MODEL)
            ys = ys.reshape(DEC_SEQ * DEC_BATCH, D_MODEL)
        else:
            o = l // 2
            wo_stack, wo_layer = w_out_odd_b, o
            yp, rp = _odd_prompt(xp.reshape(BATCH, SEQ, D_MODEL), w_in_odd_b, o, cos_p, sin_p, tab_p,
                                 *odd_small, sp_w, spb_t, *odd_ln)
            proj_s = _in_proj(xs, w_in_odd_b, o).reshape(DEC_BATCH, DEC_SEQ, ODD_IN)
            ys, ret_sample, vs = _odd_sample(proj_s, state_ret, o, cos_s, sin_s, tab_s, *odd_small,
                                             spw_rows, spb_rows, *odd_ln, ret_sample)
            r_p.append(rp)
            v_s.append(vs)
            yp = yp.reshape(BATCH * SEQ, D_MODEL)
            ys = ys.reshape(DEC_BATCH * DEC_SEQ, D_MODEL)
        xp, xs = _dense_block(yp, xp, ys, xs, wo_stack, wo_layer, l, *dense_p)
        if l + 1 < DEPTH:
            if l % 2 == 0:
                xs = jnp.swapaxes(xs.reshape(DEC_SEQ, DEC_BATCH, D_MODEL), 0, 1).reshape(DEC_BATCH * DEC_SEQ, D_MODEL)
            else:
                xs = jnp.swapaxes(xs.reshape(DEC_BATCH, DEC_SEQ, D_MODEL), 0, 1).reshape(DEC_SEQ * DEC_BATCH, D_MODEL)

    return (xp.reshape(BATCH, SEQ, D_MODEL), xs.reshape(DEC_BATCH, DEC_SEQ, D_MODEL),
            jnp.stack(h_p, axis=1), jnp.stack(h_s, axis=1),
            jnp.stack(ca_p, axis=1), jnp.stack(ca_s, axis=1),
            jnp.stack(sc_p, axis=1), jnp.stack(sc_s, axis=1),
            jnp.stack(r_p, axis=1), ret_sample, jnp.stack(v_s, axis=1))
```

```python
import jax
import jax.numpy as jnp
import numpy as np
from jax import lax
from jax.experimental import pallas as pl
from jax.experimental.pallas import tpu as pltpu

F32 = jnp.float32
BF16 = jnp.bfloat16

D_MODEL = 1024
BATCH = 8
SEQ = 2048
DEPTH = 4
DEC_BATCH = 128
DEC_SEQ = 4
PAST_LEN = 16384
N_ODD = DEPTH // 2
W_A = 512
A_BLOCKS = 8
A_BLOCK = 64
A_CONV = 4
LRU_C = 8.0
W_B = 512
B_CONV = 3
C_HEADS = 4
C_DK = 64
C_DV = 128
W_CK = C_HEADS * C_DK
W_CV = C_HEADS * C_DV
RET_CHUNK = 128
ROPE_BASE = 10000.0
D_GROUPS = 4
D_CHUNK = 128
W_D = 512
D_GROUP = W_D // D_GROUPS
EVEN_IN = 2 * W_A + 3 * W_B
ODD_IN = 2 * W_CK + 2 * W_CV + 2 * W_D
D_FF = 2816
ALPHA = (2 * DEPTH) ** 0.25
LN_EPS = 1e-5

SUBLANES = 8
LANES = 128
MIB = 1024 * 1024

ROW_TILE = 512
DENSE_TILE = 1024
DENSE_GROUP = 256
EVEN_TT = 128
EVEN_PITCH = EVEN_TT + SUBLANES
SAMPLE_NB = 32


def _params(sem, vmem_mib):
    return pltpu.CompilerParams(dimension_semantics=sem, vmem_limit_bytes=vmem_mib * MIB)


def _const_spec(shape, layer=None):
    nd = len(shape)
    if layer is None:
        return pl.BlockSpec(shape, lambda *_: (0,) * nd, pipeline_mode=pl.Buffered(1))
    return pl.BlockSpec((None,) + tuple(shape), lambda *_: (layer,) + (0,) * nd, pipeline_mode=pl.Buffered(1))


def _layer_norm(x, g, b):
    mu = jnp.mean(x, -1, keepdims=True)
    xc = x - mu
    var = jnp.mean(xc * xc, -1, keepdims=True)
    return xc * lax.rsqrt(var + LN_EPS) * g + b


def _softplus(x):
    return jnp.maximum(x, 0.0) + jnp.log1p(jnp.exp(-jnp.abs(x)))


def _dot(a, b):
    return jnp.dot(a, b, preferred_element_type=F32)


def _proj_kernel(x_ref, w_ref, o_ref):
    o_ref[...] = _dot(x_ref[...].astype(BF16), w_ref[...])


def _in_proj(x2d, w_stack, layer):
    m, k = x2d.shape
    n = w_stack.shape[2]
    tm = min(ROW_TILE, m)
    return pl.pallas_call(
        _proj_kernel,
        grid=(m // tm,),
        in_specs=[pl.BlockSpec((tm, k), lambda i: (i, 0)), _const_spec((k, n), layer)],
        out_specs=pl.BlockSpec((tm, n), lambda i: (i, 0)),
        out_shape=jax.ShapeDtypeStruct((m, n), F32),
        compiler_params=_params(("parallel",), 40),
        name="in_proj",
    )(x2d, w_stack)


def _dense_kernel(yp_ref, xp_ref, ys_ref, xs_ref, wo_ref, g1_ref, b1_ref, w1_ref, w3_ref, w2_ref, g2_ref, b2_ref,
                  op_ref, os_ref):
    n_prompt = pl.num_programs(0) - 1
    weights = (wo_ref, g1_ref, b1_ref, w1_ref, w3_ref, w2_ref, g2_ref, b2_ref)

    @pl.when(pl.program_id(0) < n_prompt)
    def _():
        _dense_rows(yp_ref, xp_ref, *weights, op_ref)

    @pl.when(pl.program_id(0) == n_prompt)
    def _():
        _dense_rows(ys_ref, xs_ref, *weights, os_ref)


def _dense_rows(y_ref, x_ref, wo_ref, g1_ref, b1_ref, w1_ref, w3_ref, w2_ref, g2_ref, b2_ref, o_ref):
    rows = DENSE_GROUP
    nparts = y_ref.shape[0] // rows
    sls = [slice(p * rows, (p + 1) * rows) for p in range(nparts)]
    mix, x1, xb, h = {}, {}, {}, {}

    def out_proj(p):
        mix[p] = _dot(y_ref[sls[p], :], wo_ref[...])

    def norm1(p):
        x1[p] = _layer_norm(ALPHA * x_ref[sls[p], :] + mix[p], g1_ref[...], b1_ref[...])
        xb[p] = x1[p].astype(BF16)

    def gate_up(p):
        h[p] = (jax.nn.silu(_dot(xb[p], w1_ref[...])) * _dot(xb[p], w3_ref[...])).astype(BF16)

    def down_norm2(p):
        f = _dot(h[p], w2_ref[...])
        o_ref[sls[p], :] = _layer_norm(ALPHA * x1[p] + f, g2_ref[...], b2_ref[...])

    out_proj(0)
    for p in range(nparts):
        norm1(p)
        if p + 1 < nparts:
            out_proj(p + 1)
    for p in range(nparts):
        gate_up(p)
    for p in range(nparts):
        down_norm2(p)


def _dense_block(yp, xp, ys, xs, wo_stack, wo_layer, layer, g1, b1, w1, w3, w2, g2, b2):
    tm, ts = DENSE_TILE, xs.shape[0]
    n_prompt = xp.shape[0] // tm
    assert xp.shape[0] % tm == 0 and tm % DENSE_GROUP == 0 and ts % DENSE_GROUP == 0
    prompt_row = lambda i: (jnp.minimum(i, n_prompt - 1), 0)
    sample_row = lambda i: (0, 0)
    sample_spec = pl.BlockSpec((ts, D_MODEL), sample_row, pipeline_mode=pl.Buffered(1))
    return pl.pallas_call(
        _dense_kernel,
        grid=(n_prompt + 1,),
        in_specs=[
            pl.BlockSpec((tm, D_MODEL), prompt_row),
            pl.BlockSpec((tm, D_MODEL), prompt_row),
            sample_spec,
            sample_spec,
            _const_spec((D_MODEL, D_MODEL), wo_layer),
            _const_spec((1, D_MODEL), layer),
            _const_spec((1, D_MODEL), layer),
            _const_spec((D_MODEL, D_FF), layer),
            _const_spec((D_MODEL, D_FF), layer),
            _const_spec((D_FF, D_MODEL), layer),
            _const_spec((1, D_MODEL), layer),
            _const_spec((1, D_MODEL), layer),
        ],
        out_specs=[pl.BlockSpec((tm, D_MODEL), prompt_row), pl.BlockSpec((ts, D_MODEL), sample_row)],
        out_shape=[jax.ShapeDtypeStruct(xp.shape, F32), jax.ShapeDtypeStruct(xs.shape, F32)],
        compiler_params=_params(("arbitrary",), 58),
        name="dense_block",
    )(yp, xp, ys, xs, wo_stack, g1, b1, w1, w3, w2, g2, b2)


def _lru_gates(xc, wblk_ref, ba, bx, sp):
    half = W_A // 2
    xb = xc.astype(BF16)
    pre0 = _dot(xb[:, :half], wblk_ref[0])
    pre1 = _dot(xb[:, half:], wblk_ref[1])
    r = jax.nn.sigmoid(jnp.concatenate([pre0[:, :half], pre1[:, :half]], axis=1) + ba)
    i = jax.nn.sigmoid(jnp.concatenate([pre0[:, half:], pre1[:, half:]], axis=1) + bx)
    log_a = (-LRU_C) * r * sp
    a = jnp.exp(log_a)
    mult = jnp.sqrt((1.0 + a * a) * jnp.tanh(-log_a))
    return a, mult, i


def _causal_taps(x_slabs, ext, b, w_ref, nrows):
    ktaps = w_ref.shape[0]
    out = []
    for s, x in enumerate(x_slabs):
        lanes = slice(s * LANES, (s + 1) * LANES)
        ext[s, b, SUBLANES:SUBLANES + nrows, :] = x
        acc = w_ref[ktaps - 1:ktaps, lanes] * x
        for k in range(ktaps - 1):
            off = SUBLANES - (ktaps - 1) + k
            acc = acc + w_ref[k:k + 1, lanes] * ext[s, b, off:off + nrows, :]
        out.append(acc)
    return jnp.concatenate(out, axis=1)


def _even_prompt_kernel(x_ref, w_ref, caw_ref, cab_ref, wblk_ref, ba_ref, bx_ref, lam_ref, cbw_ref,
                        y_ref, hlast_ref, ca_out_ref, sc_out_ref,
                        p0, p1, xa_ext, cb_ext, a_s, b_s, g_s, h_c):
    tt, pitch = EVEN_TT, EVEN_PITCH
    nslab = W_A // LANES
    j = pl.program_id(0)
    bufs = (p0, p1)

    def project(b):
        bufs[b % 2][...] = _dot(x_ref[b].astype(BF16), w_ref[...])

    @pl.when(j == 0)
    def _():
        xa_ext[:, :, 0:SUBLANES, :] = jnp.zeros((nslab, BATCH, SUBLANES, LANES), F32)
        cb_ext[:, :, 0:SUBLANES, :] = jnp.zeros((nslab, BATCH, SUBLANES, LANES), F32)
        h_c[...] = jnp.zeros(h_c.shape, F32)

    @pl.when(j > 0)
    def _():
        xa_ext[:, :, 0:SUBLANES, :] = xa_ext[:, :, tt:tt + SUBLANES, :]
        cb_ext[:, :, 0:SUBLANES, :] = cb_ext[:, :, tt:tt + SUBLANES, :]

    sp = _softplus(-lam_ref[...])
    ba = ba_ref[...]
    bx = bx_ref[...]
    cab = cab_ref[...]
    first_row = (lax.broadcasted_iota(jnp.int32, (SUBLANES, 1), 0) == 0) & (j == 0)
    cg0 = 2 * W_A + W_B
    xb0 = 2 * W_A + 2 * W_B

    def stage(b):
        p_ref = bufs[b % 2]
        xc = cab + _causal_taps([p_ref[:, s * LANES:(s + 1) * LANES] for s in range(nslab)], xa_ext, b, caw_ref, tt)
        a, mult, gate = _lru_gates(xc, wblk_ref, ba, bx, sp)
        mult = jnp.concatenate([jnp.where(first_row, 1.0, mult[:SUBLANES]), mult[SUBLANES:]], axis=0)
        bb = mult * (gate * xc)
        for s in range(nslab):
            a_s[s, b * pitch:b * pitch + tt, :] = a[:, s * LANES:(s + 1) * LANES]
            b_s[s, b * pitch:b * pitch + tt, :] = bb[:, s * LANES:(s + 1) * LANES]
        g_s[b] = jax.nn.gelu(p_ref[:, W_A:2 * W_A])
        cb = [p_ref[:, cg0 + s * LANES:cg0 + (s + 1) * LANES] * p_ref[:, xb0 + s * LANES:xb0 + (s + 1) * LANES]
              for s in range(nslab)]
        zb = _causal_taps(cb, cb_ext, b, cbw_ref, tt)
        y_ref[b, :, W_A:] = (p_ref[:, 2 * W_A:2 * W_A + W_B] * zb).astype(BF16)

    project(0)
    for b in range(BATCH):
        if b + 1 < BATCH:
            project(b + 1)
        stage(b)

    def step(t, hs):
        out = []
        for s in range(nslab):
            rows = pl.ds(t, BATCH, stride=pitch)
            hn = a_s[s, rows, :] * hs[s] + b_s[s, rows, :]
            b_s[s, rows, :] = hn
            out.append(hn)
        return tuple(out)

    hs = lax.fori_loop(0, tt, step, tuple(h_c[s] for s in range(nslab)), unroll=4)
    for s in range(nslab):
        h_c[s] = hs[s]

    def emit(b, carry):
        row0 = pl.multiple_of(b * pitch, SUBLANES)
        h = jnp.concatenate([b_s[s, pl.ds(row0, tt), :] for s in range(nslab)], axis=1)
        y_ref[b, :, 0:W_A] = (h * g_s[b]).astype(BF16)
        return carry

    lax.fori_loop(0, BATCH, emit, 0)

    @pl.when(j == pl.num_programs(0) - 1)
    def _():
        hlast_ref[...] = jnp.concatenate(list(hs), axis=1)
        for s in range(nslab):
            ca_out_ref[:, :, s * LANES:(s + 1) * LANES] = xa_ext[s, :, tt:tt + SUBLANES, :]
            sc_out_ref[:, :, s * LANES:(s + 1) * LANES] = cb_ext[s, :, tt:tt + SUBLANES, :]


def _even_param_specs(e):
    return [
        _const_spec((A_CONV, W_A), e),
        _const_spec((1, W_A), e),
        _const_spec((2, W_A // 2, W_A), e),
        _const_spec((1, W_A), e),
        _const_spec((1, W_A), e),
        _const_spec((1, W_A), e),
        _const_spec((B_CONV, W_B), e),
    ]


def _even_prompt(x3d, w_stack, e, params):
    tt, pitch = EVEN_TT, EVEN_PITCH
    nslab = W_A // LANES
    assert W_A == W_B
    return pl.pallas_call(
        _even_prompt_kernel,
        grid=(SEQ // tt,),
        in_specs=[pl.BlockSpec((BATCH, tt, D_MODEL), lambda j: (0, j, 0)),
                  _const_spec((D_MODEL, EVEN_IN), e)] + _even_param_specs(e),
        out_specs=[
            pl.BlockSpec((BATCH, tt, D_MODEL), lambda j: (0, j, 0)),
            pl.BlockSpec((BATCH, W_A), lambda j: (0, 0)),
            pl.BlockSpec((BATCH, SUBLANES, W_A), lambda j: (0, 0, 0)),
            pl.BlockSpec((BATCH, SUBLANES, W_B), lambda j: (0, 0, 0)),
        ],
        out_shape=[
            jax.ShapeDtypeStruct((BATCH, SEQ, D_MODEL), BF16),
            jax.ShapeDtypeStruct((BATCH, W_A), F32),
            jax.ShapeDtypeStruct((BATCH, SUBLANES, W_A), F32),
            jax.ShapeDtypeStruct((BATCH, SUBLANES, W_B), F32),
        ],
        scratch_shapes=[
            pltpu.VMEM((tt, EVEN_IN), F32),
            pltpu.VMEM((tt, EVEN_IN), F32),
            pltpu.VMEM((nslab, BATCH, tt + SUBLANES, LANES), F32),
            pltpu.VMEM((nslab, BATCH, tt + SUBLANES, LANES), F32),
            pltpu.VMEM((nslab, BATCH * pitch, LANES), F32),
            pltpu.VMEM((nslab, BATCH * pitch, LANES), F32),
            pltpu.VMEM((BATCH, tt, W_A), F32),
            pltpu.VMEM((nslab, BATCH, LANES), F32),
        ],
        compiler_params=_params(("arbitrary",), 48),
        name="even_prompt",
    )(x3d, w_stack, *params)


def _even_sample_kernel(proj_ref, ca_ref, sc_ref, h0_ref, caw_ref, cab_ref, wblk_ref, ba_ref, bx_ref,
                        lam_ref, cbw_ref, y_ref, hlast_ref, ca_out_ref, sc_out_ref):
    sp = _softplus(-lam_ref[...])
    rows_a = [ca_ref[k] for k in range(A_CONV - 1)] + [proj_ref[l, :, 0:W_A] for l in range(DEC_SEQ)]
    xc = []
    for l in range(DEC_SEQ):
        acc = cab_ref[...] + caw_ref[0:1, :] * rows_a[l]
        for k in range(1, A_CONV):
            acc = acc + caw_ref[k:k + 1, :] * rows_a[l + k]
        xc.append(acc)
    xc_all = jnp.concatenate(xc, axis=0)
    a, mult, gate = _lru_gates(xc_all, wblk_ref, ba_ref[...], bx_ref[...], sp)
    bb = mult * (gate * xc_all)
    h = h0_ref[...]
    rows_b = [sc_ref[k] for k in range(B_CONV - 1)]
    for l in range(DEC_SEQ):
        sl = slice(l * DEC_BATCH, (l + 1) * DEC_BATCH)
        h = a[sl] * h + bb[sl]
        y_ref[l, :, 0:W_A] = (h * jax.nn.gelu(proj_ref[l, :, W_A:2 * W_A])).astype(BF16)
        rows_b.append(proj_ref[l, :, 2 * W_A + W_B:2 * W_A + 2 * W_B] * proj_ref[l, :, 2 * W_A + 2 * W_B:])
    hlast_ref[...] = h
    for l in range(DEC_SEQ):
        zb = cbw_ref[0:1, :] * rows_b[l]
        for k in range(1, B_CONV):
            zb = zb + cbw_ref[k:k + 1, :] * rows_b[l + k]
        y_ref[l, :, W_A:] = (proj_ref[l, :, 2 * W_A:2 * W_A + W_B] * zb).astype(BF16)
    for k in range(A_CONV - 1):
        ca_out_ref[k] = rows_a[DEC_SEQ + k]
    for k in range(B_CONV - 1):
        sc_out_ref[k] = rows_b[DEC_SEQ + k]


def _even_sample(proj_tm, ca_tm, sc_tm, h0_tm, e, params):
    whole = lambda shape: pl.BlockSpec(shape, lambda i: (0,) * len(shape))
    return pl.pallas_call(
        _even_sample_kernel,
        grid=(1,),
        in_specs=[
            whole((DEC_SEQ, DEC_BATCH, EVEN_IN)),
            _const_spec((A_CONV - 1, DEC_BATCH, W_A), e),
            _const_spec((B_CONV - 1, DEC_BATCH, W_B), e),
            _const_spec((DEC_BATCH, W_A), e),
        ] + _even_param_specs(e),
        out_specs=[
            whole((DEC_SEQ, DEC_BATCH, D_MODEL)),
            whole((DEC_BATCH, W_A)),
            whole((A_CONV - 1, DEC_BATCH, W_A)),
            whole((B_CONV - 1, DEC_BATCH, W_B)),
        ],
        out_shape=[
            jax.ShapeDtypeStruct((DEC_SEQ, DEC_BATCH, D_MODEL), BF16),
            jax.ShapeDtypeStruct((DEC_BATCH, W_A), F32),
            jax.ShapeDtypeStruct((A_CONV - 1, DEC_BATCH, W_A), F32),
            jax.ShapeDtypeStruct((B_CONV - 1, DEC_BATCH, W_B), F32),
        ],
        compiler_params=_params(("arbitrary",), 40),
        name="even_sample",
    )(proj_tm, ca_tm, sc_tm, h0_tm, *params)


def _rope(x, cos, sin_signed):
    half = C_DK // 2
    ax = x.ndim - 1
    pieces = []
    for p in range(W_CK // LANES):
        xv = x[..., p * LANES:(p + 1) * LANES]
        lane = lax.broadcasted_iota(jnp.int32, xv.shape, ax)
        partner = jnp.where((lane % C_DK) < half, pltpu.roll(xv, LANES - half, axis=ax), pltpu.roll(xv, half, axis=ax))
        pieces.append(xv * cos[..., p * LANES:(p + 1) * LANES] + partner * sin_signed[..., p * LANES:(p + 1) * LANES])
    return jnp.concatenate(pieces, axis=ax)


def _group_norm(o):
    mu = jnp.mean(o, -1, keepdims=True)
    oc = o - mu
    var = jnp.mean(oc * oc, -1, keepdims=True)
    return oc * lax.rsqrt(var + LN_EPS)


def _odd_chunk(p_ref, b, cos_ref, sin_ref, dec_ref, cs_ref, kd_ref, sdec_ref, gn_ref,
               spw_ref, spb_ref, lng_ref, lnb_ref, y_ref, s_c):
    q = _rope(p_ref[:, 0:W_CK], cos_ref[...], sin_ref[...])
    k = _rope(p_ref[:, W_CK:2 * W_CK], cos_ref[...], sin_ref[...]) * (C_DK ** -0.5)
    kd = k * kd_ref[...]
    v0 = 2 * W_CK
    g0 = v0 + W_CV
    for h in range(C_HEADS):
        qh = q[:, h * C_DK:(h + 1) * C_DK].astype(BF16)
        kh = k[:, h * C_DK:(h + 1) * C_DK].astype(BF16)
        kdh = kd[:, h * C_DK:(h + 1) * C_DK].astype(BF16)
        vh = p_ref[:, v0 + h * C_DV:v0 + (h + 1) * C_DV].astype(BF16)
        s_prev = s_c[b, h]
        scores = lax.dot_general(qh, kh, (((1,), (1,)), ((), ())), preferred_element_type=F32) * dec_ref[h]
        o = _dot(scores.astype(BF16), vh) + _dot(qh, s_prev.astype(BF16)) * cs_ref[:, h * C_DV:(h + 1) * C_DV]
        s_c[b, h] = sdec_ref[h] * s_prev + lax.dot_general(kdh, vh, (((0,), (0,)), ((), ())), preferred_element_type=F32)
        gate = jax.nn.silu(p_ref[:, g0 + h * C_DV:g0 + (h + 1) * C_DV])
        y_ref[b, :, h * C_DV:(h + 1) * C_DV] = (gate * (_group_norm(o) * gn_ref[:, h * C_DV:(h + 1) * C_DV])).astype(BF16)

    u0 = g0 + W_CV
    u = jax.nn.gelu(p_ref[:, u0:u0 + W_D])
    vd = _layer_norm(jax.nn.gelu(p_ref[:, u0 + W_D:]), lng_ref[...], lnb_ref[...]).astype(BF16)
    ri = lax.broadcasted_iota(jnp.int32, (D_CHUNK, D_CHUNK), 0)
    ci = lax.broadcasted_iota(jnp.int32, (D_CHUNK, D_CHUNK), 1)
    for gi in range(D_GROUPS):
        w = jnp.where(ri >= ci, spw_ref[gi], 0.0).astype(BF16)
        s = _dot(w, vd[:, gi * D_GROUP:(gi + 1) * D_GROUP]) + spb_ref[:, gi:gi + 1]
        y_ref[b, :, W_CV + gi * D_GROUP:W_CV + (gi + 1) * D_GROUP] = (u[:, gi * D_GROUP:(gi + 1) * D_GROUP] * s).astype(BF16)


def _odd_prompt_kernel(x_ref, w_ref, cos_ref, sin_ref, dec_ref, cs_ref, kd_ref, sdec_ref, gn_ref,
                       spw_ref, spb_ref, lng_ref, lnb_ref, y_ref, s_out_ref, p0, p1, s_c):
    c = pl.program_id(0)

    @pl.when(c == 0)
    def _():
        s_c[...] = jnp.zeros(s_c.shape, F32)

    bufs = (p0, p1)

    def project(b):
        bufs[b % 2][...] = _dot(x_ref[b].astype(BF16), w_ref[...])

    project(0)
    for b in range(BATCH):
        if b + 1 < BATCH:
            project(b + 1)
        _odd_chunk(bufs[b % 2], b, cos_ref, sin_ref, dec_ref, cs_ref, kd_ref, sdec_ref, gn_ref,
                   spw_ref, spb_ref, lng_ref, lnb_ref, y_ref, s_c)

    @pl.when(c == pl.num_programs(0) - 1)
    def _():
        s_out_ref[...] = s_c[...]


def _odd_prompt(x3d, w_stack, o, cos, sin, tabs, gn, spw, spb_t, lng, lnb):
    nchunk = SEQ // RET_CHUNK
    dec, cs, kd, sdec = tabs
    return pl.pallas_call(
        _odd_prompt_kernel,
        grid=(nchunk,),
        in_specs=[
            pl.BlockSpec((BATCH, RET_CHUNK, D_MODEL), lambda c: (0, c, 0)),
            _const_spec((D_MODEL, ODD_IN), o),
            pl.BlockSpec((RET_CHUNK, W_CK), lambda c: (c, 0)),
            pl.BlockSpec((RET_CHUNK, W_CK), lambda c: (c, 0)),
            _const_spec((C_HEADS, RET_CHUNK, RET_CHUNK)),
            _const_spec((RET_CHUNK, W_CV)),
            _const_spec((RET_CHUNK, W_CK)),
            _const_spec((C_HEADS, C_DK, C_DV)),
            _const_spec((1, W_CV), o),
            _const_spec((D_GROUPS, D_CHUNK, D_CHUNK), o),
            _const_spec((D_CHUNK, D_GROUPS), o),
            _const_spec((1, W_D), o),
            _const_spec((1, W_D), o),
        ],
        out_specs=[
            pl.BlockSpec((BATCH, RET_CHUNK, D_MODEL), lambda c: (0, c, 0)),
            pl.BlockSpec((BATCH, C_HEADS, C_DK, C_DV), lambda c: (0, 0, 0, 0)),
        ],
        out_shape=[
            jax.ShapeDtypeStruct((BATCH, SEQ, D_MODEL), BF16),
            jax.ShapeDtypeStruct((BATCH, C_HEADS, C_DK, C_DV), F32),
        ],
        scratch_shapes=[
            pltpu.VMEM((RET_CHUNK, ODD_IN), F32),
            pltpu.VMEM((RET_CHUNK, ODD_IN), F32),
            pltpu.VMEM((BATCH, C_HEADS, C_DK, C_DV), F32),
        ],
        compiler_params=_params(("arbitrary",), 48),
        name="odd_prompt",
    )(x3d, w_stack, cos, sin, dec, cs, kd, sdec, gn, spw, spb_t, lng, lnb)


def _odd_sample_kernel(proj_ref, s0_ref, cos_ref, sin_ref, dec_ref, cs_ref, kd_ref, sdec_ref, gn_ref,
                       spw_ref, spb_ref, lng_ref, lnb_ref, *rest):
    y_ref, s_out_ref, vd_ref = rest[-3:]
    if len(rest) > 3:
        s_out_ref[:, 0] = rest[0][...]
        s_out_ref = s_out_ref.at[:, 1]
    q = _rope(proj_ref[:, :, 0:W_CK], cos_ref[...], sin_ref[...])
    k = _rope(proj_ref[:, :, W_CK:2 * W_CK], cos_ref[...], sin_ref[...]) * (C_DK ** -0.5)
    kd = k * kd_ref[...]
    v0 = 2 * W_CK
    g0 = v0 + W_CV
    for h in range(C_HEADS):
        qh = q[:, :, h * C_DK:(h + 1) * C_DK].astype(BF16)
        kh = k[:, :, h * C_DK:(h + 1) * C_DK].astype(BF16)
        kdh = kd[:, :, h * C_DK:(h + 1) * C_DK].astype(BF16)
        vh = proj_ref[:, :, v0 + h * C_DV:v0 + (h + 1) * C_DV].astype(BF16)
        s_prev = s0_ref[:, h]
        scores = jnp.einsum('nld,nmd->nlm', qh, kh, preferred_element_type=F32) * dec_ref[h]
        o = (jnp.einsum('nlm,nme->nle', scores.astype(BF16), vh, preferred_element_type=F32)
             + jnp.einsum('nld,nde->nle', qh, s_prev.astype(BF16), preferred_element_type=F32)
             * cs_ref[:, h * C_DV:(h + 1) * C_DV])
        s_out_ref[:, h] = sdec_ref[h] * s_prev + jnp.einsum('nld,nle->nde', kdh, vh, preferred_element_type=F32)
        gate = jax.nn.silu(proj_ref[:, :, g0 + h * C_DV:g0 + (h + 1) * C_DV])
        y_ref[:, :, h * C_DV:(h + 1) * C_DV] = (gate * (_group_norm(o) * gn_ref[:, h * C_DV:(h + 1) * C_DV])).astype(BF16)

    u0 = g0 + W_CV
    u = jax.nn.gelu(proj_ref[:, :, u0:u0 + W_D])
    vd = _layer_norm(jax.nn.gelu(proj_ref[:, :, u0 + W_D:]), lng_ref[...], lnb_ref[...])
    vd_ref[...] = vd
    s = spb_ref[...] + spw_ref[0] * vd[:, 0:1, :]
    for m in range(1, DEC_SEQ):
        s = s + spw_ref[m] * vd[:, m:m + 1, :]
    y_ref[:, :, W_CV:] = (u * s).astype(BF16)


def _odd_sample(proj, state_ret, o, cos, sin, tabs, gn, spw_rows, spb_rows, lng, lnb, prev_states):
    nb = SAMPLE_NB
    dec, cs, kd, sdec = tabs
    seq3 = lambda i: (i, 0, 0)
    one_state = pl.BlockSpec((nb, C_HEADS, C_DK, C_DV), lambda i: (i, 0, 0, 0))
    in_specs = [
        pl.BlockSpec((nb, DEC_SEQ, ODD_IN), seq3),
        pl.BlockSpec((nb, None, C_HEADS, C_DK, C_DV), lambda i: (i, o, 0, 0, 0)),
        _const_spec((DEC_SEQ, W_CK)),
        _const_spec((DEC_SEQ, W_CK)),
        _const_spec((C_HEADS, DEC_SEQ, DEC_SEQ)),
        _const_spec((DEC_SEQ, W_CV)),
        _const_spec((DEC_SEQ, W_CK)),
        _const_spec((C_HEADS, C_DK, C_DV)),
        _const_spec((1, W_CV), o),
        _const_spec((DEC_SEQ, DEC_SEQ, W_D), o),
        _const_spec((DEC_SEQ, W_D), o),
        _const_spec((1, W_D), o),
        _const_spec((1, W_D), o),
    ]
    args = [proj, state_ret, cos, sin, dec, cs, kd, sdec, gn, spw_rows, spb_rows, lng, lnb]
    if prev_states is None:
        state_spec = one_state
        state_shape = (DEC_BATCH, C_HEADS, C_DK, C_DV)
    else:
        in_specs.append(one_state)
        args.append(prev_states)
        state_spec = pl.BlockSpec((nb, N_ODD, C_HEADS, C_DK, C_DV), lambda i: (i, 0, 0, 0, 0))
        state_shape = (DEC_BATCH, N_ODD, C_HEADS, C_DK, C_DV)
    return pl.pallas_call(
        _odd_sample_kernel,
        grid=(DEC_BATCH // nb,),
        in_specs=in_specs,
        out_specs=[pl.BlockSpec((nb, DEC_SEQ, D_MODEL), seq3), state_spec, pl.BlockSpec((nb, DEC_SEQ, W_D), seq3)],
        out_shape=[
            jax.ShapeDtypeStruct((DEC_BATCH, DEC_SEQ, D_MODEL), BF16),
            jax.ShapeDtypeStruct(state_shape, F32),
            jax.ShapeDtypeStruct((DEC_BATCH, DEC_SEQ, W_D), F32),
        ],
        compiler_params=_params(("parallel",), 48),
        name="odd_sample",
    )(*args)


def _rope_tables(pos):
    half = C_DK // 2
    freq = ROPE_BASE ** (-np.arange(half, dtype=np.float64) / half)
    ang = np.asarray(pos, np.float64)[:, None] * freq
    cos, sin = np.cos(ang), np.sin(ang)
    cos_l = np.tile(np.concatenate([cos, cos], axis=-1), (1, C_HEADS))
    sin_l = np.tile(np.concatenate([-sin, sin], axis=-1), (1, C_HEADS))
    return cos_l.astype(np.float32), sin_l.astype(np.float32)


def _retention_tables(length):
    log_g = np.log1p(-np.exp2(-5.0 - np.arange(C_HEADS, dtype=np.float64)))
    idx = np.arange(length, dtype=np.float64)
    rel = idx[:, None] - idx[None, :]
    decay = np.where(rel >= 0, np.exp(log_g[:, None, None] * np.maximum(rel, 0.0)), 0.0)
    cross = np.exp(log_g[None, :] * (idx[:, None] + 1.0))
    kdec = np.exp(log_g[None, :] * (length - 1.0 - idx[:, None]))
    sdec = np.exp(log_g * length)
    tabs = (decay,
            np.repeat(cross, C_DV, axis=1),
            np.repeat(kdec, C_DK, axis=1),
            np.broadcast_to(sdec[:, None, None], (C_HEADS, C_DK, C_DV)))
    return tuple(np.ascontiguousarray(t, dtype=np.float32) for t in tabs)


def _block_diag_halves(wa, wx):
    nb = A_BLOCKS // 2
    width = nb * A_BLOCK

    def bd(w, hh):
        rows = [jnp.pad(w[:, hh * nb + g], ((0, 0), (0, 0), (g * A_BLOCK, width - (g + 1) * A_BLOCK)))
                for g in range(nb)]
        return jnp.concatenate(rows, axis=1)

    halves = [jnp.concatenate([bd(wa, hh), bd(wx, hh)], axis=-1) for hh in range(2)]
    return jnp.stack(halves, axis=1).astype(BF16)


def kernel(x_prompt, x_sample, state_lru_h, state_lru_conv, state_sconv, state_ret, w_in_even, conv_a_w, conv_a_b, lru_wa, lru_ba, lru_wx, lru_bx, lru_lam, conv_b_w, w_out_even, w_in_odd, ret_gn_g, sp_w, sp_b, gm_ln_g, gm_ln_b, w_out_odd, ffn_w1, ffn_w3, ffn_w2, ln1_g, ln1_b, ln2_g, ln2_b):
    rows = lambda v: v.reshape(v.shape[0], 1, v.shape[1])
    xp = x_prompt.reshape(BATCH * SEQ, D_MODEL)
    xs = jnp.swapaxes(x_sample, 0, 1).reshape(DEC_SEQ * DEC_BATCH, D_MODEL)

    cos_p, sin_p = _rope_tables(np.arange(SEQ))
    cos_s, sin_s = _rope_tables(PAST_LEN + np.arange(DEC_SEQ))
    tab_p = _retention_tables(RET_CHUNK)
    tab_s = _retention_tables(DEC_SEQ)

    w_in_even_b, w_out_even_b = w_in_even.astype(BF16), w_out_even.astype(BF16)
    w_in_odd_b, w_out_odd_b = w_in_odd.astype(BF16), w_out_odd.astype(BF16)
    dense_p = (rows(ln1_g), rows(ln1_b), ffn_w1.astype(BF16), ffn_w3.astype(BF16), ffn_w2.astype(BF16),
               rows(ln2_g), rows(ln2_b))
    even_p = (conv_a_w, rows(conv_a_b), _block_diag_halves(lru_wa, lru_wx), rows(lru_ba), rows(lru_bx),
              rows(lru_lam), conv_b_w)
    ca_tm = jnp.transpose(state_lru_conv, (1, 2, 0, 3))
    sc_tm = jnp.transpose(state_sconv, (1, 2, 0, 3))
    h0_tm = jnp.swapaxes(state_lru_h, 0, 1)
    tril = jnp.tril(sp_w[:, :, :DEC_SEQ, :DEC_SEQ])
    spw_rows = jnp.repeat(jnp.transpose(tril, (0, 3, 2, 1)), D_GROUP, axis=3)
    spb_rows = jnp.repeat(jnp.swapaxes(sp_b[:, :, :DEC_SEQ], 1, 2), D_GROUP, axis=2)
    spb_t = jnp.swapaxes(sp_b, 1, 2)
    odd_small = (rows(ret_gn_g),)
    odd_ln = (rows(gm_ln_g), rows(gm_ln_b))

    h_p, h_s, ca_p, ca_s, sc_p, sc_s, r_p, v_s = [], [], [], [], [], [], [], []
    ret_sample = None
    for l in range(DEPTH):
        if l % 2 == 0:
            e = l // 2
            wo_stack, wo_layer = w_out_even_b, e
            yp, hlp, cap, scp = _even_prompt(xp.reshape(BATCH, SEQ, D_MODEL), w_in_even_b, e, even_p)
            proj_s = _in_proj(xs, w_in_even_b, e).reshape(DEC_SEQ, DEC_BATCH, EVEN_IN)
            ys, hls, cas, scs = _even_sample(proj_s, ca_tm, sc_tm, h0_tm, e, even_p)
            h_p.append(hlp)
            h_s.append(hls)
            ca_p.append(cap[:, SUBLANES - (A_CONV - 1):])
            ca_s.append(jnp.swapaxes(cas, 0, 1))
            sc_p.append(scp[:, SUBLANES - (B_CONV - 1):])
            sc_s.append(jnp.swapaxes(scs, 0, 1))
            yp = yp.reshape(BATCH * SEQ, D_MODEL)
            ys = ys.reshape(DEC_SEQ * DEC_BATCH, D_MODEL)
        else:
            o = l // 2
            wo_stack, wo_layer = w_out_odd_b, o
            yp, rp = _odd_prompt(xp.reshape(BATCH, SEQ, D_MODEL), w_in_odd_b, o, cos_p, sin_p, tab_p,
                                 *odd_small, sp_w, spb_t, *odd_ln)
            proj_s = _in_proj(xs, w_in_odd_b, o).reshape(DEC_BATCH, DEC_SEQ, ODD_IN)
            ys, ret_sample, vs = _odd_sample(proj_s, state_ret, o, cos_s, sin_s, tab_s, *odd_small,
                                             spw_rows, spb_rows, *odd_ln, ret_sample)
            r_p.append(rp)
            v_s.append(vs)
            yp = yp.reshape(BATCH * SEQ, D_MODEL)
            ys = ys.reshape(DEC_BATCH * DEC_SEQ, D_MODEL)
        xp, xs = _dense_block(yp, xp, ys, xs, wo_stack, wo_layer, l, *dense_p)
        if l + 1 < DEPTH:
            if l % 2 == 0:
                xs = jnp.swapaxes(xs.reshape(DEC_SEQ, DEC_BATCH, D_MODEL), 0, 1).reshape(DEC_BATCH * DEC_SEQ, D_MODEL)
            else:
                xs = jnp.swapaxes(xs.reshape(DEC_BATCH, DEC_SEQ, D_MODEL), 0, 1).reshape(DEC_SEQ * DEC_BATCH, D_MODEL)

    return (xp.reshape(BATCH, SEQ, D_MODEL), xs.reshape(DEC_BATCH, DEC_SEQ, D_MODEL),
            jnp.stack(h_p, axis=1), jnp.stack(h_s, axis=1),
            jnp.stack(ca_p, axis=1), jnp.stack(ca_s, axis=1),
            jnp.stack(sc_p, axis=1), jnp.stack(sc_s, axis=1),
            jnp.stack(r_p, axis=1), ret_sample, jnp.stack(v_s, axis=1))
```

```python
import jax
import jax.numpy as jnp
import numpy as np
from jax import lax
from jax.experimental import pallas as pl
from jax.experimental.pallas import tpu as pltpu

F32 = jnp.float32
BF16 = jnp.bfloat16

D_MODEL = 1024
BATCH = 8
SEQ = 2048
DEPTH = 4
DEC_BATCH = 128
DEC_SEQ = 4
PAST_LEN = 16384
N_ODD = DEPTH // 2
W_A = 512
A_BLOCKS = 8
A_BLOCK = 64
A_CONV = 4
LRU_C = 8.0
W_B = 512
B_CONV = 3
C_HEADS = 4
C_DK = 64
C_DV = 128
W_CK = C_HEADS * C_DK
W_CV = C_HEADS * C_DV
RET_CHUNK = 128
ROPE_BASE = 10000.0
D_GROUPS = 4
D_CHUNK = 128
W_D = 512
D_GROUP = W_D // D_GROUPS
EVEN_IN = 2 * W_A + 3 * W_B
ODD_IN = 2 * W_CK + 2 * W_CV + 2 * W_D
D_FF = 2816
ALPHA = (2 * DEPTH) ** 0.25
LN_EPS = 1e-5

SUBLANES = 8
LANES = 128
MIB = 1024 * 1024

ROW_TILE = 512
DENSE_TILE = 512
DENSE_GROUP = 256
EVEN_TT = 128
EVEN_PITCH = EVEN_TT + SUBLANES
SAMPLE_NB = 32


def _params(sem, vmem_mib):
    return pltpu.CompilerParams(dimension_semantics=sem, vmem_limit_bytes=vmem_mib * MIB)


def _const_spec(shape, layer=None):
    nd = len(shape)
    if layer is None:
        return pl.BlockSpec(shape, lambda *_: (0,) * nd, pipeline_mode=pl.Buffered(1))
    return pl.BlockSpec((None,) + tuple(shape), lambda *_: (layer,) + (0,) * nd, pipeline_mode=pl.Buffered(1))


def _layer_norm(x, g, b):
    mu = jnp.mean(x, -1, keepdims=True)
    xc = x - mu
    var = jnp.mean(xc * xc, -1, keepdims=True)
    return xc * lax.rsqrt(var + LN_EPS) * g + b


def _softplus(x):
    return jnp.maximum(x, 0.0) + jnp.log1p(jnp.exp(-jnp.abs(x)))


def _dot(a, b):
    return jnp.dot(a, b, preferred_element_type=F32)


def _proj_kernel(x_ref, w_ref, o_ref):
    o_ref[...] = _dot(x_ref[...].astype(BF16), w_ref[...])


def _in_proj(x2d, w_stack, layer):
    m, k = x2d.shape
    n = w_stack.shape[2]
    tm = min(ROW_TILE, m)
    return pl.pallas_call(
        _proj_kernel,
        grid=(m // tm,),
        in_specs=[pl.BlockSpec((tm, k), lambda i: (i, 0)), _const_spec((k, n), layer)],
        out_specs=pl.BlockSpec((tm, n), lambda i: (i, 0)),
        out_shape=jax.ShapeDtypeStruct((m, n), F32),
        compiler_params=_params(("parallel",), 40),
        name="in_proj",
    )(x2d, w_stack)


def _dense_kernel(yp_ref, xp_ref, ys_ref, xs_ref, wo_ref, g1_ref, b1_ref, w1_ref, w3_ref, w2_ref, g2_ref, b2_ref,
                  op_ref, os_ref, x1_s, h_s):
    i = pl.program_id(0)
    n_prompt = pl.num_programs(0) - 2
    cur, prev = i % 2, (i + 1) % 2
    groups = [slice(r, r + DENSE_GROUP) for r in range(0, x1_s.shape[1], DENSE_GROUP)]

    def out_proj(y_ref, sl):
        return _dot(y_ref[sl, :], wo_ref[...])

    def norm1(x_ref, mix, sl):
        x1 = _layer_norm(ALPHA * x_ref[sl, :] + mix, g1_ref[...], b1_ref[...])
        x1_s[cur, sl, :] = x1
        return x1.astype(BF16)

    def gate_up(xb, sl):
        h_s[cur, sl, :] = (jax.nn.silu(_dot(xb, w1_ref[...])) * _dot(xb, w3_ref[...])).astype(BF16)

    def down(sl):
        return _dot(h_s[prev, sl, :], w2_ref[...])

    def norm2(f, o_ref, sl):
        o_ref[sl, :] = _layer_norm(ALPHA * x1_s[prev, sl, :] + f, g2_ref[...], b2_ref[...])

    def front_only(y_ref, x_ref):
        xb = [norm1(x_ref, out_proj(y_ref, sl), sl) for sl in groups]
        for sl, xbg in zip(groups, xb):
            gate_up(xbg, sl)

    def back_only(o_ref):
        for sl in groups:
            norm2(down(sl), o_ref, sl)

    def both(y_ref, x_ref, o_ref):
        mix = [out_proj(y_ref, sl) for sl in groups]
        xb, f = [], []
        for g, sl in enumerate(groups):
            f.append(down(sl))
            xb.append(norm1(x_ref, mix[g], sl))
        for g, sl in enumerate(groups):
            norm2(f[g], o_ref, sl)
            gate_up(xb[g], sl)

    @pl.when(i == 0)
    def _():
        front_only(yp_ref, xp_ref)

    @pl.when((i > 0) & (i < n_prompt))
    def _():
        both(yp_ref, xp_ref, op_ref)

    @pl.when(i == n_prompt)
    def _():
        both(ys_ref, xs_ref, op_ref)

    @pl.when(i == n_prompt + 1)
    def _():
        back_only(os_ref)


def _dense_block(yp, xp, ys, xs, wo_stack, wo_layer, layer, g1, b1, w1, w3, w2, g2, b2):
    tm = DENSE_TILE
    n_prompt = xp.shape[0] // tm
    assert xp.shape[0] % tm == 0 and xs.shape[0] == tm and tm % DENSE_GROUP == 0
    prompt_in = lambda i: (jnp.minimum(i, n_prompt - 1), 0)
    prompt_out = lambda i: (jnp.clip(i - 1, 0, n_prompt - 1), 0)
    sample_row = lambda i: (0, 0)
    sample_spec = pl.BlockSpec((tm, D_MODEL), sample_row, pipeline_mode=pl.Buffered(1))
    return pl.pallas_call(
        _dense_kernel,
        grid=(n_prompt + 2,),
        in_specs=[
            pl.BlockSpec((tm, D_MODEL), prompt_in),
            pl.BlockSpec((tm, D_MODEL), prompt_in),
            sample_spec,
            sample_spec,
            _const_spec((D_MODEL, D_MODEL), wo_layer),
            _const_spec((1, D_MODEL), layer),
            _const_spec((1, D_MODEL), layer),
            _const_spec((D_MODEL, D_FF), layer),
            _const_spec((D_MODEL, D_FF), layer),
            _const_spec((D_FF, D_MODEL), layer),
            _const_spec((1, D_MODEL), layer),
            _const_spec((1, D_MODEL), layer),
        ],
        out_specs=[pl.BlockSpec((tm, D_MODEL), prompt_out), pl.BlockSpec((tm, D_MODEL), sample_row)],
        out_shape=[jax.ShapeDtypeStruct(xp.shape, F32), jax.ShapeDtypeStruct(xs.shape, F32)],
        scratch_shapes=[pltpu.VMEM((2, tm, D_MODEL), F32), pltpu.VMEM((2, tm, D_FF), BF16)],
        compiler_params=_params(("arbitrary",), 58),
        name="dense_block",
    )(yp, xp, ys, xs, wo_stack, g1, b1, w1, w3, w2, g2, b2)


def _lru_gates(xc, wblk_ref, ba, bx, sp):
    half = W_A // 2
    xb = xc.astype(BF16)
    pre0 = _dot(xb[:, :half], wblk_ref[0])
    pre1 = _dot(xb[:, half:], wblk_ref[1])
    r = jax.nn.sigmoid(jnp.concatenate([pre0[:, :half], pre1[:, :half]], axis=1) + ba)
    i = jax.nn.sigmoid(jnp.concatenate([pre0[:, half:], pre1[:, half:]], axis=1) + bx)
    log_a = (-LRU_C) * r * sp
    a = jnp.exp(log_a)
    mult = jnp.sqrt((1.0 + a * a) * jnp.tanh(-log_a))
    return a, mult, i


def _causal_taps(x_slabs, ext, b, w_ref, nrows):
    ktaps = w_ref.shape[0]
    out = []
    for s, x in enumerate(x_slabs):
        lanes = slice(s * LANES, (s + 1) * LANES)
        ext[s, b, SUBLANES:SUBLANES + nrows, :] = x
        acc = w_ref[ktaps - 1:ktaps, lanes] * x
        for k in range(ktaps - 1):
            off = SUBLANES - (ktaps - 1) + k
            acc = acc + w_ref[k:k + 1, lanes] * ext[s, b, off:off + nrows, :]
        out.append(acc)
    return jnp.concatenate(out, axis=1)


def _even_prompt_kernel(x_ref, w_ref, caw_ref, cab_ref, wblk_ref, ba_ref, bx_ref, lam_ref, cbw_ref,
                        y_ref, hlast_ref, ca_out_ref, sc_out_ref,
                        p0, p1, xa_ext, cb_ext, a_s, b_s, g_s, h_c):
    tt, pitch = EVEN_TT, EVEN_PITCH
    nslab = W_A // LANES
    j = pl.program_id(0)
    bufs = (p0, p1)

    def project(b):
        bufs[b % 2][...] = _dot(x_ref[b].astype(BF16), w_ref[...])

    @pl.when(j == 0)
    def _():
        xa_ext[:, :, 0:SUBLANES, :] = jnp.zeros((nslab, BATCH, SUBLANES, LANES), F32)
        cb_ext[:, :, 0:SUBLANES, :] = jnp.zeros((nslab, BATCH, SUBLANES, LANES), F32)
        h_c[...] = jnp.zeros(h_c.shape, F32)

    @pl.when(j > 0)
    def _():
        xa_ext[:, :, 0:SUBLANES, :] = xa_ext[:, :, tt:tt + SUBLANES, :]
        cb_ext[:, :, 0:SUBLANES, :] = cb_ext[:, :, tt:tt + SUBLANES, :]

    sp = _softplus(-lam_ref[...])
    ba = ba_ref[...]
    bx = bx_ref[...]
    cab = cab_ref[...]
    first_row = (lax.broadcasted_iota(jnp.int32, (SUBLANES, 1), 0) == 0) & (j == 0)
    cg0 = 2 * W_A + W_B
    xb0 = 2 * W_A + 2 * W_B

    def stage(b):
        p_ref = bufs[b % 2]
        xc = cab + _causal_taps([p_ref[:, s * LANES:(s + 1) * LANES] for s in range(nslab)], xa_ext, b, caw_ref, tt)
        a, mult, gate = _lru_gates(xc, wblk_ref, ba, bx, sp)
        mult = jnp.concatenate([jnp.where(first_row, 1.0, mult[:SUBLANES]), mult[SUBLANES:]], axis=0)
        bb = mult * (gate * xc)
        for s in range(nslab):
            a_s[s, b * pitch:b * pitch + tt, :] = a[:, s * LANES:(s + 1) * LANES]
            b_s[s, b * pitch:b * pitch + tt, :] = bb[:, s * LANES:(s + 1) * LANES]
        g_s[b] = jax.nn.gelu(p_ref[:, W_A:2 * W_A])
        cb = [p_ref[:, cg0 + s * LANES:cg0 + (s + 1) * LANES] * p_ref[:, xb0 + s * LANES:xb0 + (s + 1) * LANES]
              for s in range(nslab)]
        zb = _causal_taps(cb, cb_ext, b, cbw_ref, tt)
        y_ref[b, :, W_A:] = (p_ref[:, 2 * W_A:2 * W_A + W_B] * zb).astype(BF16)

    project(0)
    for b in range(BATCH):
        if b + 1 < BATCH:
            project(b + 1)
        stage(b)

    def step(t, hs):
        out = []
        for s in range(nslab):
            rows = pl.ds(t, BATCH, stride=pitch)
            hn = a_s[s, rows, :] * hs[s] + b_s[s, rows, :]
            b_s[s, rows, :] = hn
            out.append(hn)
        return tuple(out)

    hs = lax.fori_loop(0, tt, step, tuple(h_c[s] for s in range(nslab)), unroll=4)
    for s in range(nslab):
        h_c[s] = hs[s]

    def emit(b, carry):
        row0 = pl.multiple_of(b * pitch, SUBLANES)
        h = jnp.concatenate([b_s[s, pl.ds(row0, tt), :] for s in range(nslab)], axis=1)
        y_ref[b, :, 0:W_A] = (h * g_s[b]).astype(BF16)
        return carry

    lax.fori_loop(0, BATCH, emit, 0)

    @pl.when(j == pl.num_programs(0) - 1)
    def _():
        hlast_ref[...] = jnp.concatenate(list(hs), axis=1)
        for s in range(nslab):
            ca_out_ref[:, :, s * LANES:(s + 1) * LANES] = xa_ext[s, :, tt:tt + SUBLANES, :]
            sc_out_ref[:, :, s * LANES:(s + 1) * LANES] = cb_ext[s, :, tt:tt + SUBLANES, :]


def _even_param_specs(e):
    return [
        _const_spec((A_CONV, W_A), e),
        _const_spec((1, W_A), e),
        _const_spec((2, W_A // 2, W_A), e),
        _const_spec((1, W_A), e),
        _const_spec((1, W_A), e),
        _const_spec((1, W_A), e),
        _const_spec((B_CONV, W_B), e),
    ]


def _even_prompt(x3d, w_stack, e, params):
    tt, pitch = EVEN_TT, EVEN_PITCH
    nslab = W_A // LANES
    assert W_A == W_B
    return pl.pallas_call(
        _even_prompt_kernel,
        grid=(SEQ // tt,),
        in_specs=[pl.BlockSpec((BATCH, tt, D_MODEL), lambda j: (0, j, 0)),
                  _const_spec((D_MODEL, EVEN_IN), e)] + _even_param_specs(e),
        out_specs=[
            pl.BlockSpec((BATCH, tt, D_MODEL), lambda j: (0, j, 0)),
            pl.BlockSpec((BATCH, W_A), lambda j: (0, 0)),
            pl.BlockSpec((BATCH, SUBLANES, W_A), lambda j: (0, 0, 0)),
            pl.BlockSpec((BATCH, SUBLANES, W_B), lambda j: (0, 0, 0)),
        ],
        out_shape=[
            jax.ShapeDtypeStruct((BATCH, SEQ, D_MODEL), BF16),
            jax.ShapeDtypeStruct((BATCH, W_A), F32),
            jax.ShapeDtypeStruct((BATCH, SUBLANES, W_A), F32),
            jax.ShapeDtypeStruct((BATCH, SUBLANES, W_B), F32),
        ],
        scratch_shapes=[
            pltpu.VMEM((tt, EVEN_IN), F32),
            pltpu.VMEM((tt, EVEN_IN), F32),
            pltpu.VMEM((nslab, BATCH, tt + SUBLANES, LANES), F32),
            pltpu.VMEM((nslab, BATCH, tt + SUBLANES, LANES), F32),
            pltpu.VMEM((nslab, BATCH * pitch, LANES), F32),
            pltpu.VMEM((nslab, BATCH * pitch, LANES), F32),
            pltpu.VMEM((BATCH, tt, W_A), F32),
            pltpu.VMEM((nslab, BATCH, LANES), F32),
        ],
        compiler_params=_params(("arbitrary",), 48),
        name="even_prompt",
    )(x3d, w_stack, *params)


def _even_sample_kernel(proj_ref, ca_ref, sc_ref, h0_ref, caw_ref, cab_ref, wblk_ref, ba_ref, bx_ref,
                        lam_ref, cbw_ref, y_ref, hlast_ref, ca_out_ref, sc_out_ref):
    sp = _softplus(-lam_ref[...])
    rows_a = [ca_ref[k] for k in range(A_CONV - 1)] + [proj_ref[l, :, 0:W_A] for l in range(DEC_SEQ)]
    xc = []
    for l in range(DEC_SEQ):
        acc = cab_ref[...] + caw_ref[0:1, :] * rows_a[l]
        for k in range(1, A_CONV):
            acc = acc + caw_ref[k:k + 1, :] * rows_a[l + k]
        xc.append(acc)
    xc_all = jnp.concatenate(xc, axis=0)
    a, mult, gate = _lru_gates(xc_all, wblk_ref, ba_ref[...], bx_ref[...], sp)
    bb = mult * (gate * xc_all)
    h = h0_ref[...]
    rows_b = [sc_ref[k] for k in range(B_CONV - 1)]
    for l in range(DEC_SEQ):
        sl = slice(l * DEC_BATCH, (l + 1) * DEC_BATCH)
        h = a[sl] * h + bb[sl]
        y_ref[l, :, 0:W_A] = (h * jax.nn.gelu(proj_ref[l, :, W_A:2 * W_A])).astype(BF16)
        rows_b.append(proj_ref[l, :, 2 * W_A + W_B:2 * W_A + 2 * W_B] * proj_ref[l, :, 2 * W_A + 2 * W_B:])
    hlast_ref[...] = h
    for l in range(DEC_SEQ):
        zb = cbw_ref[0:1, :] * rows_b[l]
        for k in range(1, B_CONV):
            zb = zb + cbw_ref[k:k + 1, :] * rows_b[l + k]
        y_ref[l, :, W_A:] = (proj_ref[l, :, 2 * W_A:2 * W_A + W_B] * zb).astype(BF16)
    for k in range(A_CONV - 1):
        ca_out_ref[k] = rows_a[DEC_SEQ + k]
    for k in range(B_CONV - 1):
        sc_out_ref[k] = rows_b[DEC_SEQ + k]


def _even_sample(proj_tm, ca_tm, sc_tm, h0_tm, e, params):
    whole = lambda shape: pl.BlockSpec(shape, lambda i: (0,) * len(shape))
    return pl.pallas_call(
        _even_sample_kernel,
        grid=(1,),
        in_specs=[
            whole((DEC_SEQ, DEC_BATCH, EVEN_IN)),
            _const_spec((A_CONV - 1, DEC_BATCH, W_A), e),
            _const_spec((B_CONV - 1, DEC_BATCH, W_B), e),
            _const_spec((DEC_BATCH, W_A), e),
        ] + _even_param_specs(e),
        out_specs=[
            whole((DEC_SEQ, DEC_BATCH, D_MODEL)),
            whole((DEC_BATCH, W_A)),
            whole((A_CONV - 1, DEC_BATCH, W_A)),
            whole((B_CONV - 1, DEC_BATCH, W_B)),
        ],
        out_shape=[
            jax.ShapeDtypeStruct((DEC_SEQ, DEC_BATCH, D_MODEL), BF16),
            jax.ShapeDtypeStruct((DEC_BATCH, W_A), F32),
            jax.ShapeDtypeStruct((A_CONV - 1, DEC_BATCH, W_A), F32),
            jax.ShapeDtypeStruct((B_CONV - 1, DEC_BATCH, W_B), F32),
        ],
        compiler_params=_params(("arbitrary",), 40),
        name="even_sample",
    )(proj_tm, ca_tm, sc_tm, h0_tm, *params)


def _rope(x, cos, sin_signed):
    half = C_DK // 2
    ax = x.ndim - 1
    pieces = []
    for p in range(W_CK // LANES):
        xv = x[..., p * LANES:(p + 1) * LANES]
        lane = lax.broadcasted_iota(jnp.int32, xv.shape, ax)
        partner = jnp.where((lane % C_DK) < half, pltpu.roll(xv, LANES - half, axis=ax), pltpu.roll(xv, half, axis=ax))
        pieces.append(xv * cos[..., p * LANES:(p + 1) * LANES] + partner * sin_signed[..., p * LANES:(p + 1) * LANES])
    return jnp.concatenate(pieces, axis=ax)


def _group_norm(o):
    mu = jnp.mean(o, -1, keepdims=True)
    oc = o - mu
    var = jnp.mean(oc * oc, -1, keepdims=True)
    return oc * lax.rsqrt(var + LN_EPS)


def _odd_chunk(p_ref, b, cos_ref, sin_ref, dec_ref, cs_ref, kd_ref, sdec_ref, gn_ref,
               spw_ref, spb_ref, lng_ref, lnb_ref, y_ref, s_c):
    q = _rope(p_ref[:, 0:W_CK], cos_ref[...], sin_ref[...])
    k = _rope(p_ref[:, W_CK:2 * W_CK], cos_ref[...], sin_ref[...]) * (C_DK ** -0.5)
    kd = k * kd_ref[...]
    v0 = 2 * W_CK
    g0 = v0 + W_CV
    for h in range(C_HEADS):
        qh = q[:, h * C_DK:(h + 1) * C_DK].astype(BF16)
        kh = k[:, h * C_DK:(h + 1) * C_DK].astype(BF16)
        kdh = kd[:, h * C_DK:(h + 1) * C_DK].astype(BF16)
        vh = p_ref[:, v0 + h * C_DV:v0 + (h + 1) * C_DV].astype(BF16)
        s_prev = s_c[b, h]
        scores = lax.dot_general(qh, kh, (((1,), (1,)), ((), ())), preferred_element_type=F32) * dec_ref[h]
        o = _dot(scores.astype(BF16), vh) + _dot(qh, s_prev.astype(BF16)) * cs_ref[:, h * C_DV:(h + 1) * C_DV]
        s_c[b, h] = sdec_ref[h] * s_prev + lax.dot_general(kdh, vh, (((0,), (0,)), ((), ())), preferred_element_type=F32)
        gate = jax.nn.silu(p_ref[:, g0 + h * C_DV:g0 + (h + 1) * C_DV])
        y_ref[b, :, h * C_DV:(h + 1) * C_DV] = (gate * (_group_norm(o) * gn_ref[:, h * C_DV:(h + 1) * C_DV])).astype(BF16)

    u0 = g0 + W_CV
    u = jax.nn.gelu(p_ref[:, u0:u0 + W_D])
    vd = _layer_norm(jax.nn.gelu(p_ref[:, u0 + W_D:]), lng_ref[...], lnb_ref[...]).astype(BF16)
    ri = lax.broadcasted_iota(jnp.int32, (D_CHUNK, D_CHUNK), 0)
    ci = lax.broadcasted_iota(jnp.int32, (D_CHUNK, D_CHUNK), 1)
    for gi in range(D_GROUPS):
        w = jnp.where(ri >= ci, spw_ref[gi], 0.0).astype(BF16)
        s = _dot(w, vd[:, gi * D_GROUP:(gi + 1) * D_GROUP]) + spb_ref[:, gi:gi + 1]
        y_ref[b, :, W_CV + gi * D_GROUP:W_CV + (gi + 1) * D_GROUP] = (u[:, gi * D_GROUP:(gi + 1) * D_GROUP] * s).astype(BF16)


def _odd_prompt_kernel(x_ref, w_ref, cos_ref, sin_ref, dec_ref, cs_ref, kd_ref, sdec_ref, gn_ref,
                       spw_ref, spb_ref, lng_ref, lnb_ref, y_ref, s_out_ref, p0, p1, s_c):
    c = pl.program_id(0)

    @pl.when(c == 0)
    def _():
        s_c[...] = jnp.zeros(s_c.shape, F32)

    bufs = (p0, p1)

    def project(b):
        bufs[b % 2][...] = _dot(x_ref[b].astype(BF16), w_ref[...])

    project(0)
    for b in range(BATCH):
        if b + 1 < BATCH:
            project(b + 1)
        _odd_chunk(bufs[b % 2], b, cos_ref, sin_ref, dec_ref, cs_ref, kd_ref, sdec_ref, gn_ref,
                   spw_ref, spb_ref, lng_ref, lnb_ref, y_ref, s_c)

    @pl.when(c == pl.num_programs(0) - 1)
    def _():
        s_out_ref[...] = s_c[...]


def _odd_prompt(x3d, w_stack, o, cos, sin, tabs, gn, spw, spb_t, lng, lnb):
    nchunk = SEQ // RET_CHUNK
    dec, cs, kd, sdec = tabs
    return pl.pallas_call(
        _odd_prompt_kernel,
        grid=(nchunk,),
        in_specs=[
            pl.BlockSpec((BATCH, RET_CHUNK, D_MODEL), lambda c: (0, c, 0)),
            _const_spec((D_MODEL, ODD_IN), o),
            pl.BlockSpec((RET_CHUNK, W_CK), lambda c: (c, 0)),
            pl.BlockSpec((RET_CHUNK, W_CK), lambda c: (c, 0)),
            _const_spec((C_HEADS, RET_CHUNK, RET_CHUNK)),
            _const_spec((RET_CHUNK, W_CV)),
            _const_spec((RET_CHUNK, W_CK)),
            _const_spec((C_HEADS, C_DK, C_DV)),
            _const_spec((1, W_CV), o),
            _const_spec((D_GROUPS, D_CHUNK, D_CHUNK), o),
            _const_spec((D_CHUNK, D_GROUPS), o),
            _const_spec((1, W_D), o),
            _const_spec((1, W_D), o),
        ],
        out_specs=[
            pl.BlockSpec((BATCH, RET_CHUNK, D_MODEL), lambda c: (0, c, 0)),
            pl.BlockSpec((BATCH, C_HEADS, C_DK, C_DV), lambda c: (0, 0, 0, 0)),
        ],
        out_shape=[
            jax.ShapeDtypeStruct((BATCH, SEQ, D_MODEL), BF16),
            jax.ShapeDtypeStruct((BATCH, C_HEADS, C_DK, C_DV), F32),
        ],
        scratch_shapes=[
            pltpu.VMEM((RET_CHUNK, ODD_IN), F32),
            pltpu.VMEM((RET_CHUNK, ODD_IN), F32),
            pltpu.VMEM((BATCH, C_HEADS, C_DK, C_DV), F32),
        ],
        compiler_params=_params(("arbitrary",), 48),
        name="odd_prompt",
    )(x3d, w_stack, cos, sin, dec, cs, kd, sdec, gn, spw, spb_t, lng, lnb)


def _odd_sample_kernel(proj_ref, s0_ref, cos_ref, sin_ref, dec_ref, cs_ref, kd_ref, sdec_ref, gn_ref,
                       spw_ref, spb_ref, lng_ref, lnb_ref, *rest):
    y_ref, s_out_ref, vd_ref = rest[-3:]
    if len(rest) > 3:
        s_out_ref[:, 0] = rest[0][...]
        s_out_ref = s_out_ref.at[:, 1]
    q = _rope(proj_ref[:, :, 0:W_CK], cos_ref[...], sin_ref[...])
    k = _rope(proj_ref[:, :, W_CK:2 * W_CK], cos_ref[...], sin_ref[...]) * (C_DK ** -0.5)
    kd = k * kd_ref[...]
    v0 = 2 * W_CK
    g0 = v0 + W_CV
    for h in range(C_HEADS):
        qh = q[:, :, h * C_DK:(h + 1) * C_DK].astype(BF16)
        kh = k[:, :, h * C_DK:(h + 1) * C_DK].astype(BF16)
        kdh = kd[:, :, h * C_DK:(h + 1) * C_DK].astype(BF16)
        vh = proj_ref[:, :, v0 + h * C_DV:v0 + (h + 1) * C_DV].astype(BF16)
        s_prev = s0_ref[:, h]
        scores = jnp.einsum('nld,nmd->nlm', qh, kh, preferred_element_type=F32) * dec_ref[h]
        o = (jnp.einsum('nlm,nme->nle', scores.astype(BF16), vh, preferred_element_type=F32)
             + jnp.einsum('nld,nde->nle', qh, s_prev.astype(BF16), preferred_element_type=F32)
             * cs_ref[:, h * C_DV:(h + 1) * C_DV])
        s_out_ref[:, h] = sdec_ref[h] * s_prev + jnp.einsum('nld,nle->nde', kdh, vh, preferred_element_type=F32)
        gate = jax.nn.silu(proj_ref[:, :, g0 + h * C_DV:g0 + (h + 1) * C_DV])
        y_ref[:, :, h * C_DV:(h + 1) * C_DV] = (gate * (_group_norm(o) * gn_ref[:, h * C_DV:(h + 1) * C_DV])).astype(BF16)

    u0 = g0 + W_CV
    u = jax.nn.gelu(proj_ref[:, :, u0:u0 + W_D])
    vd = _layer_norm(jax.nn.gelu(proj_ref[:, :, u0 + W_D:]), lng_ref[...], lnb_ref[...])
    vd_ref[...] = vd
    s = spb_ref[...] + spw_ref[0] * vd[:, 0:1, :]
    for m in range(1, DEC_SEQ):
        s = s + spw_ref[m] * vd[:, m:m + 1, :]
    y_ref[:, :, W_CV:] = (u * s).astype(BF16)


def _odd_sample(proj, state_ret, o, cos, sin, tabs, gn, spw_rows, spb_rows, lng, lnb, prev_states):
    nb = SAMPLE_NB
    dec, cs, kd, sdec = tabs
    seq3 = lambda i: (i, 0, 0)
    one_state = pl.BlockSpec((nb, C_HEADS, C_DK, C_DV), lambda i: (i, 0, 0, 0))
    in_specs = [
        pl.BlockSpec((nb, DEC_SEQ, ODD_IN), seq3),
        pl.BlockSpec((nb, None, C_HEADS, C_DK, C_DV), lambda i: (i, o, 0, 0, 0)),
        _const_spec((DEC_SEQ, W_CK)),
        _const_spec((DEC_SEQ, W_CK)),
        _const_spec((C_HEADS, DEC_SEQ, DEC_SEQ)),
        _const_spec((DEC_SEQ, W_CV)),
        _const_spec((DEC_SEQ, W_CK)),
        _const_spec((C_HEADS, C_DK, C_DV)),
        _const_spec((1, W_CV), o),
        _const_spec((DEC_SEQ, DEC_SEQ, W_D), o),
        _const_spec((DEC_SEQ, W_D), o),
        _const_spec((1, W_D), o),
        _const_spec((1, W_D), o),
    ]
    args = [proj, state_ret, cos, sin, dec, cs, kd, sdec, gn, spw_rows, spb_rows, lng, lnb]
    if prev_states is None:
        state_spec = one_state
        state_shape = (DEC_BATCH, C_HEADS, C_DK, C_DV)
    else:
        in_specs.append(one_state)
        args.append(prev_states)
        state_spec = pl.BlockSpec((nb, N_ODD, C_HEADS, C_DK, C_DV), lambda i: (i, 0, 0, 0, 0))
        state_shape = (DEC_BATCH, N_ODD, C_HEADS, C_DK, C_DV)
    return pl.pallas_call(
        _odd_sample_kernel,
        grid=(DEC_BATCH // nb,),
        in_specs=in_specs,
        out_specs=[pl.BlockSpec((nb, DEC_SEQ, D_MODEL), seq3), state_spec, pl.BlockSpec((nb, DEC_SEQ, W_D), seq3)],
        out_shape=[
            jax.ShapeDtypeStruct((DEC_BATCH, DEC_SEQ, D_MODEL), BF16),
            jax.ShapeDtypeStruct(state_shape, F32),
            jax.ShapeDtypeStruct((DEC_BATCH, DEC_SEQ, W_D), F32),
        ],
        compiler_params=_params(("parallel",), 48),
        name="odd_sample",
    )(*args)


def _rope_tables(pos):
    half = C_DK // 2
    freq = ROPE_BASE ** (-np.arange(half, dtype=np.float64) / half)
    ang = np.asarray(pos, np.float64)[:, None] * freq
    cos, sin = np.cos(ang), np.sin(ang)
    cos_l = np.tile(np.concatenate([cos, cos], axis=-1), (1, C_HEADS))
    sin_l = np.tile(np.concatenate([-sin, sin], axis=-1), (1, C_HEADS))
    return cos_l.astype(np.float32), sin_l.astype(np.float32)


def _retention_tables(length):
    log_g = np.log1p(-np.exp2(-5.0 - np.arange(C_HEADS, dtype=np.float64)))
    idx = np.arange(length, dtype=np.float64)
    rel = idx[:, None] - idx[None, :]
    decay = np.where(rel >= 0, np.exp(log_g[:, None, None] * np.maximum(rel, 0.0)), 0.0)
    cross = np.exp(log_g[None, :] * (idx[:, None] + 1.0))
    kdec = np.exp(log_g[None, :] * (length - 1.0 - idx[:, None]))
    sdec = np.exp(log_g * length)
    tabs = (decay,
            np.repeat(cross, C_DV, axis=1),
            np.repeat(kdec, C_DK, axis=1),
            np.broadcast_to(sdec[:, None, None], (C_HEADS, C_DK, C_DV)))
    return tuple(np.ascontiguousarray(t, dtype=np.float32) for t in tabs)


def _block_diag_halves(wa, wx):
    nb = A_BLOCKS // 2
    width = nb * A_BLOCK

    def bd(w, hh):
        rows = [jnp.pad(w[:, hh * nb + g], ((0, 0), (0, 0), (g * A_BLOCK, width - (g + 1) * A_BLOCK)))
                for g in range(nb)]
        return jnp.concatenate(rows, axis=1)

    halves = [jnp.concatenate([bd(wa, hh), bd(wx, hh)], axis=-1) for hh in range(2)]
    return jnp.stack(halves, axis=1).astype(BF16)


def kernel(x_prompt, x_sample, state_lru_h, state_lru_conv, state_sconv, state_ret, w_in_even, conv_a_w, conv_a_b, lru_wa, lru_ba, lru_wx, lru_bx, lru_lam, conv_b_w, w_out_even, w_in_odd, ret_gn_g, sp_w, sp_b, gm_ln_g, gm_ln_b, w_out_odd, ffn_w1, ffn_w3, ffn_w2, ln1_g, ln1_b, ln2_g, ln2_b):
    rows = lambda v: v.reshape(v.shape[0], 1, v.shape[1])
    xp = x_prompt.reshape(BATCH * SEQ, D_MODEL)
    xs = jnp.swapaxes(x_sample, 0, 1).reshape(DEC_SEQ * DEC_BATCH, D_MODEL)

    cos_p, sin_p = _rope_tables(np.arange(SEQ))
    cos_s, sin_s = _rope_tables(PAST_LEN + np.arange(DEC_SEQ))
    tab_p = _retention_tables(RET_CHUNK)
    tab_s = _retention_tables(DEC_SEQ)

    w_in_even_b, w_out_even_b = w_in_even.astype(BF16), w_out_even.astype(BF16)
    w_in_odd_b, w_out_odd_b = w_in_odd.astype(BF16), w_out_odd.astype(BF16)
    dense_p = (rows(ln1_g), rows(ln1_b), ffn_w1.astype(BF16), ffn_w3.astype(BF16), ffn_w2.astype(BF16),
               rows(ln2_g), rows(ln2_b))
    even_p = (conv_a_w, rows(conv_a_b), _block_diag_halves(lru_wa, lru_wx), rows(lru_ba), rows(lru_bx),
              rows(lru_lam), conv_b_w)
    ca_tm = jnp.transpose(state_lru_conv, (1, 2, 0, 3))
    sc_tm = jnp.transpose(state_sconv, (1, 2, 0, 3))
    h0_tm = jnp.swapaxes(state_lru_h, 0, 1)
    tril = jnp.tril(sp_w[:, :, :DEC_SEQ, :DEC_SEQ])
    spw_rows = jnp.repeat(jnp.transpose(tril, (0, 3, 2, 1)), D_GROUP, axis=3)
    spb_rows = jnp.repeat(jnp.swapaxes(sp_b[:, :, :DEC_SEQ], 1, 2), D_GROUP, axis=2)
    spb_t = jnp.swapaxes(sp_b, 1, 2)
    odd_small = (rows(ret_gn_g),)
    odd_ln = (rows(gm_ln_g), rows(gm_ln_b))

    h_p, h_s, ca_p, ca_s, sc_p, sc_s, r_p, v_s = [], [], [], [], [], [], [], []
    ret_sample = None
    for l in range(DEPTH):
        if l % 2 == 0:
            e = l // 2
            wo_stack, wo_layer = w_out_even_b, e
            yp, hlp, cap, scp = _even_prompt(xp.reshape(BATCH, SEQ, D_MODEL), w_in_even_b, e, even_p)
            proj_s = _in_proj(xs, w_in_even_b, e).reshape(DEC_SEQ, DEC_BATCH, EVEN_IN)
            ys, hls, cas, scs = _even_sample(proj_s, ca_tm, sc_tm, h0_tm, e, even_p)
            h_p.append(hlp)
            h_s.append(hls)
            ca_p.append(cap[:, SUBLANES - (A_CONV - 1):])
            ca_s.append(jnp.swapaxes(cas, 0, 1))
            sc_p.append(scp[:, SUBLANES - (B_CONV - 1):])
            sc_s.append(jnp.swapaxes(scs, 0, 1))
            yp = yp.reshape(BATCH * SEQ, D_MODEL)
            ys = ys.reshape(DEC_SEQ * DEC_BATCH, D_MODEL)
        else:
            o = l // 2
            wo_stack, wo_layer = w_out_odd_b, o
            yp, rp = _odd_prompt(xp.reshape(BATCH, SEQ, D_MODEL), w_in_odd_b, o, cos_p, sin_p, tab_p,
                                 *odd_small, sp_w, spb_t, *odd_ln)
            proj_s = _in_proj(xs, w_in_odd_b, o).reshape(DEC_BATCH, DEC_SEQ, ODD_IN)
            ys, ret_sample, vs = _odd_sample(proj_s, state_ret, o, cos_s, sin_s, tab_s, *odd_small,
                                             spw_rows, spb_rows, *odd_ln, ret_sample)
            r_p.append(rp)
            v_s.append(vs)
            yp = yp.reshape(BATCH * SEQ, D_MODEL)
            ys = ys.reshape(DEC_BATCH * DEC_SEQ, D_MODEL)
        xp, xs = _dense_block(yp, xp, ys, xs, wo_stack, wo_layer, l, *dense_p)
        if l + 1 < DEPTH:
            if l % 2 == 0:
                xs = jnp.swapaxes(xs.reshape(DEC_SEQ, DEC_BATCH, D_MODEL), 0, 1).reshape(DEC_BATCH * DEC_SEQ, D_MODEL)
            else:
                xs = jnp.swapaxes(xs.reshape(DEC_BATCH, DEC_SEQ, D_MODEL), 0, 1).reshape(DEC_SEQ * DEC_BATCH, D_MODEL)

    return (xp.reshape(BATCH, SEQ, D_MODEL), xs.reshape(DEC_BATCH, DEC_SEQ, D_MODEL),
            jnp.stack(h_p, axis=1), jnp.stack(h_s, axis=1),
            jnp.stack(ca_p, axis=1), jnp.stack(ca_s, axis=1),
            jnp.stack(sc_p, axis=1), jnp.stack(sc_s, axis=1),
            jnp.stack(r_p, axis=1), ret_sample, jnp.stack(v_s, axis=1))
```

```python
import jax
import jax.numpy as jnp
import numpy as np
from jax import lax
from jax.experimental import pallas as pl
from jax.experimental.pallas import tpu as pltpu

F32 = jnp.float32
BF16 = jnp.bfloat16

D_MODEL = 1024
BATCH = 8
SEQ = 2048
DEPTH = 4
DEC_BATCH = 128
DEC_SEQ = 4
PAST_LEN = 16384
N_ODD = DEPTH // 2
W_A = 512
A_BLOCKS = 8
A_BLOCK = 64
A_CONV = 4
LRU_C = 8.0
W_B = 512
B_CONV = 3
C_HEADS = 4
C_DK = 64
C_DV = 128
W_CK = C_HEADS * C_DK
W_CV = C_HEADS * C_DV
RET_CHUNK = 128
ROPE_BASE = 10000.0
D_GROUPS = 4
D_CHUNK = 128
W_D = 512
D_GROUP = W_D // D_GROUPS
EVEN_IN = 2 * W_A + 3 * W_B
ODD_IN = 2 * W_CK + 2 * W_CV + 2 * W_D
D_FF = 2816
ALPHA = (2 * DEPTH) ** 0.25
LN_EPS = 1e-5

SUBLANES = 8
LANES = 128
MIB = 1024 * 1024

ROW_TILE = 512
DENSE_TILE = 512
DENSE_GROUP = 256
EVEN_TT = 128
EVEN_PITCH = EVEN_TT + SUBLANES
SAMPLE_NB = 32


def _params(sem, vmem_mib):
    return pltpu.CompilerParams(dimension_semantics=sem, vmem_limit_bytes=vmem_mib * MIB)


def _const_spec(shape, layer=None):
    nd = len(shape)
    if layer is None:
        return pl.BlockSpec(shape, lambda *_: (0,) * nd, pipeline_mode=pl.Buffered(1))
    return pl.BlockSpec((None,) + tuple(shape), lambda *_: (layer,) + (0,) * nd, pipeline_mode=pl.Buffered(1))


def _layer_norm(x, g, b):
    mu = jnp.mean(x, -1, keepdims=True)
    xc = x - mu
    var = jnp.mean(xc * xc, -1, keepdims=True)
    return xc * lax.rsqrt(var + LN_EPS) * g + b


def _softplus(x):
    return jnp.maximum(x, 0.0) + jnp.log1p(jnp.exp(-jnp.abs(x)))


def _dot(a, b):
    return jnp.dot(a, b, preferred_element_type=F32)


def _proj_kernel(x_ref, w_ref, o_ref):
    o_ref[...] = _dot(x_ref[...].astype(BF16), w_ref[...])


def _in_proj(x2d, w_stack, layer):
    m, k = x2d.shape
    n = w_stack.shape[2]
    tm = min(ROW_TILE, m)
    return pl.pallas_call(
        _proj_kernel,
        grid=(m // tm,),
        in_specs=[pl.BlockSpec((tm, k), lambda i: (i, 0)), _const_spec((k, n), layer)],
        out_specs=pl.BlockSpec((tm, n), lambda i: (i, 0)),
        out_shape=jax.ShapeDtypeStruct((m, n), F32),
        compiler_params=_params(("parallel",), 40),
        name="in_proj",
    )(x2d, w_stack)


def _dense_kernel(yp_ref, xp_ref, ys_ref, xs_ref, wo_ref, g1_ref, b1_ref, w1_ref, w3_ref, w2_ref, g2_ref, b2_ref,
                  op_ref, os_ref, x1_s, h_s):
    i = pl.program_id(0)
    n_prompt = pl.num_programs(0) - 2
    cur, prev = i % 2, (i + 1) % 2
    groups = [slice(r, r + DENSE_GROUP) for r in range(0, x1_s.shape[1], DENSE_GROUP)]

    def out_proj(y_ref, sl):
        return _dot(y_ref[sl, :], wo_ref[...])

    def norm1(x_ref, mix, sl):
        x1 = _layer_norm(ALPHA * x_ref[sl, :] + mix, g1_ref[...], b1_ref[...])
        x1_s[cur, sl, :] = x1
        return x1.astype(BF16)

    def gate_up(xb, sl):
        h_s[cur, sl, :] = (jax.nn.silu(_dot(xb, w1_ref[...])) * _dot(xb, w3_ref[...])).astype(BF16)

    def down(sl):
        return _dot(h_s[prev, sl, :], w2_ref[...])

    def norm2(f, o_ref, sl):
        o_ref[sl, :] = _layer_norm(ALPHA * x1_s[prev, sl, :] + f, g2_ref[...], b2_ref[...])

    def front_only(y_ref, x_ref):
        xb = [norm1(x_ref, out_proj(y_ref, sl), sl) for sl in groups]
        for sl, xbg in zip(groups, xb):
            gate_up(xbg, sl)

    def back_only(o_ref):
        for sl in groups:
            norm2(down(sl), o_ref, sl)

    def both(y_ref, x_ref, o_ref):
        mix = [out_proj(y_ref, sl) for sl in groups]
        xb, f = [], []
        for g, sl in enumerate(groups):
            f.append(down(sl))
            xb.append(norm1(x_ref, mix[g], sl))
        for g, sl in enumerate(groups):
            norm2(f[g], o_ref, sl)
            gate_up(xb[g], sl)

    @pl.when(i == 0)
    def _():
        front_only(yp_ref, xp_ref)

    @pl.when((i > 0) & (i < n_prompt))
    def _():
        both(yp_ref, xp_ref, op_ref)

    @pl.when(i == n_prompt)
    def _():
        both(ys_ref, xs_ref, op_ref)

    @pl.when(i == n_prompt + 1)
    def _():
        back_only(os_ref)


def _dense_kernel_flat(yp_ref, xp_ref, ys_ref, xs_ref, wo_ref, g1_ref, b1_ref, w1_ref, w3_ref, w2_ref, g2_ref, b2_ref,
                       op_ref, os_ref):
    n_prompt = pl.num_programs(0) - 1
    weights = (wo_ref, g1_ref, b1_ref, w1_ref, w3_ref, w2_ref, g2_ref, b2_ref)

    @pl.when(pl.program_id(0) < n_prompt)
    def _():
        _dense_rows(yp_ref, xp_ref, *weights, op_ref)

    @pl.when(pl.program_id(0) == n_prompt)
    def _():
        _dense_rows(ys_ref, xs_ref, *weights, os_ref)


def _dense_rows(y_ref, x_ref, wo_ref, g1_ref, b1_ref, w1_ref, w3_ref, w2_ref, g2_ref, b2_ref, o_ref):
    rows = DENSE_GROUP
    nparts = y_ref.shape[0] // rows
    sls = [slice(p * rows, (p + 1) * rows) for p in range(nparts)]
    mix, x1, xb, h = {}, {}, {}, {}

    def out_proj(p):
        mix[p] = _dot(y_ref[sls[p], :], wo_ref[...])

    def norm1(p):
        x1[p] = _layer_norm(ALPHA * x_ref[sls[p], :] + mix[p], g1_ref[...], b1_ref[...])
        xb[p] = x1[p].astype(BF16)

    def gate_up(p):
        h[p] = (jax.nn.silu(_dot(xb[p], w1_ref[...])) * _dot(xb[p], w3_ref[...])).astype(BF16)

    def down_norm2(p):
        f = _dot(h[p], w2_ref[...])
        o_ref[sls[p], :] = _layer_norm(ALPHA * x1[p] + f, g2_ref[...], b2_ref[...])

    out_proj(0)
    for p in range(nparts):
        norm1(p)
        if p + 1 < nparts:
            out_proj(p + 1)
    for p in range(nparts):
        gate_up(p)
    for p in range(nparts):
        down_norm2(p)


def _dense_block_flat(yp, xp, ys, xs, wo_stack, wo_layer, layer, g1, b1, w1, w3, w2, g2, b2):
    tm = DENSE_TILE
    n_prompt = xp.shape[0] // tm
    prompt_row = lambda i: (jnp.minimum(i, n_prompt - 1), 0)
    sample_row = lambda i: (0, 0)
    return pl.pallas_call(
        _dense_kernel_flat,
        grid=(n_prompt + 1,),
        in_specs=[
            pl.BlockSpec((tm, D_MODEL), prompt_row),
            pl.BlockSpec((tm, D_MODEL), prompt_row),
            pl.BlockSpec((tm, D_MODEL), sample_row),
            pl.BlockSpec((tm, D_MODEL), sample_row),
            _const_spec((D_MODEL, D_MODEL), wo_layer),
            _const_spec((1, D_MODEL), layer),
            _const_spec((1, D_MODEL), layer),
            _const_spec((D_MODEL, D_FF), layer),
            _const_spec((D_MODEL, D_FF), layer),
            _const_spec((D_FF, D_MODEL), layer),
            _const_spec((1, D_MODEL), layer),
            _const_spec((1, D_MODEL), layer),
        ],
        out_specs=[pl.BlockSpec((tm, D_MODEL), prompt_row), pl.BlockSpec((tm, D_MODEL), sample_row)],
        out_shape=[jax.ShapeDtypeStruct(xp.shape, F32), jax.ShapeDtypeStruct(xs.shape, F32)],
        compiler_params=_params(("arbitrary",), 58),
        name="dense_flat",
    )(yp, xp, ys, xs, wo_stack, g1, b1, w1, w3, w2, g2, b2)


def _dense_block(yp, xp, ys, xs, wo_stack, wo_layer, layer, g1, b1, w1, w3, w2, g2, b2):
    if layer == 2:
        return _dense_block_flat(yp, xp, ys, xs, wo_stack, wo_layer, layer, g1, b1, w1, w3, w2, g2, b2)
    tm = DENSE_TILE
    n_prompt = xp.shape[0] // tm
    assert xp.shape[0] % tm == 0 and xs.shape[0] == tm and tm % DENSE_GROUP == 0
    prompt_in = lambda i: (jnp.minimum(i, n_prompt - 1), 0)
    prompt_out = lambda i: (jnp.clip(i - 1, 0, n_prompt - 1), 0)
    sample_row = lambda i: (0, 0)
    sample_spec = (pl.BlockSpec((tm, D_MODEL), sample_row) if layer == 0 else
                   pl.BlockSpec((tm, D_MODEL), sample_row, pipeline_mode=pl.Buffered(1)))
    return pl.pallas_call(
        _dense_kernel,
        grid=(n_prompt + 2,),
        in_specs=[
            pl.BlockSpec((tm, D_MODEL), prompt_in),
            pl.BlockSpec((tm, D_MODEL), prompt_in),
            sample_spec,
            sample_spec,
            _const_spec((D_MODEL, D_MODEL), wo_layer),
            _const_spec((1, D_MODEL), layer),
            _const_spec((1, D_MODEL), layer),
            _const_spec((D_MODEL, D_FF), layer),
            _const_spec((D_MODEL, D_FF), layer),
            _const_spec((D_FF, D_MODEL), layer),
            _const_spec((1, D_MODEL), layer),
            _const_spec((1, D_MODEL), layer),
        ],
        out_specs=[pl.BlockSpec((tm, D_MODEL), prompt_out), pl.BlockSpec((tm, D_MODEL), sample_row)],
        out_shape=[jax.ShapeDtypeStruct(xp.shape, F32), jax.ShapeDtypeStruct(xs.shape, F32)],
        scratch_shapes=[pltpu.VMEM((2, tm, D_MODEL), F32), pltpu.VMEM((2, tm, D_FF), BF16)],
        compiler_params=_params(("arbitrary",), 58),
        name="dense_block",
    )(yp, xp, ys, xs, wo_stack, g1, b1, w1, w3, w2, g2, b2)


def _lru_gates(xc, wblk_ref, ba, bx, sp):
    half = W_A // 2
    xb = xc.astype(BF16)
    pre0 = _dot(xb[:, :half], wblk_ref[0])
    pre1 = _dot(xb[:, half:], wblk_ref[1])
    r = jax.nn.sigmoid(jnp.concatenate([pre0[:, :half], pre1[:, :half]], axis=1) + ba)
    i = jax.nn.sigmoid(jnp.concatenate([pre0[:, half:], pre1[:, half:]], axis=1) + bx)
    log_a = (-LRU_C) * r * sp
    a = jnp.exp(log_a)
    mult = jnp.sqrt((1.0 + a * a) * jnp.tanh(-log_a))
    return a, mult, i


def _causal_taps(x_slabs, ext, b, w_ref, nrows):
    ktaps = w_ref.shape[0]
    out = []
    for s, x in enumerate(x_slabs):
        lanes = slice(s * LANES, (s + 1) * LANES)
        ext[s, b, SUBLANES:SUBLANES + nrows, :] = x
        acc = w_ref[ktaps - 1:ktaps, lanes] * x
        for k in range(ktaps - 1):
            off = SUBLANES - (ktaps - 1) + k
            acc = acc + w_ref[k:k + 1, lanes] * ext[s, b, off:off + nrows, :]
        out.append(acc)
    return jnp.concatenate(out, axis=1)


def _even_prompt_kernel(x_ref, w_ref, caw_ref, cab_ref, wblk_ref, ba_ref, bx_ref, lam_ref, cbw_ref,
                        y_ref, hlast_ref, ca_out_ref, sc_out_ref,
                        p0, p1, xa_ext, cb_ext, a_s, b_s, g_s, h_c):
    tt, pitch = EVEN_TT, EVEN_PITCH
    nslab = W_A // LANES
    j = pl.program_id(0)
    bufs = (p0, p1)
    group = p0.shape[0] // tt

    def project(g):
        xg = x_ref[g * group:(g + 1) * group].reshape(group * tt, D_MODEL)
        bufs[g % 2][...] = _dot(xg.astype(BF16), w_ref[...])

    @pl.when(j == 0)
    def _():
        xa_ext[:, :, 0:SUBLANES, :] = jnp.zeros((nslab, BATCH, SUBLANES, LANES), F32)
        cb_ext[:, :, 0:SUBLANES, :] = jnp.zeros((nslab, BATCH, SUBLANES, LANES), F32)
        h_c[...] = jnp.zeros(h_c.shape, F32)

    @pl.when(j > 0)
    def _():
        xa_ext[:, :, 0:SUBLANES, :] = xa_ext[:, :, tt:tt + SUBLANES, :]
        cb_ext[:, :, 0:SUBLANES, :] = cb_ext[:, :, tt:tt + SUBLANES, :]

    sp = _softplus(-lam_ref[...])
    ba = ba_ref[...]
    bx = bx_ref[...]
    cab = cab_ref[...]
    first_row = (lax.broadcasted_iota(jnp.int32, (SUBLANES, 1), 0) == 0) & (j == 0)
    cg0 = 2 * W_A + W_B
    xb0 = 2 * W_A + 2 * W_B

    def stage(b):
        p_ref = bufs[(b // group) % 2].at[(b % group) * tt:(b % group + 1) * tt]
        xc = cab + _causal_taps([p_ref[:, s * LANES:(s + 1) * LANES] for s in range(nslab)], xa_ext, b, caw_ref, tt)
        a, mult, gate = _lru_gates(xc, wblk_ref, ba, bx, sp)
        mult = jnp.concatenate([jnp.where(first_row, 1.0, mult[:SUBLANES]), mult[SUBLANES:]], axis=0)
        bb = mult * (gate * xc)
        for s in range(nslab):
            a_s[s, b * pitch:b * pitch + tt, :] = a[:, s * LANES:(s + 1) * LANES]
            b_s[s, b * pitch:b * pitch + tt, :] = bb[:, s * LANES:(s + 1) * LANES]
        g_s[b] = jax.nn.gelu(p_ref[:, W_A:2 * W_A])
        cb = [p_ref[:, cg0 + s * LANES:cg0 + (s + 1) * LANES] * p_ref[:, xb0 + s * LANES:xb0 + (s + 1) * LANES]
              for s in range(nslab)]
        zb = _causal_taps(cb, cb_ext, b, cbw_ref, tt)
        y_ref[b, :, W_A:] = (p_ref[:, 2 * W_A:2 * W_A + W_B] * zb).astype(BF16)

    project(0)
    for g in range(BATCH // group):
        if g + 1 < BATCH // group:
            project(g + 1)
        for b in range(g * group, (g + 1) * group):
            stage(b)

    def step(t, hs):
        out = []
        for s in range(nslab):
            rows = pl.ds(t, BATCH, stride=pitch)
            hn = a_s[s, rows, :] * hs[s] + b_s[s, rows, :]
            b_s[s, rows, :] = hn
            out.append(hn)
        return tuple(out)

    hs = lax.fori_loop(0, tt, step, tuple(h_c[s] for s in range(nslab)), unroll=4)
    for s in range(nslab):
        h_c[s] = hs[s]

    def emit(b, carry):
        row0 = pl.multiple_of(b * pitch, SUBLANES)
        h = jnp.concatenate([b_s[s, pl.ds(row0, tt), :] for s in range(nslab)], axis=1)
        y_ref[b, :, 0:W_A] = (h * g_s[b]).astype(BF16)
        return carry

    lax.fori_loop(0, BATCH, emit, 0)

    @pl.when(j == pl.num_programs(0) - 1)
    def _():
        hlast_ref[...] = jnp.concatenate(list(hs), axis=1)
        for s in range(nslab):
            ca_out_ref[:, :, s * LANES:(s + 1) * LANES] = xa_ext[s, :, tt:tt + SUBLANES, :]
            sc_out_ref[:, :, s * LANES:(s + 1) * LANES] = cb_ext[s, :, tt:tt + SUBLANES, :]


def _even_param_specs(e):
    return [
        _const_spec((A_CONV, W_A), e),
        _const_spec((1, W_A), e),
        _const_spec((2, W_A // 2, W_A), e),
        _const_spec((1, W_A), e),
        _const_spec((1, W_A), e),
        _const_spec((1, W_A), e),
        _const_spec((B_CONV, W_B), e),
    ]


def _even_prompt(x3d, w_stack, e, params):
    tt, pitch = EVEN_TT, EVEN_PITCH
    nslab = W_A // LANES
    assert W_A == W_B
    return pl.pallas_call(
        _even_prompt_kernel,
        grid=(SEQ // tt,),
        in_specs=[pl.BlockSpec((BATCH, tt, D_MODEL), lambda j: (0, j, 0)),
                  _const_spec((D_MODEL, EVEN_IN), e)] + _even_param_specs(e),
        out_specs=[
            pl.BlockSpec((BATCH, tt, D_MODEL), lambda j: (0, j, 0)),
            pl.BlockSpec((BATCH, W_A), lambda j: (0, 0)),
            pl.BlockSpec((BATCH, SUBLANES, W_A), lambda j: (0, 0, 0)),
            pl.BlockSpec((BATCH, SUBLANES, W_B), lambda j: (0, 0, 0)),
        ],
        out_shape=[
            jax.ShapeDtypeStruct((BATCH, SEQ, D_MODEL), BF16),
            jax.ShapeDtypeStruct((BATCH, W_A), F32),
            jax.ShapeDtypeStruct((BATCH, SUBLANES, W_A), F32),
            jax.ShapeDtypeStruct((BATCH, SUBLANES, W_B), F32),
        ],
        scratch_shapes=[
            pltpu.VMEM(((1 + e) * tt, EVEN_IN), F32),
            pltpu.VMEM(((1 + e) * tt, EVEN_IN), F32),
            pltpu.VMEM((nslab, BATCH, tt + SUBLANES, LANES), F32),
            pltpu.VMEM((nslab, BATCH, tt + SUBLANES, LANES), F32),
            pltpu.VMEM((nslab, BATCH * pitch, LANES), F32),
            pltpu.VMEM((nslab, BATCH * pitch, LANES), F32),
            pltpu.VMEM((BATCH, tt, W_A), F32),
            pltpu.VMEM((nslab, BATCH, LANES), F32),
        ],
        compiler_params=_params(("arbitrary",), 48),
        name="even_prompt",
    )(x3d, w_stack, *params)


def _even_sample_kernel(proj_ref, ca_ref, sc_ref, h0_ref, caw_ref, cab_ref, wblk_ref, ba_ref, bx_ref,
                        lam_ref, cbw_ref, y_ref, hlast_ref, ca_out_ref, sc_out_ref):
    sp = _softplus(-lam_ref[...])
    rows_a = [ca_ref[k] for k in range(A_CONV - 1)] + [proj_ref[l, :, 0:W_A] for l in range(DEC_SEQ)]
    xc = []
    for l in range(DEC_SEQ):
        acc = cab_ref[...] + caw_ref[0:1, :] * rows_a[l]
        for k in range(1, A_CONV):
            acc = acc + caw_ref[k:k + 1, :] * rows_a[l + k]
        xc.append(acc)
    xc_all = jnp.concatenate(xc, axis=0)
    a, mult, gate = _lru_gates(xc_all, wblk_ref, ba_ref[...], bx_ref[...], sp)
    bb = mult * (gate * xc_all)
    h = h0_ref[...]
    rows_b = [sc_ref[k] for k in range(B_CONV - 1)]
    for l in range(DEC_SEQ):
        sl = slice(l * DEC_BATCH, (l + 1) * DEC_BATCH)
        h = a[sl] * h + bb[sl]
        y_ref[l, :, 0:W_A] = (h * jax.nn.gelu(proj_ref[l, :, W_A:2 * W_A])).astype(BF16)
        rows_b.append(proj_ref[l, :, 2 * W_A + W_B:2 * W_A + 2 * W_B] * proj_ref[l, :, 2 * W_A + 2 * W_B:])
    hlast_ref[...] = h
    for l in range(DEC_SEQ):
        zb = cbw_ref[0:1, :] * rows_b[l]
        for k in range(1, B_CONV):
            zb = zb + cbw_ref[k:k + 1, :] * rows_b[l + k]
        y_ref[l, :, W_A:] = (proj_ref[l, :, 2 * W_A:2 * W_A + W_B] * zb).astype(BF16)
    for k in range(A_CONV - 1):
        ca_out_ref[k] = rows_a[DEC_SEQ + k]
    for k in range(B_CONV - 1):
        sc_out_ref[k] = rows_b[DEC_SEQ + k]


def _even_sample(proj_tm, ca_tm, sc_tm, h0_tm, e, params):
    whole = lambda shape: pl.BlockSpec(shape, lambda i: (0,) * len(shape))
    return pl.pallas_call(
        _even_sample_kernel,
        grid=(1,),
        in_specs=[
            whole((DEC_SEQ, DEC_BATCH, EVEN_IN)),
            _const_spec((A_CONV - 1, DEC_BATCH, W_A), e),
            _const_spec((B_CONV - 1, DEC_BATCH, W_B), e),
            _const_spec((DEC_BATCH, W_A), e),
        ] + _even_param_specs(e),
        out_specs=[
            whole((DEC_SEQ, DEC_BATCH, D_MODEL)),
            whole((DEC_BATCH, W_A)),
            whole((A_CONV - 1, DEC_BATCH, W_A)),
            whole((B_CONV - 1, DEC_BATCH, W_B)),
        ],
        out_shape=[
            jax.ShapeDtypeStruct((DEC_SEQ, DEC_BATCH, D_MODEL), BF16),
            jax.ShapeDtypeStruct((DEC_BATCH, W_A), F32),
            jax.ShapeDtypeStruct((A_CONV - 1, DEC_BATCH, W_A), F32),
            jax.ShapeDtypeStruct((B_CONV - 1, DEC_BATCH, W_B), F32),
        ],
        compiler_params=_params(("arbitrary",), 40),
        name="even_sample",
    )(proj_tm, ca_tm, sc_tm, h0_tm, *params)


def _rope(x, cos, sin_signed):
    half = C_DK // 2
    ax = x.ndim - 1
    pieces = []
    for p in range(W_CK // LANES):
        xv = x[..., p * LANES:(p + 1) * LANES]
        lane = lax.broadcasted_iota(jnp.int32, xv.shape, ax)
        partner = jnp.where((lane % C_DK) < half, pltpu.roll(xv, LANES - half, axis=ax), pltpu.roll(xv, half, axis=ax))
        pieces.append(xv * cos[..., p * LANES:(p + 1) * LANES] + partner * sin_signed[..., p * LANES:(p + 1) * LANES])
    return jnp.concatenate(pieces, axis=ax)


def _group_norm(o):
    mu = jnp.mean(o, -1, keepdims=True)
    oc = o - mu
    var = jnp.mean(oc * oc, -1, keepdims=True)
    return oc * lax.rsqrt(var + LN_EPS)


def _odd_chunk(p_ref, b, cos_ref, sin_ref, dec_ref, cs_ref, kd_ref, sdec_ref, gn_ref,
               spw_ref, spb_ref, lng_ref, lnb_ref, y_ref, s_c):
    q = _rope(p_ref[:, 0:W_CK], cos_ref[...], sin_ref[...])
    k = _rope(p_ref[:, W_CK:2 * W_CK], cos_ref[...], sin_ref[...]) * (C_DK ** -0.5)
    kd = k * kd_ref[...]
    v0 = 2 * W_CK
    g0 = v0 + W_CV
    for h in range(C_HEADS):
        qh = q[:, h * C_DK:(h + 1) * C_DK].astype(BF16)
        kh = k[:, h * C_DK:(h + 1) * C_DK].astype(BF16)
        kdh = kd[:, h * C_DK:(h + 1) * C_DK].astype(BF16)
        vh = p_ref[:, v0 + h * C_DV:v0 + (h + 1) * C_DV].astype(BF16)
        s_prev = s_c[b, h]
        scores = lax.dot_general(qh, kh, (((1,), (1,)), ((), ())), preferred_element_type=F32) * dec_ref[h]
        o = _dot(scores.astype(BF16), vh) + _dot(qh, s_prev.astype(BF16)) * cs_ref[:, h * C_DV:(h + 1) * C_DV]
        s_c[b, h] = sdec_ref[h] * s_prev + lax.dot_general(kdh, vh, (((0,), (0,)), ((), ())), preferred_element_type=F32)
        gate = jax.nn.silu(p_ref[:, g0 + h * C_DV:g0 + (h + 1) * C_DV])
        y_ref[b, :, h * C_DV:(h + 1) * C_DV] = (gate * (_group_norm(o) * gn_ref[:, h * C_DV:(h + 1) * C_DV])).astype(BF16)

    u0 = g0 + W_CV
    u = jax.nn.gelu(p_ref[:, u0:u0 + W_D])
    vd = _layer_norm(jax.nn.gelu(p_ref[:, u0 + W_D:]), lng_ref[...], lnb_ref[...]).astype(BF16)
    ri = lax.broadcasted_iota(jnp.int32, (D_CHUNK, D_CHUNK), 0)
    ci = lax.broadcasted_iota(jnp.int32, (D_CHUNK, D_CHUNK), 1)
    for gi in range(D_GROUPS):
        w = jnp.where(ri >= ci, spw_ref[gi], 0.0).astype(BF16)
        s = _dot(w, vd[:, gi * D_GROUP:(gi + 1) * D_GROUP]) + spb_ref[:, gi:gi + 1]
        y_ref[b, :, W_CV + gi * D_GROUP:W_CV + (gi + 1) * D_GROUP] = (u[:, gi * D_GROUP:(gi + 1) * D_GROUP] * s).astype(BF16)


def _odd_prompt_kernel(x_ref, w_ref, cos_ref, sin_ref, dec_ref, cs_ref, kd_ref, sdec_ref, gn_ref,
                       spw_ref, spb_ref, lng_ref, lnb_ref, y_ref, s_out_ref, p0, p1, s_c):
    c = pl.program_id(0)

    @pl.when(c == 0)
    def _():
        s_c[...] = jnp.zeros(s_c.shape, F32)

    bufs = (p0, p1)
    group = p0.shape[0] // RET_CHUNK

    def project(g):
        xg = x_ref[g * group:(g + 1) * group].reshape(group * RET_CHUNK, D_MODEL)
        bufs[g % 2][...] = _dot(xg.astype(BF16), w_ref[...])

    project(0)
    for g in range(BATCH // group):
        if g + 1 < BATCH // group:
            project(g + 1)
        for r in range(group):
            _odd_chunk(bufs[g % 2].at[r * RET_CHUNK:(r + 1) * RET_CHUNK], g * group + r,
                       cos_ref, sin_ref, dec_ref, cs_ref, kd_ref, sdec_ref, gn_ref,
                       spw_ref, spb_ref, lng_ref, lnb_ref, y_ref, s_c)

    @pl.when(c == pl.num_programs(0) - 1)
    def _():
        s_out_ref[...] = s_c[...]


def _odd_prompt(x3d, w_stack, o, cos, sin, tabs, gn, spw, spb_t, lng, lnb):
    nchunk = SEQ // RET_CHUNK
    dec, cs, kd, sdec = tabs
    return pl.pallas_call(
        _odd_prompt_kernel,
        grid=(nchunk,),
        in_specs=[
            pl.BlockSpec((BATCH, RET_CHUNK, D_MODEL), lambda c: (0, c, 0)),
            _const_spec((D_MODEL, ODD_IN), o),
            pl.BlockSpec((RET_CHUNK, W_CK), lambda c: (c, 0)),
            pl.BlockSpec((RET_CHUNK, W_CK), lambda c: (c, 0)),
            _const_spec((C_HEADS, RET_CHUNK, RET_CHUNK)),
            _const_spec((RET_CHUNK, W_CV)),
            _const_spec((RET_CHUNK, W_CK)),
            _const_spec((C_HEADS, C_DK, C_DV)),
            _const_spec((1, W_CV), o),
            _const_spec((D_GROUPS, D_CHUNK, D_CHUNK), o),
            _const_spec((D_CHUNK, D_GROUPS), o),
            _const_spec((1, W_D), o),
            _const_spec((1, W_D), o),
        ],
        out_specs=[
            pl.BlockSpec((BATCH, RET_CHUNK, D_MODEL), lambda c: (0, c, 0)),
            pl.BlockSpec((BATCH, C_HEADS, C_DK, C_DV), lambda c: (0, 0, 0, 0)),
        ],
        out_shape=[
            jax.ShapeDtypeStruct((BATCH, SEQ, D_MODEL), BF16),
            jax.ShapeDtypeStruct((BATCH, C_HEADS, C_DK, C_DV), F32),
        ],
        scratch_shapes=[
            pltpu.VMEM(((1 + o) * RET_CHUNK, ODD_IN), F32),
            pltpu.VMEM(((1 + o) * RET_CHUNK, ODD_IN), F32),
            pltpu.VMEM((BATCH, C_HEADS, C_DK, C_DV), F32),
        ],
        compiler_params=_params(("arbitrary",), 48),
        name="odd_prompt",
    )(x3d, w_stack, cos, sin, dec, cs, kd, sdec, gn, spw, spb_t, lng, lnb)


def _odd_sample_kernel(proj_ref, s0_ref, cos_ref, sin_ref, dec_ref, cs_ref, kd_ref, sdec_ref, gn_ref,
                       spw_ref, spb_ref, lng_ref, lnb_ref, *rest):
    y_ref, s_out_ref, vd_ref = rest[-3:]
    if len(rest) > 3:
        s_out_ref[:, 0] = rest[0][...]
        s_out_ref = s_out_ref.at[:, 1]
    q = _rope(proj_ref[:, :, 0:W_CK], cos_ref[...], sin_ref[...])
    k = _rope(proj_ref[:, :, W_CK:2 * W_CK], cos_ref[...], sin_ref[...]) * (C_DK ** -0.5)
    kd = k * kd_ref[...]
    v0 = 2 * W_CK
    g0 = v0 + W_CV
    for h in range(C_HEADS):
        qh = q[:, :, h * C_DK:(h + 1) * C_DK].astype(BF16)
        kh = k[:, :, h * C_DK:(h + 1) * C_DK].astype(BF16)
        kdh = kd[:, :, h * C_DK:(h + 1) * C_DK].astype(BF16)
        vh = proj_ref[:, :, v0 + h * C_DV:v0 + (h + 1) * C_DV].astype(BF16)
        s_prev = s0_ref[:, h]
        scores = jnp.einsum('nld,nmd->nlm', qh, kh, preferred_element_type=F32) * dec_ref[h]
        o = (jnp.einsum('nlm,nme->nle', scores.astype(BF16), vh, preferred_element_type=F32)
             + jnp.einsum('nld,nde->nle', qh, s_prev.astype(BF16), preferred_element_type=F32)
             * cs_ref[:, h * C_DV:(h + 1) * C_DV])
        s_out_ref[:, h] = sdec_ref[h] * s_prev + jnp.einsum('nld,nle->nde', kdh, vh, preferred_element_type=F32)
        gate = jax.nn.silu(proj_ref[:, :, g0 + h * C_DV:g0 + (h + 1) * C_DV])
        y_ref[:, :, h * C_DV:(h + 1) * C_DV] = (gate * (_group_norm(o) * gn_ref[:, h * C_DV:(h + 1) * C_DV])).astype(BF16)

    u0 = g0 + W_CV
    u = jax.nn.gelu(proj_ref[:, :, u0:u0 + W_D])
    vd = _layer_norm(jax.nn.gelu(proj_ref[:, :, u0 + W_D:]), lng_ref[...], lnb_ref[...])
    vd_ref[...] = vd
    s = spb_ref[...] + spw_ref[0] * vd[:, 0:1, :]
    for m in range(1, DEC_SEQ):
        s = s + spw_ref[m] * vd[:, m:m + 1, :]
    y_ref[:, :, W_CV:] = (u * s).astype(BF16)


def _odd_sample(proj, state_ret, o, cos, sin, tabs, gn, spw_rows, spb_rows, lng, lnb, prev_states):
    nb = SAMPLE_NB
    dec, cs, kd, sdec = tabs
    seq3 = lambda i: (i, 0, 0)
    one_state = pl.BlockSpec((nb, C_HEADS, C_DK, C_DV), lambda i: (i, 0, 0, 0))
    in_specs = [
        pl.BlockSpec((nb, DEC_SEQ, ODD_IN), seq3),
        pl.BlockSpec((nb, None, C_HEADS, C_DK, C_DV), lambda i: (i, o, 0, 0, 0)),
        _const_spec((DEC_SEQ, W_CK)),
        _const_spec((DEC_SEQ, W_CK)),
        _const_spec((C_HEADS, DEC_SEQ, DEC_SEQ)),
        _const_spec((DEC_SEQ, W_CV)),
        _const_spec((DEC_SEQ, W_CK)),
        _const_spec((C_HEADS, C_DK, C_DV)),
        _const_spec((1, W_CV), o),
        _const_spec((DEC_SEQ, DEC_SEQ, W_D), o),
        _const_spec((DEC_SEQ, W_D), o),
        _const_spec((1, W_D), o),
        _const_spec((1, W_D), o),
    ]
    args = [proj, state_ret, cos, sin, dec, cs, kd, sdec, gn, spw_rows, spb_rows, lng, lnb]
    if prev_states is None:
        state_spec = one_state
        state_shape = (DEC_BATCH, C_HEADS, C_DK, C_DV)
    else:
        in_specs.append(one_state)
        args.append(prev_states)
        state_spec = pl.BlockSpec((nb, N_ODD, C_HEADS, C_DK, C_DV), lambda i: (i, 0, 0, 0, 0))
        state_shape = (DEC_BATCH, N_ODD, C_HEADS, C_DK, C_DV)
    return pl.pallas_call(
        _odd_sample_kernel,
        grid=(DEC_BATCH // nb,),
        in_specs=in_specs,
        out_specs=[pl.BlockSpec((nb, DEC_SEQ, D_MODEL), seq3), state_spec, pl.BlockSpec((nb, DEC_SEQ, W_D), seq3)],
        out_shape=[
            jax.ShapeDtypeStruct((DEC_BATCH, DEC_SEQ, D_MODEL), BF16),
            jax.ShapeDtypeStruct(state_shape, F32),
            jax.ShapeDtypeStruct((DEC_BATCH, DEC_SEQ, W_D), F32),
        ],
        compiler_params=_params(("parallel",), 48),
        name="odd_sample",
    )(*args)


def _rope_tables(pos):
    half = C_DK // 2
    freq = ROPE_BASE ** (-np.arange(half, dtype=np.float64) / half)
    ang = np.asarray(pos, np.float64)[:, None] * freq
    cos, sin = np.cos(ang), np.sin(ang)
    cos_l = np.tile(np.concatenate([cos, cos], axis=-1), (1, C_HEADS))
    sin_l = np.tile(np.concatenate([-sin, sin], axis=-1), (1, C_HEADS))
    return cos_l.astype(np.float32), sin_l.astype(np.float32)


def _retention_tables(length):
    log_g = np.log1p(-np.exp2(-5.0 - np.arange(C_HEADS, dtype=np.float64)))
    idx = np.arange(length, dtype=np.float64)
    rel = idx[:, None] - idx[None, :]
    decay = np.where(rel >= 0, np.exp(log_g[:, None, None] * np.maximum(rel, 0.0)), 0.0)
    cross = np.exp(log_g[None, :] * (idx[:, None] + 1.0))
    kdec = np.exp(log_g[None, :] * (length - 1.0 - idx[:, None]))
    sdec = np.exp(log_g * length)
    tabs = (decay,
            np.repeat(cross, C_DV, axis=1),
            np.repeat(kdec, C_DK, axis=1),
            np.broadcast_to(sdec[:, None, None], (C_HEADS, C_DK, C_DV)))
    return tuple(np.ascontiguousarray(t, dtype=np.float32) for t in tabs)


def _block_diag_halves(wa, wx):
    nb = A_BLOCKS // 2
    width = nb * A_BLOCK

    def bd(w, hh):
        rows = [jnp.pad(w[:, hh * nb + g], ((0, 0), (0, 0), (g * A_BLOCK, width - (g + 1) * A_BLOCK)))
                for g in range(nb)]
        return jnp.concatenate(rows, axis=1)

    halves = [jnp.concatenate([bd(wa, hh), bd(wx, hh)], axis=-1) for hh in range(2)]
    return jnp.stack(halves, axis=1).astype(BF16)


def kernel(x_prompt, x_sample, state_lru_h, state_lru_conv, state_sconv, state_ret, w_in_even, conv_a_w, conv_a_b, lru_wa, lru_ba, lru_wx, lru_bx, lru_lam, conv_b_w, w_out_even, w_in_odd, ret_gn_g, sp_w, sp_b, gm_ln_g, gm_ln_b, w_out_odd, ffn_w1, ffn_w3, ffn_w2, ln1_g, ln1_b, ln2_g, ln2_b):
    rows = lambda v: v.reshape(v.shape[0], 1, v.shape[1])
    xp = x_prompt.reshape(BATCH * SEQ, D_MODEL)
    xs = jnp.swapaxes(x_sample, 0, 1).reshape(DEC_SEQ * DEC_BATCH, D_MODEL)

    cos_p, sin_p = _rope_tables(np.arange(SEQ))
    cos_s, sin_s = _rope_tables(PAST_LEN + np.arange(DEC_SEQ))
    tab_p = _retention_tables(RET_CHUNK)
    tab_s = _retention_tables(DEC_SEQ)

    w_in_even_b, w_out_even_b = w_in_even.astype(BF16), w_out_even.astype(BF16)
    w_in_odd_b, w_out_odd_b = w_in_odd.astype(BF16), w_out_odd.astype(BF16)
    dense_p = (rows(ln1_g), rows(ln1_b), ffn_w1.astype(BF16), ffn_w3.astype(BF16), ffn_w2.astype(BF16),
               rows(ln2_g), rows(ln2_b))
    even_p = (conv_a_w, rows(conv_a_b), _block_diag_halves(lru_wa, lru_wx), rows(lru_ba), rows(lru_bx),
              rows(lru_lam), conv_b_w)
    ca_tm = jnp.transpose(state_lru_conv, (1, 2, 0, 3))
    sc_tm = jnp.transpose(state_sconv, (1, 2, 0, 3))
    h0_tm = jnp.swapaxes(state_lru_h, 0, 1)
    tril = jnp.tril(sp_w[:, :, :DEC_SEQ, :DEC_SEQ])
    spw_rows = jnp.repeat(jnp.transpose(tril, (0, 3, 2, 1)), D_GROUP, axis=3)
    spb_rows = jnp.repeat(jnp.swapaxes(sp_b[:, :, :DEC_SEQ], 1, 2), D_GROUP, axis=2)
    spb_t = jnp.swapaxes(sp_b, 1, 2)
    odd_small = (rows(ret_gn_g),)
    odd_ln = (rows(gm_ln_g), rows(gm_ln_b))

    h_p, h_s, ca_p, ca_s, sc_p, sc_s, r_p, v_s = [], [], [], [], [], [], [], []
    ret_sample = None
    for l in range(DEPTH):
        if l % 2 == 0:
            e = l // 2
            wo_stack, wo_layer = w_out_even_b, e
            yp, hlp, cap, scp = _even_prompt(xp.reshape(BATCH, SEQ, D_MODEL), w_in_even_b, e, even_p)
            proj_s = _in_proj(xs, w_in_even_b, e).reshape(DEC_SEQ, DEC_BATCH, EVEN_IN)
            ys, hls, cas, scs = _even_sample(proj_s, ca_tm, sc_tm, h0_tm, e, even_p)
            h_p.append(hlp)
            h_s.append(hls)
            ca_p.append(cap[:, SUBLANES - (A_CONV - 1):])
            ca_s.append(jnp.swapaxes(cas, 0, 1))
            sc_p.append(scp[:, SUBLANES - (B_CONV - 1):])
            sc_s.append(jnp.swapaxes(scs, 0, 1))
            yp = yp.reshape(BATCH * SEQ, D_MODEL)
            ys = ys.reshape(DEC_SEQ * DEC_BATCH, D_MODEL)
        else:
            o = l // 2
            wo_stack, wo_layer = w_out_odd_b, o
            yp, rp = _odd_prompt(xp.reshape(BATCH, SEQ, D_MODEL), w_in_odd_b, o, cos_p, sin_p, tab_p,
                                 *odd_small, sp_w, spb_t, *odd_ln)
            proj_s = _in_proj(xs, w_in_odd_b, o).reshape(DEC_BATCH, DEC_SEQ, ODD_IN)
            ys, ret_sample, vs = _odd_sample(proj_s, state_ret, o, cos_s, sin_s, tab_s, *odd_small,
                                             spw_rows, spb_rows, *odd_ln, ret_sample)
            r_p.append(rp)
            v_s.append(vs)
            yp = yp.reshape(BATCH * SEQ, D_MODEL)
            ys = ys.reshape(DEC_BATCH * DEC_SEQ, D_MODEL)
        xp, xs = _dense_block(yp, xp, ys, xs, wo_stack, wo_layer, l, *dense_p)
        if l + 1 < DEPTH:
            if l % 2 == 0:
                xs = jnp.swapaxes(xs.reshape(DEC_SEQ, DEC_BATCH, D_MODEL), 0, 1).reshape(DEC_BATCH * DEC_SEQ, D_MODEL)
            else:
                xs = jnp.swapaxes(xs.reshape(DEC_BATCH, DEC_SEQ, D_MODEL), 0, 1).reshape(DEC_SEQ * DEC_BATCH, D_MODEL)

    return (xp.reshape(BATCH, SEQ, D_MODEL), xs.reshape(DEC_BATCH, DEC_SEQ, D_MODEL),
            jnp.stack(h_p, axis=1), jnp.stack(h_s, axis=1),
            jnp.stack(ca_p, axis=1), jnp.stack(ca_s, axis=1),
            jnp.stack(sc_p, axis=1), jnp.stack(sc_s, axis=1),
            jnp.stack(r_p, axis=1), ret_sample, jnp.stack(v_s, axis=1))
```

```python
import jax
import jax.numpy as jnp
import numpy as np
from jax import lax
from jax.experimental import pallas as pl
from jax.experimental.pallas import tpu as pltpu

F32 = jnp.float32
BF16 = jnp.bfloat16

D_MODEL = 1024
BATCH = 8
SEQ = 2048
DEPTH = 4
DEC_BATCH = 128
DEC_SEQ = 4
PAST_LEN = 16384
N_ODD = DEPTH // 2
W_A = 512
A_BLOCKS = 8
A_BLOCK = 64
A_CONV = 4
LRU_C = 8.0
W_B = 512
B_CONV = 3
C_HEADS = 4
C_DK = 64
C_DV = 128
W_CK = C_HEADS * C_DK
W_CV = C_HEADS * C_DV
RET_CHUNK = 128
ROPE_BASE = 10000.0
D_GROUPS = 4
D_CHUNK = 128
W_D = 512
D_GROUP = W_D // D_GROUPS
EVEN_IN = 2 * W_A + 3 * W_B
ODD_IN = 2 * W_CK + 2 * W_CV + 2 * W_D
D_FF = 2816
ALPHA = (2 * DEPTH) ** 0.25
LN_EPS = 1e-5

SUBLANES = 8
LANES = 128
MIB = 1024 * 1024

ROW_TILE = 512
DENSE_TILE = 512
DENSE_GROUP = 256
EVEN_TT = 128
EVEN_PITCH = EVEN_TT + SUBLANES
SAMPLE_NB = 32


def _params(sem, vmem_mib):
    return pltpu.CompilerParams(dimension_semantics=sem, vmem_limit_bytes=vmem_mib * MIB)


def _const_spec(shape, layer=None):
    nd = len(shape)
    if layer is None:
        return pl.BlockSpec(shape, lambda *_: (0,) * nd, pipeline_mode=pl.Buffered(1))
    return pl.BlockSpec((None,) + tuple(shape), lambda *_: (layer,) + (0,) * nd, pipeline_mode=pl.Buffered(1))


def _layer_norm(x, g, b):
    mu = jnp.mean(x, -1, keepdims=True)
    xc = x - mu
    var = jnp.mean(xc * xc, -1, keepdims=True)
    return xc * lax.rsqrt(var + LN_EPS) * g + b


def _softplus(x):
    return jnp.maximum(x, 0.0) + jnp.log1p(jnp.exp(-jnp.abs(x)))


def _dot(a, b):
    return jnp.dot(a, b, preferred_element_type=F32)


def _proj_kernel(x_ref, w_ref, o_ref):
    o_ref[...] = _dot(x_ref[...].astype(BF16), w_ref[...])


def _in_proj(x2d, w_stack, layer):
    m, k = x2d.shape
    n = w_stack.shape[2]
    tm = min(ROW_TILE, m)
    return pl.pallas_call(
        _proj_kernel,
        grid=(m // tm,),
        in_specs=[pl.BlockSpec((tm, k), lambda i: (i, 0)), _const_spec((k, n), layer)],
        out_specs=pl.BlockSpec((tm, n), lambda i: (i, 0)),
        out_shape=jax.ShapeDtypeStruct((m, n), F32),
        compiler_params=_params(("parallel",), 40),
        name="in_proj",
    )(x2d, w_stack)


def _dense_kernel(yp_ref, xp_ref, ys_ref, xs_ref, wo_ref, g1_ref, b1_ref, w1_ref, w3_ref, w2_ref, g2_ref, b2_ref,
                  op_ref, os_ref):
    n_prompt = pl.num_programs(0) - 1
    weights = (wo_ref, g1_ref, b1_ref, w1_ref, w3_ref, w2_ref, g2_ref, b2_ref)

    @pl.when(pl.program_id(0) < n_prompt)
    def _():
        _dense_rows(yp_ref, xp_ref, *weights, op_ref)

    @pl.when(pl.program_id(0) == n_prompt)
    def _():
        _dense_rows(ys_ref, xs_ref, *weights, os_ref)


def _dense_rows(y_ref, x_ref, wo_ref, g1_ref, b1_ref, w1_ref, w3_ref, w2_ref, g2_ref, b2_ref, o_ref):
    rows = DENSE_GROUP
    nparts = y_ref.shape[0] // rows
    sls = [slice(p * rows, (p + 1) * rows) for p in range(nparts)]
    mix, x1, xb, h = {}, {}, {}, {}

    def out_proj(p):
        mix[p] = _dot(y_ref[sls[p], :], wo_ref[...])

    def norm1(p):
        x1[p] = _layer_norm(ALPHA * x_ref[sls[p], :] + mix[p], g1_ref[...], b1_ref[...])
        xb[p] = x1[p].astype(BF16)

    def gate_up(p):
        h[p] = (jax.nn.silu(_dot(xb[p], w1_ref[...])) * _dot(xb[p], w3_ref[...])).astype(BF16)

    def down_norm2(p):
        f = _dot(h[p], w2_ref[...])
        o_ref[sls[p], :] = _layer_norm(ALPHA * x1[p] + f, g2_ref[...], b2_ref[...])

    out_proj(0)
    for p in range(nparts):
        norm1(p)
        if p + 1 < nparts:
            out_proj(p + 1)
    for p in range(nparts):
        gate_up(p)
    for p in range(nparts):
        down_norm2(p)


def _dense_block(yp, xp, ys, xs, wo_stack, wo_layer, layer, g1, b1, w1, w3, w2, g2, b2):
    tm = DENSE_TILE
    n_prompt = xp.shape[0] // tm
    assert xp.shape[0] % tm == 0 and xs.shape[0] == tm and tm % DENSE_GROUP == 0
    prompt_row = lambda i: (jnp.minimum(i, n_prompt - 1), 0)
    sample_row = lambda i: (0, 0)
    return pl.pallas_call(
        _dense_kernel,
        grid=(n_prompt + 1,),
        in_specs=[
            pl.BlockSpec((tm, D_MODEL), prompt_row),
            pl.BlockSpec((tm, D_MODEL), prompt_row),
            pl.BlockSpec((tm, D_MODEL), sample_row),
            pl.BlockSpec((tm, D_MODEL), sample_row),
            _const_spec((D_MODEL, D_MODEL), wo_layer),
            _const_spec((1, D_MODEL), layer),
            _const_spec((1, D_MODEL), layer),
            _const_spec((D_MODEL, D_FF), layer),
            _const_spec((D_MODEL, D_FF), layer),
            _const_spec((D_FF, D_MODEL), layer),
            _const_spec((1, D_MODEL), layer),
            _const_spec((1, D_MODEL), layer),
        ],
        out_specs=[pl.BlockSpec((tm, D_MODEL), prompt_row), pl.BlockSpec((tm, D_MODEL), sample_row)],
        out_shape=[jax.ShapeDtypeStruct(xp.shape, F32), jax.ShapeDtypeStruct(xs.shape, F32)],
        compiler_params=_params(("arbitrary",), 58),
        name="dense_block",
    )(yp, xp, ys, xs, wo_stack, g1, b1, w1, w3, w2, g2, b2)


def _lru_gates(xc, wblk_ref, ba, bx, sp):
    half = W_A // 2
    xb = xc.astype(BF16)
    pre0 = _dot(xb[:, :half], wblk_ref[0])
    pre1 = _dot(xb[:, half:], wblk_ref[1])
    r = jax.nn.sigmoid(jnp.concatenate([pre0[:, :half], pre1[:, :half]], axis=1) + ba)
    i = jax.nn.sigmoid(jnp.concatenate([pre0[:, half:], pre1[:, half:]], axis=1) + bx)
    log_a = (-LRU_C) * r * sp
    a = jnp.exp(log_a)
    mult = jnp.sqrt((1.0 + a * a) * jnp.tanh(-log_a))
    return a, mult, i


def _causal_taps(x_slabs, ext, b, w_ref, nrows):
    ktaps = w_ref.shape[0]
    out = []
    for s, x in enumerate(x_slabs):
        lanes = slice(s * LANES, (s + 1) * LANES)
        ext[s, b, SUBLANES:SUBLANES + nrows, :] = x
        acc = w_ref[ktaps - 1:ktaps, lanes] * x
        for k in range(ktaps - 1):
            off = SUBLANES - (ktaps - 1) + k
            acc = acc + w_ref[k:k + 1, lanes] * ext[s, b, off:off + nrows, :]
        out.append(acc)
    return jnp.concatenate(out, axis=1)


def _even_prompt_kernel(x_ref, w_ref, caw_ref, cab_ref, wblk_ref, ba_ref, bx_ref, lam_ref, cbw_ref,
                        y_ref, hlast_ref, ca_out_ref, sc_out_ref,
                        p0, p1, xa_ext, cb_ext, a_s, b_s, g_s, h_c):
    tt, pitch = EVEN_TT, EVEN_PITCH
    nslab = W_A // LANES
    j = pl.program_id(0)
    bufs = (p0, p1)
    group = p0.shape[0] // tt

    def project(g):
        xg = x_ref[g * group:(g + 1) * group].reshape(group * tt, D_MODEL)
        bufs[g % 2][...] = _dot(xg.astype(BF16), w_ref[...])

    @pl.when(j == 0)
    def _():
        xa_ext[:, :, 0:SUBLANES, :] = jnp.zeros((nslab, BATCH, SUBLANES, LANES), F32)
        cb_ext[:, :, 0:SUBLANES, :] = jnp.zeros((nslab, BATCH, SUBLANES, LANES), F32)
        h_c[...] = jnp.zeros(h_c.shape, F32)

    @pl.when(j > 0)
    def _():
        xa_ext[:, :, 0:SUBLANES, :] = xa_ext[:, :, tt:tt + SUBLANES, :]
        cb_ext[:, :, 0:SUBLANES, :] = cb_ext[:, :, tt:tt + SUBLANES, :]

    sp = _softplus(-lam_ref[...])
    ba = ba_ref[...]
    bx = bx_ref[...]
    cab = cab_ref[...]
    first_row = (lax.broadcasted_iota(jnp.int32, (SUBLANES, 1), 0) == 0) & (j == 0)
    cg0 = 2 * W_A + W_B
    xb0 = 2 * W_A + 2 * W_B

    def stage(b):
        p_ref = bufs[(b // group) % 2].at[(b % group) * tt:(b % group + 1) * tt]
        xc = cab + _causal_taps([p_ref[:, s * LANES:(s + 1) * LANES] for s in range(nslab)], xa_ext, b, caw_ref, tt)
        a, mult, gate = _lru_gates(xc, wblk_ref, ba, bx, sp)
        mult = jnp.concatenate([jnp.where(first_row, 1.0, mult[:SUBLANES]), mult[SUBLANES:]], axis=0)
        bb = mult * (gate * xc)
        for s in range(nslab):
            a_s[s, b * pitch:b * pitch + tt, :] = a[:, s * LANES:(s + 1) * LANES]
            b_s[s, b * pitch:b * pitch + tt, :] = bb[:, s * LANES:(s + 1) * LANES]
        g_s[b] = jax.nn.gelu(p_ref[:, W_A:2 * W_A])
        cb = [p_ref[:, cg0 + s * LANES:cg0 + (s + 1) * LANES] * p_ref[:, xb0 + s * LANES:xb0 + (s + 1) * LANES]
              for s in range(nslab)]
        zb = _causal_taps(cb, cb_ext, b, cbw_ref, tt)
        y_ref[b, :, W_A:] = (p_ref[:, 2 * W_A:2 * W_A + W_B] * zb).astype(BF16)

    project(0)
    for g in range(BATCH // group):
        if g + 1 < BATCH // group:
            project(g + 1)
        for b in range(g * group, (g + 1) * group):
            stage(b)

    def step(t, hs):
        out = []
        for s in range(nslab):
            rows = pl.ds(t, BATCH, stride=pitch)
            hn = a_s[s, rows, :] * hs[s] + b_s[s, rows, :]
            b_s[s, rows, :] = hn
            out.append(hn)
        return tuple(out)

    hs = lax.fori_loop(0, tt, step, tuple(h_c[s] for s in range(nslab)), unroll=4)
    for s in range(nslab):
        h_c[s] = hs[s]

    def emit(b, carry):
        row0 = pl.multiple_of(b * pitch, SUBLANES)
        h = jnp.concatenate([b_s[s, pl.ds(row0, tt), :] for s in range(nslab)], axis=1)
        y_ref[b, :, 0:W_A] = (h * g_s[b]).astype(BF16)
        return carry

    lax.fori_loop(0, BATCH, emit, 0)

    @pl.when(j == pl.num_programs(0) - 1)
    def _():
        hlast_ref[...] = jnp.concatenate(list(hs), axis=1)
        for s in range(nslab):
            ca_out_ref[:, :, s * LANES:(s + 1) * LANES] = xa_ext[s, :, tt:tt + SUBLANES, :]
            sc_out_ref[:, :, s * LANES:(s + 1) * LANES] = cb_ext[s, :, tt:tt + SUBLANES, :]


def _even_param_specs(e):
    return [
        _const_spec((A_CONV, W_A), e),
        _const_spec((1, W_A), e),
        _const_spec((2, W_A // 2, W_A), e),
        _const_spec((1, W_A), e),
        _const_spec((1, W_A), e),
        _const_spec((1, W_A), e),
        _const_spec((B_CONV, W_B), e),
    ]


def _even_prompt(x3d, w_stack, e, params):
    tt, pitch = EVEN_TT, EVEN_PITCH
    nslab = W_A // LANES
    assert W_A == W_B
    return pl.pallas_call(
        _even_prompt_kernel,
        grid=(SEQ // tt,),
        in_specs=[pl.BlockSpec((BATCH, tt, D_MODEL), lambda j: (0, j, 0)),
                  _const_spec((D_MODEL, EVEN_IN), e)] + _even_param_specs(e),
        out_specs=[
            pl.BlockSpec((BATCH, tt, D_MODEL), lambda j: (0, j, 0)),
            pl.BlockSpec((BATCH, W_A), lambda j: (0, 0)),
            pl.BlockSpec((BATCH, SUBLANES, W_A), lambda j: (0, 0, 0)),
            pl.BlockSpec((BATCH, SUBLANES, W_B), lambda j: (0, 0, 0)),
        ],
        out_shape=[
            jax.ShapeDtypeStruct((BATCH, SEQ, D_MODEL), BF16),
            jax.ShapeDtypeStruct((BATCH, W_A), F32),
            jax.ShapeDtypeStruct((BATCH, SUBLANES, W_A), F32),
            jax.ShapeDtypeStruct((BATCH, SUBLANES, W_B), F32),
        ],
        scratch_shapes=[
            pltpu.VMEM(((2 + 2 * e) * tt, EVEN_IN), F32),
            pltpu.VMEM(((2 + 2 * e) * tt, EVEN_IN), F32),
            pltpu.VMEM((nslab, BATCH, tt + SUBLANES, LANES), F32),
            pltpu.VMEM((nslab, BATCH, tt + SUBLANES, LANES), F32),
            pltpu.VMEM((nslab, BATCH * pitch, LANES), F32),
            pltpu.VMEM((nslab, BATCH * pitch, LANES), F32),
            pltpu.VMEM((BATCH, tt, W_A), F32),
            pltpu.VMEM((nslab, BATCH, LANES), F32),
        ],
        compiler_params=_params(("arbitrary",), 48),
        name="even_prompt",
    )(x3d, w_stack, *params)


def _even_sample_kernel(proj_ref, ca_ref, sc_ref, h0_ref, caw_ref, cab_ref, wblk_ref, ba_ref, bx_ref,
                        lam_ref, cbw_ref, y_ref, hlast_ref, ca_out_ref, sc_out_ref):
    sp = _softplus(-lam_ref[...])
    rows_a = [ca_ref[k] for k in range(A_CONV - 1)] + [proj_ref[l, :, 0:W_A] for l in range(DEC_SEQ)]
    xc = []
    for l in range(DEC_SEQ):
        acc = cab_ref[...] + caw_ref[0:1, :] * rows_a[l]
        for k in range(1, A_CONV):
            acc = acc + caw_ref[k:k + 1, :] * rows_a[l + k]
        xc.append(acc)
    xc_all = jnp.concatenate(xc, axis=0)
    a, mult, gate = _lru_gates(xc_all, wblk_ref, ba_ref[...], bx_ref[...], sp)
    bb = mult * (gate * xc_all)
    h = h0_ref[...]
    rows_b = [sc_ref[k] for k in range(B_CONV - 1)]
    for l in range(DEC_SEQ):
        sl = slice(l * DEC_BATCH, (l + 1) * DEC_BATCH)
        h = a[sl] * h + bb[sl]
        y_ref[l, :, 0:W_A] = (h * jax.nn.gelu(proj_ref[l, :, W_A:2 * W_A])).astype(BF16)
        rows_b.append(proj_ref[l, :, 2 * W_A + W_B:2 * W_A + 2 * W_B] * proj_ref[l, :, 2 * W_A + 2 * W_B:])
    hlast_ref[...] = h
    for l in range(DEC_SEQ):
        zb = cbw_ref[0:1, :] * rows_b[l]
        for k in range(1, B_CONV):
            zb = zb + cbw_ref[k:k + 1, :] * rows_b[l + k]
        y_ref[l, :, W_A:] = (proj_ref[l, :, 2 * W_A:2 * W_A + W_B] * zb).astype(BF16)
    for k in range(A_CONV - 1):
        ca_out_ref[k] = rows_a[DEC_SEQ + k]
    for k in range(B_CONV - 1):
        sc_out_ref[k] = rows_b[DEC_SEQ + k]


def _even_sample(proj_tm, ca_tm, sc_tm, h0_tm, e, params):
    whole = lambda shape: pl.BlockSpec(shape, lambda i: (0,) * len(shape))
    return pl.pallas_call(
        _even_sample_kernel,
        grid=(1,),
        in_specs=[
            whole((DEC_SEQ, DEC_BATCH, EVEN_IN)),
            _const_spec((A_CONV - 1, DEC_BATCH, W_A), e),
            _const_spec((B_CONV - 1, DEC_BATCH, W_B), e),
            _const_spec((DEC_BATCH, W_A), e),
        ] + _even_param_specs(e),
        out_specs=[
            whole((DEC_SEQ, DEC_BATCH, D_MODEL)),
            whole((DEC_BATCH, W_A)),
            whole((A_CONV - 1, DEC_BATCH, W_A)),
            whole((B_CONV - 1, DEC_BATCH, W_B)),
        ],
        out_shape=[
            jax.ShapeDtypeStruct((DEC_SEQ, DEC_BATCH, D_MODEL), BF16),
            jax.ShapeDtypeStruct((DEC_BATCH, W_A), F32),
            jax.ShapeDtypeStruct((A_CONV - 1, DEC_BATCH, W_A), F32),
            jax.ShapeDtypeStruct((B_CONV - 1, DEC_BATCH, W_B), F32),
        ],
        compiler_params=_params(("arbitrary",), 40),
        name="even_sample",
    )(proj_tm, ca_tm, sc_tm, h0_tm, *params)


def _rope(x, cos, sin_signed):
    half = C_DK // 2
    ax = x.ndim - 1
    pieces = []
    for p in range(W_CK // LANES):
        xv = x[..., p * LANES:(p + 1) * LANES]
        lane = lax.broadcasted_iota(jnp.int32, xv.shape, ax)
        partner = jnp.where((lane % C_DK) < half, pltpu.roll(xv, LANES - half, axis=ax), pltpu.roll(xv, half, axis=ax))
        pieces.append(xv * cos[..., p * LANES:(p + 1) * LANES] + partner * sin_signed[..., p * LANES:(p + 1) * LANES])
    return jnp.concatenate(pieces, axis=ax)


def _group_norm(o):
    mu = jnp.mean(o, -1, keepdims=True)
    oc = o - mu
    var = jnp.mean(oc * oc, -1, keepdims=True)
    return oc * lax.rsqrt(var + LN_EPS)


def _odd_chunk(p_ref, b, cos_ref, sin_ref, dec_ref, cs_ref, kd_ref, sdec_ref, gn_ref,
               spw_ref, spb_ref, lng_ref, lnb_ref, y_ref, s_c):
    q = _rope(p_ref[:, 0:W_CK], cos_ref[...], sin_ref[...])
    k = _rope(p_ref[:, W_CK:2 * W_CK], cos_ref[...], sin_ref[...]) * (C_DK ** -0.5)
    kd = k * kd_ref[...]
    v0 = 2 * W_CK
    g0 = v0 + W_CV
    for h in range(C_HEADS):
        qh = q[:, h * C_DK:(h + 1) * C_DK].astype(BF16)
        kh = k[:, h * C_DK:(h + 1) * C_DK].astype(BF16)
        kdh = kd[:, h * C_DK:(h + 1) * C_DK].astype(BF16)
        vh = p_ref[:, v0 + h * C_DV:v0 + (h + 1) * C_DV].astype(BF16)
        s_prev = s_c[b, h]
        scores = lax.dot_general(qh, kh, (((1,), (1,)), ((), ())), preferred_element_type=F32) * dec_ref[h]
        o = _dot(scores.astype(BF16), vh) + _dot(qh, s_prev.astype(BF16)) * cs_ref[:, h * C_DV:(h + 1) * C_DV]
        s_c[b, h] = sdec_ref[h] * s_prev + lax.dot_general(kdh, vh, (((0,), (0,)), ((), ())), preferred_element_type=F32)
        gate = jax.nn.silu(p_ref[:, g0 + h * C_DV:g0 + (h + 1) * C_DV])
        y_ref[b, :, h * C_DV:(h + 1) * C_DV] = (gate * (_group_norm(o) * gn_ref[:, h * C_DV:(h + 1) * C_DV])).astype(BF16)

    u0 = g0 + W_CV
    u = jax.nn.gelu(p_ref[:, u0:u0 + W_D])
    vd = _layer_norm(jax.nn.gelu(p_ref[:, u0 + W_D:]), lng_ref[...], lnb_ref[...]).astype(BF16)
    ri = lax.broadcasted_iota(jnp.int32, (D_CHUNK, D_CHUNK), 0)
    ci = lax.broadcasted_iota(jnp.int32, (D_CHUNK, D_CHUNK), 1)
    for gi in range(D_GROUPS):
        w = jnp.where(ri >= ci, spw_ref[gi], 0.0).astype(BF16)
        s = _dot(w, vd[:, gi * D_GROUP:(gi + 1) * D_GROUP]) + spb_ref[:, gi:gi + 1]
        y_ref[b, :, W_CV + gi * D_GROUP:W_CV + (gi + 1) * D_GROUP] = (u[:, gi * D_GROUP:(gi + 1) * D_GROUP] * s).astype(BF16)


def _odd_prompt_kernel(x_ref, w_ref, cos_ref, sin_ref, dec_ref, cs_ref, kd_ref, sdec_ref, gn_ref,
                       spw_ref, spb_ref, lng_ref, lnb_ref, y_ref, s_out_ref, p0, p1, s_c):
    c = pl.program_id(0)

    @pl.when(c == 0)
    def _():
        s_c[...] = jnp.zeros(s_c.shape, F32)

    bufs = (p0, p1)
    group = p0.shape[0] // RET_CHUNK

    def project(g):
        xg = x_ref[g * group:(g + 1) * group].reshape(group * RET_CHUNK, D_MODEL)
        bufs[g % 2][...] = _dot(xg.astype(BF16), w_ref[...])

    project(0)
    for g in range(BATCH // group):
        if g + 1 < BATCH // group:
            project(g + 1)
        for r in range(group):
            _odd_chunk(bufs[g % 2].at[r * RET_CHUNK:(r + 1) * RET_CHUNK], g * group + r,
                       cos_ref, sin_ref, dec_ref, cs_ref, kd_ref, sdec_ref, gn_ref,
                       spw_ref, spb_ref, lng_ref, lnb_ref, y_ref, s_c)

    @pl.when(c == pl.num_programs(0) - 1)
    def _():
        s_out_ref[...] = s_c[...]


def _odd_prompt(x3d, w_stack, o, cos, sin, tabs, gn, spw, spb_t, lng, lnb):
    nchunk = SEQ // RET_CHUNK
    dec, cs, kd, sdec = tabs
    return pl.pallas_call(
        _odd_prompt_kernel,
        grid=(nchunk,),
        in_specs=[
            pl.BlockSpec((BATCH, RET_CHUNK, D_MODEL), lambda c: (0, c, 0)),
            _const_spec((D_MODEL, ODD_IN), o),
            pl.BlockSpec((RET_CHUNK, W_CK), lambda c: (c, 0)),
            pl.BlockSpec((RET_CHUNK, W_CK), lambda c: (c, 0)),
            _const_spec((C_HEADS, RET_CHUNK, RET_CHUNK)),
            _const_spec((RET_CHUNK, W_CV)),
            _const_spec((RET_CHUNK, W_CK)),
            _const_spec((C_HEADS, C_DK, C_DV)),
            _const_spec((1, W_CV), o),
            _const_spec((D_GROUPS, D_CHUNK, D_CHUNK), o),
            _const_spec((D_CHUNK, D_GROUPS), o),
            _const_spec((1, W_D), o),
            _const_spec((1, W_D), o),
        ],
        out_specs=[
            pl.BlockSpec((BATCH, RET_CHUNK, D_MODEL), lambda c: (0, c, 0)),
            pl.BlockSpec((BATCH, C_HEADS, C_DK, C_DV), lambda c: (0, 0, 0, 0)),
        ],
        out_shape=[
            jax.ShapeDtypeStruct((BATCH, SEQ, D_MODEL), BF16),
            jax.ShapeDtypeStruct((BATCH, C_HEADS, C_DK, C_DV), F32),
        ],
        scratch_shapes=[
            pltpu.VMEM(((2 + 2 * o) * RET_CHUNK, ODD_IN), F32),
            pltpu.VMEM(((2 + 2 * o) * RET_CHUNK, ODD_IN), F32),
            pltpu.VMEM((BATCH, C_HEADS, C_DK, C_DV), F32),
        ],
        compiler_params=_params(("arbitrary",), 48),
        name="odd_prompt",
    )(x3d, w_stack, cos, sin, dec, cs, kd, sdec, gn, spw, spb_t, lng, lnb)


def _odd_sample_kernel(proj_ref, s0_ref, cos_ref, sin_ref, dec_ref, cs_ref, kd_ref, sdec_ref, gn_ref,
                       spw_ref, spb_ref, lng_ref, lnb_ref, *rest):
    y_ref, s_out_ref, vd_ref = rest[-3:]
    if len(rest) > 3:
        s_out_ref[:, 0] = rest[0][...]
        s_out_ref = s_out_ref.at[:, 1]
    q = _rope(proj_ref[:, :, 0:W_CK], cos_ref[...], sin_ref[...])
    k = _rope(proj_ref[:, :, W_CK:2 * W_CK], cos_ref[...], sin_ref[...]) * (C_DK ** -0.5)
    kd = k * kd_ref[...]
    v0 = 2 * W_CK
    g0 = v0 + W_CV
    for h in range(C_HEADS):
        qh = q[:, :, h * C_DK:(h + 1) * C_DK].astype(BF16)
        kh = k[:, :, h * C_DK:(h + 1) * C_DK].astype(BF16)
        kdh = kd[:, :, h * C_DK:(h + 1) * C_DK].astype(BF16)
        vh = proj_ref[:, :, v0 + h * C_DV:v0 + (h + 1) * C_DV].astype(BF16)
        s_prev = s0_ref[:, h]
        scores = jnp.einsum('nld,nmd->nlm', qh, kh, preferred_element_type=F32) * dec_ref[h]
        o = (jnp.einsum('nlm,nme->nle', scores.astype(BF16), vh, preferred_element_type=F32)
             + jnp.einsum('nld,nde->nle', qh, s_prev.astype(BF16), preferred_element_type=F32)
             * cs_ref[:, h * C_DV:(h + 1) * C_DV])
        s_out_ref[:, h] = sdec_ref[h] * s_prev + jnp.einsum('nld,nle->nde', kdh, vh, preferred_element_type=F32)
        gate = jax.nn.silu(proj_ref[:, :, g0 + h * C_DV:g0 + (h + 1) * C_DV])
        y_ref[:, :, h * C_DV:(h + 1) * C_DV] = (gate * (_group_norm(o) * gn_ref[:, h * C_DV:(h + 1) * C_DV])).astype(BF16)

    u0 = g0 + W_CV
    u = jax.nn.gelu(proj_ref[:, :, u0:u0 + W_D])
    vd = _layer_norm(jax.nn.gelu(proj_ref[:, :, u0 + W_D:]), lng_ref[...], lnb_ref[...])
    vd_ref[...] = vd
    s = spb_ref[...] + spw_ref[0] * vd[:, 0:1, :]
    for m in range(1, DEC_SEQ):
        s = s + spw_ref[m] * vd[:, m:m + 1, :]
    y_ref[:, :, W_CV:] = (u * s).astype(BF16)


def _odd_sample(proj, state_ret, o, cos, sin, tabs, gn, spw_rows, spb_rows, lng, lnb, prev_states):
    nb = SAMPLE_NB
    dec, cs, kd, sdec = tabs
    seq3 = lambda i: (i, 0, 0)
    one_state = pl.BlockSpec((nb, C_HEADS, C_DK, C_DV), lambda i: (i, 0, 0, 0))
    in_specs = [
        pl.BlockSpec((nb, DEC_SEQ, ODD_IN), seq3),
        pl.BlockSpec((nb, None, C_HEADS, C_DK, C_DV), lambda i: (i, o, 0, 0, 0)),
        _const_spec((DEC_SEQ, W_CK)),
        _const_spec((DEC_SEQ, W_CK)),
        _const_spec((C_HEADS, DEC_SEQ, DEC_SEQ)),
        _const_spec((DEC_SEQ, W_CV)),
        _const_spec((DEC_SEQ, W_CK)),
        _const_spec((C_HEADS, C_DK, C_DV)),
        _const_spec((1, W_CV), o),
        _const_spec((DEC_SEQ, DEC_SEQ, W_D), o),
        _const_spec((DEC_SEQ, W_D), o),
        _const_spec((1, W_D), o),
        _const_spec((1, W_D), o),
    ]
    args = [proj, state_ret, cos, sin, dec, cs, kd, sdec, gn, spw_rows, spb_rows, lng, lnb]
    if prev_states is None:
        state_spec = one_state
        state_shape = (DEC_BATCH, C_HEADS, C_DK, C_DV)
    else:
        in_specs.append(one_state)
        args.append(prev_states)
        state_spec = pl.BlockSpec((nb, N_ODD, C_HEADS, C_DK, C_DV), lambda i: (i, 0, 0, 0, 0))
        state_shape = (DEC_BATCH, N_ODD, C_HEADS, C_DK, C_DV)
    return pl.pallas_call(
        _odd_sample_kernel,
        grid=(DEC_BATCH // nb,),
        in_specs=in_specs,
        out_specs=[pl.BlockSpec((nb, DEC_SEQ, D_MODEL), seq3), state_spec, pl.BlockSpec((nb, DEC_SEQ, W_D), seq3)],
        out_shape=[
            jax.ShapeDtypeStruct((DEC_BATCH, DEC_SEQ, D_MODEL), BF16),
            jax.ShapeDtypeStruct(state_shape, F32),
            jax.ShapeDtypeStruct((DEC_BATCH, DEC_SEQ, W_D), F32),
        ],
        compiler_params=_params(("parallel",), 48),
        name="odd_sample",
    )(*args)


def _rope_tables(pos):
    half = C_DK // 2
    freq = ROPE_BASE ** (-np.arange(half, dtype=np.float64) / half)
    ang = np.asarray(pos, np.float64)[:, None] * freq
    cos, sin = np.cos(ang), np.sin(ang)
    cos_l = np.tile(np.concatenate([cos, cos], axis=-1), (1, C_HEADS))
    sin_l = np.tile(np.concatenate([-sin, sin], axis=-1), (1, C_HEADS))
    return cos_l.astype(np.float32), sin_l.astype(np.float32)


def _retention_tables(length):
    log_g = np.log1p(-np.exp2(-5.0 - np.arange(C_HEADS, dtype=np.float64)))
    idx = np.arange(length, dtype=np.float64)
    rel = idx[:, None] - idx[None, :]
    decay = np.where(rel >= 0, np.exp(log_g[:, None, None] * np.maximum(rel, 0.0)), 0.0)
    cross = np.exp(log_g[None, :] * (idx[:, None] + 1.0))
    kdec = np.exp(log_g[None, :] * (length - 1.0 - idx[:, None]))
    sdec = np.exp(log_g * length)
    tabs = (decay,
            np.repeat(cross, C_DV, axis=1),
            np.repeat(kdec, C_DK, axis=1),
            np.broadcast_to(sdec[:, None, None], (C_HEADS, C_DK, C_DV)))
    return tuple(np.ascontiguousarray(t, dtype=np.float32) for t in tabs)


def _block_diag_halves(wa, wx):
    nb = A_BLOCKS // 2
    width = nb * A_BLOCK

    def bd(w, hh):
        rows = [jnp.pad(w[:, hh * nb + g], ((0, 0), (0, 0), (g * A_BLOCK, width - (g + 1) * A_BLOCK)))
                for g in range(nb)]
        return jnp.concatenate(rows, axis=1)

    halves = [jnp.concatenate([bd(wa, hh), bd(wx, hh)], axis=-1) for hh in range(2)]
    return jnp.stack(halves, axis=1).astype(BF16)


def kernel(x_prompt, x_sample, state_lru_h, state_lru_conv, state_sconv, state_ret, w_in_even, conv_a_w, conv_a_b, lru_wa, lru_ba, lru_wx, lru_bx, lru_lam, conv_b_w, w_out_even, w_in_odd, ret_gn_g, sp_w, sp_b, gm_ln_g, gm_ln_b, w_out_odd, ffn_w1, ffn_w3, ffn_w2, ln1_g, ln1_b, ln2_g, ln2_b):
    rows = lambda v: v.reshape(v.shape[0], 1, v.shape[1])
    xp = x_prompt.reshape(BATCH * SEQ, D_MODEL)
    xs = jnp.swapaxes(x_sample, 0, 1).reshape(DEC_SEQ * DEC_BATCH, D_MODEL)

    cos_p, sin_p = _rope_tables(np.arange(SEQ))
    cos_s, sin_s = _rope_tables(PAST_LEN + np.arange(DEC_SEQ))
    tab_p = _retention_tables(RET_CHUNK)
    tab_s = _retention_tables(DEC_SEQ)

    w_in_even_b, w_out_even_b = w_in_even.astype(BF16), w_out_even.astype(BF16)
    w_in_odd_b, w_out_odd_b = w_in_odd.astype(BF16), w_out_odd.astype(BF16)
    dense_p = (rows(ln1_g), rows(ln1_b), ffn_w1.astype(BF16), ffn_w3.astype(BF16), ffn_w2.astype(BF16),
               rows(ln2_g), rows(ln2_b))
    even_p = (conv_a_w, rows(conv_a_b), _block_diag_halves(lru_wa, lru_wx), rows(lru_ba), rows(lru_bx),
              rows(lru_lam), conv_b_w)
    ca_tm = jnp.transpose(state_lru_conv, (1, 2, 0, 3))
    sc_tm = jnp.transpose(state_sconv, (1, 2, 0, 3))
    h0_tm = jnp.swapaxes(state_lru_h, 0, 1)
    tril = jnp.tril(sp_w[:, :, :DEC_SEQ, :DEC_SEQ])
    spw_rows = jnp.repeat(jnp.transpose(tril, (0, 3, 2, 1)), D_GROUP, axis=3)
    spb_rows = jnp.repeat(jnp.swapaxes(sp_b[:, :, :DEC_SEQ], 1, 2), D_GROUP, axis=2)
    spb_t = jnp.swapaxes(sp_b, 1, 2)
    odd_small = (rows(ret_gn_g),)
    odd_ln = (rows(gm_ln_g), rows(gm_ln_b))

    h_p, h_s, ca_p, ca_s, sc_p, sc_s, r_p, v_s = [], [], [], [], [], [], [], []
    ret_sample = None
    for l in range(DEPTH):
        if l % 2 == 0:
            e = l // 2
            wo_stack, wo_layer = w_out_even_b, e
            yp, hlp, cap, scp = _even_prompt(xp.reshape(BATCH, SEQ, D_MODEL), w_in_even_b, e, even_p)
            proj_s = _in_proj(xs, w_in_even_b, e).reshape(DEC_SEQ, DEC_BATCH, EVEN_IN)
            ys, hls, cas, scs = _even_sample(proj_s, ca_tm, sc_tm, h0_tm, e, even_p)
            h_p.append(hlp)
            h_s.append(hls)
            ca_p.append(cap[:, SUBLANES - (A_CONV - 1):])
            ca_s.append(jnp.swapaxes(cas, 0, 1))
            sc_p.append(scp[:, SUBLANES - (B_CONV - 1):])
            sc_s.append(jnp.swapaxes(scs, 0, 1))
            yp = yp.reshape(BATCH * SEQ, D_MODEL)
            ys = ys.reshape(DEC_SEQ * DEC_BATCH, D_MODEL)
        else:
            o = l // 2
            wo_stack, wo_layer = w_out_odd_b, o
            yp, rp = _odd_prompt(xp.reshape(BATCH, SEQ, D_MODEL), w_in_odd_b, o, cos_p, sin_p, tab_p,
                                 *odd_small, sp_w, spb_t, *odd_ln)
            proj_s = _in_proj(xs, w_in_odd_b, o).reshape(DEC_BATCH, DEC_SEQ, ODD_IN)
            ys, ret_sample, vs = _odd_sample(proj_s, state_ret, o, cos_s, sin_s, tab_s, *odd_small,
                                             spw_rows, spb_rows, *odd_ln, ret_sample)
            r_p.append(rp)
            v_s.append(vs)
            yp = yp.reshape(BATCH * SEQ, D_MODEL)
            ys = ys.reshape(DEC_BATCH * DEC_SEQ, D_MODEL)
        xp, xs = _dense_block(yp, xp, ys, xs, wo_stack, wo_layer, l, *dense_p)
        if l + 1 < DEPTH:
            if l % 2 == 0:
                xs = jnp.swapaxes(xs.reshape(DEC_SEQ, DEC_BATCH, D_MODEL), 0, 1).reshape(DEC_BATCH * DEC_SEQ, D_MODEL)
            else:
                xs = jnp.swapaxes(xs.reshape(DEC_BATCH, DEC_SEQ, D_MODEL), 0, 1).reshape(DEC_SEQ * DEC_BATCH, D_MODEL)

    return (xp.reshape(BATCH, SEQ, D_MODEL), xs.reshape(DEC_BATCH, DEC_SEQ, D_MODEL),
            jnp.stack(h_p, axis=1), jnp.stack(h_s, axis=1),
            jnp.stack(ca_p, axis=1), jnp.stack(ca_s, axis=1),
            jnp.stack(sc_p, axis=1), jnp.stack(sc_s, axis=1),
            jnp.stack(r_p, axis=1), ret_sample, jnp.stack(v_s, axis=1))
```

```python
import jax
import jax.numpy as jnp
import numpy as np
from jax import lax
from jax.experimental import pallas as pl
from jax.experimental.pallas import tpu as pltpu

F32 = jnp.float32
BF16 = jnp.bfloat16

D_MODEL = 1024
BATCH = 8
SEQ = 2048
DEPTH = 4
DEC_BATCH = 128
DEC_SEQ = 4
PAST_LEN = 16384
N_ODD = DEPTH // 2
W_A = 512
A_BLOCKS = 8
A_BLOCK = 64
A_CONV = 4
LRU_C = 8.0
W_B = 512
B_CONV = 3
C_HEADS = 4
C_DK = 64
C_DV = 128
W_CK = C_HEADS * C_DK
W_CV = C_HEADS * C_DV
RET_CHUNK = 128
ROPE_BASE = 10000.0
D_GROUPS = 4
D_CHUNK = 128
W_D = 512
D_GROUP = W_D // D_GROUPS
EVEN_IN = 2 * W_A + 3 * W_B
ODD_IN = 2 * W_CK + 2 * W_CV + 2 * W_D
D_FF = 2816
ALPHA = (2 * DEPTH) ** 0.25
LN_EPS = 1e-5

SUBLANES = 8
LANES = 128
MIB = 1024 * 1024

ROW_TILE = 512
DENSE_TILE = 512
DENSE_GROUP = 256
EVEN_TT = 128
EVEN_GROUP = 1
ODD_GROUP = 2
EVEN_PITCH = EVEN_TT + SUBLANES
SAMPLE_NB = 32


def _params(sem, vmem_mib):
    return pltpu.CompilerParams(dimension_semantics=sem, vmem_limit_bytes=vmem_mib * MIB)


def _const_spec(shape, layer=None):
    nd = len(shape)
    if layer is None:
        return pl.BlockSpec(shape, lambda *_: (0,) * nd, pipeline_mode=pl.Buffered(1))
    return pl.BlockSpec((None,) + tuple(shape), lambda *_: (layer,) + (0,) * nd, pipeline_mode=pl.Buffered(1))


def _layer_norm(x, g, b):
    mu = jnp.mean(x, -1, keepdims=True)
    xc = x - mu
    var = jnp.mean(xc * xc, -1, keepdims=True)
    return xc * lax.rsqrt(var + LN_EPS) * g + b


def _softplus(x):
    return jnp.maximum(x, 0.0) + jnp.log1p(jnp.exp(-jnp.abs(x)))


def _dot(a, b):
    return jnp.dot(a, b, preferred_element_type=F32)


N_CAST = 4


def _cast_specs(stacks_and_layers, nsteps):
    in_specs, out_specs, out_shapes = [], [], []
    for w, layer in stacks_and_layers:
        nrow, ncol = w.shape[1:]
        rb = nrow // nsteps
        assert nrow % nsteps == 0 and rb % (2 * SUBLANES) == 0
        in_specs.append(pl.BlockSpec((None, rb, ncol), lambda j, layer=layer: (layer, j, 0)))
        out_specs.append(pl.BlockSpec((rb, ncol), lambda j: (j, 0)))
        out_shapes.append(jax.ShapeDtypeStruct((nrow, ncol), BF16))
    return in_specs, out_specs, out_shapes


def _cast_slabs(src_refs, dst_refs):
    for src, dst in zip(src_refs, dst_refs):
        dst[...] = src[...].astype(BF16)


def _proj_kernel(x_ref, w_ref, o_ref):
    o_ref[...] = _dot(x_ref[...].astype(BF16), w_ref[...])


def _in_proj(x2d, w_stack, layer):
    m, k = x2d.shape
    n = w_stack.shape[2]
    tm = min(ROW_TILE, m)
    return pl.pallas_call(
        _proj_kernel,
        grid=(m // tm,),
        in_specs=[pl.BlockSpec((tm, k), lambda i: (i, 0)), _const_spec((k, n), layer)],
        out_specs=pl.BlockSpec((tm, n), lambda i: (i, 0)),
        out_shape=jax.ShapeDtypeStruct((m, n), F32),
        compiler_params=_params(("parallel",), 40),
        name="in_proj",
    )(x2d, w_stack)


def _dense_kernel(yp_ref, xp_ref, ys_ref, xs_ref, wo_ref, g1_ref, b1_ref, w1_ref, w3_ref, w2_ref, g2_ref, b2_ref,
                  op_ref, os_ref):
    n_prompt = pl.num_programs(0) - 1
    weights = (wo_ref, g1_ref, b1_ref, w1_ref, w3_ref, w2_ref, g2_ref, b2_ref)

    @pl.when(pl.program_id(0) < n_prompt)
    def _():
        _dense_rows(yp_ref, xp_ref, *weights, op_ref)

    @pl.when(pl.program_id(0) == n_prompt)
    def _():
        _dense_rows(ys_ref, xs_ref, *weights, os_ref)


def _dense_rows(y_ref, x_ref, wo_ref, g1_ref, b1_ref, w1_ref, w3_ref, w2_ref, g2_ref, b2_ref, o_ref):
    rows = DENSE_GROUP
    nparts = y_ref.shape[0] // rows
    sls = [slice(p * rows, (p + 1) * rows) for p in range(nparts)]
    mix, x1, xb, h = {}, {}, {}, {}

    def out_proj(p):
        mix[p] = _dot(y_ref[sls[p], :], wo_ref[...])

    def norm1(p):
        x1[p] = _layer_norm(ALPHA * x_ref[sls[p], :] + mix[p], g1_ref[...], b1_ref[...])
        xb[p] = x1[p].astype(BF16)

    def gate_up(p):
        h[p] = (jax.nn.silu(_dot(xb[p], w1_ref[...])) * _dot(xb[p], w3_ref[...])).astype(BF16)

    def down_norm2(p):
        f = _dot(h[p], w2_ref[...])
        o_ref[sls[p], :] = _layer_norm(ALPHA * x1[p] + f, g2_ref[...], b2_ref[...])

    out_proj(0)
    for p in range(nparts):
        norm1(p)
        if p + 1 < nparts:
            out_proj(p + 1)
    for p in range(nparts):
        gate_up(p)
    for p in range(nparts):
        down_norm2(p)


def _dense_block(yp, xp, ys, xs, layer, wo, g1, b1, w1, w3, w2, g2, b2):
    tm = DENSE_TILE
    n_prompt = xp.shape[0] // tm
    assert xp.shape[0] % tm == 0 and xs.shape[0] == tm and tm % DENSE_GROUP == 0
    prompt_row = lambda i: (jnp.minimum(i, n_prompt - 1), 0)
    sample_row = lambda i: (0, 0)
    return pl.pallas_call(
        _dense_kernel,
        grid=(n_prompt + 1,),
        in_specs=[
            pl.BlockSpec((tm, D_MODEL), prompt_row),
            pl.BlockSpec((tm, D_MODEL), prompt_row),
            pl.BlockSpec((tm, D_MODEL), sample_row),
            pl.BlockSpec((tm, D_MODEL), sample_row),
            _const_spec((D_MODEL, D_MODEL)),
            _const_spec((1, D_MODEL), layer),
            _const_spec((1, D_MODEL), layer),
            _const_spec((D_MODEL, D_FF)),
            _const_spec((D_MODEL, D_FF)),
            _const_spec((D_FF, D_MODEL)),
            _const_spec((1, D_MODEL), layer),
            _const_spec((1, D_MODEL), layer),
        ],
        out_specs=[pl.BlockSpec((tm, D_MODEL), prompt_row), pl.BlockSpec((tm, D_MODEL), sample_row)],
        out_shape=[jax.ShapeDtypeStruct(xp.shape, F32), jax.ShapeDtypeStruct(xs.shape, F32)],
        compiler_params=_params(("arbitrary",), 58),
        name="dense_block",
    )(yp, xp, ys, xs, wo, g1, b1, w1, w3, w2, g2, b2)


def _lru_gates(xc, wblk_ref, ba, bx, sp):
    half = W_A // 2
    xb = xc.astype(BF16)
    pre0 = _dot(xb[:, :half], wblk_ref[0])
    pre1 = _dot(xb[:, half:], wblk_ref[1])
    r = jax.nn.sigmoid(jnp.concatenate([pre0[:, :half], pre1[:, :half]], axis=1) + ba)
    i = jax.nn.sigmoid(jnp.concatenate([pre0[:, half:], pre1[:, half:]], axis=1) + bx)
    log_a = (-LRU_C) * r * sp
    a = jnp.exp(log_a)
    mult = jnp.sqrt((1.0 + a * a) * jnp.tanh(-log_a))
    return a, mult, i


def _causal_taps(x_slabs, ext, b, w_ref, nrows):
    ktaps = w_ref.shape[0]
    out = []
    for s, x in enumerate(x_slabs):
        lanes = slice(s * LANES, (s + 1) * LANES)
        ext[s, b, SUBLANES:SUBLANES + nrows, :] = x
        acc = w_ref[ktaps - 1:ktaps, lanes] * x
        for k in range(ktaps - 1):
            off = SUBLANES - (ktaps - 1) + k
            acc = acc + w_ref[k:k + 1, lanes] * ext[s, b, off:off + nrows, :]
        out.append(acc)
    return jnp.concatenate(out, axis=1)


def _even_prompt_kernel(x_ref, w_ref, caw_ref, cab_ref, wblk_ref, ba_ref, bx_ref, lam_ref, cbw_ref, *rest):
    cast_src, rest = rest[:N_CAST], rest[N_CAST:]
    y_ref, hlast_ref, ca_out_ref, sc_out_ref = rest[:4]
    cast_dst, rest = rest[4:4 + N_CAST], rest[4 + N_CAST:]
    p0, p1, xa_ext, cb_ext, a_s, b_s, g_s, h_c = rest
    tt, pitch = EVEN_TT, EVEN_PITCH
    nslab = W_A // LANES
    j = pl.program_id(0)
    _cast_slabs(cast_src, cast_dst)
    bufs = (p0, p1)
    group = p0.shape[0] // tt

    def project(g):
        xg = x_ref[g * group:(g + 1) * group].reshape(group * tt, D_MODEL)
        bufs[g % 2][...] = _dot(xg.astype(BF16), w_ref[...])

    @pl.when(j == 0)
    def _():
        xa_ext[:, :, 0:SUBLANES, :] = jnp.zeros((nslab, BATCH, SUBLANES, LANES), F32)
        cb_ext[:, :, 0:SUBLANES, :] = jnp.zeros((nslab, BATCH, SUBLANES, LANES), F32)
        h_c[...] = jnp.zeros(h_c.shape, F32)

    @pl.when(j > 0)
    def _():
        xa_ext[:, :, 0:SUBLANES, :] = xa_ext[:, :, tt:tt + SUBLANES, :]
        cb_ext[:, :, 0:SUBLANES, :] = cb_ext[:, :, tt:tt + SUBLANES, :]

    sp = _softplus(-lam_ref[...])
    ba = ba_ref[...]
    bx = bx_ref[...]
    cab = cab_ref[...]
    first_row = (lax.broadcasted_iota(jnp.int32, (SUBLANES, 1), 0) == 0) & (j == 0)
    cg0 = 2 * W_A + W_B
    xb0 = 2 * W_A + 2 * W_B

    def stage(b):
        p_ref = bufs[(b // group) % 2].at[(b % group) * tt:(b % group + 1) * tt]
        xc = cab + _causal_taps([p_ref[:, s * LANES:(s + 1) * LANES] for s in range(nslab)], xa_ext, b, caw_ref, tt)
        a, mult, gate = _lru_gates(xc, wblk_ref, ba, bx, sp)
        mult = jnp.concatenate([jnp.where(first_row, 1.0, mult[:SUBLANES]), mult[SUBLANES:]], axis=0)
        bb = mult * (gate * xc)
        for s in range(nslab):
            a_s[s, b * pitch:b * pitch + tt, :] = a[:, s * LANES:(s + 1) * LANES]
            b_s[s, b * pitch:b * pitch + tt, :] = bb[:, s * LANES:(s + 1) * LANES]
        g_s[b] = jax.nn.gelu(p_ref[:, W_A:2 * W_A])
        cb = [p_ref[:, cg0 + s * LANES:cg0 + (s + 1) * LANES] * p_ref[:, xb0 + s * LANES:xb0 + (s + 1) * LANES]
              for s in range(nslab)]
        zb = _causal_taps(cb, cb_ext, b, cbw_ref, tt)
        y_ref[b, :, W_A:] = (p_ref[:, 2 * W_A:2 * W_A + W_B] * zb).astype(BF16)

    project(0)
    for g in range(BATCH // group):
        if g + 1 < BATCH // group:
            project(g + 1)
        for b in range(g * group, (g + 1) * group):
            stage(b)

    def step(t, hs):
        out = []
        for s in range(nslab):
            rows = pl.ds(t, BATCH, stride=pitch)
            hn = a_s[s, rows, :] * hs[s] + b_s[s, rows, :]
            b_s[s, rows, :] = hn
            out.append(hn)
        return tuple(out)

    hs = lax.fori_loop(0, tt, step, tuple(h_c[s] for s in range(nslab)), unroll=4)
    for s in range(nslab):
        h_c[s] = hs[s]

    def emit(b, carry):
        row0 = pl.multiple_of(b * pitch, SUBLANES)
        h = jnp.concatenate([b_s[s, pl.ds(row0, tt), :] for s in range(nslab)], axis=1)
        y_ref[b, :, 0:W_A] = (h * g_s[b]).astype(BF16)
        return carry

    lax.fori_loop(0, BATCH, emit, 0)

    @pl.when(j == pl.num_programs(0) - 1)
    def _():
        hlast_ref[...] = jnp.concatenate(list(hs), axis=1)
        for s in range(nslab):
            ca_out_ref[:, :, s * LANES:(s + 1) * LANES] = xa_ext[s, :, tt:tt + SUBLANES, :]
            sc_out_ref[:, :, s * LANES:(s + 1) * LANES] = cb_ext[s, :, tt:tt + SUBLANES, :]


def _even_param_specs(e):
    return [
        _const_spec((A_CONV, W_A), e),
        _const_spec((1, W_A), e),
        _const_spec((2, W_A // 2, W_A), e),
        _const_spec((1, W_A), e),
        _const_spec((1, W_A), e),
        _const_spec((1, W_A), e),
        _const_spec((B_CONV, W_B), e),
    ]


def _even_prompt(x3d, w_stack, e, params, casts):
    tt, pitch = EVEN_TT, EVEN_PITCH
    nslab = W_A // LANES
    nblock = SEQ // tt
    assert W_A == W_B and len(casts) == N_CAST
    cast_in, cast_out, cast_shapes = _cast_specs(casts, nblock)
    return pl.pallas_call(
        _even_prompt_kernel,
        grid=(nblock,),
        in_specs=[pl.BlockSpec((BATCH, tt, D_MODEL), lambda j: (0, j, 0)),
                  _const_spec((D_MODEL, EVEN_IN), e)] + _even_param_specs(e) + cast_in,
        out_specs=[
            pl.BlockSpec((BATCH, tt, D_MODEL), lambda j: (0, j, 0)),
            pl.BlockSpec((BATCH, W_A), lambda j: (0, 0)),
            pl.BlockSpec((BATCH, SUBLANES, W_A), lambda j: (0, 0, 0)),
            pl.BlockSpec((BATCH, SUBLANES, W_B), lambda j: (0, 0, 0)),
        ] + cast_out,
        out_shape=[
            jax.ShapeDtypeStruct((BATCH, SEQ, D_MODEL), BF16),
            jax.ShapeDtypeStruct((BATCH, W_A), F32),
            jax.ShapeDtypeStruct((BATCH, SUBLANES, W_A), F32),
            jax.ShapeDtypeStruct((BATCH, SUBLANES, W_B), F32),
        ] + cast_shapes,
        scratch_shapes=[
            pltpu.VMEM((EVEN_GROUP * tt, EVEN_IN), F32),
            pltpu.VMEM((EVEN_GROUP * tt, EVEN_IN), F32),
            pltpu.VMEM((nslab, BATCH, tt + SUBLANES, LANES), F32),
            pltpu.VMEM((nslab, BATCH, tt + SUBLANES, LANES), F32),
            pltpu.VMEM((nslab, BATCH * pitch, LANES), F32),
            pltpu.VMEM((nslab, BATCH * pitch, LANES), F32),
            pltpu.VMEM((BATCH, tt, W_A), F32),
            pltpu.VMEM((nslab, BATCH, LANES), F32),
        ],
        compiler_params=_params(("arbitrary",), 52),
        name="even_prompt",
    )(x3d, w_stack, *params, *[w for w, _ in casts])


def _even_sample_kernel(proj_ref, ca_ref, sc_ref, h0_ref, caw_ref, cab_ref, wblk_ref, ba_ref, bx_ref,
                        lam_ref, cbw_ref, y_ref, hlast_ref, ca_out_ref, sc_out_ref):
    sp = _softplus(-lam_ref[...])
    rows_a = [ca_ref[k] for k in range(A_CONV - 1)] + [proj_ref[l, :, 0:W_A] for l in range(DEC_SEQ)]
    xc = []
    for l in range(DEC_SEQ):
        acc = cab_ref[...] + caw_ref[0:1, :] * rows_a[l]
        for k in range(1, A_CONV):
            acc = acc + caw_ref[k:k + 1, :] * rows_a[l + k]
        xc.append(acc)
    xc_all = jnp.concatenate(xc, axis=0)
    a, mult, gate = _lru_gates(xc_all, wblk_ref, ba_ref[...], bx_ref[...], sp)
    bb = mult * (gate * xc_all)
    h = h0_ref[...]
    rows_b = [sc_ref[k] for k in range(B_CONV - 1)]
    for l in range(DEC_SEQ):
        sl = slice(l * DEC_BATCH, (l + 1) * DEC_BATCH)
        h = a[sl] * h + bb[sl]
        y_ref[l, :, 0:W_A] = (h * jax.nn.gelu(proj_ref[l, :, W_A:2 * W_A])).astype(BF16)
        rows_b.append(proj_ref[l, :, 2 * W_A + W_B:2 * W_A + 2 * W_B] * proj_ref[l, :, 2 * W_A + 2 * W_B:])
    hlast_ref[...] = h
    for l in range(DEC_SEQ):
        zb = cbw_ref[0:1, :] * rows_b[l]
        for k in range(1, B_CONV):
            zb = zb + cbw_ref[k:k + 1, :] * rows_b[l + k]
        y_ref[l, :, W_A:] = (proj_ref[l, :, 2 * W_A:2 * W_A + W_B] * zb).astype(BF16)
    for k in range(A_CONV - 1):
        ca_out_ref[k] = rows_a[DEC_SEQ + k]
    for k in range(B_CONV - 1):
        sc_out_ref[k] = rows_b[DEC_SEQ + k]


def _even_sample(proj_tm, ca_tm, sc_tm, h0_tm, e, params):
    whole = lambda shape: pl.BlockSpec(shape, lambda i: (0,) * len(shape))
    return pl.pallas_call(
        _even_sample_kernel,
        grid=(1,),
        in_specs=[
            whole((DEC_SEQ, DEC_BATCH, EVEN_IN)),
            _const_spec((A_CONV - 1, DEC_BATCH, W_A), e),
            _const_spec((B_CONV - 1, DEC_BATCH, W_B), e),
            _const_spec((DEC_BATCH, W_A), e),
        ] + _even_param_specs(e),
        out_specs=[
            whole((DEC_SEQ, DEC_BATCH, D_MODEL)),
            whole((DEC_BATCH, W_A)),
            whole((A_CONV - 1, DEC_BATCH, W_A)),
            whole((B_CONV - 1, DEC_BATCH, W_B)),
        ],
        out_shape=[
            jax.ShapeDtypeStruct((DEC_SEQ, DEC_BATCH, D_MODEL), BF16),
            jax.ShapeDtypeStruct((DEC_BATCH, W_A), F32),
            jax.ShapeDtypeStruct((A_CONV - 1, DEC_BATCH, W_A), F32),
            jax.ShapeDtypeStruct((B_CONV - 1, DEC_BATCH, W_B), F32),
        ],
        compiler_params=_params(("arbitrary",), 40),
        name="even_sample",
    )(proj_tm, ca_tm, sc_tm, h0_tm, *params)


def _rope(x, cos, sin_signed):
    half = C_DK // 2
    ax = x.ndim - 1
    pieces = []
    for p in range(W_CK // LANES):
        xv = x[..., p * LANES:(p + 1) * LANES]
        lane = lax.broadcasted_iota(jnp.int32, xv.shape, ax)
        partner = jnp.where((lane % C_DK) < half, pltpu.roll(xv, LANES - half, axis=ax), pltpu.roll(xv, half, axis=ax))
        pieces.append(xv * cos[..., p * LANES:(p + 1) * LANES] + partner * sin_signed[..., p * LANES:(p + 1) * LANES])
    return jnp.concatenate(pieces, axis=ax)


def _group_norm(o):
    mu = jnp.mean(o, -1, keepdims=True)
    oc = o - mu
    var = jnp.mean(oc * oc, -1, keepdims=True)
    return oc * lax.rsqrt(var + LN_EPS)


def _odd_chunk(p_ref, b, cos_ref, sin_ref, dec_ref, cs_ref, kd_ref, sdec_ref, gn_ref,
               spw_ref, spb_ref, lng_ref, lnb_ref, y_ref, s_c):
    q = _rope(p_ref[:, 0:W_CK], cos_ref[...], sin_ref[...])
    k = _rope(p_ref[:, W_CK:2 * W_CK], cos_ref[...], sin_ref[...]) * (C_DK ** -0.5)
    kd = k * kd_ref[...]
    v0 = 2 * W_CK
    g0 = v0 + W_CV
    for h in range(C_HEADS):
        qh = q[:, h * C_DK:(h + 1) * C_DK].astype(BF16)
        kh = k[:, h * C_DK:(h + 1) * C_DK].astype(BF16)
        kdh = kd[:, h * C_DK:(h + 1) * C_DK].astype(BF16)
        vh = p_ref[:, v0 + h * C_DV:v0 + (h + 1) * C_DV].astype(BF16)
        s_prev = s_c[b, h]
        scores = lax.dot_general(qh, kh, (((1,), (1,)), ((), ())), preferred_element_type=F32) * dec_ref[h]
        o = _dot(scores.astype(BF16), vh) + _dot(qh, s_prev.astype(BF16)) * cs_ref[:, h * C_DV:(h + 1) * C_DV]
        s_c[b, h] = sdec_ref[h] * s_prev + lax.dot_general(kdh, vh, (((0,), (0,)), ((), ())), preferred_element_type=F32)
        gate = jax.nn.silu(p_ref[:, g0 + h * C_DV:g0 + (h + 1) * C_DV])
        y_ref[b, :, h * C_DV:(h + 1) * C_DV] = (gate * (_group_norm(o) * gn_ref[:, h * C_DV:(h + 1) * C_DV])).astype(BF16)

    u0 = g0 + W_CV
    u = jax.nn.gelu(p_ref[:, u0:u0 + W_D])
    vd = _layer_norm(jax.nn.gelu(p_ref[:, u0 + W_D:]), lng_ref[...], lnb_ref[...]).astype(BF16)
    ri = lax.broadcasted_iota(jnp.int32, (D_CHUNK, D_CHUNK), 0)
    ci = lax.broadcasted_iota(jnp.int32, (D_CHUNK, D_CHUNK), 1)
    for gi in range(D_GROUPS):
        w = jnp.where(ri >= ci, spw_ref[gi], 0.0).astype(BF16)
        s = _dot(w, vd[:, gi * D_GROUP:(gi + 1) * D_GROUP]) + spb_ref[:, gi:gi + 1]
        y_ref[b, :, W_CV + gi * D_GROUP:W_CV + (gi + 1) * D_GROUP] = (u[:, gi * D_GROUP:(gi + 1) * D_GROUP] * s).astype(BF16)


def _odd_prompt_kernel(x_ref, w_ref, cos_ref, sin_ref, dec_ref, cs_ref, kd_ref, sdec_ref, gn_ref,
                       spw_ref, spb_ref, lng_ref, lnb_ref, *rest):
    cast_src, rest = rest[:N_CAST], rest[N_CAST:]
    y_ref, s_out_ref = rest[:2]
    cast_dst, rest = rest[2:2 + N_CAST], rest[2 + N_CAST:]
    p0, p1, s_c = rest
    c = pl.program_id(0)
    _cast_slabs(cast_src, cast_dst)

    @pl.when(c == 0)
    def _():
        s_c[...] = jnp.zeros(s_c.shape, F32)

    bufs = (p0, p1)
    group = p0.shape[0] // RET_CHUNK

    def project(g):
        xg = x_ref[g * group:(g + 1) * group].reshape(group * RET_CHUNK, D_MODEL)
        bufs[g % 2][...] = _dot(xg.astype(BF16), w_ref[...])

    project(0)
    for g in range(BATCH // group):
        if g + 1 < BATCH // group:
            project(g + 1)
        for r in range(group):
            _odd_chunk(bufs[g % 2].at[r * RET_CHUNK:(r + 1) * RET_CHUNK], g * group + r,
                       cos_ref, sin_ref, dec_ref, cs_ref, kd_ref, sdec_ref, gn_ref,
                       spw_ref, spb_ref, lng_ref, lnb_ref, y_ref, s_c)

    @pl.when(c == pl.num_programs(0) - 1)
    def _():
        s_out_ref[...] = s_c[...]


def _odd_prompt(x3d, w_stack, o, cos, sin, tabs, gn, spw, spb_t, lng, lnb, casts):
    nchunk = SEQ // RET_CHUNK
    dec, cs, kd, sdec = tabs
    assert len(casts) == N_CAST
    cast_in, cast_out, cast_shapes = _cast_specs(casts, nchunk)
    return pl.pallas_call(
        _odd_prompt_kernel,
        grid=(nchunk,),
        in_specs=[
            pl.BlockSpec((BATCH, RET_CHUNK, D_MODEL), lambda c: (0, c, 0)),
            _const_spec((D_MODEL, ODD_IN), o),
            pl.BlockSpec((RET_CHUNK, W_CK), lambda c: (c, 0)),
            pl.BlockSpec((RET_CHUNK, W_CK), lambda c: (c, 0)),
            _const_spec((C_HEADS, RET_CHUNK, RET_CHUNK)),
            _const_spec((RET_CHUNK, W_CV)),
            _const_spec((RET_CHUNK, W_CK)),
            _const_spec((C_HEADS, C_DK, C_DV)),
            _const_spec((1, W_CV), o),
            _const_spec((D_GROUPS, D_CHUNK, D_CHUNK), o),
            _const_spec((D_CHUNK, D_GROUPS), o),
            _const_spec((1, W_D), o),
            _const_spec((1, W_D), o),
        ] + cast_in,
        out_specs=[
            pl.BlockSpec((BATCH, RET_CHUNK, D_MODEL), lambda c: (0, c, 0)),
            pl.BlockSpec((BATCH, C_HEADS, C_DK, C_DV), lambda c: (0, 0, 0, 0)),
        ] + cast_out,
        out_shape=[
            jax.ShapeDtypeStruct((BATCH, SEQ, D_MODEL), BF16),
            jax.ShapeDtypeStruct((BATCH, C_HEADS, C_DK, C_DV), F32),
        ] + cast_shapes,
        scratch_shapes=[
            pltpu.VMEM((ODD_GROUP * RET_CHUNK, ODD_IN), F32),
            pltpu.VMEM((ODD_GROUP * RET_CHUNK, ODD_IN), F32),
            pltpu.VMEM((BATCH, C_HEADS, C_DK, C_DV), F32),
        ],
        compiler_params=_params(("arbitrary",), 52),
        name="odd_prompt",
    )(x3d, w_stack, cos, sin, dec, cs, kd, sdec, gn, spw, spb_t, lng, lnb, *[w for w, _ in casts])


def _odd_sample_kernel(proj_ref, s0_ref, cos_ref, sin_ref, dec_ref, cs_ref, kd_ref, sdec_ref, gn_ref,
                       spw_ref, spb_ref, lng_ref, lnb_ref, *rest):
    y_ref, s_out_ref, vd_ref = rest[-3:]
    if len(rest) > 3:
        s_out_ref[:, 0] = rest[0][...]
        s_out_ref = s_out_ref.at[:, 1]
    q = _rope(proj_ref[:, :, 0:W_CK], cos_ref[...], sin_ref[...])
    k = _rope(proj_ref[:, :, W_CK:2 * W_CK], cos_ref[...], sin_ref[...]) * (C_DK ** -0.5)
    kd = k * kd_ref[...]
    v0 = 2 * W_CK
    g0 = v0 + W_CV
    for h in range(C_HEADS):
        qh = q[:, :, h * C_DK:(h + 1) * C_DK].astype(BF16)
        kh = k[:, :, h * C_DK:(h + 1) * C_DK].astype(BF16)
        kdh = kd[:, :, h * C_DK:(h + 1) * C_DK].astype(BF16)
        vh = proj_ref[:, :, v0 + h * C_DV:v0 + (h + 1) * C_DV].astype(BF16)
        s_prev = s0_ref[:, h]
        scores = jnp.einsum('nld,nmd->nlm', qh, kh, preferred_element_type=F32) * dec_ref[h]
        o = (jnp.einsum('nlm,nme->nle', scores.astype(BF16), vh, preferred_element_type=F32)
             + jnp.einsum('nld,nde->nle', qh, s_prev.astype(BF16), preferred_element_type=F32)
             * cs_ref[:, h * C_DV:(h + 1) * C_DV])
        s_out_ref[:, h] = sdec_ref[h] * s_prev + jnp.einsum('nld,nle->nde', kdh, vh, preferred_element_type=F32)
        gate = jax.nn.silu(proj_ref[:, :, g0 + h * C_DV:g0 + (h + 1) * C_DV])
        y_ref[:, :, h * C_DV:(h + 1) * C_DV] = (gate * (_group_norm(o) * gn_ref[:, h * C_DV:(h + 1) * C_DV])).astype(BF16)

    u0 = g0 + W_CV
    u = jax.nn.gelu(proj_ref[:, :, u0:u0 + W_D])
    vd = _layer_norm(jax.nn.gelu(proj_ref[:, :, u0 + W_D:]), lng_ref[...], lnb_ref[...])
    vd_ref[...] = vd
    s = spb_ref[...] + spw_ref[0] * vd[:, 0:1, :]
    for m in range(1, DEC_SEQ):
        s = s + spw_ref[m] * vd[:, m:m + 1, :]
    y_ref[:, :, W_CV:] = (u * s).astype(BF16)


def _odd_sample(proj, state_ret, o, cos, sin, tabs, gn, spw_rows, spb_rows, lng, lnb, prev_states):
    nb = SAMPLE_NB
    dec, cs, kd, sdec = tabs
    seq3 = lambda i: (i, 0, 0)
    one_state = pl.BlockSpec((nb, C_HEADS, C_DK, C_DV), lambda i: (i, 0, 0, 0))
    in_specs = [
        pl.BlockSpec((nb, DEC_SEQ, ODD_IN), seq3),
        pl.BlockSpec((nb, None, C_HEADS, C_DK, C_DV), lambda i: (i, o, 0, 0, 0)),
        _const_spec((DEC_SEQ, W_CK)),
        _const_spec((DEC_SEQ, W_CK)),
        _const_spec((C_HEADS, DEC_SEQ, DEC_SEQ)),
        _const_spec((DEC_SEQ, W_CV)),
        _const_spec((DEC_SEQ, W_CK)),
        _const_spec((C_HEADS, C_DK, C_DV)),
        _const_spec((1, W_CV), o),
        _const_spec((DEC_SEQ, DEC_SEQ, W_D), o),
        _const_spec((DEC_SEQ, W_D), o),
        _const_spec((1, W_D), o),
        _const_spec((1, W_D), o),
    ]
    args = [proj, state_ret, cos, sin, dec, cs, kd, sdec, gn, spw_rows, spb_rows, lng, lnb]
    if prev_states is None:
        state_spec = one_state
        state_shape = (DEC_BATCH, C_HEADS, C_DK, C_DV)
    else:
        in_specs.append(one_state)
        args.append(prev_states)
        state_spec = pl.BlockSpec((nb, N_ODD, C_HEADS, C_DK, C_DV), lambda i: (i, 0, 0, 0, 0))
        state_shape = (DEC_BATCH, N_ODD, C_HEADS, C_DK, C_DV)
    return pl.pallas_call(
        _odd_sample_kernel,
        grid=(DEC_BATCH // nb,),
        in_specs=in_specs,
        out_specs=[pl.BlockSpec((nb, DEC_SEQ, D_MODEL), seq3), state_spec, pl.BlockSpec((nb, DEC_SEQ, W_D), seq3)],
        out_shape=[
            jax.ShapeDtypeStruct((DEC_BATCH, DEC_SEQ, D_MODEL), BF16),
            jax.ShapeDtypeStruct(state_shape, F32),
            jax.ShapeDtypeStruct((DEC_BATCH, DEC_SEQ, W_D), F32),
        ],
        compiler_params=_params(("parallel",), 48),
        name="odd_sample",
    )(*args)


def _rope_tables(pos):
    half = C_DK // 2
    freq = ROPE_BASE ** (-np.arange(half, dtype=np.float64) / half)
    ang = np.asarray(pos, np.float64)[:, None] * freq
    cos, sin = np.cos(ang), np.sin(ang)
    cos_l = np.tile(np.concatenate([cos, cos], axis=-1), (1, C_HEADS))
    sin_l = np.tile(np.concatenate([-sin, sin], axis=-1), (1, C_HEADS))
    return cos_l.astype(np.float32), sin_l.astype(np.float32)


def _retention_tables(length):
    log_g = np.log1p(-np.exp2(-5.0 - np.arange(C_HEADS, dtype=np.float64)))
    idx = np.arange(length, dtype=np.float64)
    rel = idx[:, None] - idx[None, :]
    decay = np.where(rel >= 0, np.exp(log_g[:, None, None] * np.maximum(rel, 0.0)), 0.0)
    cross = np.exp(log_g[None, :] * (idx[:, None] + 1.0))
    kdec = np.exp(log_g[None, :] * (length - 1.0 - idx[:, None]))
    sdec = np.exp(log_g * length)
    tabs = (decay,
            np.repeat(cross, C_DV, axis=1),
            np.repeat(kdec, C_DK, axis=1),
            np.broadcast_to(sdec[:, None, None], (C_HEADS, C_DK, C_DV)))
    return tuple(np.ascontiguousarray(t, dtype=np.float32) for t in tabs)


def _block_diag_halves(wa, wx):
    nb = A_BLOCKS // 2
    width = nb * A_BLOCK

    def bd(w, hh):
        rows = [jnp.pad(w[:, hh * nb + g], ((0, 0), (0, 0), (g * A_BLOCK, width - (g + 1) * A_BLOCK)))
                for g in range(nb)]
        return jnp.concatenate(rows, axis=1)

    halves = [jnp.concatenate([bd(wa, hh), bd(wx, hh)], axis=-1) for hh in range(2)]
    return jnp.stack(halves, axis=1).astype(BF16)


def kernel(x_prompt, x_sample, state_lru_h, state_lru_conv, state_sconv, state_ret, w_in_even, conv_a_w, conv_a_b, lru_wa, lru_ba, lru_wx, lru_bx, lru_lam, conv_b_w, w_out_even, w_in_odd, ret_gn_g, sp_w, sp_b, gm_ln_g, gm_ln_b, w_out_odd, ffn_w1, ffn_w3, ffn_w2, ln1_g, ln1_b, ln2_g, ln2_b):
    rows = lambda v: v.reshape(v.shape[0], 1, v.shape[1])
    xp = x_prompt.reshape(BATCH * SEQ, D_MODEL)
    xs = jnp.swapaxes(x_sample, 0, 1).reshape(DEC_SEQ * DEC_BATCH, D_MODEL)

    cos_p, sin_p = _rope_tables(np.arange(SEQ))
    cos_s, sin_s = _rope_tables(PAST_LEN + np.arange(DEC_SEQ))
    tab_p = _retention_tables(RET_CHUNK)
    tab_s = _retention_tables(DEC_SEQ)

    w_in_even_b, w_in_odd_b = w_in_even.astype(BF16), w_in_odd.astype(BF16)
    ln1, ln2 = (rows(ln1_g), rows(ln1_b)), (rows(ln2_g), rows(ln2_b))
    even_p = (conv_a_w, rows(conv_a_b), _block_diag_halves(lru_wa, lru_wx), rows(lru_ba), rows(lru_bx),
              rows(lru_lam), conv_b_w)
    ca_tm = jnp.transpose(state_lru_conv, (1, 2, 0, 3))
    sc_tm = jnp.transpose(state_sconv, (1, 2, 0, 3))
    h0_tm = jnp.swapaxes(state_lru_h, 0, 1)
    tril = jnp.tril(sp_w[:, :, :DEC_SEQ, :DEC_SEQ])
    spw_rows = jnp.repeat(jnp.transpose(tril, (0, 3, 2, 1)), D_GROUP, axis=3)
    spb_rows = jnp.repeat(jnp.swapaxes(sp_b[:, :, :DEC_SEQ], 1, 2), D_GROUP, axis=2)
    spb_t = jnp.swapaxes(sp_b, 1, 2)
    odd_small = (rows(ret_gn_g),)
    odd_ln = (rows(gm_ln_g), rows(gm_ln_b))

    h_p, h_s, ca_p, ca_s, sc_p, sc_s, r_p, v_s = [], [], [], [], [], [], [], []
    ret_sample = None
    for l in range(DEPTH):
        if l % 2 == 0:
            e = l // 2
            casts = [(w_out_even, e), (ffn_w1, l), (ffn_w3, l), (ffn_w2, l)]
            yp, hlp, cap, scp, wo, w1, w3, w2 = _even_prompt(xp.reshape(BATCH, SEQ, D_MODEL), w_in_even_b, e,
                                                             even_p, casts)
            proj_s = _in_proj(xs, w_in_even_b, e).reshape(DEC_SEQ, DEC_BATCH, EVEN_IN)
            ys, hls, cas, scs = _even_sample(proj_s, ca_tm, sc_tm, h0_tm, e, even_p)
            h_p.append(hlp)
            h_s.append(hls)
            ca_p.append(cap[:, SUBLANES - (A_CONV - 1):])
            ca_s.append(jnp.swapaxes(cas, 0, 1))
            sc_p.append(scp[:, SUBLANES - (B_CONV - 1):])
            sc_s.append(jnp.swapaxes(scs, 0, 1))
            yp = yp.reshape(BATCH * SEQ, D_MODEL)
            ys = ys.reshape(DEC_SEQ * DEC_BATCH, D_MODEL)
        else:
            o = l // 2
            casts = [(w_out_odd, o), (ffn_w1, l), (ffn_w3, l), (ffn_w2, l)]
            yp, rp, wo, w1, w3, w2 = _odd_prompt(xp.reshape(BATCH, SEQ, D_MODEL), w_in_odd_b, o, cos_p, sin_p, tab_p,
                                                 *odd_small, sp_w, spb_t, *odd_ln, casts)
            proj_s = _in_proj(xs, w_in_odd_b, o).reshape(DEC_BATCH, DEC_SEQ, ODD_IN)
            ys, ret_sample, vs = _odd_sample(proj_s, state_ret, o, cos_s, sin_s, tab_s, *odd_small,
                                             spw_rows, spb_rows, *odd_ln, ret_sample)
            r_p.append(rp)
            v_s.append(vs)
            yp = yp.reshape(BATCH * SEQ, D_MODEL)
            ys = ys.reshape(DEC_BATCH * DEC_SEQ, D_MODEL)
        xp, xs = _dense_block(yp, xp, ys, xs, l, wo, *ln1, w1, w3, w2, *ln2)
        if l + 1 < DEPTH:
            if l % 2 == 0:
                xs = jnp.swapaxes(xs.reshape(DEC_SEQ, DEC_BATCH, D_MODEL), 0, 1).reshape(DEC_BATCH * DEC_SEQ, D_MODEL)
            else:
                xs = jnp.swapaxes(xs.reshape(DEC_BATCH, DEC_SEQ, D_MODEL), 0, 1).reshape(DEC_SEQ * DEC_BATCH, D_MODEL)

    return (xp.reshape(BATCH, SEQ, D_MODEL), xs.reshape(DEC_BATCH, DEC_SEQ, D_MODEL),
            jnp.stack(h_p, axis=1), jnp.stack(h_s, axis=1),
            jnp.stack(ca_p, axis=1), jnp.stack(ca_s, axis=1),
            jnp.stack(sc_p, axis=1), jnp.stack(sc_s, axis=1),
            jnp.stack(r_p, axis=1), ret_sample, jnp.stack(v_s, axis=1))
```

```python
import functools

import jax
import jax.numpy as jnp
import numpy as np
from jax import lax
from jax.experimental import pallas as pl
from jax.experimental.pallas import tpu as pltpu

F32 = jnp.float32
BF16 = jnp.bfloat16

D_MODEL = 1024
BATCH = 8
SEQ = 2048
DEPTH = 4
DEC_BATCH = 128
DEC_SEQ = 4
PAST_LEN = 16384
N_ODD = DEPTH // 2
W_A = 512
A_BLOCKS = 8
A_BLOCK = 64
A_CONV = 4
LRU_C = 8.0
W_B = 512
B_CONV = 3
C_HEADS = 4
C_DK = 64
C_DV = 128
W_CK = C_HEADS * C_DK
W_CV = C_HEADS * C_DV
RET_CHUNK = 128
ROPE_BASE = 10000.0
D_GROUPS = 4
D_CHUNK = 128
W_D = 512
D_GROUP = W_D // D_GROUPS
EVEN_IN = 2 * W_A + 3 * W_B
ODD_IN = 2 * W_CK + 2 * W_CV + 2 * W_D
D_FF = 2816
ALPHA = (2 * DEPTH) ** 0.25
LN_EPS = 1e-5

SUBLANES = 8
LANES = 128
MIB = 1024 * 1024

ROW_TILE = 512
DENSE_TILE = 512
DENSE_GROUP = 256
DENSE_COLS = 256
EVEN_TT = 128
EVEN_GROUP = 1
ODD_GROUP = 2
EVEN_PITCH = EVEN_TT + SUBLANES
SAMPLE_NB = 32


def _params(sem, vmem_mib):
    return pltpu.CompilerParams(dimension_semantics=sem, vmem_limit_bytes=vmem_mib * MIB)


def _const_spec(shape, layer=None):
    nd = len(shape)
    if layer is None:
        return pl.BlockSpec(shape, lambda *_: (0,) * nd, pipeline_mode=pl.Buffered(1))
    return pl.BlockSpec((None,) + tuple(shape), lambda *_: (layer,) + (0,) * nd, pipeline_mode=pl.Buffered(1))


def _layer_norm(x, g, b):
    mu = jnp.mean(x, -1, keepdims=True)
    xc = x - mu
    var = jnp.mean(xc * xc, -1, keepdims=True)
    return xc * lax.rsqrt(var + LN_EPS) * g + b


def _softplus(x):
    return jnp.maximum(x, 0.0) + jnp.log1p(jnp.exp(-jnp.abs(x)))


def _dot(a, b):
    return jnp.dot(a, b, preferred_element_type=F32)


N_CAST = 4


def _cast_specs(stacks_and_layers, nsteps):
    in_specs, out_specs, out_shapes = [], [], []
    for w, layer in stacks_and_layers:
        nrow, ncol = w.shape[1:]
        rb = nrow // nsteps
        assert nrow % nsteps == 0 and rb % (2 * SUBLANES) == 0
        in_specs.append(pl.BlockSpec((None, rb, ncol), lambda j, layer=layer: (layer, j, 0)))
        out_specs.append(pl.BlockSpec((rb, ncol), lambda j: (j, 0)))
        out_shapes.append(jax.ShapeDtypeStruct((nrow, ncol), BF16))
    return in_specs, out_specs, out_shapes


def _cast_slabs(src_refs, dst_refs):
    for src, dst in zip(src_refs, dst_refs):
        dst[...] = src[...].astype(BF16)


def _proj_kernel(x_ref, w_ref, o_ref):
    o_ref[...] = _dot(x_ref[...].astype(BF16), w_ref[...])


def _in_proj(x2d, w_stack, layer):
    m, k = x2d.shape
    n = w_stack.shape[2]
    tm = min(ROW_TILE, m)
    return pl.pallas_call(
        _proj_kernel,
        grid=(m // tm,),
        in_specs=[pl.BlockSpec((tm, k), lambda i: (i, 0)), _const_spec((k, n), layer)],
        out_specs=pl.BlockSpec((tm, n), lambda i: (i, 0)),
        out_shape=jax.ShapeDtypeStruct((m, n), F32),
        compiler_params=_params(("parallel",), 40),
        name="in_proj",
    )(x2d, w_stack)


def _dense_kernel(yp_ref, xp_ref, ys_ref, xs_ref, wo_ref, g1_ref, b1_ref, w1_ref, w3_ref, w2_ref, g2_ref, b2_ref,
                  op_ref, os_ref, h_s, *, cols):
    n_prompt = pl.num_programs(0) - 1
    weights = (wo_ref, g1_ref, b1_ref, w1_ref, w3_ref, w2_ref, g2_ref, b2_ref)

    @pl.when(pl.program_id(0) < n_prompt)
    def _():
        _dense_rows(yp_ref, xp_ref, *weights, op_ref, h_s, cols)

    @pl.when(pl.program_id(0) == n_prompt)
    def _():
        _dense_rows(ys_ref, xs_ref, *weights, os_ref, h_s, cols)


def _dense_rows(y_ref, x_ref, wo_ref, g1_ref, b1_ref, w1_ref, w3_ref, w2_ref, g2_ref, b2_ref, o_ref, h_s, ncols):
    rows = DENSE_GROUP
    nparts = y_ref.shape[0] // rows
    sls = [slice(p * rows, (p + 1) * rows) for p in range(nparts)]
    mix, x1, xb = {}, {}, {}

    def out_proj(p):
        mix[p] = _dot(y_ref[sls[p], :], wo_ref[...])

    def norm1(p):
        x1[p] = _layer_norm(ALPHA * x_ref[sls[p], :] + mix[p], g1_ref[...], b1_ref[...])
        xb[p] = x1[p].astype(BF16)

    def gate_up(p):
        for c in range(0, D_FF, ncols):
            cols = slice(c, c + ncols)
            h_s[sls[p], cols] = (jax.nn.silu(_dot(xb[p], w1_ref[:, cols]))
                                 * _dot(xb[p], w3_ref[:, cols])).astype(BF16)

    def down_norm2(p):
        f = _dot(h_s[sls[p], :], w2_ref[...])
        o_ref[sls[p], :] = _layer_norm(ALPHA * x1[p] + f, g2_ref[...], b2_ref[...])

    out_proj(0)
    for p in range(nparts):
        norm1(p)
        if p + 1 < nparts:
            out_proj(p + 1)
    for p in range(nparts):
        gate_up(p)
    for p in range(nparts):
        down_norm2(p)


def _dense_block(yp, xp, ys, xs, layer, wo, g1, b1, w1, w3, w2, g2, b2):
    tm = DENSE_TILE
    n_prompt = xp.shape[0] // tm
    assert xp.shape[0] % tm == 0 and xs.shape[0] == tm and tm % DENSE_GROUP == 0
    prompt_row = lambda i: (jnp.minimum(i, n_prompt - 1), 0)
    sample_row = lambda i: (0, 0)
    return pl.pallas_call(
        functools.partial(_dense_kernel, cols=(DENSE_COLS if layer < 2 else D_FF)),
        grid=(n_prompt + 1,),
        in_specs=[
            pl.BlockSpec((tm, D_MODEL), prompt_row),
            pl.BlockSpec((tm, D_MODEL), prompt_row),
            pl.BlockSpec((tm, D_MODEL), sample_row),
            pl.BlockSpec((tm, D_MODEL), sample_row),
            _const_spec((D_MODEL, D_MODEL)),
            _const_spec((1, D_MODEL), layer),
            _const_spec((1, D_MODEL), layer),
            _const_spec((D_MODEL, D_FF)),
            _const_spec((D_MODEL, D_FF)),
            _const_spec((D_FF, D_MODEL)),
            _const_spec((1, D_MODEL), layer),
            _const_spec((1, D_MODEL), layer),
        ],
        out_specs=[pl.BlockSpec((tm, D_MODEL), prompt_row), pl.BlockSpec((tm, D_MODEL), sample_row)],
        out_shape=[jax.ShapeDtypeStruct(xp.shape, F32), jax.ShapeDtypeStruct(xs.shape, F32)],
        scratch_shapes=[pltpu.VMEM((tm, D_FF), BF16)],
        compiler_params=_params(("arbitrary",), 58),
        name="dense_block",
    )(yp, xp, ys, xs, wo, g1, b1, w1, w3, w2, g2, b2)


def _lru_gates(xc, wblk_ref, ba, bx, sp):
    half = W_A // 2
    xb = xc.astype(BF16)
    pre0 = _dot(xb[:, :half], wblk_ref[0])
    pre1 = _dot(xb[:, half:], wblk_ref[1])
    r = jax.nn.sigmoid(jnp.concatenate([pre0[:, :half], pre1[:, :half]], axis=1) + ba)
    i = jax.nn.sigmoid(jnp.concatenate([pre0[:, half:], pre1[:, half:]], axis=1) + bx)
    log_a = (-LRU_C) * r * sp
    a = jnp.exp(log_a)
    mult = jnp.sqrt((1.0 + a * a) * jnp.tanh(-log_a))
    return a, mult, i


def _causal_taps(x_slabs, ext, b, w_ref, nrows):
    ktaps = w_ref.shape[0]
    out = []
    for s, x in enumerate(x_slabs):
        lanes = slice(s * LANES, (s + 1) * LANES)
        ext[s, b, SUBLANES:SUBLANES + nrows, :] = x
        acc = w_ref[ktaps - 1:ktaps, lanes] * x
        for k in range(ktaps - 1):
            off = SUBLANES - (ktaps - 1) + k
            acc = acc + w_ref[k:k + 1, lanes] * ext[s, b, off:off + nrows, :]
        out.append(acc)
    return jnp.concatenate(out, axis=1)


def _even_prompt_kernel(x_ref, w_ref, caw_ref, cab_ref, wblk_ref, ba_ref, bx_ref, lam_ref, cbw_ref, *rest,
                        time_major):
    cast_src, rest = rest[:N_CAST], rest[N_CAST:]
    y_ref, hlast_ref, ca_out_ref, sc_out_ref = rest[:4]
    cast_dst, rest = rest[4:4 + N_CAST], rest[4 + N_CAST:]
    p0, p1, xa_ext, cb_ext, a_s, b_s, g_s, h_c = rest
    tt, pitch = EVEN_TT, EVEN_PITCH
    nslab = W_A // LANES
    j = pl.program_id(0)
    _cast_slabs(cast_src, cast_dst)
    bufs = (p0, p1)
    group = p0.shape[0] // tt

    def project(g):
        xg = x_ref[g * group:(g + 1) * group].reshape(group * tt, D_MODEL)
        bufs[g % 2][...] = _dot(xg.astype(BF16), w_ref[...])

    @pl.when(j == 0)
    def _():
        xa_ext[:, :, 0:SUBLANES, :] = jnp.zeros((nslab, BATCH, SUBLANES, LANES), F32)
        cb_ext[:, :, 0:SUBLANES, :] = jnp.zeros((nslab, BATCH, SUBLANES, LANES), F32)
        h_c[...] = jnp.zeros(h_c.shape, F32)

    @pl.when(j > 0)
    def _():
        xa_ext[:, :, 0:SUBLANES, :] = xa_ext[:, :, tt:tt + SUBLANES, :]
        cb_ext[:, :, 0:SUBLANES, :] = cb_ext[:, :, tt:tt + SUBLANES, :]

    sp = _softplus(-lam_ref[...])
    ba = ba_ref[...]
    bx = bx_ref[...]
    cab = cab_ref[...]
    first_row = (lax.broadcasted_iota(jnp.int32, (SUBLANES, 1), 0) == 0) & (j == 0)
    cg0 = 2 * W_A + W_B
    xb0 = 2 * W_A + 2 * W_B

    def stage(b):
        p_ref = bufs[(b // group) % 2].at[(b % group) * tt:(b % group + 1) * tt]
        xc = cab + _causal_taps([p_ref[:, s * LANES:(s + 1) * LANES] for s in range(nslab)], xa_ext, b, caw_ref, tt)
        a, mult, gate = _lru_gates(xc, wblk_ref, ba, bx, sp)
        mult = jnp.concatenate([jnp.where(first_row, 1.0, mult[:SUBLANES]), mult[SUBLANES:]], axis=0)
        bb = mult * (gate * xc)
        for s in range(nslab):
            rows = pl.ds(b, tt, stride=BATCH) if time_major else pl.ds(b * pitch, tt)
            a_s[s, rows, :] = a[:, s * LANES:(s + 1) * LANES]
            b_s[s, rows, :] = bb[:, s * LANES:(s + 1) * LANES]
        g_s[b] = jax.nn.gelu(p_ref[:, W_A:2 * W_A])
        cb = [p_ref[:, cg0 + s * LANES:cg0 + (s + 1) * LANES] * p_ref[:, xb0 + s * LANES:xb0 + (s + 1) * LANES]
              for s in range(nslab)]
        zb = _causal_taps(cb, cb_ext, b, cbw_ref, tt)
        y_ref[b, :, W_A:] = (p_ref[:, 2 * W_A:2 * W_A + W_B] * zb).astype(BF16)

    project(0)
    for g in range(BATCH // group):
        if g + 1 < BATCH // group:
            project(g + 1)
        for b in range(g * group, (g + 1) * group):
            stage(b)

    def step(t, hs):
        out = []
        for s in range(nslab):
            rows = pl.ds(pl.multiple_of(t * BATCH, BATCH), BATCH) if time_major else pl.ds(t, BATCH, stride=pitch)
            hn = a_s[s, rows, :] * hs[s] + b_s[s, rows, :]
            b_s[s, rows, :] = hn
            out.append(hn)
        return tuple(out)

    hs = lax.fori_loop(0, tt, step, tuple(h_c[s] for s in range(nslab)), unroll=4)
    for s in range(nslab):
        h_c[s] = hs[s]

    def emit(b, carry):
        rows = pl.ds(b, tt, stride=BATCH) if time_major else pl.ds(pl.multiple_of(b * pitch, SUBLANES), tt)
        h = jnp.concatenate([b_s[s, rows, :] for s in range(nslab)], axis=1)
        y_ref[b, :, 0:W_A] = (h * g_s[b]).astype(BF16)
        return carry

    lax.fori_loop(0, BATCH, emit, 0)

    @pl.when(j == pl.num_programs(0) - 1)
    def _():
        hlast_ref[...] = jnp.concatenate(list(hs), axis=1)
        for s in range(nslab):
            ca_out_ref[:, :, s * LANES:(s + 1) * LANES] = xa_ext[s, :, tt:tt + SUBLANES, :]
            sc_out_ref[:, :, s * LANES:(s + 1) * LANES] = cb_ext[s, :, tt:tt + SUBLANES, :]


def _even_param_specs(e):
    return [
        _const_spec((A_CONV, W_A), e),
        _const_spec((1, W_A), e),
        _const_spec((2, W_A // 2, W_A), e),
        _const_spec((1, W_A), e),
        _const_spec((1, W_A), e),
        _const_spec((1, W_A), e),
        _const_spec((B_CONV, W_B), e),
    ]


def _even_prompt(x3d, w_stack, e, params, casts):
    tt, pitch = EVEN_TT, EVEN_PITCH
    nslab = W_A // LANES
    nblock = SEQ // tt
    assert W_A == W_B and len(casts) == N_CAST
    cast_in, cast_out, cast_shapes = _cast_specs(casts, nblock)
    return pl.pallas_call(
        functools.partial(_even_prompt_kernel, time_major=(e == 1)),
        grid=(nblock,),
        in_specs=[pl.BlockSpec((BATCH, tt, D_MODEL), lambda j: (0, j, 0)),
                  _const_spec((D_MODEL, EVEN_IN), e)] + _even_param_specs(e) + cast_in,
        out_specs=[
            pl.BlockSpec((BATCH, tt, D_MODEL), lambda j: (0, j, 0)),
            pl.BlockSpec((BATCH, W_A), lambda j: (0, 0)),
            pl.BlockSpec((BATCH, SUBLANES, W_A), lambda j: (0, 0, 0)),
            pl.BlockSpec((BATCH, SUBLANES, W_B), lambda j: (0, 0, 0)),
        ] + cast_out,
        out_shape=[
            jax.ShapeDtypeStruct((BATCH, SEQ, D_MODEL), BF16),
            jax.ShapeDtypeStruct((BATCH, W_A), F32),
            jax.ShapeDtypeStruct((BATCH, SUBLANES, W_A), F32),
            jax.ShapeDtypeStruct((BATCH, SUBLANES, W_B), F32),
        ] + cast_shapes,
        scratch_shapes=[
            pltpu.VMEM((EVEN_GROUP * tt, EVEN_IN), F32),
            pltpu.VMEM((EVEN_GROUP * tt, EVEN_IN), F32),
            pltpu.VMEM((nslab, BATCH, tt + SUBLANES, LANES), F32),
            pltpu.VMEM((nslab, BATCH, tt + SUBLANES, LANES), F32),
            pltpu.VMEM((nslab, BATCH * pitch, LANES), F32),
            pltpu.VMEM((nslab, BATCH * pitch, LANES), F32),
            pltpu.VMEM((BATCH, tt, W_A), F32),
            pltpu.VMEM((nslab, BATCH, LANES), F32),
        ],
        compiler_params=_params(("arbitrary",), 52),
        name="even_prompt",
    )(x3d, w_stack, *params, *[w for w, _ in casts])


def _even_sample_kernel(proj_ref, ca_ref, sc_ref, h0_ref, caw_ref, cab_ref, wblk_ref, ba_ref, bx_ref,
                        lam_ref, cbw_ref, y_ref, hlast_ref, ca_out_ref, sc_out_ref):
    sp = _softplus(-lam_ref[...])
    rows_a = [ca_ref[k] for k in range(A_CONV - 1)] + [proj_ref[l, :, 0:W_A] for l in range(DEC_SEQ)]
    xc = []
    for l in range(DEC_SEQ):
        acc = cab_ref[...] + caw_ref[0:1, :] * rows_a[l]
        for k in range(1, A_CONV):
            acc = acc + caw_ref[k:k + 1, :] * rows_a[l + k]
        xc.append(acc)
    xc_all = jnp.concatenate(xc, axis=0)
    a, mult, gate = _lru_gates(xc_all, wblk_ref, ba_ref[...], bx_ref[...], sp)
    bb = mult * (gate * xc_all)
    h = h0_ref[...]
    rows_b = [sc_ref[k] for k in range(B_CONV - 1)]
    for l in range(DEC_SEQ):
        sl = slice(l * DEC_BATCH, (l + 1) * DEC_BATCH)
        h = a[sl] * h + bb[sl]
        y_ref[l, :, 0:W_A] = (h * jax.nn.gelu(proj_ref[l, :, W_A:2 * W_A])).astype(BF16)
        rows_b.append(proj_ref[l, :, 2 * W_A + W_B:2 * W_A + 2 * W_B] * proj_ref[l, :, 2 * W_A + 2 * W_B:])
    hlast_ref[...] = h
    for l in range(DEC_SEQ):
        zb = cbw_ref[0:1, :] * rows_b[l]
        for k in range(1, B_CONV):
            zb = zb + cbw_ref[k:k + 1, :] * rows_b[l + k]
        y_ref[l, :, W_A:] = (proj_ref[l, :, 2 * W_A:2 * W_A + W_B] * zb).astype(BF16)
    for k in range(A_CONV - 1):
        ca_out_ref[k] = rows_a[DEC_SEQ + k]
    for k in range(B_CONV - 1):
        sc_out_ref[k] = rows_b[DEC_SEQ + k]


def _even_sample(proj_tm, ca_tm, sc_tm, h0_tm, e, params):
    whole = lambda shape: pl.BlockSpec(shape, lambda i: (0,) * len(shape))
    return pl.pallas_call(
        _even_sample_kernel,
        grid=(1,),
        in_specs=[
            whole((DEC_SEQ, DEC_BATCH, EVEN_IN)),
            _const_spec((A_CONV - 1, DEC_BATCH, W_A), e),
            _const_spec((B_CONV - 1, DEC_BATCH, W_B), e),
            _const_spec((DEC_BATCH, W_A), e),
        ] + _even_param_specs(e),
        out_specs=[
            whole((DEC_SEQ, DEC_BATCH, D_MODEL)),
            whole((DEC_BATCH, W_A)),
            whole((A_CONV - 1, DEC_BATCH, W_A)),
            whole((B_CONV - 1, DEC_BATCH, W_B)),
        ],
        out_shape=[
            jax.ShapeDtypeStruct((DEC_SEQ, DEC_BATCH, D_MODEL), BF16),
            jax.ShapeDtypeStruct((DEC_BATCH, W_A), F32),
            jax.ShapeDtypeStruct((A_CONV - 1, DEC_BATCH, W_A), F32),
            jax.ShapeDtypeStruct((B_CONV - 1, DEC_BATCH, W_B), F32),
        ],
        compiler_params=_params(("arbitrary",), 40),
        name="even_sample",
    )(proj_tm, ca_tm, sc_tm, h0_tm, *params)


def _rope(x, cos, sin_signed):
    half = C_DK // 2
    ax = x.ndim - 1
    pieces = []
    for p in range(W_CK // LANES):
        xv = x[..., p * LANES:(p + 1) * LANES]
        lane = lax.broadcasted_iota(jnp.int32, xv.shape, ax)
        partner = jnp.where((lane % C_DK) < half, pltpu.roll(xv, LANES - half, axis=ax), pltpu.roll(xv, half, axis=ax))
        pieces.append(xv * cos[..., p * LANES:(p + 1) * LANES] + partner * sin_signed[..., p * LANES:(p + 1) * LANES])
    return jnp.concatenate(pieces, axis=ax)


def _group_norm(o):
    mu = jnp.mean(o, -1, keepdims=True)
    oc = o - mu
    var = jnp.mean(oc * oc, -1, keepdims=True)
    return oc * lax.rsqrt(var + LN_EPS)


def _odd_chunk(p_ref, b, cos_ref, sin_ref, dec_ref, cs_ref, kd_ref, sdec_ref, gn_ref,
               spw_ref, spb_ref, lng_ref, lnb_ref, y_ref, s_c):
    q = _rope(p_ref[:, 0:W_CK], cos_ref[...], sin_ref[...])
    k = _rope(p_ref[:, W_CK:2 * W_CK], cos_ref[...], sin_ref[...]) * (C_DK ** -0.5)
    kd = k * kd_ref[...]
    v0 = 2 * W_CK
    g0 = v0 + W_CV
    for h in range(C_HEADS):
        qh = q[:, h * C_DK:(h + 1) * C_DK].astype(BF16)
        kh = k[:, h * C_DK:(h + 1) * C_DK].astype(BF16)
        kdh = kd[:, h * C_DK:(h + 1) * C_DK].astype(BF16)
        vh = p_ref[:, v0 + h * C_DV:v0 + (h + 1) * C_DV].astype(BF16)
        s_prev = s_c[b, h]
        scores = lax.dot_general(qh, kh, (((1,), (1,)), ((), ())), preferred_element_type=F32) * dec_ref[h]
        o = _dot(scores.astype(BF16), vh) + _dot(qh, s_prev.astype(BF16)) * cs_ref[:, h * C_DV:(h + 1) * C_DV]
        s_c[b, h] = sdec_ref[h] * s_prev + lax.dot_general(kdh, vh, (((0,), (0,)), ((), ())), preferred_element_type=F32)
        gate = jax.nn.silu(p_ref[:, g0 + h * C_DV:g0 + (h + 1) * C_DV])
        y_ref[b, :, h * C_DV:(h + 1) * C_DV] = (gate * (_group_norm(o) * gn_ref[:, h * C_DV:(h + 1) * C_DV])).astype(BF16)

    u0 = g0 + W_CV
    u = jax.nn.gelu(p_ref[:, u0:u0 + W_D])
    vd = _layer_norm(jax.nn.gelu(p_ref[:, u0 + W_D:]), lng_ref[...], lnb_ref[...]).astype(BF16)
    ri = lax.broadcasted_iota(jnp.int32, (D_CHUNK, D_CHUNK), 0)
    ci = lax.broadcasted_iota(jnp.int32, (D_CHUNK, D_CHUNK), 1)
    for gi in range(D_GROUPS):
        w = jnp.where(ri >= ci, spw_ref[gi], 0.0).astype(BF16)
        s = _dot(w, vd[:, gi * D_GROUP:(gi + 1) * D_GROUP]) + spb_ref[:, gi:gi + 1]
        y_ref[b, :, W_CV + gi * D_GROUP:W_CV + (gi + 1) * D_GROUP] = (u[:, gi * D_GROUP:(gi + 1) * D_GROUP] * s).astype(BF16)


def _odd_prompt_kernel(x_ref, w_ref, cos_ref, sin_ref, dec_ref, cs_ref, kd_ref, sdec_ref, gn_ref,
                       spw_ref, spb_ref, lng_ref, lnb_ref, *rest):
    cast_src, rest = rest[:N_CAST], rest[N_CAST:]
    y_ref, s_out_ref = rest[:2]
    cast_dst, rest = rest[2:2 + N_CAST], rest[2 + N_CAST:]
    p0, p1, s_c = rest
    c = pl.program_id(0)
    _cast_slabs(cast_src, cast_dst)

    @pl.when(c == 0)
    def _():
        s_c[...] = jnp.zeros(s_c.shape, F32)

    bufs = (p0, p1)
    group = p0.shape[0] // RET_CHUNK

    def project(g):
        xg = x_ref[g * group:(g + 1) * group].reshape(group * RET_CHUNK, D_MODEL)
        bufs[g % 2][...] = _dot(xg.astype(BF16), w_ref[...])

    project(0)
    for g in range(BATCH // group):
        if g + 1 < BATCH // group:
            project(g + 1)
        for r in range(group):
            _odd_chunk(bufs[g % 2].at[r * RET_CHUNK:(r + 1) * RET_CHUNK], g * group + r,
                       cos_ref, sin_ref, dec_ref, cs_ref, kd_ref, sdec_ref, gn_ref,
                       spw_ref, spb_ref, lng_ref, lnb_ref, y_ref, s_c)

    @pl.when(c == pl.num_programs(0) - 1)
    def _():
        s_out_ref[...] = s_c[...]


def _odd_prompt(x3d, w_stack, o, cos, sin, tabs, gn, spw, spb_t, lng, lnb, casts):
    nchunk = SEQ // RET_CHUNK
    dec, cs, kd, sdec = tabs
    assert len(casts) == N_CAST
    cast_in, cast_out, cast_shapes = _cast_specs(casts, nchunk)
    return pl.pallas_call(
        _odd_prompt_kernel,
        grid=(nchunk,),
        in_specs=[
            pl.BlockSpec((BATCH, RET_CHUNK, D_MODEL), lambda c: (0, c, 0)),
            _const_spec((D_MODEL, ODD_IN), o),
            pl.BlockSpec((RET_CHUNK, W_CK), lambda c: (c, 0)),
            pl.BlockSpec((RET_CHUNK, W_CK), lambda c: (c, 0)),
            _const_spec((C_HEADS, RET_CHUNK, RET_CHUNK)),
            _const_spec((RET_CHUNK, W_CV)),
            _const_spec((RET_CHUNK, W_CK)),
            _const_spec((C_HEADS, C_DK, C_DV)),
            _const_spec((1, W_CV), o),
            _const_spec((D_GROUPS, D_CHUNK, D_CHUNK), o),
            _const_spec((D_CHUNK, D_GROUPS), o),
            _const_spec((1, W_D), o),
            _const_spec((1, W_D), o),
        ] + cast_in,
        out_specs=[
            pl.BlockSpec((BATCH, RET_CHUNK, D_MODEL), lambda c: (0, c, 0)),
            pl.BlockSpec((BATCH, C_HEADS, C_DK, C_DV), lambda c: (0, 0, 0, 0)),
        ] + cast_out,
        out_shape=[
            jax.ShapeDtypeStruct((BATCH, SEQ, D_MODEL), BF16),
            jax.ShapeDtypeStruct((BATCH, C_HEADS, C_DK, C_DV), F32),
        ] + cast_shapes,
        scratch_shapes=[
            pltpu.VMEM((ODD_GROUP * RET_CHUNK, ODD_IN), F32),
            pltpu.VMEM((ODD_GROUP * RET_CHUNK, ODD_IN), F32),
            pltpu.VMEM((BATCH, C_HEADS, C_DK, C_DV), F32),
        ],
        compiler_params=_params(("arbitrary",), 52),
        name="odd_prompt",
    )(x3d, w_stack, cos, sin, dec, cs, kd, sdec, gn, spw, spb_t, lng, lnb, *[w for w, _ in casts])


def _odd_sample_kernel(proj_ref, s0_ref, cos_ref, sin_ref, dec_ref, cs_ref, kd_ref, sdec_ref, gn_ref,
                       spw_ref, spb_ref, lng_ref, lnb_ref, *rest):
    y_ref, s_out_ref, vd_ref = rest[-3:]
    if len(rest) > 3:
        s_out_ref[:, 0] = rest[0][...]
        s_out_ref = s_out_ref.at[:, 1]
    q = _rope(proj_ref[:, :, 0:W_CK], cos_ref[...], sin_ref[...])
    k = _rope(proj_ref[:, :, W_CK:2 * W_CK], cos_ref[...], sin_ref[...]) * (C_DK ** -0.5)
    kd = k * kd_ref[...]
    v0 = 2 * W_CK
    g0 = v0 + W_CV
    for h in range(C_HEADS):
        qh = q[:, :, h * C_DK:(h + 1) * C_DK].astype(BF16)
        kh = k[:, :, h * C_DK:(h + 1) * C_DK].astype(BF16)
        kdh = kd[:, :, h * C_DK:(h + 1) * C_DK].astype(BF16)
        vh = proj_ref[:, :, v0 + h * C_DV:v0 + (h + 1) * C_DV].astype(BF16)
        s_prev = s0_ref[:, h]
        scores = jnp.einsum('nld,nmd->nlm', qh, kh, preferred_element_type=F32) * dec_ref[h]
        o = (jnp.einsum('nlm,nme->nle', scores.astype(BF16), vh, preferred_element_type=F32)
             + jnp.einsum('nld,nde->nle', qh, s_prev.astype(BF16), preferred_element_type=F32)
             * cs_ref[:, h * C_DV:(h + 1) * C_DV])
        s_out_ref[:, h] = sdec_ref[h] * s_prev + jnp.einsum('nld,nle->nde', kdh, vh, preferred_element_type=F32)
        gate = jax.nn.silu(proj_ref[:, :, g0 + h * C_DV:g0 + (h + 1) * C_DV])
        y_ref[:, :, h * C_DV:(h + 1) * C_DV] = (gate * (_group_norm(o) * gn_ref[:, h * C_DV:(h + 1) * C_DV])).astype(BF16)

    u0 = g0 + W_CV
    u = jax.nn.gelu(proj_ref[:, :, u0:u0 + W_D])
    vd = _layer_norm(jax.nn.gelu(proj_ref[:, :, u0 + W_D:]), lng_ref[...], lnb_ref[...])
    vd_ref[...] = vd
    s = spb_ref[...] + spw_ref[0] * vd[:, 0:1, :]
    for m in range(1, DEC_SEQ):
        s = s + spw_ref[m] * vd[:, m:m + 1, :]
    y_ref[:, :, W_CV:] = (u * s).astype(BF16)


def _odd_sample(proj, state_ret, o, cos, sin, tabs, gn, spw_rows, spb_rows, lng, lnb, prev_states):
    nb = SAMPLE_NB
    dec, cs, kd, sdec = tabs
    seq3 = lambda i: (i, 0, 0)
    one_state = pl.BlockSpec((nb, C_HEADS, C_DK, C_DV), lambda i: (i, 0, 0, 0))
    in_specs = [
        pl.BlockSpec((nb, DEC_SEQ, ODD_IN), seq3),
        pl.BlockSpec((nb, None, C_HEADS, C_DK, C_DV), lambda i: (i, o, 0, 0, 0)),
        _const_spec((DEC_SEQ, W_CK)),
        _const_spec((DEC_SEQ, W_CK)),
        _const_spec((C_HEADS, DEC_SEQ, DEC_SEQ)),
        _const_spec((DEC_SEQ, W_CV)),
        _const_spec((DEC_SEQ, W_CK)),
        _const_spec((C_HEADS, C_DK, C_DV)),
        _const_spec((1, W_CV), o),
        _const_spec((DEC_SEQ, DEC_SEQ, W_D), o),
        _const_spec((DEC_SEQ, W_D), o),
        _const_spec((1, W_D), o),
        _const_spec((1, W_D), o),
    ]
    args = [proj, state_ret, cos, sin, dec, cs, kd, sdec, gn, spw_rows, spb_rows, lng, lnb]
    if prev_states is None:
        state_spec = one_state
        state_shape = (DEC_BATCH, C_HEADS, C_DK, C_DV)
    else:
        in_specs.append(one_state)
        args.append(prev_states)
        state_spec = pl.BlockSpec((nb, N_ODD, C_HEADS, C_DK, C_DV), lambda i: (i, 0, 0, 0, 0))
        state_shape = (DEC_BATCH, N_ODD, C_HEADS, C_DK, C_DV)
    return pl.pallas_call(
        _odd_sample_kernel,
        grid=(DEC_BATCH // nb,),
        in_specs=in_specs,
        out_specs=[pl.BlockSpec((nb, DEC_SEQ, D_MODEL), seq3), state_spec, pl.BlockSpec((nb, DEC_SEQ, W_D), seq3)],
        out_shape=[
            jax.ShapeDtypeStruct((DEC_BATCH, DEC_SEQ, D_MODEL), BF16),
            jax.ShapeDtypeStruct(state_shape, F32),
            jax.ShapeDtypeStruct((DEC_BATCH, DEC_SEQ, W_D), F32),
        ],
        compiler_params=_params(("parallel",), 48),
        name="odd_sample",
    )(*args)


def _rope_tables(pos):
    half = C_DK // 2
    freq = ROPE_BASE ** (-np.arange(half, dtype=np.float64) / half)
    ang = np.asarray(pos, np.float64)[:, None] * freq
    cos, sin = np.cos(ang), np.sin(ang)
    cos_l = np.tile(np.concatenate([cos, cos], axis=-1), (1, C_HEADS))
    sin_l = np.tile(np.concatenate([-sin, sin], axis=-1), (1, C_HEADS))
    return cos_l.astype(np.float32), sin_l.astype(np.float32)


def _retention_tables(length):
    log_g = np.log1p(-np.exp2(-5.0 - np.arange(C_HEADS, dtype=np.float64)))
    idx = np.arange(length, dtype=np.float64)
    rel = idx[:, None] - idx[None, :]
    decay = np.where(rel >= 0, np.exp(log_g[:, None, None] * np.maximum(rel, 0.0)), 0.0)
    cross = np.exp(log_g[None, :] * (idx[:, None] + 1.0))
    kdec = np.exp(log_g[None, :] * (length - 1.0 - idx[:, None]))
    sdec = np.exp(log_g * length)
    tabs = (decay,
            np.repeat(cross, C_DV, axis=1),
            np.repeat(kdec, C_DK, axis=1),
            np.broadcast_to(sdec[:, None, None], (C_HEADS, C_DK, C_DV)))
    return tuple(np.ascontiguousarray(t, dtype=np.float32) for t in tabs)


def _block_diag_halves(wa, wx):
    nb = A_BLOCKS // 2
    width = nb * A_BLOCK

    def bd(w, hh):
        rows = [jnp.pad(w[:, hh * nb + g], ((0, 0), (0, 0), (g * A_BLOCK, width - (g + 1) * A_BLOCK)))
                for g in range(nb)]
        return jnp.concatenate(rows, axis=1)

    halves = [jnp.concatenate([bd(wa, hh), bd(wx, hh)], axis=-1) for hh in range(2)]
    return jnp.stack(halves, axis=1).astype(BF16)


def kernel(x_prompt, x_sample, state_lru_h, state_lru_conv, state_sconv, state_ret, w_in_even, conv_a_w, conv_a_b, lru_wa, lru_ba, lru_wx, lru_bx, lru_lam, conv_b_w, w_out_even, w_in_odd, ret_gn_g, sp_w, sp_b, gm_ln_g, gm_ln_b, w_out_odd, ffn_w1, ffn_w3, ffn_w2, ln1_g, ln1_b, ln2_g, ln2_b):
    rows = lambda v: v.reshape(v.shape[0], 1, v.shape[1])
    xp = x_prompt.reshape(BATCH * SEQ, D_MODEL)
    xs = jnp.swapaxes(x_sample, 0, 1).reshape(DEC_SEQ * DEC_BATCH, D_MODEL)

    cos_p, sin_p = _rope_tables(np.arange(SEQ))
    cos_s, sin_s = _rope_tables(PAST_LEN + np.arange(DEC_SEQ))
    tab_p = _retention_tables(RET_CHUNK)
    tab_s = _retention_tables(DEC_SEQ)

    w_in_even_b, w_in_odd_b = w_in_even.astype(BF16), w_in_odd.astype(BF16)
    ln1, ln2 = (rows(ln1_g), rows(ln1_b)), (rows(ln2_g), rows(ln2_b))
    even_p = (conv_a_w, rows(conv_a_b), _block_diag_halves(lru_wa, lru_wx), rows(lru_ba), rows(lru_bx),
              rows(lru_lam), conv_b_w)
    ca_tm = jnp.transpose(state_lru_conv, (1, 2, 0, 3))
    sc_tm = jnp.transpose(state_sconv, (1, 2, 0, 3))
    h0_tm = jnp.swapaxes(state_lru_h, 0, 1)
    tril = jnp.tril(sp_w[:, :, :DEC_SEQ, :DEC_SEQ])
    spw_rows = jnp.repeat(jnp.transpose(tril, (0, 3, 2, 1)), D_GROUP, axis=3)
    spb_rows = jnp.repeat(jnp.swapaxes(sp_b[:, :, :DEC_SEQ], 1, 2), D_GROUP, axis=2)
    spb_t = jnp.swapaxes(sp_b, 1, 2)
    odd_small = (rows(ret_gn_g),)
    odd_ln = (rows(gm_ln_g), rows(gm_ln_b))

    h_p, h_s, ca_p, ca_s, sc_p, sc_s, r_p, v_s = [], [], [], [], [], [], [], []
    ret_sample = None
    for l in range(DEPTH):
        if l % 2 == 0:
            e = l // 2
            casts = [(w_out_even, e), (ffn_w1, l), (ffn_w3, l), (ffn_w2, l)]
            yp, hlp, cap, scp, wo, w1, w3, w2 = _even_prompt(xp.reshape(BATCH, SEQ, D_MODEL), w_in_even_b, e,
                                                             even_p, casts)
            proj_s = _in_proj(xs, w_in_even_b, e).reshape(DEC_SEQ, DEC_BATCH, EVEN_IN)
            ys, hls, cas, scs = _even_sample(proj_s, ca_tm, sc_tm, h0_tm, e, even_p)
            h_p.append(hlp)
            h_s.append(hls)
            ca_p.append(cap[:, SUBLANES - (A_CONV - 1):])
            ca_s.append(jnp.swapaxes(cas, 0, 1))
            sc_p.append(scp[:, SUBLANES - (B_CONV - 1):])
            sc_s.append(jnp.swapaxes(scs, 0, 1))
            yp = yp.reshape(BATCH * SEQ, D_MODEL)
            ys = ys.reshape(DEC_SEQ * DEC_BATCH, D_MODEL)
        else:
            o = l // 2
            casts = [(w_out_odd, o), (ffn_w1, l), (ffn_w3, l), (ffn_w2, l)]
            yp, rp, wo, w1, w3, w2 = _odd_prompt(xp.reshape(BATCH, SEQ, D_MODEL), w_in_odd_b, o, cos_p, sin_p, tab_p,
                                                 *odd_small, sp_w, spb_t, *odd_ln, casts)
            proj_s = _in_proj(xs, w_in_odd_b, o).reshape(DEC_BATCH, DEC_SEQ, ODD_IN)
            ys, ret_sample, vs = _odd_sample(proj_s, state_ret, o, cos_s, sin_s, tab_s, *odd_small,
                                             spw_rows, spb_rows, *odd_ln, ret_sample)
            r_p.append(rp)
            v_s.append(vs)
            yp = yp.reshape(BATCH * SEQ, D_MODEL)
            ys = ys.reshape(DEC_BATCH * DEC_SEQ, D_MODEL)
        xp, xs = _dense_block(yp, xp, ys, xs, l, wo, *ln1, w1, w3, w2, *ln2)
        if l + 1 < DEPTH:
            if l % 2 == 0:
                xs = jnp.swapaxes(xs.reshape(DEC_SEQ, DEC_BATCH, D_MODEL), 0, 1).reshape(DEC_BATCH * DEC_SEQ, D_MODEL)
            else:
                xs = jnp.swapaxes(xs.reshape(DEC_BATCH, DEC_SEQ, D_MODEL), 0, 1).reshape(DEC_SEQ * DEC_BATCH, D_MODEL)

    return (xp.reshape(BATCH, SEQ, D_MODEL), xs.reshape(DEC_BATCH, DEC_SEQ, D_MODEL),
            jnp.stack(h_p, axis=1), jnp.stack(h_s, axis=1),
            jnp.stack(ca_p, axis=1), jnp.stack(ca_s, axis=1),
            jnp.stack(sc_p, axis=1), jnp.stack(sc_s, axis=1),
            jnp.stack(r_p, axis=1), ret_sample, jnp.stack(v_s, axis=1))
```

```python
import functools

import jax
import jax.numpy as jnp
import numpy as np
from jax import lax
from jax.experimental import pallas as pl
from jax.experimental.pallas import tpu as pltpu

F32 = jnp.float32
BF16 = jnp.bfloat16

D_MODEL = 1024
BATCH = 8
SEQ = 2048
DEPTH = 4
DEC_BATCH = 128
DEC_SEQ = 4
PAST_LEN = 16384
N_ODD = DEPTH // 2
W_A = 512
A_BLOCKS = 8
A_BLOCK = 64
A_CONV = 4
LRU_C = 8.0
W_B = 512
B_CONV = 3
C_HEADS = 4
C_DK = 64
C_DV = 128
W_CK = C_HEADS * C_DK
W_CV = C_HEADS * C_DV
RET_CHUNK = 128
ROPE_BASE = 10000.0
D_GROUPS = 4
D_CHUNK = 128
W_D = 512
D_GROUP = W_D // D_GROUPS
EVEN_IN = 2 * W_A + 3 * W_B
ODD_IN = 2 * W_CK + 2 * W_CV + 2 * W_D
D_FF = 2816
ALPHA = (2 * DEPTH) ** 0.25
LN_EPS = 1e-5

SUBLANES = 8
LANES = 128
MIB = 1024 * 1024

ROW_TILE = 512
DENSE_TILE = 512
DENSE_GROUP = 256
DENSE_COLS = 256
EVEN_TT = 128
EVEN_GROUP = 1
ODD_GROUP = 2
EVEN_PITCH = EVEN_TT + SUBLANES
SAMPLE_NB = 32


def _params(sem, vmem_mib):
    return pltpu.CompilerParams(dimension_semantics=sem, vmem_limit_bytes=vmem_mib * MIB)


def _const_spec(shape, layer=None):
    nd = len(shape)
    if layer is None:
        return pl.BlockSpec(shape, lambda *_: (0,) * nd, pipeline_mode=pl.Buffered(1))
    return pl.BlockSpec((None,) + tuple(shape), lambda *_: (layer,) + (0,) * nd, pipeline_mode=pl.Buffered(1))


def _layer_norm(x, g, b):
    mu = jnp.mean(x, -1, keepdims=True)
    xc = x - mu
    var = jnp.mean(xc * xc, -1, keepdims=True)
    return xc * lax.rsqrt(var + LN_EPS) * g + b


def _softplus(x):
    return jnp.maximum(x, 0.0) + jnp.log1p(jnp.exp(-jnp.abs(x)))


def _dot(a, b):
    return jnp.dot(a, b, preferred_element_type=F32)


N_CAST = 4


def _cast_specs(stacks_and_layers, nsteps):
    in_specs, out_specs, out_shapes = [], [], []
    for w, layer in stacks_and_layers:
        nrow, ncol = w.shape[1:]
        rb = nrow // nsteps
        assert nrow % nsteps == 0 and rb % (2 * SUBLANES) == 0
        in_specs.append(pl.BlockSpec((None, rb, ncol), lambda j, layer=layer: (layer, j, 0)))
        out_specs.append(pl.BlockSpec((rb, ncol), lambda j: (j, 0)))
        out_shapes.append(jax.ShapeDtypeStruct((nrow, ncol), BF16))
    return in_specs, out_specs, out_shapes


def _cast_slabs(src_refs, dst_refs):
    for src, dst in zip(src_refs, dst_refs):
        dst[...] = src[...].astype(BF16)


def _proj_kernel(x_ref, w_ref, o_ref):
    o_ref[...] = _dot(x_ref[...].astype(BF16), w_ref[...])


def _in_proj(x2d, w_stack, layer):
    m, k = x2d.shape
    n = w_stack.shape[2]
    tm = min(ROW_TILE, m)
    return pl.pallas_call(
        _proj_kernel,
        grid=(m // tm,),
        in_specs=[pl.BlockSpec((tm, k), lambda i: (i, 0)), _const_spec((k, n), layer)],
        out_specs=pl.BlockSpec((tm, n), lambda i: (i, 0)),
        out_shape=jax.ShapeDtypeStruct((m, n), F32),
        compiler_params=_params(("parallel",), 40),
        name="in_proj",
    )(x2d, w_stack)


def _dense_kernel(yp_ref, xp_ref, ys_ref, xs_ref, wo_ref, g1_ref, b1_ref, w1_ref, w3_ref, w2_ref, g2_ref, b2_ref,
                  op_ref, os_ref, h_s, *, cols):
    n_prompt = pl.num_programs(0) - 1
    weights = (wo_ref, g1_ref, b1_ref, w1_ref, w3_ref, w2_ref, g2_ref, b2_ref)

    @pl.when(pl.program_id(0) < n_prompt)
    def _():
        _dense_rows(yp_ref, xp_ref, *weights, op_ref, h_s, cols)

    @pl.when(pl.program_id(0) == n_prompt)
    def _():
        _dense_rows(ys_ref, xs_ref, *weights, os_ref, h_s, cols)


def _dense_rows(y_ref, x_ref, wo_ref, g1_ref, b1_ref, w1_ref, w3_ref, w2_ref, g2_ref, b2_ref, o_ref, h_s, ncols):
    rows, ncols = ncols
    nparts = y_ref.shape[0] // rows
    sls = [slice(p * rows, (p + 1) * rows) for p in range(nparts)]
    mix, x1, xb = {}, {}, {}

    def out_proj(p):
        mix[p] = _dot(y_ref[sls[p], :], wo_ref[...])

    def norm1(p):
        x1[p] = _layer_norm(ALPHA * x_ref[sls[p], :] + mix[p], g1_ref[...], b1_ref[...])
        xb[p] = x1[p].astype(BF16)

    def gate_up(p):
        for c in range(0, D_FF, ncols):
            cols = slice(c, c + ncols)
            h_s[sls[p], cols] = (jax.nn.silu(_dot(xb[p], w1_ref[:, cols]))
                                 * _dot(xb[p], w3_ref[:, cols])).astype(BF16)

    def down_norm2(p):
        f = _dot(h_s[sls[p], :], w2_ref[...])
        o_ref[sls[p], :] = _layer_norm(ALPHA * x1[p] + f, g2_ref[...], b2_ref[...])

    out_proj(0)
    for p in range(nparts):
        norm1(p)
        if p + 1 < nparts:
            out_proj(p + 1)
    for p in range(nparts):
        gate_up(p)
    for p in range(nparts):
        down_norm2(p)


def _dense_block(yp, xp, ys, xs, layer, wo, g1, b1, w1, w3, w2, g2, b2):
    tm = DENSE_TILE
    n_prompt = xp.shape[0] // tm
    assert xp.shape[0] % tm == 0 and xs.shape[0] == tm and tm % DENSE_GROUP == 0
    prompt_row = lambda i: (jnp.minimum(i, n_prompt - 1), 0)
    sample_row = lambda i: (0, 0)
    return pl.pallas_call(
        functools.partial(_dense_kernel, cols=[(256, 256), (512, 256), (256, 128), (128, 256)][layer]),
        grid=(n_prompt + 1,),
        in_specs=[
            pl.BlockSpec((tm, D_MODEL), prompt_row),
            pl.BlockSpec((tm, D_MODEL), prompt_row),
            pl.BlockSpec((tm, D_MODEL), sample_row),
            pl.BlockSpec((tm, D_MODEL), sample_row),
            _const_spec((D_MODEL, D_MODEL)),
            _const_spec((1, D_MODEL), layer),
            _const_spec((1, D_MODEL), layer),
            _const_spec((D_MODEL, D_FF)),
            _const_spec((D_MODEL, D_FF)),
            _const_spec((D_FF, D_MODEL)),
            _const_spec((1, D_MODEL), layer),
            _const_spec((1, D_MODEL), layer),
        ],
        out_specs=[pl.BlockSpec((tm, D_MODEL), prompt_row), pl.BlockSpec((tm, D_MODEL), sample_row)],
        out_shape=[jax.ShapeDtypeStruct(xp.shape, F32), jax.ShapeDtypeStruct(xs.shape, F32)],
        scratch_shapes=[pltpu.VMEM((tm, D_FF), BF16)],
        compiler_params=_params(("arbitrary",), 58),
        name="dense_block",
    )(yp, xp, ys, xs, wo, g1, b1, w1, w3, w2, g2, b2)


def _lru_gates(xc, wblk_ref, ba, bx, sp):
    half = W_A // 2
    xb = xc.astype(BF16)
    pre0 = _dot(xb[:, :half], wblk_ref[0])
    pre1 = _dot(xb[:, half:], wblk_ref[1])
    r = jax.nn.sigmoid(jnp.concatenate([pre0[:, :half], pre1[:, :half]], axis=1) + ba)
    i = jax.nn.sigmoid(jnp.concatenate([pre0[:, half:], pre1[:, half:]], axis=1) + bx)
    log_a = (-LRU_C) * r * sp
    a = jnp.exp(log_a)
    mult = jnp.sqrt((1.0 + a * a) * jnp.tanh(-log_a))
    return a, mult, i


def _causal_taps(x_slabs, ext, b, w_ref, nrows):
    ktaps = w_ref.shape[0]
    out = []
    for s, x in enumerate(x_slabs):
        lanes = slice(s * LANES, (s + 1) * LANES)
        ext[s, b, SUBLANES:SUBLANES + nrows, :] = x
        acc = w_ref[ktaps - 1:ktaps, lanes] * x
        for k in range(ktaps - 1):
            off = SUBLANES - (ktaps - 1) + k
            acc = acc + w_ref[k:k + 1, lanes] * ext[s, b, off:off + nrows, :]
        out.append(acc)
    return jnp.concatenate(out, axis=1)


def _even_prompt_kernel(x_ref, w_ref, caw_ref, cab_ref, wblk_ref, ba_ref, bx_ref, lam_ref, cbw_ref, *rest,
                        time_major):
    cast_src, rest = rest[:N_CAST], rest[N_CAST:]
    y_ref, hlast_ref, ca_out_ref, sc_out_ref = rest[:4]
    cast_dst, rest = rest[4:4 + N_CAST], rest[4 + N_CAST:]
    p0, p1, xa_ext, cb_ext, a_s, b_s, g_s, h_c = rest
    tt, pitch = EVEN_TT, EVEN_PITCH
    nslab = W_A // LANES
    j = pl.program_id(0)
    _cast_slabs(cast_src, cast_dst)
    bufs = (p0, p1)
    group = p0.shape[0] // tt

    def project(g):
        xg = x_ref[g * group:(g + 1) * group].reshape(group * tt, D_MODEL)
        bufs[g % 2][...] = _dot(xg.astype(BF16), w_ref[...])

    @pl.when(j == 0)
    def _():
        xa_ext[:, :, 0:SUBLANES, :] = jnp.zeros((nslab, BATCH, SUBLANES, LANES), F32)
        cb_ext[:, :, 0:SUBLANES, :] = jnp.zeros((nslab, BATCH, SUBLANES, LANES), F32)
        h_c[...] = jnp.zeros(h_c.shape, F32)

    @pl.when(j > 0)
    def _():
        xa_ext[:, :, 0:SUBLANES, :] = xa_ext[:, :, tt:tt + SUBLANES, :]
        cb_ext[:, :, 0:SUBLANES, :] = cb_ext[:, :, tt:tt + SUBLANES, :]

    sp = _softplus(-lam_ref[...])
    ba = ba_ref[...]
    bx = bx_ref[...]
    cab = cab_ref[...]
    first_row = (lax.broadcasted_iota(jnp.int32, (SUBLANES, 1), 0) == 0) & (j == 0)
    cg0 = 2 * W_A + W_B
    xb0 = 2 * W_A + 2 * W_B

    def stage(b):
        p_ref = bufs[(b // group) % 2].at[(b % group) * tt:(b % group + 1) * tt]
        xc = cab + _causal_taps([p_ref[:, s * LANES:(s + 1) * LANES] for s in range(nslab)], xa_ext, b, caw_ref, tt)
        a, mult, gate = _lru_gates(xc, wblk_ref, ba, bx, sp)
        mult = jnp.concatenate([jnp.where(first_row, 1.0, mult[:SUBLANES]), mult[SUBLANES:]], axis=0)
        bb = mult * (gate * xc)
        for s in range(nslab):
            rows = pl.ds(b, tt, stride=BATCH) if time_major else pl.ds(b * pitch, tt)
            a_s[s, rows, :] = a[:, s * LANES:(s + 1) * LANES]
            b_s[s, rows, :] = bb[:, s * LANES:(s + 1) * LANES]
        g_s[b] = jax.nn.gelu(p_ref[:, W_A:2 * W_A])
        cb = [p_ref[:, cg0 + s * LANES:cg0 + (s + 1) * LANES] * p_ref[:, xb0 + s * LANES:xb0 + (s + 1) * LANES]
              for s in range(nslab)]
        zb = _causal_taps(cb, cb_ext, b, cbw_ref, tt)
        y_ref[b, :, W_A:] = (p_ref[:, 2 * W_A:2 * W_A + W_B] * zb).astype(BF16)

    project(0)
    for g in range(BATCH // group):
        if g + 1 < BATCH // group:
            project(g + 1)
        for b in range(g * group, (g + 1) * group):
            stage(b)

    def step(t, hs):
        out = []
        for s in range(nslab):
            rows = pl.ds(pl.multiple_of(t * BATCH, BATCH), BATCH) if time_major else pl.ds(t, BATCH, stride=pitch)
            hn = a_s[s, rows, :] * hs[s] + b_s[s, rows, :]
            b_s[s, rows, :] = hn
            out.append(hn)
        return tuple(out)

    hs = lax.fori_loop(0, tt, step, tuple(h_c[s] for s in range(nslab)), unroll=4)
    for s in range(nslab):
        h_c[s] = hs[s]

    def emit(b, carry):
        rows = pl.ds(b, tt, stride=BATCH) if time_major else pl.ds(pl.multiple_of(b * pitch, SUBLANES), tt)
        h = jnp.concatenate([b_s[s, rows, :] for s in range(nslab)], axis=1)
        y_ref[b, :, 0:W_A] = (h * g_s[b]).astype(BF16)
        return carry

    lax.fori_loop(0, BATCH, emit, 0)

    @pl.when(j == pl.num_programs(0) - 1)
    def _():
        hlast_ref[...] = jnp.concatenate(list(hs), axis=1)
        for s in range(nslab):
            ca_out_ref[:, :, s * LANES:(s + 1) * LANES] = xa_ext[s, :, tt:tt + SUBLANES, :]
            sc_out_ref[:, :, s * LANES:(s + 1) * LANES] = cb_ext[s, :, tt:tt + SUBLANES, :]


def _even_param_specs(e):
    return [
        _const_spec((A_CONV, W_A), e),
        _const_spec((1, W_A), e),
        _const_spec((2, W_A // 2, W_A), e),
        _const_spec((1, W_A), e),
        _const_spec((1, W_A), e),
        _const_spec((1, W_A), e),
        _const_spec((B_CONV, W_B), e),
    ]


def _even_prompt(x3d, w_stack, e, params, casts):
    tt, pitch = EVEN_TT, EVEN_PITCH
    nslab = W_A // LANES
    nblock = SEQ // tt
    assert W_A == W_B and len(casts) == N_CAST
    cast_in, cast_out, cast_shapes = _cast_specs(casts, nblock)
    return pl.pallas_call(
        functools.partial(_even_prompt_kernel, time_major=False),
        grid=(nblock,),
        in_specs=[pl.BlockSpec((BATCH, tt, D_MODEL), lambda j: (0, j, 0)),
                  _const_spec((D_MODEL, EVEN_IN), e)] + _even_param_specs(e) + cast_in,
        out_specs=[
            pl.BlockSpec((BATCH, tt, D_MODEL), lambda j: (0, j, 0)),
            pl.BlockSpec((BATCH, W_A), lambda j: (0, 0)),
            pl.BlockSpec((BATCH, SUBLANES, W_A), lambda j: (0, 0, 0)),
            pl.BlockSpec((BATCH, SUBLANES, W_B), lambda j: (0, 0, 0)),
        ] + cast_out,
        out_shape=[
            jax.ShapeDtypeStruct((BATCH, SEQ, D_MODEL), BF16),
            jax.ShapeDtypeStruct((BATCH, W_A), F32),
            jax.ShapeDtypeStruct((BATCH, SUBLANES, W_A), F32),
            jax.ShapeDtypeStruct((BATCH, SUBLANES, W_B), F32),
        ] + cast_shapes,
        scratch_shapes=[
            pltpu.VMEM((EVEN_GROUP * tt, EVEN_IN), F32),
            pltpu.VMEM((EVEN_GROUP * tt, EVEN_IN), F32),
            pltpu.VMEM((nslab, BATCH, tt + SUBLANES, LANES), F32),
            pltpu.VMEM((nslab, BATCH, tt + SUBLANES, LANES), F32),
            pltpu.VMEM((nslab, BATCH * pitch, LANES), F32),
            pltpu.VMEM((nslab, BATCH * pitch, LANES), F32),
            pltpu.VMEM((BATCH, tt, W_A), F32),
            pltpu.VMEM((nslab, BATCH, LANES), F32),
        ],
        compiler_params=_params(("arbitrary",), 52),
        name="even_prompt",
    )(x3d, w_stack, *params, *[w for w, _ in casts])


def _even_sample_kernel(proj_ref, ca_ref, sc_ref, h0_ref, caw_ref, cab_ref, wblk_ref, ba_ref, bx_ref,
                        lam_ref, cbw_ref, y_ref, hlast_ref, ca_out_ref, sc_out_ref):
    sp = _softplus(-lam_ref[...])
    rows_a = [ca_ref[k] for k in range(A_CONV - 1)] + [proj_ref[l, :, 0:W_A] for l in range(DEC_SEQ)]
    xc = []
    for l in range(DEC_SEQ):
        acc = cab_ref[...] + caw_ref[0:1, :] * rows_a[l]
        for k in range(1, A_CONV):
            acc = acc + caw_ref[k:k + 1, :] * rows_a[l + k]
        xc.append(acc)
    xc_all = jnp.concatenate(xc, axis=0)
    a, mult, gate = _lru_gates(xc_all, wblk_ref, ba_ref[...], bx_ref[...], sp)
    bb = mult * (gate * xc_all)
    h = h0_ref[...]
    rows_b = [sc_ref[k] for k in range(B_CONV - 1)]
    for l in range(DEC_SEQ):
        sl = slice(l * DEC_BATCH, (l + 1) * DEC_BATCH)
        h = a[sl] * h + bb[sl]
        y_ref[l, :, 0:W_A] = (h * jax.nn.gelu(proj_ref[l, :, W_A:2 * W_A])).astype(BF16)
        rows_b.append(proj_ref[l, :, 2 * W_A + W_B:2 * W_A + 2 * W_B] * proj_ref[l, :, 2 * W_A + 2 * W_B:])
    hlast_ref[...] = h
    for l in range(DEC_SEQ):
        zb = cbw_ref[0:1, :] * rows_b[l]
        for k in range(1, B_CONV):
            zb = zb + cbw_ref[k:k + 1, :] * rows_b[l + k]
        y_ref[l, :, W_A:] = (proj_ref[l, :, 2 * W_A:2 * W_A + W_B] * zb).astype(BF16)
    for k in range(A_CONV - 1):
        ca_out_ref[k] = rows_a[DEC_SEQ + k]
    for k in range(B_CONV - 1):
        sc_out_ref[k] = rows_b[DEC_SEQ + k]


def _even_sample(proj_tm, ca_tm, sc_tm, h0_tm, e, params):
    whole = lambda shape: pl.BlockSpec(shape, lambda i: (0,) * len(shape))
    return pl.pallas_call(
        _even_sample_kernel,
        grid=(1,),
        in_specs=[
            whole((DEC_SEQ, DEC_BATCH, EVEN_IN)),
            _const_spec((A_CONV - 1, DEC_BATCH, W_A), e),
            _const_spec((B_CONV - 1, DEC_BATCH, W_B), e),
            _const_spec((DEC_BATCH, W_A), e),
        ] + _even_param_specs(e),
        out_specs=[
            whole((DEC_SEQ, DEC_BATCH, D_MODEL)),
            whole((DEC_BATCH, W_A)),
            whole((A_CONV - 1, DEC_BATCH, W_A)),
            whole((B_CONV - 1, DEC_BATCH, W_B)),
        ],
        out_shape=[
            jax.ShapeDtypeStruct((DEC_SEQ, DEC_BATCH, D_MODEL), BF16),
            jax.ShapeDtypeStruct((DEC_BATCH, W_A), F32),
            jax.ShapeDtypeStruct((A_CONV - 1, DEC_BATCH, W_A), F32),
            jax.ShapeDtypeStruct((B_CONV - 1, DEC_BATCH, W_B), F32),
        ],
        compiler_params=_params(("arbitrary",), 40),
        name="even_sample",
    )(proj_tm, ca_tm, sc_tm, h0_tm, *params)


def _rope(x, cos, sin_signed):
    half = C_DK // 2
    ax = x.ndim - 1
    pieces = []
    for p in range(W_CK // LANES):
        xv = x[..., p * LANES:(p + 1) * LANES]
        lane = lax.broadcasted_iota(jnp.int32, xv.shape, ax)
        partner = jnp.where((lane % C_DK) < half, pltpu.roll(xv, LANES - half, axis=ax), pltpu.roll(xv, half, axis=ax))
        pieces.append(xv * cos[..., p * LANES:(p + 1) * LANES] + partner * sin_signed[..., p * LANES:(p + 1) * LANES])
    return jnp.concatenate(pieces, axis=ax)


def _group_norm(o):
    mu = jnp.mean(o, -1, keepdims=True)
    oc = o - mu
    var = jnp.mean(oc * oc, -1, keepdims=True)
    return oc * lax.rsqrt(var + LN_EPS)


def _odd_chunk(p_ref, b, cos_ref, sin_ref, dec_ref, cs_ref, kd_ref, sdec_ref, gn_ref,
               spw_ref, spb_ref, lng_ref, lnb_ref, y_ref, s_c):
    q = _rope(p_ref[:, 0:W_CK], cos_ref[...], sin_ref[...])
    k = _rope(p_ref[:, W_CK:2 * W_CK], cos_ref[...], sin_ref[...]) * (C_DK ** -0.5)
    kd = k * kd_ref[...]
    v0 = 2 * W_CK
    g0 = v0 + W_CV
    for h in range(C_HEADS):
        qh = q[:, h * C_DK:(h + 1) * C_DK].astype(BF16)
        kh = k[:, h * C_DK:(h + 1) * C_DK].astype(BF16)
        kdh = kd[:, h * C_DK:(h + 1) * C_DK].astype(BF16)
        vh = p_ref[:, v0 + h * C_DV:v0 + (h + 1) * C_DV].astype(BF16)
        s_prev = s_c[b, h]
        scores = lax.dot_general(qh, kh, (((1,), (1,)), ((), ())), preferred_element_type=F32) * dec_ref[h]
        o = _dot(scores.astype(BF16), vh) + _dot(qh, s_prev.astype(BF16)) * cs_ref[:, h * C_DV:(h + 1) * C_DV]
        s_c[b, h] = sdec_ref[h] * s_prev + lax.dot_general(kdh, vh, (((0,), (0,)), ((), ())), preferred_element_type=F32)
        gate = jax.nn.silu(p_ref[:, g0 + h * C_DV:g0 + (h + 1) * C_DV])
        y_ref[b, :, h * C_DV:(h + 1) * C_DV] = (gate * (_group_norm(o) * gn_ref[:, h * C_DV:(h + 1) * C_DV])).astype(BF16)

    u0 = g0 + W_CV
    u = jax.nn.gelu(p_ref[:, u0:u0 + W_D])
    vd = _layer_norm(jax.nn.gelu(p_ref[:, u0 + W_D:]), lng_ref[...], lnb_ref[...]).astype(BF16)
    ri = lax.broadcasted_iota(jnp.int32, (D_CHUNK, D_CHUNK), 0)
    ci = lax.broadcasted_iota(jnp.int32, (D_CHUNK, D_CHUNK), 1)
    for gi in range(D_GROUPS):
        w = jnp.where(ri >= ci, spw_ref[gi], 0.0).astype(BF16)
        s = _dot(w, vd[:, gi * D_GROUP:(gi + 1) * D_GROUP]) + spb_ref[:, gi:gi + 1]
        y_ref[b, :, W_CV + gi * D_GROUP:W_CV + (gi + 1) * D_GROUP] = (u[:, gi * D_GROUP:(gi + 1) * D_GROUP] * s).astype(BF16)


def _odd_prompt_kernel(x_ref, w_ref, cos_ref, sin_ref, dec_ref, cs_ref, kd_ref, sdec_ref, gn_ref,
                       spw_ref, spb_ref, lng_ref, lnb_ref, *rest):
    cast_src, rest = rest[:N_CAST], rest[N_CAST:]
    y_ref, s_out_ref = rest[:2]
    cast_dst, rest = rest[2:2 + N_CAST], rest[2 + N_CAST:]
    p0, p1, s_c = rest
    c = pl.program_id(0)
    _cast_slabs(cast_src, cast_dst)

    @pl.when(c == 0)
    def _():
        s_c[...] = jnp.zeros(s_c.shape, F32)

    bufs = (p0, p1)
    group = p0.shape[0] // RET_CHUNK

    def project(g):
        xg = x_ref[g * group:(g + 1) * group].reshape(group * RET_CHUNK, D_MODEL)
        bufs[g % 2][...] = _dot(xg.astype(BF16), w_ref[...])

    project(0)
    for g in range(BATCH // group):
        if g + 1 < BATCH // group:
            project(g + 1)
        for r in range(group):
            _odd_chunk(bufs[g % 2].at[r * RET_CHUNK:(r + 1) * RET_CHUNK], g * group + r,
                       cos_ref, sin_ref, dec_ref, cs_ref, kd_ref, sdec_ref, gn_ref,
                       spw_ref, spb_ref, lng_ref, lnb_ref, y_ref, s_c)

    @pl.when(c == pl.num_programs(0) - 1)
    def _():
        s_out_ref[...] = s_c[...]


def _odd_prompt(x3d, w_stack, o, cos, sin, tabs, gn, spw, spb_t, lng, lnb, casts):
    nchunk = SEQ // RET_CHUNK
    dec, cs, kd, sdec = tabs
    assert len(casts) == N_CAST
    cast_in, cast_out, cast_shapes = _cast_specs(casts, nchunk)
    return pl.pallas_call(
        _odd_prompt_kernel,
        grid=(nchunk,),
        in_specs=[
            pl.BlockSpec((BATCH, RET_CHUNK, D_MODEL), lambda c: (0, c, 0)),
            _const_spec((D_MODEL, ODD_IN), o),
            pl.BlockSpec((RET_CHUNK, W_CK), lambda c: (c, 0)),
            pl.BlockSpec((RET_CHUNK, W_CK), lambda c: (c, 0)),
            _const_spec((C_HEADS, RET_CHUNK, RET_CHUNK)),
            _const_spec((RET_CHUNK, W_CV)),
            _const_spec((RET_CHUNK, W_CK)),
            _const_spec((C_HEADS, C_DK, C_DV)),
            _const_spec((1, W_CV), o),
            _const_spec((D_GROUPS, D_CHUNK, D_CHUNK), o),
            _const_spec((D_CHUNK, D_GROUPS), o),
            _const_spec((1, W_D), o),
            _const_spec((1, W_D), o),
        ] + cast_in,
        out_specs=[
            pl.BlockSpec((BATCH, RET_CHUNK, D_MODEL), lambda c: (0, c, 0)),
            pl.BlockSpec((BATCH, C_HEADS, C_DK, C_DV), lambda c: (0, 0, 0, 0)),
        ] + cast_out,
        out_shape=[
            jax.ShapeDtypeStruct((BATCH, SEQ, D_MODEL), BF16),
            jax.ShapeDtypeStruct((BATCH, C_HEADS, C_DK, C_DV), F32),
        ] + cast_shapes,
        scratch_shapes=[
            pltpu.VMEM((ODD_GROUP * RET_CHUNK, ODD_IN), F32),
            pltpu.VMEM((ODD_GROUP * RET_CHUNK, ODD_IN), F32),
            pltpu.VMEM((BATCH, C_HEADS, C_DK, C_DV), F32),
        ],
        compiler_params=_params(("arbitrary",), 52),
        name="odd_prompt",
    )(x3d, w_stack, cos, sin, dec, cs, kd, sdec, gn, spw, spb_t, lng, lnb, *[w for w, _ in casts])


def _odd_sample_kernel(proj_ref, s0_ref, cos_ref, sin_ref, dec_ref, cs_ref, kd_ref, sdec_ref, gn_ref,
                       spw_ref, spb_ref, lng_ref, lnb_ref, *rest):
    y_ref, s_out_ref, vd_ref = rest[-3:]
    if len(rest) > 3:
        s_out_ref[:, 0] = rest[0][...]
        s_out_ref = s_out_ref.at[:, 1]
    q = _rope(proj_ref[:, :, 0:W_CK], cos_ref[...], sin_ref[...])
    k = _rope(proj_ref[:, :, W_CK:2 * W_CK], cos_ref[...], sin_ref[...]) * (C_DK ** -0.5)
    kd = k * kd_ref[...]
    v0 = 2 * W_CK
    g0 = v0 + W_CV
    for h in range(C_HEADS):
        qh = q[:, :, h * C_DK:(h + 1) * C_DK].astype(BF16)
        kh = k[:, :, h * C_DK:(h + 1) * C_DK].astype(BF16)
        kdh = kd[:, :, h * C_DK:(h + 1) * C_DK].astype(BF16)
        vh = proj_ref[:, :, v0 + h * C_DV:v0 + (h + 1) * C_DV].astype(BF16)
        s_prev = s0_ref[:, h]
        scores = jnp.einsum('nld,nmd->nlm', qh, kh, preferred_element_type=F32) * dec_ref[h]
        o = (jnp.einsum('nlm,nme->nle', scores.astype(BF16), vh, preferred_element_type=F32)
             + jnp.einsum('nld,nde->nle', qh, s_prev.astype(BF16), preferred_element_type=F32)
             * cs_ref[:, h * C_DV:(h + 1) * C_DV])
        s_out_ref[:, h] = sdec_ref[h] * s_prev + jnp.einsum('nld,nle->nde', kdh, vh, preferred_element_type=F32)
        gate = jax.nn.silu(proj_ref[:, :, g0 + h * C_DV:g0 + (h + 1) * C_DV])
        y_ref[:, :, h * C_DV:(h + 1) * C_DV] = (gate * (_group_norm(o) * gn_ref[:, h * C_DV:(h + 1) * C_DV])).astype(BF16)

    u0 = g0 + W_CV
    u = jax.nn.gelu(proj_ref[:, :, u0:u0 + W_D])
    vd = _layer_norm(jax.nn.gelu(proj_ref[:, :, u0 + W_D:]), lng_ref[...], lnb_ref[...])
    vd_ref[...] = vd
    s = spb_ref[...] + spw_ref[0] * vd[:, 0:1, :]
    for m in range(1, DEC_SEQ):
        s = s + spw_ref[m] * vd[:, m:m + 1, :]
    y_ref[:, :, W_CV:] = (u * s).astype(BF16)


def _odd_sample(proj, state_ret, o, cos, sin, tabs, gn, spw_rows, spb_rows, lng, lnb, prev_states):
    nb = SAMPLE_NB
    dec, cs, kd, sdec = tabs
    seq3 = lambda i: (i, 0, 0)
    one_state = pl.BlockSpec((nb, C_HEADS, C_DK, C_DV), lambda i: (i, 0, 0, 0))
    in_specs = [
        pl.BlockSpec((nb, DEC_SEQ, ODD_IN), seq3),
        pl.BlockSpec((nb, None, C_HEADS, C_DK, C_DV), lambda i: (i, o, 0, 0, 0)),
        _const_spec((DEC_SEQ, W_CK)),
        _const_spec((DEC_SEQ, W_CK)),
        _const_spec((C_HEADS, DEC_SEQ, DEC_SEQ)),
        _const_spec((DEC_SEQ, W_CV)),
        _const_spec((DEC_SEQ, W_CK)),
        _const_spec((C_HEADS, C_DK, C_DV)),
        _const_spec((1, W_CV), o),
        _const_spec((DEC_SEQ, DEC_SEQ, W_D), o),
        _const_spec((DEC_SEQ, W_D), o),
        _const_spec((1, W_D), o),
        _const_spec((1, W_D), o),
    ]
    args = [proj, state_ret, cos, sin, dec, cs, kd, sdec, gn, spw_rows, spb_rows, lng, lnb]
    if prev_states is None:
        state_spec = one_state
        state_shape = (DEC_BATCH, C_HEADS, C_DK, C_DV)
    else:
        in_specs.append(one_state)
        args.append(prev_states)
        state_spec = pl.BlockSpec((nb, N_ODD, C_HEADS, C_DK, C_DV), lambda i: (i, 0, 0, 0, 0))
        state_shape = (DEC_BATCH, N_ODD, C_HEADS, C_DK, C_DV)
    return pl.pallas_call(
        _odd_sample_kernel,
        grid=(DEC_BATCH // nb,),
        in_specs=in_specs,
        out_specs=[pl.BlockSpec((nb, DEC_SEQ, D_MODEL), seq3), state_spec, pl.BlockSpec((nb, DEC_SEQ, W_D), seq3)],
        out_shape=[
            jax.ShapeDtypeStruct((DEC_BATCH, DEC_SEQ, D_MODEL), BF16),
            jax.ShapeDtypeStruct(state_shape, F32),
            jax.ShapeDtypeStruct((DEC_BATCH, DEC_SEQ, W_D), F32),
        ],
        compiler_params=_params(("parallel",), 48),
        name="odd_sample",
    )(*args)


def _rope_tables(pos):
    half = C_DK // 2
    freq = ROPE_BASE ** (-np.arange(half, dtype=np.float64) / half)
    ang = np.asarray(pos, np.float64)[:, None] * freq
    cos, sin = np.cos(ang), np.sin(ang)
    cos_l = np.tile(np.concatenate([cos, cos], axis=-1), (1, C_HEADS))
    sin_l = np.tile(np.concatenate([-sin, sin], axis=-1), (1, C_HEADS))
    return cos_l.astype(np.float32), sin_l.astype(np.float32)


def _retention_tables(length):
    log_g = np.log1p(-np.exp2(-5.0 - np.arange(C_HEADS, dtype=np.float64)))
    idx = np.arange(length, dtype=np.float64)
    rel = idx[:, None] - idx[None, :]
    decay = np.where(rel >= 0, np.exp(log_g[:, None, None] * np.maximum(rel, 0.0)), 0.0)
    cross = np.exp(log_g[None, :] * (idx[:, None] + 1.0))
    kdec = np.exp(log_g[None, :] * (length - 1.0 - idx[:, None]))
    sdec = np.exp(log_g * length)
    tabs = (decay,
            np.repeat(cross, C_DV, axis=1),
            np.repeat(kdec, C_DK, axis=1),
            np.broadcast_to(sdec[:, None, None], (C_HEADS, C_DK, C_DV)))
    return tuple(np.ascontiguousarray(t, dtype=np.float32) for t in tabs)


def _block_diag_halves(wa, wx):
    nb = A_BLOCKS // 2
    width = nb * A_BLOCK

    def bd(w, hh):
        rows = [jnp.pad(w[:, hh * nb + g], ((0, 0), (0, 0), (g * A_BLOCK, width - (g + 1) * A_BLOCK)))
                for g in range(nb)]
        return jnp.concatenate(rows, axis=1)

    halves = [jnp.concatenate([bd(wa, hh), bd(wx, hh)], axis=-1) for hh in range(2)]
    return jnp.stack(halves, axis=1).astype(BF16)


def kernel(x_prompt, x_sample, state_lru_h, state_lru_conv, state_sconv, state_ret, w_in_even, conv_a_w, conv_a_b, lru_wa, lru_ba, lru_wx, lru_bx, lru_lam, conv_b_w, w_out_even, w_in_odd, ret_gn_g, sp_w, sp_b, gm_ln_g, gm_ln_b, w_out_odd, ffn_w1, ffn_w3, ffn_w2, ln1_g, ln1_b, ln2_g, ln2_b):
    rows = lambda v: v.reshape(v.shape[0], 1, v.shape[1])
    xp = x_prompt.reshape(BATCH * SEQ, D_MODEL)
    xs = jnp.swapaxes(x_sample, 0, 1).reshape(DEC_SEQ * DEC_BATCH, D_MODEL)

    cos_p, sin_p = _rope_tables(np.arange(SEQ))
    cos_s, sin_s = _rope_tables(PAST_LEN + np.arange(DEC_SEQ))
    tab_p = _retention_tables(RET_CHUNK)
    tab_s = _retention_tables(DEC_SEQ)

    w_in_even_b, w_in_odd_b = w_in_even.astype(BF16), w_in_odd.astype(BF16)
    ln1, ln2 = (rows(ln1_g), rows(ln1_b)), (rows(ln2_g), rows(ln2_b))
    even_p = (conv_a_w, rows(conv_a_b), _block_diag_halves(lru_wa, lru_wx), rows(lru_ba), rows(lru_bx),
              rows(lru_lam), conv_b_w)
    ca_tm = jnp.transpose(state_lru_conv, (1, 2, 0, 3))
    sc_tm = jnp.transpose(state_sconv, (1, 2, 0, 3))
    h0_tm = jnp.swapaxes(state_lru_h, 0, 1)
    tril = jnp.tril(sp_w[:, :, :DEC_SEQ, :DEC_SEQ])
    spw_rows = jnp.repeat(jnp.transpose(tril, (0, 3, 2, 1)), D_GROUP, axis=3)
    spb_rows = jnp.repeat(jnp.swapaxes(sp_b[:, :, :DEC_SEQ], 1, 2), D_GROUP, axis=2)
    spb_t = jnp.swapaxes(sp_b, 1, 2)
    odd_small = (rows(ret_gn_g),)
    odd_ln = (rows(gm_ln_g), rows(gm_ln_b))

    h_p, h_s, ca_p, ca_s, sc_p, sc_s, r_p, v_s = [], [], [], [], [], [], [], []
    ret_sample = None
    for l in range(DEPTH):
        if l % 2 == 0:
            e = l // 2
            casts = [(w_out_even, e), (ffn_w1, l), (ffn_w3, l), (ffn_w2, l)]
            yp, hlp, cap, scp, wo, w1, w3, w2 = _even_prompt(xp.reshape(BATCH, SEQ, D_MODEL), w_in_even_b, e,
                                                             even_p, casts)
            proj_s = _in_proj(xs, w_in_even_b, e).reshape(DEC_SEQ, DEC_BATCH, EVEN_IN)
            ys, hls, cas, scs = _even_sample(proj_s, ca_tm, sc_tm, h0_tm, e, even_p)
            h_p.append(hlp)
            h_s.append(hls)
            ca_p.append(cap[:, SUBLANES - (A_CONV - 1):])
            ca_s.append(jnp.swapaxes(cas, 0, 1))
            sc_p.append(scp[:, SUBLANES - (B_CONV - 1):])
            sc_s.append(jnp.swapaxes(scs, 0, 1))
            yp = yp.reshape(BATCH * SEQ, D_MODEL)
            ys = ys.reshape(DEC_SEQ * DEC_BATCH, D_MODEL)
        else:
            o = l // 2
            casts = [(w_out_odd, o), (ffn_w1, l), (ffn_w3, l), (ffn_w2, l)]
            yp, rp, wo, w1, w3, w2 = _odd_prompt(xp.reshape(BATCH, SEQ, D_MODEL), w_in_odd_b, o, cos_p, sin_p, tab_p,
                                                 *odd_small, sp_w, spb_t, *odd_ln, casts)
            proj_s = _in_proj(xs, w_in_odd_b, o).reshape(DEC_BATCH, DEC_SEQ, ODD_IN)
            ys, ret_sample, vs = _odd_sample(proj_s, state_ret, o, cos_s, sin_s, tab_s, *odd_small,
                                             spw_rows, spb_rows, *odd_ln, ret_sample)
            r_p.append(rp)
            v_s.append(vs)
            yp = yp.reshape(BATCH * SEQ, D_MODEL)
            ys = ys.reshape(DEC_BATCH * DEC_SEQ, D_MODEL)
        xp, xs = _dense_block(yp, xp, ys, xs, l, wo, *ln1, w1, w3, w2, *ln2)
        if l + 1 < DEPTH:
            if l % 2 == 0:
                xs = jnp.swapaxes(xs.reshape(DEC_SEQ, DEC_BATCH, D_MODEL), 0, 1).reshape(DEC_BATCH * DEC_SEQ, D_MODEL)
            else:
                xs = jnp.swapaxes(xs.reshape(DEC_BATCH, DEC_SEQ, D_MODEL), 0, 1).reshape(DEC_SEQ * DEC_BATCH, D_MODEL)

    return (xp.reshape(BATCH, SEQ, D_MODEL), xs.reshape(DEC_BATCH, DEC_SEQ, D_MODEL),
            jnp.stack(h_p, axis=1), jnp.stack(h_s, axis=1),
            jnp.stack(ca_p, axis=1), jnp.stack(ca_s, axis=1),
            jnp.stack(sc_p, axis=1), jnp.stack(sc_s, axis=1),
            jnp.stack(r_p, axis=1), ret_sample, jnp.stack(v_s, axis=1))
```

```python
import functools

import jax
import jax.numpy as jnp
import numpy as np
from jax import lax
from jax.experimental import pallas as pl
from jax.experimental.pallas import tpu as pltpu

F32 = jnp.float32
BF16 = jnp.bfloat16

D_MODEL = 1024
BATCH = 8
SEQ = 2048
DEPTH = 4
DEC_BATCH = 128
DEC_SEQ = 4
PAST_LEN = 16384
N_ODD = DEPTH // 2
W_A = 512
A_BLOCKS = 8
A_BLOCK = 64
A_CONV = 4
LRU_C = 8.0
W_B = 512
B_CONV = 3
C_HEADS = 4
C_DK = 64
C_DV = 128
W_CK = C_HEADS * C_DK
W_CV = C_HEADS * C_DV
RET_CHUNK = 128
ROPE_BASE = 10000.0
D_GROUPS = 4
D_CHUNK = 128
W_D = 512
D_GROUP = W_D // D_GROUPS
EVEN_IN = 2 * W_A + 3 * W_B
ODD_IN = 2 * W_CK + 2 * W_CV + 2 * W_D
D_FF = 2816
ALPHA = (2 * DEPTH) ** 0.25
LN_EPS = 1e-5

SUBLANES = 8
LANES = 128
MIB = 1024 * 1024

ROW_TILE = 512
DENSE_TILE = 512
DENSE_GROUP = 256
DENSE_COLS = 256
EVEN_TT = 128
EVEN_GROUP = 1
ODD_GROUP = 2
EVEN_PITCH = EVEN_TT + SUBLANES
SAMPLE_NB = 32


def _params(sem, vmem_mib):
    return pltpu.CompilerParams(dimension_semantics=sem, vmem_limit_bytes=vmem_mib * MIB)


def _const_spec(shape, layer=None):
    nd = len(shape)
    if layer is None:
        return pl.BlockSpec(shape, lambda *_: (0,) * nd, pipeline_mode=pl.Buffered(1))
    return pl.BlockSpec((None,) + tuple(shape), lambda *_: (layer,) + (0,) * nd, pipeline_mode=pl.Buffered(1))


def _layer_norm(x, g, b):
    mu = jnp.mean(x, -1, keepdims=True)
    xc = x - mu
    var = jnp.mean(xc * xc, -1, keepdims=True)
    return xc * lax.rsqrt(var + LN_EPS) * g + b


def _softplus(x):
    return jnp.maximum(x, 0.0) + jnp.log1p(jnp.exp(-jnp.abs(x)))


def _dot(a, b):
    return jnp.dot(a, b, preferred_element_type=F32)


def _cast_specs(stacks_and_layers, nsteps):
    in_specs, out_specs, out_shapes = [], [], []
    for w, layer in stacks_and_layers:
        nrow, ncol = w.shape[1:]
        rb = nrow // nsteps
        assert nrow % nsteps == 0 and rb % (2 * SUBLANES) == 0
        in_specs.append(pl.BlockSpec((None, rb, ncol), lambda j, layer=layer: (layer, j, 0)))
        out_specs.append(pl.BlockSpec((rb, ncol), lambda j: (j, 0)))
        out_shapes.append(jax.ShapeDtypeStruct((nrow, ncol), BF16))
    return in_specs, out_specs, out_shapes


def _cast_slabs(src_refs, dst_refs):
    for src, dst in zip(src_refs, dst_refs):
        dst[...] = src[...].astype(BF16)


def _proj_kernel(x_ref, w_ref, o_ref):
    o_ref[...] = _dot(x_ref[...].astype(BF16), w_ref[...])


def _in_proj(x2d, w):
    m, k = x2d.shape
    n = w.shape[1]
    tm = min(ROW_TILE, m)
    return pl.pallas_call(
        _proj_kernel,
        grid=(m // tm,),
        in_specs=[pl.BlockSpec((tm, k), lambda i: (i, 0)), _const_spec((k, n))],
        out_specs=pl.BlockSpec((tm, n), lambda i: (i, 0)),
        out_shape=jax.ShapeDtypeStruct((m, n), F32),
        compiler_params=_params(("parallel",), 40),
        name="in_proj",
    )(x2d, w)


def _dense_kernel(yp_ref, xp_ref, ys_ref, xs_ref, wo_ref, g1_ref, b1_ref, w1_ref, w3_ref, w2_ref, g2_ref, b2_ref,
                  op_ref, os_ref, h_s, *, cols):
    n_prompt = pl.num_programs(0) - 1
    weights = (wo_ref, g1_ref, b1_ref, w1_ref, w3_ref, w2_ref, g2_ref, b2_ref)

    @pl.when(pl.program_id(0) < n_prompt)
    def _():
        _dense_rows(yp_ref, xp_ref, *weights, op_ref, h_s, cols)

    @pl.when(pl.program_id(0) == n_prompt)
    def _():
        _dense_rows(ys_ref, xs_ref, *weights, os_ref, h_s, cols)


def _dense_rows(y_ref, x_ref, wo_ref, g1_ref, b1_ref, w1_ref, w3_ref, w2_ref, g2_ref, b2_ref, o_ref, h_s, ncols):
    rows, ncols = ncols
    nparts = y_ref.shape[0] // rows
    sls = [slice(p * rows, (p + 1) * rows) for p in range(nparts)]
    mix, x1, xb = {}, {}, {}

    def out_proj(p):
        mix[p] = _dot(y_ref[sls[p], :], wo_ref[...])

    def norm1(p):
        x1[p] = _layer_norm(ALPHA * x_ref[sls[p], :] + mix[p], g1_ref[...], b1_ref[...])
        xb[p] = x1[p].astype(BF16)

    def gate_up(p):
        for c in range(0, D_FF, ncols):
            cols = slice(c, c + ncols)
            h_s[sls[p], cols] = (jax.nn.silu(_dot(xb[p], w1_ref[:, cols]))
                                 * _dot(xb[p], w3_ref[:, cols])).astype(BF16)

    def down_norm2(p):
        f = _dot(h_s[sls[p], :], w2_ref[...])
        o_ref[sls[p], :] = _layer_norm(ALPHA * x1[p] + f, g2_ref[...], b2_ref[...])

    out_proj(0)
    for p in range(nparts):
        norm1(p)
        if p + 1 < nparts:
            out_proj(p + 1)
    for p in range(nparts):
        gate_up(p)
    for p in range(nparts):
        down_norm2(p)


def _dense_block(yp, xp, ys, xs, layer, wo, g1, b1, w1, w3, w2, g2, b2):
    tm = DENSE_TILE
    n_prompt = xp.shape[0] // tm
    assert xp.shape[0] % tm == 0 and xs.shape[0] == tm and tm % DENSE_GROUP == 0
    prompt_row = lambda i: (jnp.minimum(i, n_prompt - 1), 0)
    sample_row = lambda i: (0, 0)
    return pl.pallas_call(
        functools.partial(_dense_kernel, cols=(DENSE_GROUP, DENSE_COLS)),
        grid=(n_prompt + 1,),
        in_specs=[
            pl.BlockSpec((tm, D_MODEL), prompt_row),
            pl.BlockSpec((tm, D_MODEL), prompt_row),
            pl.BlockSpec((tm, D_MODEL), sample_row),
            pl.BlockSpec((tm, D_MODEL), sample_row),
            _const_spec((D_MODEL, D_MODEL)),
            _const_spec((1, D_MODEL), layer),
            _const_spec((1, D_MODEL), layer),
            _const_spec((D_MODEL, D_FF)),
            _const_spec((D_MODEL, D_FF)),
            _const_spec((D_FF, D_MODEL)),
            _const_spec((1, D_MODEL), layer),
            _const_spec((1, D_MODEL), layer),
        ],
        out_specs=[pl.BlockSpec((tm, D_MODEL), prompt_row), pl.BlockSpec((tm, D_MODEL), sample_row)],
        out_shape=[jax.ShapeDtypeStruct(xp.shape, F32), jax.ShapeDtypeStruct(xs.shape, F32)],
        scratch_shapes=[pltpu.VMEM((tm, D_FF), BF16)],
        compiler_params=_params(("arbitrary",), 58),
        name="dense_block",
    )(yp, xp, ys, xs, wo, g1, b1, w1, w3, w2, g2, b2)


def _lru_gates(xc, wblk_ref, ba, bx, sp):
    half = W_A // 2
    xb = xc.astype(BF16)
    pre0 = _dot(xb[:, :half], wblk_ref[0])
    pre1 = _dot(xb[:, half:], wblk_ref[1])
    r = jax.nn.sigmoid(jnp.concatenate([pre0[:, :half], pre1[:, :half]], axis=1) + ba)
    i = jax.nn.sigmoid(jnp.concatenate([pre0[:, half:], pre1[:, half:]], axis=1) + bx)
    log_a = (-LRU_C) * r * sp
    a = jnp.exp(log_a)
    mult = jnp.sqrt((1.0 + a * a) * jnp.tanh(-log_a))
    return a, mult, i


def _causal_taps(x_slabs, ext, b, w_ref, nrows):
    ktaps = w_ref.shape[0]
    out = []
    for s, x in enumerate(x_slabs):
        lanes = slice(s * LANES, (s + 1) * LANES)
        ext[s, b, SUBLANES:SUBLANES + nrows, :] = x
        acc = w_ref[ktaps - 1:ktaps, lanes] * x
        for k in range(ktaps - 1):
            off = SUBLANES - (ktaps - 1) + k
            acc = acc + w_ref[k:k + 1, lanes] * ext[s, b, off:off + nrows, :]
        out.append(acc)
    return jnp.concatenate(out, axis=1)


def _even_prompt_kernel(x_ref, w_ref, caw_ref, cab_ref, wblk_ref, ba_ref, bx_ref, lam_ref, cbw_ref, *rest,
                        ncast):
    cast_src, rest = rest[:ncast], rest[ncast:]
    y_ref, hlast_ref, ca_out_ref, sc_out_ref = rest[:4]
    cast_dst, rest = rest[4:4 + ncast], rest[4 + ncast:]
    p0, p1, xa_ext, cb_ext, a_s, b_s, g_s, h_c = rest
    tt, pitch = EVEN_TT, EVEN_PITCH
    nslab = W_A // LANES
    j = pl.program_id(0)
    _cast_slabs(cast_src, cast_dst)
    bufs = (p0, p1)
    group = p0.shape[0] // tt

    def project(g):
        xg = x_ref[g * group:(g + 1) * group].reshape(group * tt, D_MODEL)
        bufs[g % 2][...] = _dot(xg.astype(BF16), w_ref[...])

    @pl.when(j == 0)
    def _():
        xa_ext[:, :, 0:SUBLANES, :] = jnp.zeros((nslab, BATCH, SUBLANES, LANES), F32)
        cb_ext[:, :, 0:SUBLANES, :] = jnp.zeros((nslab, BATCH, SUBLANES, LANES), F32)
        h_c[...] = jnp.zeros(h_c.shape, F32)

    @pl.when(j > 0)
    def _():
        xa_ext[:, :, 0:SUBLANES, :] = xa_ext[:, :, tt:tt + SUBLANES, :]
        cb_ext[:, :, 0:SUBLANES, :] = cb_ext[:, :, tt:tt + SUBLANES, :]

    sp = _softplus(-lam_ref[...])
    ba = ba_ref[...]
    bx = bx_ref[...]
    cab = cab_ref[...]
    first_row = (lax.broadcasted_iota(jnp.int32, (SUBLANES, 1), 0) == 0) & (j == 0)
    cg0 = 2 * W_A + W_B
    xb0 = 2 * W_A + 2 * W_B

    def stage(b):
        p_ref = bufs[(b // group) % 2].at[(b % group) * tt:(b % group + 1) * tt]
        xc = cab + _causal_taps([p_ref[:, s * LANES:(s + 1) * LANES] for s in range(nslab)], xa_ext, b, caw_ref, tt)
        a, mult, gate = _lru_gates(xc, wblk_ref, ba, bx, sp)
        mult = jnp.concatenate([jnp.where(first_row, 1.0, mult[:SUBLANES]), mult[SUBLANES:]], axis=0)
        bb = mult * (gate * xc)
        for s in range(nslab):
            a_s[s, b * pitch:b * pitch + tt, :] = a[:, s * LANES:(s + 1) * LANES]
            b_s[s, b * pitch:b * pitch + tt, :] = bb[:, s * LANES:(s + 1) * LANES]
        g_s[b] = jax.nn.gelu(p_ref[:, W_A:2 * W_A])
        cb = [p_ref[:, cg0 + s * LANES:cg0 + (s + 1) * LANES] * p_ref[:, xb0 + s * LANES:xb0 + (s + 1) * LANES]
              for s in range(nslab)]
        zb = _causal_taps(cb, cb_ext, b, cbw_ref, tt)
        y_ref[b, :, W_A:] = (p_ref[:, 2 * W_A:2 * W_A + W_B] * zb).astype(BF16)

    project(0)
    for g in range(BATCH // group):
        if g + 1 < BATCH // group:
            project(g + 1)
        for b in range(g * group, (g + 1) * group):
            stage(b)

    def step(t, hs):
        out = []
        for s in range(nslab):
            rows = pl.ds(t, BATCH, stride=pitch)
            hn = a_s[s, rows, :] * hs[s] + b_s[s, rows, :]
            b_s[s, rows, :] = hn
            out.append(hn)
        return tuple(out)

    hs = lax.fori_loop(0, tt, step, tuple(h_c[s] for s in range(nslab)), unroll=4)
    for s in range(nslab):
        h_c[s] = hs[s]

    def emit(b, carry):
        row0 = pl.multiple_of(b * pitch, SUBLANES)
        h = jnp.concatenate([b_s[s, pl.ds(row0, tt), :] for s in range(nslab)], axis=1)
        y_ref[b, :, 0:W_A] = (h * g_s[b]).astype(BF16)
        return carry

    lax.fori_loop(0, BATCH, emit, 0)

    @pl.when(j == pl.num_programs(0) - 1)
    def _():
        hlast_ref[...] = jnp.concatenate(list(hs), axis=1)
        for s in range(nslab):
            ca_out_ref[:, :, s * LANES:(s + 1) * LANES] = xa_ext[s, :, tt:tt + SUBLANES, :]
            sc_out_ref[:, :, s * LANES:(s + 1) * LANES] = cb_ext[s, :, tt:tt + SUBLANES, :]


def _even_param_specs(e):
    return [
        _const_spec((A_CONV, W_A), e),
        _const_spec((1, W_A), e),
        _const_spec((2, W_A // 2, W_A), e),
        _const_spec((1, W_A), e),
        _const_spec((1, W_A), e),
        _const_spec((1, W_A), e),
        _const_spec((B_CONV, W_B), e),
    ]


def _even_prompt(x3d, w, e, params, casts):
    tt, pitch = EVEN_TT, EVEN_PITCH
    nslab = W_A // LANES
    nblock = SEQ // tt
    assert W_A == W_B
    cast_in, cast_out, cast_shapes = _cast_specs(casts, nblock)
    return pl.pallas_call(
        functools.partial(_even_prompt_kernel, ncast=len(casts)),
        grid=(nblock,),
        in_specs=[pl.BlockSpec((BATCH, tt, D_MODEL), lambda j: (0, j, 0)),
                  _const_spec((D_MODEL, EVEN_IN))] + _even_param_specs(e) + cast_in,
        out_specs=[
            pl.BlockSpec((BATCH, tt, D_MODEL), lambda j: (0, j, 0)),
            pl.BlockSpec((BATCH, W_A), lambda j: (0, 0)),
            pl.BlockSpec((BATCH, SUBLANES, W_A), lambda j: (0, 0, 0)),
            pl.BlockSpec((BATCH, SUBLANES, W_B), lambda j: (0, 0, 0)),
        ] + cast_out,
        out_shape=[
            jax.ShapeDtypeStruct((BATCH, SEQ, D_MODEL), BF16),
            jax.ShapeDtypeStruct((BATCH, W_A), F32),
            jax.ShapeDtypeStruct((BATCH, SUBLANES, W_A), F32),
            jax.ShapeDtypeStruct((BATCH, SUBLANES, W_B), F32),
        ] + cast_shapes,
        scratch_shapes=[
            pltpu.VMEM((EVEN_GROUP * tt, EVEN_IN), F32),
            pltpu.VMEM((EVEN_GROUP * tt, EVEN_IN), F32),
            pltpu.VMEM((nslab, BATCH, tt + SUBLANES, LANES), F32),
            pltpu.VMEM((nslab, BATCH, tt + SUBLANES, LANES), F32),
            pltpu.VMEM((nslab, BATCH * pitch, LANES), F32),
            pltpu.VMEM((nslab, BATCH * pitch, LANES), F32),
            pltpu.VMEM((BATCH, tt, W_A), F32),
            pltpu.VMEM((nslab, BATCH, LANES), F32),
        ],
        compiler_params=_params(("arbitrary",), 52),
        name="even_prompt",
    )(x3d, w, *params, *[stack for stack, _ in casts])


def _even_sample_kernel(proj_ref, ca_ref, sc_ref, h0_ref, caw_ref, cab_ref, wblk_ref, ba_ref, bx_ref,
                        lam_ref, cbw_ref, y_ref, hlast_ref, ca_out_ref, sc_out_ref):
    sp = _softplus(-lam_ref[...])
    rows_a = [ca_ref[k] for k in range(A_CONV - 1)] + [proj_ref[l, :, 0:W_A] for l in range(DEC_SEQ)]
    xc = []
    for l in range(DEC_SEQ):
        acc = cab_ref[...] + caw_ref[0:1, :] * rows_a[l]
        for k in range(1, A_CONV):
            acc = acc + caw_ref[k:k + 1, :] * rows_a[l + k]
        xc.append(acc)
    xc_all = jnp.concatenate(xc, axis=0)
    a, mult, gate = _lru_gates(xc_all, wblk_ref, ba_ref[...], bx_ref[...], sp)
    bb = mult * (gate * xc_all)
    h = h0_ref[...]
    rows_b = [sc_ref[k] for k in range(B_CONV - 1)]
    for l in range(DEC_SEQ):
        sl = slice(l * DEC_BATCH, (l + 1) * DEC_BATCH)
        h = a[sl] * h + bb[sl]
        y_ref[l, :, 0:W_A] = (h * jax.nn.gelu(proj_ref[l, :, W_A:2 * W_A])).astype(BF16)
        rows_b.append(proj_ref[l, :, 2 * W_A + W_B:2 * W_A + 2 * W_B] * proj_ref[l, :, 2 * W_A + 2 * W_B:])
    hlast_ref[...] = h
    for l in range(DEC_SEQ):
        zb = cbw_ref[0:1, :] * rows_b[l]
        for k in range(1, B_CONV):
            zb = zb + cbw_ref[k:k + 1, :] * rows_b[l + k]
        y_ref[l, :, W_A:] = (proj_ref[l, :, 2 * W_A:2 * W_A + W_B] * zb).astype(BF16)
    for k in range(A_CONV - 1):
        ca_out_ref[k] = rows_a[DEC_SEQ + k]
    for k in range(B_CONV - 1):
        sc_out_ref[k] = rows_b[DEC_SEQ + k]


def _even_sample(proj_tm, ca_tm, sc_tm, h0_tm, e, params):
    whole = lambda shape: pl.BlockSpec(shape, lambda i: (0,) * len(shape))
    return pl.pallas_call(
        _even_sample_kernel,
        grid=(1,),
        in_specs=[
            whole((DEC_SEQ, DEC_BATCH, EVEN_IN)),
            _const_spec((A_CONV - 1, DEC_BATCH, W_A), e),
            _const_spec((B_CONV - 1, DEC_BATCH, W_B), e),
            _const_spec((DEC_BATCH, W_A), e),
        ] + _even_param_specs(e),
        out_specs=[
            whole((DEC_SEQ, DEC_BATCH, D_MODEL)),
            whole((DEC_BATCH, W_A)),
            whole((A_CONV - 1, DEC_BATCH, W_A)),
            whole((B_CONV - 1, DEC_BATCH, W_B)),
        ],
        out_shape=[
            jax.ShapeDtypeStruct((DEC_SEQ, DEC_BATCH, D_MODEL), BF16),
            jax.ShapeDtypeStruct((DEC_BATCH, W_A), F32),
            jax.ShapeDtypeStruct((A_CONV - 1, DEC_BATCH, W_A), F32),
            jax.ShapeDtypeStruct((B_CONV - 1, DEC_BATCH, W_B), F32),
        ],
        compiler_params=_params(("arbitrary",), 40),
        name="even_sample",
    )(proj_tm, ca_tm, sc_tm, h0_tm, *params)


def _rope(x, cos, sin_signed):
    half = C_DK // 2
    ax = x.ndim - 1
    pieces = []
    for p in range(W_CK // LANES):
        xv = x[..., p * LANES:(p + 1) * LANES]
        lane = lax.broadcasted_iota(jnp.int32, xv.shape, ax)
        partner = jnp.where((lane % C_DK) < half, pltpu.roll(xv, LANES - half, axis=ax), pltpu.roll(xv, half, axis=ax))
        pieces.append(xv * cos[..., p * LANES:(p + 1) * LANES] + partner * sin_signed[..., p * LANES:(p + 1) * LANES])
    return jnp.concatenate(pieces, axis=ax)


def _group_norm(o):
    mu = jnp.mean(o, -1, keepdims=True)
    oc = o - mu
    var = jnp.mean(oc * oc, -1, keepdims=True)
    return oc * lax.rsqrt(var + LN_EPS)


def _odd_chunk(p_ref, b, cos_ref, sin_ref, dec_ref, cs_ref, kd_ref, sdec_ref, gn_ref,
               spw_ref, spb_ref, lng_ref, lnb_ref, y_ref, s_c):
    q = _rope(p_ref[:, 0:W_CK], cos_ref[...], sin_ref[...])
    k = _rope(p_ref[:, W_CK:2 * W_CK], cos_ref[...], sin_ref[...]) * (C_DK ** -0.5)
    kd = k * kd_ref[...]
    v0 = 2 * W_CK
    g0 = v0 + W_CV
    for h in range(C_HEADS):
        qh = q[:, h * C_DK:(h + 1) * C_DK].astype(BF16)
        kh = k[:, h * C_DK:(h + 1) * C_DK].astype(BF16)
        kdh = kd[:, h * C_DK:(h + 1) * C_DK].astype(BF16)
        vh = p_ref[:, v0 + h * C_DV:v0 + (h + 1) * C_DV].astype(BF16)
        s_prev = s_c[b, h]
        scores = lax.dot_general(qh, kh, (((1,), (1,)), ((), ())), preferred_element_type=F32) * dec_ref[h]
        o = _dot(scores.astype(BF16), vh) + _dot(qh, s_prev.astype(BF16)) * cs_ref[:, h * C_DV:(h + 1) * C_DV]
        s_c[b, h] = sdec_ref[h] * s_prev + lax.dot_general(kdh, vh, (((0,), (0,)), ((), ())), preferred_element_type=F32)
        gate = jax.nn.silu(p_ref[:, g0 + h * C_DV:g0 + (h + 1) * C_DV])
        y_ref[b, :, h * C_DV:(h + 1) * C_DV] = (gate * (_group_norm(o) * gn_ref[:, h * C_DV:(h + 1) * C_DV])).astype(BF16)

    u0 = g0 + W_CV
    u = jax.nn.gelu(p_ref[:, u0:u0 + W_D])
    vd = _layer_norm(jax.nn.gelu(p_ref[:, u0 + W_D:]), lng_ref[...], lnb_ref[...]).astype(BF16)
    ri = lax.broadcasted_iota(jnp.int32, (D_CHUNK, D_CHUNK), 0)
    ci = lax.broadcasted_iota(jnp.int32, (D_CHUNK, D_CHUNK), 1)
    for gi in range(D_GROUPS):
        w = jnp.where(ri >= ci, spw_ref[gi], 0.0).astype(BF16)
        s = _dot(w, vd[:, gi * D_GROUP:(gi + 1) * D_GROUP]) + spb_ref[:, gi:gi + 1]
        y_ref[b, :, W_CV + gi * D_GROUP:W_CV + (gi + 1) * D_GROUP] = (u[:, gi * D_GROUP:(gi + 1) * D_GROUP] * s).astype(BF16)


def _odd_prompt_kernel(x_ref, w_ref, cos_ref, sin_ref, dec_ref, cs_ref, kd_ref, sdec_ref, gn_ref,
                       spw_ref, spb_ref, lng_ref, lnb_ref, *rest, ncast):
    cast_src, rest = rest[:ncast], rest[ncast:]
    y_ref, s_out_ref = rest[:2]
    cast_dst, rest = rest[2:2 + ncast], rest[2 + ncast:]
    p0, p1, s_c = rest
    c = pl.program_id(0)
    _cast_slabs(cast_src, cast_dst)

    @pl.when(c == 0)
    def _():
        s_c[...] = jnp.zeros(s_c.shape, F32)

    bufs = (p0, p1)
    group = p0.shape[0] // RET_CHUNK

    def project(g):
        xg = x_ref[g * group:(g + 1) * group].reshape(group * RET_CHUNK, D_MODEL)
        bufs[g % 2][...] = _dot(xg.astype(BF16), w_ref[...])

    project(0)
    for g in range(BATCH // group):
        if g + 1 < BATCH // group:
            project(g + 1)
        for r in range(group):
            _odd_chunk(bufs[g % 2].at[r * RET_CHUNK:(r + 1) * RET_CHUNK], g * group + r,
                       cos_ref, sin_ref, dec_ref, cs_ref, kd_ref, sdec_ref, gn_ref,
                       spw_ref, spb_ref, lng_ref, lnb_ref, y_ref, s_c)

    @pl.when(c == pl.num_programs(0) - 1)
    def _():
        s_out_ref[...] = s_c[...]


def _odd_prompt(x3d, w, o, cos, sin, tabs, gn, spw, spb_t, lng, lnb, casts):
    nchunk = SEQ // RET_CHUNK
    dec, cs, kd, sdec = tabs
    cast_in, cast_out, cast_shapes = _cast_specs(casts, nchunk)
    return pl.pallas_call(
        functools.partial(_odd_prompt_kernel, ncast=len(casts)),
        grid=(nchunk,),
        in_specs=[
            pl.BlockSpec((BATCH, RET_CHUNK, D_MODEL), lambda c: (0, c, 0)),
            _const_spec((D_MODEL, ODD_IN)),
            pl.BlockSpec((RET_CHUNK, W_CK), lambda c: (c, 0)),
            pl.BlockSpec((RET_CHUNK, W_CK), lambda c: (c, 0)),
            _const_spec((C_HEADS, RET_CHUNK, RET_CHUNK)),
            _const_spec((RET_CHUNK, W_CV)),
            _const_spec((RET_CHUNK, W_CK)),
            _const_spec((C_HEADS, C_DK, C_DV)),
            _const_spec((1, W_CV), o),
            _const_spec((D_GROUPS, D_CHUNK, D_CHUNK), o),
            _const_spec((D_CHUNK, D_GROUPS), o),
            _const_spec((1, W_D), o),
            _const_spec((1, W_D), o),
        ] + cast_in,
        out_specs=[
            pl.BlockSpec((BATCH, RET_CHUNK, D_MODEL), lambda c: (0, c, 0)),
            pl.BlockSpec((BATCH, C_HEADS, C_DK, C_DV), lambda c: (0, 0, 0, 0)),
        ] + cast_out,
        out_shape=[
            jax.ShapeDtypeStruct((BATCH, SEQ, D_MODEL), BF16),
            jax.ShapeDtypeStruct((BATCH, C_HEADS, C_DK, C_DV), F32),
        ] + cast_shapes,
        scratch_shapes=[
            pltpu.VMEM((ODD_GROUP * RET_CHUNK, ODD_IN), F32),
            pltpu.VMEM((ODD_GROUP * RET_CHUNK, ODD_IN), F32),
            pltpu.VMEM((BATCH, C_HEADS, C_DK, C_DV), F32),
        ],
        compiler_params=_params(("arbitrary",), 52),
        name="odd_prompt",
    )(x3d, w, cos, sin, dec, cs, kd, sdec, gn, spw, spb_t, lng, lnb, *[stack for stack, _ in casts])


def _odd_sample_kernel(proj_ref, s0_ref, cos_ref, sin_ref, dec_ref, cs_ref, kd_ref, sdec_ref, gn_ref,
                       spw_ref, spb_ref, lng_ref, lnb_ref, *rest):
    y_ref, s_out_ref, vd_ref = rest[-3:]
    if len(rest) > 3:
        s_out_ref[:, 0] = rest[0][...]
        s_out_ref = s_out_ref.at[:, 1]
    q = _rope(proj_ref[:, :, 0:W_CK], cos_ref[...], sin_ref[...])
    k = _rope(proj_ref[:, :, W_CK:2 * W_CK], cos_ref[...], sin_ref[...]) * (C_DK ** -0.5)
    kd = k * kd_ref[...]
    v0 = 2 * W_CK
    g0 = v0 + W_CV
    for h in range(C_HEADS):
        qh = q[:, :, h * C_DK:(h + 1) * C_DK].astype(BF16)
        kh = k[:, :, h * C_DK:(h + 1) * C_DK].astype(BF16)
        kdh = kd[:, :, h * C_DK:(h + 1) * C_DK].astype(BF16)
        vh = proj_ref[:, :, v0 + h * C_DV:v0 + (h + 1) * C_DV].astype(BF16)
        s_prev = s0_ref[:, h]
        scores = jnp.einsum('nld,nmd->nlm', qh, kh, preferred_element_type=F32) * dec_ref[h]
        o = (jnp.einsum('nlm,nme->nle', scores.astype(BF16), vh, preferred_element_type=F32)
             + jnp.einsum('nld,nde->nle', qh, s_prev.astype(BF16), preferred_element_type=F32)
             * cs_ref[:, h * C_DV:(h + 1) * C_DV])
        s_out_ref[:, h] = sdec_ref[h] * s_prev + jnp.einsum('nld,nle->nde', kdh, vh, preferred_element_type=F32)
        gate = jax.nn.silu(proj_ref[:, :, g0 + h * C_DV:g0 + (h + 1) * C_DV])
        y_ref[:, :, h * C_DV:(h + 1) * C_DV] = (gate * (_group_norm(o) * gn_ref[:, h * C_DV:(h + 1) * C_DV])).astype(BF16)

    u0 = g0 + W_CV
    u = jax.nn.gelu(proj_ref[:, :, u0:u0 + W_D])
    vd = _layer_norm(jax.nn.gelu(proj_ref[:, :, u0 + W_D:]), lng_ref[...], lnb_ref[...])
    vd_ref[...] = vd
    s = spb_ref[...] + spw_ref[0] * vd[:, 0:1, :]
    for m in range(1, DEC_SEQ):
        s = s + spw_ref[m] * vd[:, m:m + 1, :]
    y_ref[:, :, W_CV:] = (u * s).astype(BF16)


def _odd_sample(proj, state_ret, o, cos, sin, tabs, gn, spw_rows, spb_rows, lng, lnb, prev_states):
    nb = SAMPLE_NB
    dec, cs, kd, sdec = tabs
    seq3 = lambda i: (i, 0, 0)
    one_state = pl.BlockSpec((nb, C_HEADS, C_DK, C_DV), lambda i: (i, 0, 0, 0))
    in_specs = [
        pl.BlockSpec((nb, DEC_SEQ, ODD_IN), seq3),
        pl.BlockSpec((nb, None, C_HEADS, C_DK, C_DV), lambda i: (i, o, 0, 0, 0)),
        _const_spec((DEC_SEQ, W_CK)),
        _const_spec((DEC_SEQ, W_CK)),
        _const_spec((C_HEADS, DEC_SEQ, DEC_SEQ)),
        _const_spec((DEC_SEQ, W_CV)),
        _const_spec((DEC_SEQ, W_CK)),
        _const_spec((C_HEADS, C_DK, C_DV)),
        _const_spec((1, W_CV), o),
        _const_spec((DEC_SEQ, DEC_SEQ, W_D), o),
        _const_spec((DEC_SEQ, W_D), o),
        _const_spec((1, W_D), o),
        _const_spec((1, W_D), o),
    ]
    args = [proj, state_ret, cos, sin, dec, cs, kd, sdec, gn, spw_rows, spb_rows, lng, lnb]
    if prev_states is None:
        state_spec = one_state
        state_shape = (DEC_BATCH, C_HEADS, C_DK, C_DV)
    else:
        in_specs.append(one_state)
        args.append(prev_states)
        state_spec = pl.BlockSpec((nb, N_ODD, C_HEADS, C_DK, C_DV), lambda i: (i, 0, 0, 0, 0))
        state_shape = (DEC_BATCH, N_ODD, C_HEADS, C_DK, C_DV)
    return pl.pallas_call(
        _odd_sample_kernel,
        grid=(DEC_BATCH // nb,),
        in_specs=in_specs,
        out_specs=[pl.BlockSpec((nb, DEC_SEQ, D_MODEL), seq3), state_spec, pl.BlockSpec((nb, DEC_SEQ, W_D), seq3)],
        out_shape=[
            jax.ShapeDtypeStruct((DEC_BATCH, DEC_SEQ, D_MODEL), BF16),
            jax.ShapeDtypeStruct(state_shape, F32),
            jax.ShapeDtypeStruct((DEC_BATCH, DEC_SEQ, W_D), F32),
        ],
        compiler_params=_params(("parallel",), 48),
        name="odd_sample",
    )(*args)


def _rope_tables(pos):
    half = C_DK // 2
    freq = ROPE_BASE ** (-np.arange(half, dtype=np.float64) / half)
    ang = np.asarray(pos, np.float64)[:, None] * freq
    cos, sin = np.cos(ang), np.sin(ang)
    cos_l = np.tile(np.concatenate([cos, cos], axis=-1), (1, C_HEADS))
    sin_l = np.tile(np.concatenate([-sin, sin], axis=-1), (1, C_HEADS))
    return cos_l.astype(np.float32), sin_l.astype(np.float32)


def _retention_tables(length):
    log_g = np.log1p(-np.exp2(-5.0 - np.arange(C_HEADS, dtype=np.float64)))
    idx = np.arange(length, dtype=np.float64)
    rel = idx[:, None] - idx[None, :]
    decay = np.where(rel >= 0, np.exp(log_g[:, None, None] * np.maximum(rel, 0.0)), 0.0)
    cross = np.exp(log_g[None, :] * (idx[:, None] + 1.0))
    kdec = np.exp(log_g[None, :] * (length - 1.0 - idx[:, None]))
    sdec = np.exp(log_g * length)
    tabs = (decay,
            np.repeat(cross, C_DV, axis=1),
            np.repeat(kdec, C_DK, axis=1),
            np.broadcast_to(sdec[:, None, None], (C_HEADS, C_DK, C_DV)))
    return tuple(np.ascontiguousarray(t, dtype=np.float32) for t in tabs)


def _block_diag_halves(wa, wx):
    nb = A_BLOCKS // 2
    width = nb * A_BLOCK

    def bd(w, hh):
        rows = [jnp.pad(w[:, hh * nb + g], ((0, 0), (0, 0), (g * A_BLOCK, width - (g + 1) * A_BLOCK)))
                for g in range(nb)]
        return jnp.concatenate(rows, axis=1)

    halves = [jnp.concatenate([bd(wa, hh), bd(wx, hh)], axis=-1) for hh in range(2)]
    return jnp.stack(halves, axis=1).astype(BF16)


def kernel(x_prompt, x_sample, state_lru_h, state_lru_conv, state_sconv, state_ret, w_in_even, conv_a_w, conv_a_b, lru_wa, lru_ba, lru_wx, lru_bx, lru_lam, conv_b_w, w_out_even, w_in_odd, ret_gn_g, sp_w, sp_b, gm_ln_g, gm_ln_b, w_out_odd, ffn_w1, ffn_w3, ffn_w2, ln1_g, ln1_b, ln2_g, ln2_b):
    rows = lambda v: v.reshape(v.shape[0], 1, v.shape[1])
    xp = x_prompt.reshape(BATCH * SEQ, D_MODEL)
    xs = jnp.swapaxes(x_sample, 0, 1).reshape(DEC_SEQ * DEC_BATCH, D_MODEL)

    cos_p, sin_p = _rope_tables(np.arange(SEQ))
    cos_s, sin_s = _rope_tables(PAST_LEN + np.arange(DEC_SEQ))
    tab_p = _retention_tables(RET_CHUNK)
    tab_s = _retention_tables(DEC_SEQ)

    w_in = w_in_even[0].astype(BF16)
    ln1, ln2 = (rows(ln1_g), rows(ln1_b)), (rows(ln2_g), rows(ln2_b))
    even_p = (conv_a_w, rows(conv_a_b), _block_diag_halves(lru_wa, lru_wx), rows(lru_ba), rows(lru_bx),
              rows(lru_lam), conv_b_w)
    ca_tm = jnp.transpose(state_lru_conv, (1, 2, 0, 3))
    sc_tm = jnp.transpose(state_sconv, (1, 2, 0, 3))
    h0_tm = jnp.swapaxes(state_lru_h, 0, 1)
    tril = jnp.tril(sp_w[:, :, :DEC_SEQ, :DEC_SEQ])
    spw_rows = jnp.repeat(jnp.transpose(tril, (0, 3, 2, 1)), D_GROUP, axis=3)
    spb_rows = jnp.repeat(jnp.swapaxes(sp_b[:, :, :DEC_SEQ], 1, 2), D_GROUP, axis=2)
    spb_t = jnp.swapaxes(sp_b, 1, 2)
    odd_small = (rows(ret_gn_g),)
    odd_ln = (rows(gm_ln_g), rows(gm_ln_b))

    h_p, h_s, ca_p, ca_s, sc_p, sc_s, r_p, v_s = [], [], [], [], [], [], [], []
    ret_sample = None
    for l in range(DEPTH):
        if l % 2 == 0:
            e = l // 2
            casts = [(w_out_even, e), (ffn_w1, l), (ffn_w3, l), (ffn_w2, l), (w_in_odd, e)]
            yp, hlp, cap, scp, wo, w1, w3, w2, w_in_next = _even_prompt(xp.reshape(BATCH, SEQ, D_MODEL), w_in, e,
                                                                        even_p, casts)
            proj_s = _in_proj(xs, w_in).reshape(DEC_SEQ, DEC_BATCH, EVEN_IN)
            ys, hls, cas, scs = _even_sample(proj_s, ca_tm, sc_tm, h0_tm, e, even_p)
            h_p.append(hlp)
            h_s.append(hls)
            ca_p.append(cap[:, SUBLANES - (A_CONV - 1):])
            ca_s.append(jnp.swapaxes(cas, 0, 1))
            sc_p.append(scp[:, SUBLANES - (B_CONV - 1):])
            sc_s.append(jnp.swapaxes(scs, 0, 1))
            yp = yp.reshape(BATCH * SEQ, D_MODEL)
            ys = ys.reshape(DEC_SEQ * DEC_BATCH, D_MODEL)
        else:
            o = l // 2
            casts = [(w_out_odd, o), (ffn_w1, l), (ffn_w3, l), (ffn_w2, l)]
            if l + 1 < DEPTH:
                casts.append((w_in_even, o + 1))
            yp, rp, wo, w1, w3, w2, *w_in_next = _odd_prompt(xp.reshape(BATCH, SEQ, D_MODEL), w_in, o, cos_p, sin_p,
                                                             tab_p, *odd_small, sp_w, spb_t, *odd_ln, casts)
            w_in_next = w_in_next[0] if w_in_next else None
            proj_s = _in_proj(xs, w_in).reshape(DEC_BATCH, DEC_SEQ, ODD_IN)
            ys, ret_sample, vs = _odd_sample(proj_s, state_ret, o, cos_s, sin_s, tab_s, *odd_small,
                                             spw_rows, spb_rows, *odd_ln, ret_sample)
            r_p.append(rp)
            v_s.append(vs)
            yp = yp.reshape(BATCH * SEQ, D_MODEL)
            ys = ys.reshape(DEC_BATCH * DEC_SEQ, D_MODEL)
        xp, xs = _dense_block(yp, xp, ys, xs, l, wo, *ln1, w1, w3, w2, *ln2)
        w_in = w_in_next
        if l + 1 < DEPTH:
            if l % 2 == 0:
                xs = jnp.swapaxes(xs.reshape(DEC_SEQ, DEC_BATCH, D_MODEL), 0, 1).reshape(DEC_BATCH * DEC_SEQ, D_MODEL)
            else:
                xs = jnp.swapaxes(xs.reshape(DEC_BATCH, DEC_SEQ, D_MODEL), 0, 1).reshape(DEC_SEQ * DEC_BATCH, D_MODEL)

    return (xp.reshape(BATCH, SEQ, D_MODEL), xs.reshape(DEC_BATCH, DEC_SEQ, D_MODEL),
            jnp.stack(h_p, axis=1), jnp.stack(h_s, axis=1),
            jnp.stack(ca_p, axis=1), jnp.stack(ca_s, axis=1),
            jnp.stack(sc_p, axis=1), jnp.stack(sc_s, axis=1),
            jnp.stack(r_p, axis=1), ret_sample, jnp.stack(v_s, axis=1))
```

```python
import functools

import jax
import jax.numpy as jnp
import numpy as np
from jax import lax
from jax.experimental import pallas as pl
from jax.experimental.pallas import tpu as pltpu

F32 = jnp.float32
BF16 = jnp.bfloat16

D_MODEL = 1024
BATCH = 8
SEQ = 2048
DEPTH = 4
DEC_BATCH = 128
DEC_SEQ = 4
PAST_LEN = 16384
N_ODD = DEPTH // 2
W_A = 512
A_BLOCKS = 8
A_BLOCK = 64
A_CONV = 4
LRU_C = 8.0
W_B = 512
B_CONV = 3
C_HEADS = 4
C_DK = 64
C_DV = 128
W_CK = C_HEADS * C_DK
W_CV = C_HEADS * C_DV
RET_CHUNK = 128
ROPE_BASE = 10000.0
D_GROUPS = 4
D_CHUNK = 128
W_D = 512
D_GROUP = W_D // D_GROUPS
EVEN_IN = 2 * W_A + 3 * W_B
ODD_IN = 2 * W_CK + 2 * W_CV + 2 * W_D
D_FF = 2816
ALPHA = (2 * DEPTH) ** 0.25
LN_EPS = 1e-5

SUBLANES = 8
LANES = 128
MIB = 1024 * 1024

ROW_TILE = 512
DENSE_TILE = 512
DENSE_GROUP = 256
DENSE_COLS = 256
EVEN_TT = 128
EVEN_GROUP = 1
ODD_GROUP = 2
EVEN_PITCH = EVEN_TT + SUBLANES
SAMPLE_NB = 32


def _params(sem, vmem_mib):
    return pltpu.CompilerParams(dimension_semantics=sem, vmem_limit_bytes=vmem_mib * MIB)


def _const_spec(shape, layer=None):
    nd = len(shape)
    if layer is None:
        return pl.BlockSpec(shape, lambda *_: (0,) * nd, pipeline_mode=pl.Buffered(1))
    return pl.BlockSpec((None,) + tuple(shape), lambda *_: (layer,) + (0,) * nd, pipeline_mode=pl.Buffered(1))


def _layer_norm(x, g, b):
    mu = jnp.mean(x, -1, keepdims=True)
    xc = x - mu
    var = jnp.mean(xc * xc, -1, keepdims=True)
    return xc * lax.rsqrt(var + LN_EPS) * g + b


def _softplus(x):
    return jnp.maximum(x, 0.0) + jnp.log1p(jnp.exp(-jnp.abs(x)))


def _dot(a, b):
    return jnp.dot(a, b, preferred_element_type=F32)


def _cast_specs(stacks_and_layers, nsteps):
    in_specs, out_specs, out_shapes = [], [], []
    for w, layer in stacks_and_layers:
        nrow, ncol = w.shape[1:]
        rb = nrow // nsteps
        assert nrow % nsteps == 0 and rb % (2 * SUBLANES) == 0
        in_specs.append(pl.BlockSpec((None, rb, ncol), lambda j, layer=layer: (layer, j, 0)))
        out_specs.append(pl.BlockSpec((rb, ncol), lambda j: (j, 0)))
        out_shapes.append(jax.ShapeDtypeStruct((nrow, ncol), BF16))
    return in_specs, out_specs, out_shapes


def _cast_slabs(src_refs, dst_refs):
    for src, dst in zip(src_refs, dst_refs):
        dst[...] = src[...].astype(BF16)


def _proj_kernel(x_ref, w_ref, o_ref):
    o_ref[...] = _dot(x_ref[...].astype(BF16), w_ref[...])


def _in_proj(x2d, w):
    m, k = x2d.shape
    n = w.shape[1]
    tm = min(ROW_TILE, m)
    return pl.pallas_call(
        _proj_kernel,
        grid=(m // tm,),
        in_specs=[pl.BlockSpec((tm, k), lambda i: (i, 0)), _const_spec((k, n))],
        out_specs=pl.BlockSpec((tm, n), lambda i: (i, 0)),
        out_shape=jax.ShapeDtypeStruct((m, n), F32),
        compiler_params=_params(("parallel",), 40),
        name="in_proj",
    )(x2d, w)


def _dense_kernel(yp_ref, xp_ref, ys_ref, xs_ref, wo_ref, g1_ref, b1_ref, w1_ref, w3_ref, w2_ref, g2_ref, b2_ref,
                  op_ref, os_ref, h_s):
    n_prompt = pl.num_programs(0) - 1
    weights = (wo_ref, g1_ref, b1_ref, w1_ref, w3_ref, w2_ref, g2_ref, b2_ref)

    @pl.when(pl.program_id(0) < n_prompt)
    def _():
        _dense_rows(yp_ref, xp_ref, *weights, op_ref, h_s)

    @pl.when(pl.program_id(0) == n_prompt)
    def _():
        _dense_rows(ys_ref, xs_ref, *weights, os_ref, h_s)


def _dense_rows(y_ref, x_ref, wo_ref, g1_ref, b1_ref, w1_ref, w3_ref, w2_ref, g2_ref, b2_ref, o_ref, h_s):
    rows, ncols = DENSE_GROUP, DENSE_COLS
    nparts = y_ref.shape[0] // rows
    sls = [slice(p * rows, (p + 1) * rows) for p in range(nparts)]
    mix, x1, xb = {}, {}, {}

    def out_proj(p):
        mix[p] = _dot(y_ref[sls[p], :], wo_ref[...])

    def norm1(p):
        x1[p] = _layer_norm(ALPHA * x_ref[sls[p], :] + mix[p], g1_ref[...], b1_ref[...])
        xb[p] = x1[p].astype(BF16)

    def gate_up(p):
        for c in range(0, D_FF, ncols):
            cols = slice(c, c + ncols)
            h_s[sls[p], cols] = (jax.nn.silu(_dot(xb[p], w1_ref[:, cols]))
                                 * _dot(xb[p], w3_ref[:, cols])).astype(BF16)

    def down_norm2(p):
        f = _dot(h_s[sls[p], :], w2_ref[...])
        o_ref[sls[p], :] = _layer_norm(ALPHA * x1[p] + f, g2_ref[...], b2_ref[...])

    out_proj(0)
    for p in range(nparts):
        norm1(p)
        if p + 1 < nparts:
            out_proj(p + 1)
    for p in range(nparts):
        gate_up(p)
    for p in range(nparts):
        down_norm2(p)


def _dense_block(yp, xp, ys, xs, layer, wo, g1, b1, w1, w3, w2, g2, b2):
    tm = DENSE_TILE
    n_prompt = xp.shape[0] // tm
    assert xp.shape[0] % tm == 0 and xs.shape[0] == tm and tm % DENSE_GROUP == 0
    prompt_row = lambda i: (jnp.minimum(i, n_prompt - 1), 0)
    sample_row = lambda i: (0, 0)
    return pl.pallas_call(
        _dense_kernel,
        grid=(n_prompt + 1,),
        in_specs=[
            pl.BlockSpec((tm, D_MODEL), prompt_row),
            pl.BlockSpec((tm, D_MODEL), prompt_row),
            pl.BlockSpec((tm, D_MODEL), sample_row),
            pl.BlockSpec((tm, D_MODEL), sample_row),
            _const_spec((D_MODEL, D_MODEL)),
            _const_spec((1, D_MODEL), layer),
            _const_spec((1, D_MODEL), layer),
            _const_spec((D_MODEL, D_FF)),
            _const_spec((D_MODEL, D_FF)),
            _const_spec((D_FF, D_MODEL)),
            _const_spec((1, D_MODEL), layer),
            _const_spec((1, D_MODEL), layer),
        ],
        out_specs=[pl.BlockSpec((tm, D_MODEL), prompt_row), pl.BlockSpec((tm, D_MODEL), sample_row)],
        out_shape=[jax.ShapeDtypeStruct(xp.shape, F32), jax.ShapeDtypeStruct(xs.shape, F32)],
        scratch_shapes=[pltpu.VMEM((tm, D_FF), BF16)],
        compiler_params=_params(("arbitrary",), 58),
        name="dense_block",
    )(yp, xp, ys, xs, wo, g1, b1, w1, w3, w2, g2, b2)


def _lru_gates(xc, wblk_ref, ba, bx, sp):
    half = W_A // 2
    xb = xc.astype(BF16)
    pre0 = _dot(xb[:, :half], wblk_ref[0])
    pre1 = _dot(xb[:, half:], wblk_ref[1])
    r = jax.nn.sigmoid(jnp.concatenate([pre0[:, :half], pre1[:, :half]], axis=1) + ba)
    i = jax.nn.sigmoid(jnp.concatenate([pre0[:, half:], pre1[:, half:]], axis=1) + bx)
    log_a = (-LRU_C) * r * sp
    a = jnp.exp(log_a)
    mult = jnp.sqrt((1.0 + a * a) * jnp.tanh(-log_a))
    return a, mult, i


def _causal_taps(x_slabs, ext, b, w_ref, nrows):
    ktaps = w_ref.shape[0]
    out = []
    for s, x in enumerate(x_slabs):
        lanes = slice(s * LANES, (s + 1) * LANES)
        ext[s, b, SUBLANES:SUBLANES + nrows, :] = x
        acc = w_ref[ktaps - 1:ktaps, lanes] * x
        for k in range(ktaps - 1):
            off = SUBLANES - (ktaps - 1) + k
            acc = acc + w_ref[k:k + 1, lanes] * ext[s, b, off:off + nrows, :]
        out.append(acc)
    return jnp.concatenate(out, axis=1)


def _even_prompt_kernel(x_ref, w_ref, caw_ref, cab_ref, wblk_ref, ba_ref, bx_ref, lam_ref, cbw_ref, *rest,
                        ncast):
    cast_src, rest = rest[:ncast], rest[ncast:]
    y_ref, hlast_ref, ca_out_ref, sc_out_ref = rest[:4]
    cast_dst, rest = rest[4:4 + ncast], rest[4 + ncast:]
    p0, p1, xa_ext, cb_ext, a_s, b_s, g_s, h_c = rest
    tt, pitch = EVEN_TT, EVEN_PITCH
    nslab = W_A // LANES
    j = pl.program_id(0)
    _cast_slabs(cast_src, cast_dst)
    bufs = (p0, p1)
    group = p0.shape[0] // tt

    def project(g):
        xg = x_ref[g * group:(g + 1) * group].reshape(group * tt, D_MODEL)
        bufs[g % 2][...] = _dot(xg.astype(BF16), w_ref[...])

    @pl.when(j == 0)
    def _():
        xa_ext[:, :, 0:SUBLANES, :] = jnp.zeros((nslab, BATCH, SUBLANES, LANES), F32)
        cb_ext[:, :, 0:SUBLANES, :] = jnp.zeros((nslab, BATCH, SUBLANES, LANES), F32)
        h_c[...] = jnp.zeros(h_c.shape, F32)

    @pl.when(j > 0)
    def _():
        xa_ext[:, :, 0:SUBLANES, :] = xa_ext[:, :, tt:tt + SUBLANES, :]
        cb_ext[:, :, 0:SUBLANES, :] = cb_ext[:, :, tt:tt + SUBLANES, :]

    sp = _softplus(-lam_ref[...])
    ba = ba_ref[...]
    bx = bx_ref[...]
    cab = cab_ref[...]
    first_row = (lax.broadcasted_iota(jnp.int32, (SUBLANES, 1), 0) == 0) & (j == 0)
    cg0 = 2 * W_A + W_B
    xb0 = 2 * W_A + 2 * W_B

    def stage(b):
        p_ref = bufs[(b // group) % 2].at[(b % group) * tt:(b % group + 1) * tt]
        xc = cab + _causal_taps([p_ref[:, s * LANES:(s + 1) * LANES] for s in range(nslab)], xa_ext, b, caw_ref, tt)
        a, mult, gate = _lru_gates(xc, wblk_ref, ba, bx, sp)
        mult = jnp.concatenate([jnp.where(first_row, 1.0, mult[:SUBLANES]), mult[SUBLANES:]], axis=0)
        bb = mult * (gate * xc)
        for s in range(nslab):
            a_s[s, b * pitch:b * pitch + tt, :] = a[:, s * LANES:(s + 1) * LANES]
            b_s[s, b * pitch:b * pitch + tt, :] = bb[:, s * LANES:(s + 1) * LANES]
        g_s[b] = jax.nn.gelu(p_ref[:, W_A:2 * W_A])
        cb = [p_ref[:, cg0 + s * LANES:cg0 + (s + 1) * LANES] * p_ref[:, xb0 + s * LANES:xb0 + (s + 1) * LANES]
              for s in range(nslab)]
        zb = _causal_taps(cb, cb_ext, b, cbw_ref, tt)
        y_ref[b, :, W_A:] = (p_ref[:, 2 * W_A:2 * W_A + W_B] * zb).astype(BF16)

    project(0)
    for g in range(BATCH // group):
        if g + 1 < BATCH // group:
            project(g + 1)
        for b in range(g * group, (g + 1) * group):
            stage(b)

    def step(t, hs):
        out = []
        for s in range(nslab):
            rows = pl.ds(t, BATCH, stride=pitch)
            hn = a_s[s, rows, :] * hs[s] + b_s[s, rows, :]
            b_s[s, rows, :] = hn
            out.append(hn)
        return tuple(out)

    hs = lax.fori_loop(0, tt, step, tuple(h_c[s] for s in range(nslab)), unroll=4)
    for s in range(nslab):
        h_c[s] = hs[s]

    def emit(b, carry):
        row0 = pl.multiple_of(b * pitch, SUBLANES)
        h = jnp.concatenate([b_s[s, pl.ds(row0, tt), :] for s in range(nslab)], axis=1)
        y_ref[b, :, 0:W_A] = (h * g_s[b]).astype(BF16)
        return carry

    lax.fori_loop(0, BATCH, emit, 0)

    @pl.when(j == pl.num_programs(0) - 1)
    def _():
        hlast_ref[...] = jnp.concatenate(list(hs), axis=1)
        for s in range(nslab):
            ca_out_ref[:, :, s * LANES:(s + 1) * LANES] = xa_ext[s, :, tt:tt + SUBLANES, :]
            sc_out_ref[:, :, s * LANES:(s + 1) * LANES] = cb_ext[s, :, tt:tt + SUBLANES, :]


def _even_param_specs(e):
    return [
        _const_spec((A_CONV, W_A), e),
        _const_spec((1, W_A), e),
        _const_spec((2, W_A // 2, W_A), e),
        _const_spec((1, W_A), e),
        _const_spec((1, W_A), e),
        _const_spec((1, W_A), e),
        _const_spec((B_CONV, W_B), e),
    ]


def _even_prompt(x3d, w, e, params, casts):
    tt, pitch = EVEN_TT, EVEN_PITCH
    nslab = W_A // LANES
    nblock = SEQ // tt
    assert W_A == W_B
    cast_in, cast_out, cast_shapes = _cast_specs(casts, nblock)
    return pl.pallas_call(
        functools.partial(_even_prompt_kernel, ncast=len(casts)),
        grid=(nblock,),
        in_specs=[pl.BlockSpec((BATCH, tt, D_MODEL), lambda j: (0, j, 0)),
                  _const_spec((D_MODEL, EVEN_IN))] + _even_param_specs(e) + cast_in,
        out_specs=[
            pl.BlockSpec((BATCH, tt, D_MODEL), lambda j: (0, j, 0)),
            pl.BlockSpec((BATCH, W_A), lambda j: (0, 0)),
            pl.BlockSpec((BATCH, SUBLANES, W_A), lambda j: (0, 0, 0)),
            pl.BlockSpec((BATCH, SUBLANES, W_B), lambda j: (0, 0, 0)),
        ] + cast_out,
        out_shape=[
            jax.ShapeDtypeStruct((BATCH, SEQ, D_MODEL), BF16),
            jax.ShapeDtypeStruct((BATCH, W_A), F32),
            jax.ShapeDtypeStruct((BATCH, SUBLANES, W_A), F32),
            jax.ShapeDtypeStruct((BATCH, SUBLANES, W_B), F32),
        ] + cast_shapes,
        scratch_shapes=[
            pltpu.VMEM((EVEN_GROUP * tt, EVEN_IN), F32),
            pltpu.VMEM((EVEN_GROUP * tt, EVEN_IN), F32),
            pltpu.VMEM((nslab, BATCH, tt + SUBLANES, LANES), F32),
            pltpu.VMEM((nslab, BATCH, tt + SUBLANES, LANES), F32),
            pltpu.VMEM((nslab, BATCH * pitch, LANES), F32),
            pltpu.VMEM((nslab, BATCH * pitch, LANES), F32),
            pltpu.VMEM((BATCH, tt, W_A), F32),
            pltpu.VMEM((nslab, BATCH, LANES), F32),
        ],
        compiler_params=_params(("arbitrary",), 52),
        name="even_prompt",
    )(x3d, w, *params, *[stack for stack, _ in casts])


def _even_sample_kernel(proj_ref, ca_ref, sc_ref, h0_ref, caw_ref, cab_ref, wblk_ref, ba_ref, bx_ref,
                        lam_ref, cbw_ref, y_ref, hlast_ref, ca_out_ref, sc_out_ref):
    sp = _softplus(-lam_ref[...])
    rows_a = [ca_ref[k] for k in range(A_CONV - 1)] + [proj_ref[l, :, 0:W_A] for l in range(DEC_SEQ)]
    xc = []
    for l in range(DEC_SEQ):
        acc = cab_ref[...] + caw_ref[0:1, :] * rows_a[l]
        for k in range(1, A_CONV):
            acc = acc + caw_ref[k:k + 1, :] * rows_a[l + k]
        xc.append(acc)
    xc_all = jnp.concatenate(xc, axis=0)
    a, mult, gate = _lru_gates(xc_all, wblk_ref, ba_ref[...], bx_ref[...], sp)
    bb = mult * (gate * xc_all)
    h = h0_ref[...]
    rows_b = [sc_ref[k] for k in range(B_CONV - 1)]
    for l in range(DEC_SEQ):
        sl = slice(l * DEC_BATCH, (l + 1) * DEC_BATCH)
        h = a[sl] * h + bb[sl]
        y_ref[l, :, 0:W_A] = (h * jax.nn.gelu(proj_ref[l, :, W_A:2 * W_A])).astype(BF16)
        rows_b.append(proj_ref[l, :, 2 * W_A + W_B:2 * W_A + 2 * W_B] * proj_ref[l, :, 2 * W_A + 2 * W_B:])
    hlast_ref[...] = h
    for l in range(DEC_SEQ):
        zb = cbw_ref[0:1, :] * rows_b[l]
        for k in range(1, B_CONV):
            zb = zb + cbw_ref[k:k + 1, :] * rows_b[l + k]
        y_ref[l, :, W_A:] = (proj_ref[l, :, 2 * W_A:2 * W_A + W_B] * zb).astype(BF16)
    for k in range(A_CONV - 1):
        ca_out_ref[k] = rows_a[DEC_SEQ + k]
    for k in range(B_CONV - 1):
        sc_out_ref[k] = rows_b[DEC_SEQ + k]


def _even_sample(proj_tm, ca_tm, sc_tm, h0_tm, e, params):
    whole = lambda shape: pl.BlockSpec(shape, lambda i: (0,) * len(shape))
    return pl.pallas_call(
        _even_sample_kernel,
        grid=(1,),
        in_specs=[
            whole((DEC_SEQ, DEC_BATCH, EVEN_IN)),
            _const_spec((A_CONV - 1, DEC_BATCH, W_A), e),
            _const_spec((B_CONV - 1, DEC_BATCH, W_B), e),
            _const_spec((DEC_BATCH, W_A), e),
        ] + _even_param_specs(e),
        out_specs=[
            whole((DEC_SEQ, DEC_BATCH, D_MODEL)),
            whole((DEC_BATCH, W_A)),
            whole((A_CONV - 1, DEC_BATCH, W_A)),
            whole((B_CONV - 1, DEC_BATCH, W_B)),
        ],
        out_shape=[
            jax.ShapeDtypeStruct((DEC_SEQ, DEC_BATCH, D_MODEL), BF16),
            jax.ShapeDtypeStruct((DEC_BATCH, W_A), F32),
            jax.ShapeDtypeStruct((A_CONV - 1, DEC_BATCH, W_A), F32),
            jax.ShapeDtypeStruct((B_CONV - 1, DEC_BATCH, W_B), F32),
        ],
        compiler_params=_params(("arbitrary",), 40),
        name="even_sample",
    )(proj_tm, ca_tm, sc_tm, h0_tm, *params)


def _rope(x, cos, sin_signed):
    half = C_DK // 2
    ax = x.ndim - 1
    pieces = []
    for p in range(W_CK // LANES):
        xv = x[..., p * LANES:(p + 1) * LANES]
        lane = lax.broadcasted_iota(jnp.int32, xv.shape, ax)
        partner = jnp.where((lane % C_DK) < half, pltpu.roll(xv, LANES - half, axis=ax), pltpu.roll(xv, half, axis=ax))
        pieces.append(xv * cos[..., p * LANES:(p + 1) * LANES] + partner * sin_signed[..., p * LANES:(p + 1) * LANES])
    return jnp.concatenate(pieces, axis=ax)


def _group_norm(o):
    mu = jnp.mean(o, -1, keepdims=True)
    oc = o - mu
    var = jnp.mean(oc * oc, -1, keepdims=True)
    return oc * lax.rsqrt(var + LN_EPS)


def _odd_chunk(p_ref, b, cos_ref, sin_ref, dec_ref, cs_ref, kd_ref, sdec_ref, gn_ref,
               spw_ref, spb_ref, lng_ref, lnb_ref, y_ref, s_c):
    q = _rope(p_ref[:, 0:W_CK], cos_ref[...], sin_ref[...])
    k = _rope(p_ref[:, W_CK:2 * W_CK], cos_ref[...], sin_ref[...]) * (C_DK ** -0.5)
    kd = k * kd_ref[...]
    v0 = 2 * W_CK
    g0 = v0 + W_CV
    for h in range(C_HEADS):
        qh = q[:, h * C_DK:(h + 1) * C_DK].astype(BF16)
        kh = k[:, h * C_DK:(h + 1) * C_DK].astype(BF16)
        kdh = kd[:, h * C_DK:(h + 1) * C_DK].astype(BF16)
        vh = p_ref[:, v0 + h * C_DV:v0 + (h + 1) * C_DV].astype(BF16)
        s_prev = s_c[b, h]
        scores = lax.dot_general(qh, kh, (((1,), (1,)), ((), ())), preferred_element_type=F32) * dec_ref[h]
        o = _dot(scores.astype(BF16), vh) + _dot(qh, s_prev.astype(BF16)) * cs_ref[:, h * C_DV:(h + 1) * C_DV]
        s_c[b, h] = sdec_ref[h] * s_prev + lax.dot_general(kdh, vh, (((0,), (0,)), ((), ())), preferred_element_type=F32)
        gate = jax.nn.silu(p_ref[:, g0 + h * C_DV:g0 + (h + 1) * C_DV])
        y_ref[b, :, h * C_DV:(h + 1) * C_DV] = (gate * (_group_norm(o) * gn_ref[:, h * C_DV:(h + 1) * C_DV])).astype(BF16)

    u0 = g0 + W_CV
    u = jax.nn.gelu(p_ref[:, u0:u0 + W_D])
    vd = _layer_norm(jax.nn.gelu(p_ref[:, u0 + W_D:]), lng_ref[...], lnb_ref[...]).astype(BF16)
    ri = lax.broadcasted_iota(jnp.int32, (D_CHUNK, D_CHUNK), 0)
    ci = lax.broadcasted_iota(jnp.int32, (D_CHUNK, D_CHUNK), 1)
    for gi in range(D_GROUPS):
        w = jnp.where(ri >= ci, spw_ref[gi], 0.0).astype(BF16)
        s = _dot(w, vd[:, gi * D_GROUP:(gi + 1) * D_GROUP]) + spb_ref[:, gi:gi + 1]
        y_ref[b, :, W_CV + gi * D_GROUP:W_CV + (gi + 1) * D_GROUP] = (u[:, gi * D_GROUP:(gi + 1) * D_GROUP] * s).astype(BF16)


def _odd_prompt_kernel(x_ref, w_ref, cos_ref, sin_ref, dec_ref, cs_ref, kd_ref, sdec_ref, gn_ref,
                       spw_ref, spb_ref, lng_ref, lnb_ref, *rest, ncast):
    cast_src, rest = rest[:ncast], rest[ncast:]
    y_ref, s_out_ref = rest[:2]
    cast_dst, rest = rest[2:2 + ncast], rest[2 + ncast:]
    p0, p1, s_c = rest
    c = pl.program_id(0)
    _cast_slabs(cast_src, cast_dst)

    @pl.when(c == 0)
    def _():
        s_c[...] = jnp.zeros(s_c.shape, F32)

    bufs = (p0, p1)
    group = p0.shape[0] // RET_CHUNK

    def project(g):
        xg = x_ref[g * group:(g + 1) * group].reshape(group * RET_CHUNK, D_MODEL)
        bufs[g % 2][...] = _dot(xg.astype(BF16), w_ref[...])

    project(0)
    for g in range(BATCH // group):
        if g + 1 < BATCH // group:
            project(g + 1)
        for r in range(group):
            _odd_chunk(bufs[g % 2].at[r * RET_CHUNK:(r + 1) * RET_CHUNK], g * group + r,
                       cos_ref, sin_ref, dec_ref, cs_ref, kd_ref, sdec_ref, gn_ref,
                       spw_ref, spb_ref, lng_ref, lnb_ref, y_ref, s_c)

    @pl.when(c == pl.num_programs(0) - 1)
    def _():
        s_out_ref[...] = s_c[...]


def _odd_prompt(x3d, w, o, cos, sin, tabs, gn, spw, spb_t, lng, lnb, casts):
    nchunk = SEQ // RET_CHUNK
    dec, cs, kd, sdec = tabs
    cast_in, cast_out, cast_shapes = _cast_specs(casts, nchunk)
    return pl.pallas_call(
        functools.partial(_odd_prompt_kernel, ncast=len(casts)),
        grid=(nchunk,),
        in_specs=[
            pl.BlockSpec((BATCH, RET_CHUNK, D_MODEL), lambda c: (0, c, 0)),
            _const_spec((D_MODEL, ODD_IN)),
            pl.BlockSpec((RET_CHUNK, W_CK), lambda c: (c, 0)),
            pl.BlockSpec((RET_CHUNK, W_CK), lambda c: (c, 0)),
            _const_spec((C_HEADS, RET_CHUNK, RET_CHUNK)),
            _const_spec((RET_CHUNK, W_CV)),
            _const_spec((RET_CHUNK, W_CK)),
            _const_spec((C_HEADS, C_DK, C_DV)),
            _const_spec((1, W_CV), o),
            _const_spec((D_GROUPS, D_CHUNK, D_CHUNK), o),
            _const_spec((D_CHUNK, D_GROUPS), o),
            _const_spec((1, W_D), o),
            _const_spec((1, W_D), o),
        ] + cast_in,
        out_specs=[
            pl.BlockSpec((BATCH, RET_CHUNK, D_MODEL), lambda c: (0, c, 0)),
            pl.BlockSpec((BATCH, C_HEADS, C_DK, C_DV), lambda c: (0, 0, 0, 0)),
        ] + cast_out,
        out_shape=[
            jax.ShapeDtypeStruct((BATCH, SEQ, D_MODEL), BF16),
            jax.ShapeDtypeStruct((BATCH, C_HEADS, C_DK, C_DV), F32),
        ] + cast_shapes,
        scratch_shapes=[
            pltpu.VMEM((ODD_GROUP * RET_CHUNK, ODD_IN), F32),
            pltpu.VMEM((ODD_GROUP * RET_CHUNK, ODD_IN), F32),
            pltpu.VMEM((BATCH, C_HEADS, C_DK, C_DV), F32),
        ],
        compiler_params=_params(("arbitrary",), 52),
        name="odd_prompt",
    )(x3d, w, cos, sin, dec, cs, kd, sdec, gn, spw, spb_t, lng, lnb, *[stack for stack, _ in casts])


def _odd_sample_kernel(proj_ref, s0_ref, cos_ref, sin_ref, dec_ref, cs_ref, kd_ref, sdec_ref, gn_ref,
                       spw_ref, spb_ref, lng_ref, lnb_ref, *rest):
    y_ref, s_out_ref, vd_ref = rest[-3:]
    if len(rest) > 3:
        s_out_ref[:, 0] = rest[0][...]
        s_out_ref = s_out_ref.at[:, 1]
    nb = s0_ref.shape[0]

    def cols(a, b):
        return proj_ref[:, a:b].reshape(nb, DEC_SEQ, b - a)

    def put(a, b, val):
        y_ref[:, a:b] = val.reshape(nb * DEC_SEQ, b - a).astype(BF16)

    q = _rope(cols(0, W_CK), cos_ref[...], sin_ref[...])
    k = _rope(cols(W_CK, 2 * W_CK), cos_ref[...], sin_ref[...]) * (C_DK ** -0.5)
    kd = k * kd_ref[...]
    v0 = 2 * W_CK
    g0 = v0 + W_CV
    for h in range(C_HEADS):
        qh = q[:, :, h * C_DK:(h + 1) * C_DK].astype(BF16)
        kh = k[:, :, h * C_DK:(h + 1) * C_DK].astype(BF16)
        kdh = kd[:, :, h * C_DK:(h + 1) * C_DK].astype(BF16)
        vh = cols(v0 + h * C_DV, v0 + (h + 1) * C_DV).astype(BF16)
        s_prev = s0_ref[:, h]
        scores = jnp.einsum('nld,nmd->nlm', qh, kh, preferred_element_type=F32) * dec_ref[h]
        o = (jnp.einsum('nlm,nme->nle', scores.astype(BF16), vh, preferred_element_type=F32)
             + jnp.einsum('nld,nde->nle', qh, s_prev.astype(BF16), preferred_element_type=F32)
             * cs_ref[:, h * C_DV:(h + 1) * C_DV])
        s_out_ref[:, h] = sdec_ref[h] * s_prev + jnp.einsum('nld,nle->nde', kdh, vh, preferred_element_type=F32)
        gate = jax.nn.silu(cols(g0 + h * C_DV, g0 + (h + 1) * C_DV))
        put(h * C_DV, (h + 1) * C_DV, gate * (_group_norm(o) * gn_ref[:, h * C_DV:(h + 1) * C_DV]))

    u0 = g0 + W_CV
    u = jax.nn.gelu(cols(u0, u0 + W_D))
    vd = _layer_norm(jax.nn.gelu(cols(u0 + W_D, ODD_IN)), lng_ref[...], lnb_ref[...])
    vd_ref[...] = vd
    s = spb_ref[...] + spw_ref[0] * vd[:, 0:1, :]
    for m in range(1, DEC_SEQ):
        s = s + spw_ref[m] * vd[:, m:m + 1, :]
    put(W_CV, D_MODEL, u * s)


def _odd_sample(proj, state_ret, o, cos, sin, tabs, gn, spw_rows, spb_rows, lng, lnb, prev_states):
    nb = SAMPLE_NB
    dec, cs, kd, sdec = tabs
    seq3 = lambda i: (i, 0, 0)
    one_state = pl.BlockSpec((nb, C_HEADS, C_DK, C_DV), lambda i: (i, 0, 0, 0))
    in_specs = [
        pl.BlockSpec((nb * DEC_SEQ, ODD_IN), lambda i: (i, 0)),
        pl.BlockSpec((nb, None, C_HEADS, C_DK, C_DV), lambda i: (i, o, 0, 0, 0)),
        _const_spec((DEC_SEQ, W_CK)),
        _const_spec((DEC_SEQ, W_CK)),
        _const_spec((C_HEADS, DEC_SEQ, DEC_SEQ)),
        _const_spec((DEC_SEQ, W_CV)),
        _const_spec((DEC_SEQ, W_CK)),
        _const_spec((C_HEADS, C_DK, C_DV)),
        _const_spec((1, W_CV), o),
        _const_spec((DEC_SEQ, DEC_SEQ, W_D), o),
        _const_spec((DEC_SEQ, W_D), o),
        _const_spec((1, W_D), o),
        _const_spec((1, W_D), o),
    ]
    args = [proj, state_ret, cos, sin, dec, cs, kd, sdec, gn, spw_rows, spb_rows, lng, lnb]
    if prev_states is None:
        state_spec = one_state
        state_shape = (DEC_BATCH, C_HEADS, C_DK, C_DV)
    else:
        in_specs.append(one_state)
        args.append(prev_states)
        state_spec = pl.BlockSpec((nb, N_ODD, C_HEADS, C_DK, C_DV), lambda i: (i, 0, 0, 0, 0))
        state_shape = (DEC_BATCH, N_ODD, C_HEADS, C_DK, C_DV)
    return pl.pallas_call(
        _odd_sample_kernel,
        grid=(DEC_BATCH // nb,),
        in_specs=in_specs,
        out_specs=[pl.BlockSpec((nb * DEC_SEQ, D_MODEL), lambda i: (i, 0)), state_spec,
                   pl.BlockSpec((nb, DEC_SEQ, W_D), seq3)],
        out_shape=[
            jax.ShapeDtypeStruct((DEC_BATCH * DEC_SEQ, D_MODEL), BF16),
            jax.ShapeDtypeStruct(state_shape, F32),
            jax.ShapeDtypeStruct((DEC_BATCH, DEC_SEQ, W_D), F32),
        ],
        compiler_params=_params(("parallel",), 48),
        name="odd_sample",
    )(*args)


def _rope_tables(pos):
    half = C_DK // 2
    freq = ROPE_BASE ** (-np.arange(half, dtype=np.float64) / half)
    ang = np.asarray(pos, np.float64)[:, None] * freq
    cos, sin = np.cos(ang), np.sin(ang)
    cos_l = np.tile(np.concatenate([cos, cos], axis=-1), (1, C_HEADS))
    sin_l = np.tile(np.concatenate([-sin, sin], axis=-1), (1, C_HEADS))
    return cos_l.astype(np.float32), sin_l.astype(np.float32)


def _retention_tables(length):
    log_g = np.log1p(-np.exp2(-5.0 - np.arange(C_HEADS, dtype=np.float64)))
    idx = np.arange(length, dtype=np.float64)
    rel = idx[:, None] - idx[None, :]
    decay = np.where(rel >= 0, np.exp(log_g[:, None, None] * np.maximum(rel, 0.0)), 0.0)
    cross = np.exp(log_g[None, :] * (idx[:, None] + 1.0))
    kdec = np.exp(log_g[None, :] * (length - 1.0 - idx[:, None]))
    sdec = np.exp(log_g * length)
    tabs = (decay,
            np.repeat(cross, C_DV, axis=1),
            np.repeat(kdec, C_DK, axis=1),
            np.broadcast_to(sdec[:, None, None], (C_HEADS, C_DK, C_DV)))
    return tuple(np.ascontiguousarray(t, dtype=np.float32) for t in tabs)


def _block_diag_halves(wa, wx):
    nb = A_BLOCKS // 2
    width = nb * A_BLOCK

    def bd(w, hh):
        rows = [jnp.pad(w[:, hh * nb + g], ((0, 0), (0, 0), (g * A_BLOCK, width - (g + 1) * A_BLOCK)))
                for g in range(nb)]
        return jnp.concatenate(rows, axis=1)

    halves = [jnp.concatenate([bd(wa, hh), bd(wx, hh)], axis=-1) for hh in range(2)]
    return jnp.stack(halves, axis=1).astype(BF16)


def kernel(x_prompt, x_sample, state_lru_h, state_lru_conv, state_sconv, state_ret, w_in_even, conv_a_w, conv_a_b, lru_wa, lru_ba, lru_wx, lru_bx, lru_lam, conv_b_w, w_out_even, w_in_odd, ret_gn_g, sp_w, sp_b, gm_ln_g, gm_ln_b, w_out_odd, ffn_w1, ffn_w3, ffn_w2, ln1_g, ln1_b, ln2_g, ln2_b):
    rows = lambda v: v.reshape(v.shape[0], 1, v.shape[1])
    xp = x_prompt.reshape(BATCH * SEQ, D_MODEL)
    xs = jnp.swapaxes(x_sample, 0, 1).reshape(DEC_SEQ * DEC_BATCH, D_MODEL)

    cos_p, sin_p = _rope_tables(np.arange(SEQ))
    cos_s, sin_s = _rope_tables(PAST_LEN + np.arange(DEC_SEQ))
    tab_p = _retention_tables(RET_CHUNK)
    tab_s = _retention_tables(DEC_SEQ)

    w_in = w_in_even[0].astype(BF16)
    ln1, ln2 = (rows(ln1_g), rows(ln1_b)), (rows(ln2_g), rows(ln2_b))
    even_p = (conv_a_w, rows(conv_a_b), _block_diag_halves(lru_wa, lru_wx), rows(lru_ba), rows(lru_bx),
              rows(lru_lam), conv_b_w)
    ca_tm = jnp.transpose(state_lru_conv, (1, 2, 0, 3))
    sc_tm = jnp.transpose(state_sconv, (1, 2, 0, 3))
    h0_tm = jnp.swapaxes(state_lru_h, 0, 1)
    tril = jnp.tril(sp_w[:, :, :DEC_SEQ, :DEC_SEQ])
    spw_rows = jnp.repeat(jnp.transpose(tril, (0, 3, 2, 1)), D_GROUP, axis=3)
    spb_rows = jnp.repeat(jnp.swapaxes(sp_b[:, :, :DEC_SEQ], 1, 2), D_GROUP, axis=2)
    spb_t = jnp.swapaxes(sp_b, 1, 2)
    odd_small = (rows(ret_gn_g),)
    odd_ln = (rows(gm_ln_g), rows(gm_ln_b))

    h_p, h_s, ca_p, ca_s, sc_p, sc_s, r_p, v_s = [], [], [], [], [], [], [], []
    ret_sample = None
    for l in range(DEPTH):
        if l % 2 == 0:
            e = l // 2
            casts = [(w_out_even, e), (ffn_w1, l), (ffn_w3, l), (ffn_w2, l), (w_in_odd, e)]
            yp, hlp, cap, scp, wo, w1, w3, w2, w_in_next = _even_prompt(xp.reshape(BATCH, SEQ, D_MODEL), w_in, e,
                                                                        even_p, casts)
            proj_s = _in_proj(xs, w_in).reshape(DEC_SEQ, DEC_BATCH, EVEN_IN)
            ys, hls, cas, scs = _even_sample(proj_s, ca_tm, sc_tm, h0_tm, e, even_p)
            h_p.append(hlp)
            h_s.append(hls)
            ca_p.append(cap[:, SUBLANES - (A_CONV - 1):])
            ca_s.append(jnp.swapaxes(cas, 0, 1))
            sc_p.append(scp[:, SUBLANES - (B_CONV - 1):])
            sc_s.append(jnp.swapaxes(scs, 0, 1))
            yp = yp.reshape(BATCH * SEQ, D_MODEL)
            ys = ys.reshape(DEC_SEQ * DEC_BATCH, D_MODEL)
        else:
            o = l // 2
            casts = [(w_out_odd, o), (ffn_w1, l), (ffn_w3, l), (ffn_w2, l)]
            if l + 1 < DEPTH:
                casts.append((w_in_even, o + 1))
            yp, rp, wo, w1, w3, w2, *w_in_next = _odd_prompt(xp.reshape(BATCH, SEQ, D_MODEL), w_in, o, cos_p, sin_p,
                                                             tab_p, *odd_small, sp_w, spb_t, *odd_ln, casts)
            w_in_next = w_in_next[0] if w_in_next else None
            proj_s = _in_proj(xs, w_in)
            ys, ret_sample, vs = _odd_sample(proj_s, state_ret, o, cos_s, sin_s, tab_s, *odd_small,
                                             spw_rows, spb_rows, *odd_ln, ret_sample)
            r_p.append(rp)
            v_s.append(vs)
            yp = yp.reshape(BATCH * SEQ, D_MODEL)
        xp, xs = _dense_block(yp, xp, ys, xs, l, wo, *ln1, w1, w3, w2, *ln2)
        w_in = w_in_next
        if l + 1 < DEPTH:
            if l % 2 == 0:
                xs = jnp.swapaxes(xs.reshape(DEC_SEQ, DEC_BATCH, D_MODEL), 0, 1).reshape(DEC_BATCH * DEC_SEQ, D_MODEL)
            else:
                xs = jnp.swapaxes(xs.reshape(DEC_BATCH, DEC_SEQ, D_MODEL), 0, 1).reshape(DEC_SEQ * DEC_BATCH, D_MODEL)

    return (xp.reshape(BATCH, SEQ, D_MODEL), xs.reshape(DEC_BATCH, DEC_SEQ, D_MODEL),
            jnp.stack(h_p, axis=1), jnp.stack(h_s, axis=1),
            jnp.stack(ca_p, axis=1), jnp.stack(ca_s, axis=1),
            jnp.stack(sc_p, axis=1), jnp.stack(sc_s, axis=1),
            jnp.stack(r_p, axis=1), ret_sample, jnp.stack(v_s, axis=1))
```

```python
import functools

import jax
import jax.numpy as jnp
import numpy as np
from jax import lax
from jax.experimental import pallas as pl
from jax.experimental.pallas import tpu as pltpu

F32 = jnp.float32
BF16 = jnp.bfloat16

D_MODEL = 1024
BATCH = 8
SEQ = 2048
DEPTH = 4
DEC_BATCH = 128
DEC_SEQ = 4
PAST_LEN = 16384
N_EVEN = (DEPTH + 1) // 2
N_ODD = DEPTH // 2
W_A = 512
A_BLOCKS = 8
A_BLOCK = 64
A_CONV = 4
LRU_C = 8.0
W_B = 512
B_CONV = 3
C_HEADS = 4
C_DK = 64
C_DV = 128
W_CK = C_HEADS * C_DK
W_CV = C_HEADS * C_DV
RET_CHUNK = 128
ROPE_BASE = 10000.0
D_GROUPS = 4
D_CHUNK = 128
W_D = 512
D_GROUP = W_D // D_GROUPS
EVEN_IN = 2 * W_A + 3 * W_B
ODD_IN = 2 * W_CK + 2 * W_CV + 2 * W_D
D_FF = 2816
ALPHA = (2 * DEPTH) ** 0.25
LN_EPS = 1e-5

SUBLANES = 8
LANES = 128
MIB = 1024 * 1024

ROW_TILE = 512
DENSE_TILE = 512
DENSE_GROUP = 256
DENSE_COLS = 256
EVEN_TT = 128
EVEN_GROUP = 1
ODD_GROUP = 2
EVEN_PITCH = EVEN_TT + SUBLANES
SAMPLE_NB = 32


def _params(sem, vmem_mib):
    return pltpu.CompilerParams(dimension_semantics=sem, vmem_limit_bytes=vmem_mib * MIB)


def _const_spec(shape, layer=None):
    nd = len(shape)
    if layer is None:
        return pl.BlockSpec(shape, lambda *_: (0,) * nd, pipeline_mode=pl.Buffered(1))
    return pl.BlockSpec((None,) + tuple(shape), lambda *_: (layer,) + (0,) * nd, pipeline_mode=pl.Buffered(1))


def _layer_rows(layer, *refs):
    return [r.at[layer:layer + 1] for r in refs]


def _layer_norm(x, g, b):
    mu = jnp.mean(x, -1, keepdims=True)
    xc = x - mu
    var = jnp.mean(xc * xc, -1, keepdims=True)
    return xc * lax.rsqrt(var + LN_EPS) * g + b


def _softplus(x):
    return jnp.maximum(x, 0.0) + jnp.log1p(jnp.exp(-jnp.abs(x)))


def _dot(a, b):
    return jnp.dot(a, b, preferred_element_type=F32)


def _cast_specs(stacks_and_layers, nsteps):
    in_specs, out_specs, out_shapes = [], [], []
    for w, layer in stacks_and_layers:
        nrow, ncol = w.shape[1:]
        rb = nrow // nsteps
        assert nrow % nsteps == 0 and rb % (2 * SUBLANES) == 0
        in_specs.append(pl.BlockSpec((None, rb, ncol), lambda j, layer=layer: (layer, j, 0)))
        out_specs.append(pl.BlockSpec((rb, ncol), lambda j: (j, 0)))
        out_shapes.append(jax.ShapeDtypeStruct((nrow, ncol), BF16))
    return in_specs, out_specs, out_shapes


def _cast_slabs(src_refs, dst_refs):
    for src, dst in zip(src_refs, dst_refs):
        dst[...] = src[...].astype(BF16)


def _proj_kernel(x_ref, w_ref, o_ref):
    o_ref[...] = _dot(x_ref[...].astype(BF16), w_ref[...])


def _in_proj(x2d, w):
    m, k = x2d.shape
    n = w.shape[1]
    tm = min(ROW_TILE, m)
    return pl.pallas_call(
        _proj_kernel,
        grid=(m // tm,),
        in_specs=[pl.BlockSpec((tm, k), lambda i: (i, 0)), _const_spec((k, n))],
        out_specs=pl.BlockSpec((tm, n), lambda i: (i, 0)),
        out_shape=jax.ShapeDtypeStruct((m, n), F32),
        compiler_params=_params(("parallel",), 40),
        name="in_proj",
    )(x2d, w)


def _dense_kernel(yp_ref, xp_ref, ys_ref, xs_ref, wo_ref, g1_ref, b1_ref, w1_ref, w3_ref, w2_ref, g2_ref, b2_ref,
                  op_ref, os_ref, h_s, *, layer):
    n_prompt = pl.num_programs(0) - 1
    g1_ref, b1_ref, g2_ref, b2_ref = _layer_rows(layer, g1_ref, b1_ref, g2_ref, b2_ref)
    weights = (wo_ref, g1_ref, b1_ref, w1_ref, w3_ref, w2_ref, g2_ref, b2_ref)

    @pl.when(pl.program_id(0) < n_prompt)
    def _():
        _dense_rows(yp_ref, xp_ref, *weights, op_ref, h_s)

    @pl.when(pl.program_id(0) == n_prompt)
    def _():
        _dense_rows(ys_ref, xs_ref, *weights, os_ref, h_s)


def _dense_rows(y_ref, x_ref, wo_ref, g1_ref, b1_ref, w1_ref, w3_ref, w2_ref, g2_ref, b2_ref, o_ref, h_s):
    rows, ncols = DENSE_GROUP, DENSE_COLS
    nparts = y_ref.shape[0] // rows
    sls = [slice(p * rows, (p + 1) * rows) for p in range(nparts)]
    mix, x1, xb = {}, {}, {}

    def out_proj(p):
        mix[p] = _dot(y_ref[sls[p], :], wo_ref[...])

    def norm1(p):
        x1[p] = _layer_norm(ALPHA * x_ref[sls[p], :] + mix[p], g1_ref[...], b1_ref[...])
        xb[p] = x1[p].astype(BF16)

    def gate_up(p):
        for c in range(0, D_FF, ncols):
            cols = slice(c, c + ncols)
            h_s[sls[p], cols] = (jax.nn.silu(_dot(xb[p], w1_ref[:, cols]))
                                 * _dot(xb[p], w3_ref[:, cols])).astype(BF16)

    def down_norm2(p):
        f = _dot(h_s[sls[p], :], w2_ref[...])
        o_ref[sls[p], :] = _layer_norm(ALPHA * x1[p] + f, g2_ref[...], b2_ref[...])

    out_proj(0)
    for p in range(nparts):
        norm1(p)
        if p + 1 < nparts:
            out_proj(p + 1)
    for p in range(nparts):
        gate_up(p)
    for p in range(nparts):
        down_norm2(p)


def _dense_block(yp, xp, ys, xs, layer, wo, g1, b1, w1, w3, w2, g2, b2):
    tm = DENSE_TILE
    n_prompt = xp.shape[0] // tm
    assert xp.shape[0] % tm == 0 and xs.shape[0] == tm and tm % DENSE_GROUP == 0
    prompt_row = lambda i: (jnp.minimum(i, n_prompt - 1), 0)
    sample_row = lambda i: (0, 0)
    return pl.pallas_call(
        functools.partial(_dense_kernel, layer=layer),
        grid=(n_prompt + 1,),
        in_specs=[
            pl.BlockSpec((tm, D_MODEL), prompt_row),
            pl.BlockSpec((tm, D_MODEL), prompt_row),
            pl.BlockSpec((tm, D_MODEL), sample_row),
            pl.BlockSpec((tm, D_MODEL), sample_row),
            _const_spec((D_MODEL, D_MODEL)),
            _const_spec((DEPTH, D_MODEL)),
            _const_spec((DEPTH, D_MODEL)),
            _const_spec((D_MODEL, D_FF)),
            _const_spec((D_MODEL, D_FF)),
            _const_spec((D_FF, D_MODEL)),
            _const_spec((DEPTH, D_MODEL)),
            _const_spec((DEPTH, D_MODEL)),
        ],
        out_specs=[pl.BlockSpec((tm, D_MODEL), prompt_row), pl.BlockSpec((tm, D_MODEL), sample_row)],
        out_shape=[jax.ShapeDtypeStruct(xp.shape, F32), jax.ShapeDtypeStruct(xs.shape, F32)],
        scratch_shapes=[pltpu.VMEM((tm, D_FF), BF16)],
        compiler_params=_params(("arbitrary",), 58),
        name="dense_block",
    )(yp, xp, ys, xs, wo, g1, b1, w1, w3, w2, g2, b2)


def _lru_gates(xc, wblk_ref, ba, bx, sp):
    half = W_A // 2
    xb = xc.astype(BF16)
    pre0 = _dot(xb[:, :half], wblk_ref[0])
    pre1 = _dot(xb[:, half:], wblk_ref[1])
    r = jax.nn.sigmoid(jnp.concatenate([pre0[:, :half], pre1[:, :half]], axis=1) + ba)
    i = jax.nn.sigmoid(jnp.concatenate([pre0[:, half:], pre1[:, half:]], axis=1) + bx)
    log_a = (-LRU_C) * r * sp
    a = jnp.exp(log_a)
    mult = jnp.sqrt((1.0 + a * a) * jnp.tanh(-log_a))
    return a, mult, i


def _causal_taps(x_slabs, ext, b, w_ref, nrows):
    ktaps = w_ref.shape[0]
    out = []
    for s, x in enumerate(x_slabs):
        lanes = slice(s * LANES, (s + 1) * LANES)
        ext[s, b, SUBLANES:SUBLANES + nrows, :] = x
        acc = w_ref[ktaps - 1:ktaps, lanes] * x
        for k in range(ktaps - 1):
            off = SUBLANES - (ktaps - 1) + k
            acc = acc + w_ref[k:k + 1, lanes] * ext[s, b, off:off + nrows, :]
        out.append(acc)
    return jnp.concatenate(out, axis=1)


def _even_prompt_kernel(x_ref, w_ref, caw_ref, cab_ref, wblk_ref, ba_ref, bx_ref, lam_ref, cbw_ref, *rest,
                        ncast, layer):
    cab_ref, ba_ref, bx_ref, lam_ref = _layer_rows(layer, cab_ref, ba_ref, bx_ref, lam_ref)
    cast_src, rest = rest[:ncast], rest[ncast:]
    y_ref, hlast_ref, ca_out_ref, sc_out_ref = rest[:4]
    cast_dst, rest = rest[4:4 + ncast], rest[4 + ncast:]
    p0, p1, xa_ext, cb_ext, a_s, b_s, g_s, h_c = rest
    tt, pitch = EVEN_TT, EVEN_PITCH
    nslab = W_A // LANES
    j = pl.program_id(0)
    _cast_slabs(cast_src, cast_dst)
    bufs = (p0, p1)
    group = p0.shape[0] // tt

    def project(g):
        xg = x_ref[g * group:(g + 1) * group].reshape(group * tt, D_MODEL)
        bufs[g % 2][...] = _dot(xg.astype(BF16), w_ref[...])

    @pl.when(j == 0)
    def _():
        xa_ext[:, :, 0:SUBLANES, :] = jnp.zeros((nslab, BATCH, SUBLANES, LANES), F32)
        cb_ext[:, :, 0:SUBLANES, :] = jnp.zeros((nslab, BATCH, SUBLANES, LANES), F32)
        h_c[...] = jnp.zeros(h_c.shape, F32)

    @pl.when(j > 0)
    def _():
        xa_ext[:, :, 0:SUBLANES, :] = xa_ext[:, :, tt:tt + SUBLANES, :]
        cb_ext[:, :, 0:SUBLANES, :] = cb_ext[:, :, tt:tt + SUBLANES, :]

    sp = _softplus(-lam_ref[...])
    ba = ba_ref[...]
    bx = bx_ref[...]
    cab = cab_ref[...]
    first_row = (lax.broadcasted_iota(jnp.int32, (SUBLANES, 1), 0) == 0) & (j == 0)
    cg0 = 2 * W_A + W_B
    xb0 = 2 * W_A + 2 * W_B

    def stage(b):
        p_ref = bufs[(b // group) % 2].at[(b % group) * tt:(b % group + 1) * tt]
        xc = cab + _causal_taps([p_ref[:, s * LANES:(s + 1) * LANES] for s in range(nslab)], xa_ext, b, caw_ref, tt)
        a, mult, gate = _lru_gates(xc, wblk_ref, ba, bx, sp)
        mult = jnp.concatenate([jnp.where(first_row, 1.0, mult[:SUBLANES]), mult[SUBLANES:]], axis=0)
        bb = mult * (gate * xc)
        for s in range(nslab):
            a_s[s, b * pitch:b * pitch + tt, :] = a[:, s * LANES:(s + 1) * LANES]
            b_s[s, b * pitch:b * pitch + tt, :] = bb[:, s * LANES:(s + 1) * LANES]
        g_s[b] = jax.nn.gelu(p_ref[:, W_A:2 * W_A])
        cb = [p_ref[:, cg0 + s * LANES:cg0 + (s + 1) * LANES] * p_ref[:, xb0 + s * LANES:xb0 + (s + 1) * LANES]
              for s in range(nslab)]
        zb = _causal_taps(cb, cb_ext, b, cbw_ref, tt)
        y_ref[b, :, W_A:] = (p_ref[:, 2 * W_A:2 * W_A + W_B] * zb).astype(BF16)

    project(0)
    for g in range(BATCH // group):
        if g + 1 < BATCH // group:
            project(g + 1)
        for b in range(g * group, (g + 1) * group):
            stage(b)

    def step(t, hs):
        out = []
        for s in range(nslab):
            rows = pl.ds(t, BATCH, stride=pitch)
            hn = a_s[s, rows, :] * hs[s] + b_s[s, rows, :]
            b_s[s, rows, :] = hn
            out.append(hn)
        return tuple(out)

    hs = lax.fori_loop(0, tt, step, tuple(h_c[s] for s in range(nslab)), unroll=4)
    for s in range(nslab):
        h_c[s] = hs[s]

    def emit(b, carry):
        row0 = pl.multiple_of(b * pitch, SUBLANES)
        h = jnp.concatenate([b_s[s, pl.ds(row0, tt), :] for s in range(nslab)], axis=1)
        y_ref[b, :, 0:W_A] = (h * g_s[b]).astype(BF16)
        return carry

    lax.fori_loop(0, BATCH, emit, 0)

    @pl.when(j == pl.num_programs(0) - 1)
    def _():
        hlast_ref[...] = jnp.concatenate(list(hs), axis=1)
        for s in range(nslab):
            ca_out_ref[:, :, s * LANES:(s + 1) * LANES] = xa_ext[s, :, tt:tt + SUBLANES, :]
            sc_out_ref[:, :, s * LANES:(s + 1) * LANES] = cb_ext[s, :, tt:tt + SUBLANES, :]


def _even_param_specs(e):
    rows = _const_spec((N_EVEN, W_A))
    return [
        _const_spec((A_CONV, W_A), e),
        rows,
        _const_spec((2, W_A // 2, W_A), e),
        rows,
        rows,
        rows,
        _const_spec((B_CONV, W_B), e),
    ]


def _even_prompt(x3d, w, e, params, casts):
    tt, pitch = EVEN_TT, EVEN_PITCH
    nslab = W_A // LANES
    nblock = SEQ // tt
    assert W_A == W_B
    cast_in, cast_out, cast_shapes = _cast_specs(casts, nblock)
    return pl.pallas_call(
        functools.partial(_even_prompt_kernel, ncast=len(casts), layer=e),
        grid=(nblock,),
        in_specs=[pl.BlockSpec((BATCH, tt, D_MODEL), lambda j: (0, j, 0)),
                  _const_spec((D_MODEL, EVEN_IN))] + _even_param_specs(e) + cast_in,
        out_specs=[
            pl.BlockSpec((BATCH, tt, D_MODEL), lambda j: (0, j, 0)),
            pl.BlockSpec((BATCH, W_A), lambda j: (0, 0)),
            pl.BlockSpec((BATCH, SUBLANES, W_A), lambda j: (0, 0, 0)),
            pl.BlockSpec((BATCH, SUBLANES, W_B), lambda j: (0, 0, 0)),
        ] + cast_out,
        out_shape=[
            jax.ShapeDtypeStruct((BATCH, SEQ, D_MODEL), BF16),
            jax.ShapeDtypeStruct((BATCH, W_A), F32),
            jax.ShapeDtypeStruct((BATCH, SUBLANES, W_A), F32),
            jax.ShapeDtypeStruct((BATCH, SUBLANES, W_B), F32),
        ] + cast_shapes,
        scratch_shapes=[
            pltpu.VMEM((EVEN_GROUP * tt, EVEN_IN), F32),
            pltpu.VMEM((EVEN_GROUP * tt, EVEN_IN), F32),
            pltpu.VMEM((nslab, BATCH, tt + SUBLANES, LANES), F32),
            pltpu.VMEM((nslab, BATCH, tt + SUBLANES, LANES), F32),
            pltpu.VMEM((nslab, BATCH * pitch, LANES), F32),
            pltpu.VMEM((nslab, BATCH * pitch, LANES), F32),
            pltpu.VMEM((BATCH, tt, W_A), F32),
            pltpu.VMEM((nslab, BATCH, LANES), F32),
        ],
        compiler_params=_params(("arbitrary",), 52),
        name="even_prompt",
    )(x3d, w, *params, *[stack for stack, _ in casts])


def _even_sample_kernel(proj_ref, ca_ref, sc_ref, h0_ref, caw_ref, cab_ref, wblk_ref, ba_ref, bx_ref,
                        lam_ref, cbw_ref, y_ref, hlast_ref, ca_out_ref, sc_out_ref, *, layer):
    cab_ref, ba_ref, bx_ref, lam_ref = _layer_rows(layer, cab_ref, ba_ref, bx_ref, lam_ref)
    sp = _softplus(-lam_ref[...])
    rows_a = [ca_ref[k] for k in range(A_CONV - 1)] + [proj_ref[l, :, 0:W_A] for l in range(DEC_SEQ)]
    xc = []
    for l in range(DEC_SEQ):
        acc = cab_ref[...] + caw_ref[0:1, :] * rows_a[l]
        for k in range(1, A_CONV):
            acc = acc + caw_ref[k:k + 1, :] * rows_a[l + k]
        xc.append(acc)
    xc_all = jnp.concatenate(xc, axis=0)
    a, mult, gate = _lru_gates(xc_all, wblk_ref, ba_ref[...], bx_ref[...], sp)
    bb = mult * (gate * xc_all)
    h = h0_ref[...]
    rows_b = [sc_ref[k] for k in range(B_CONV - 1)]
    for l in range(DEC_SEQ):
        sl = slice(l * DEC_BATCH, (l + 1) * DEC_BATCH)
        h = a[sl] * h + bb[sl]
        y_ref[l, :, 0:W_A] = (h * jax.nn.gelu(proj_ref[l, :, W_A:2 * W_A])).astype(BF16)
        rows_b.append(proj_ref[l, :, 2 * W_A + W_B:2 * W_A + 2 * W_B] * proj_ref[l, :, 2 * W_A + 2 * W_B:])
    hlast_ref[...] = h
    for l in range(DEC_SEQ):
        zb = cbw_ref[0:1, :] * rows_b[l]
        for k in range(1, B_CONV):
            zb = zb + cbw_ref[k:k + 1, :] * rows_b[l + k]
        y_ref[l, :, W_A:] = (proj_ref[l, :, 2 * W_A:2 * W_A + W_B] * zb).astype(BF16)
    for k in range(A_CONV - 1):
        ca_out_ref[k] = rows_a[DEC_SEQ + k]
    for k in range(B_CONV - 1):
        sc_out_ref[k] = rows_b[DEC_SEQ + k]


def _even_sample(proj_tm, ca_tm, sc_tm, h0_tm, e, params):
    whole = lambda shape: pl.BlockSpec(shape, lambda i: (0,) * len(shape))
    return pl.pallas_call(
        functools.partial(_even_sample_kernel, layer=e),
        grid=(1,),
        in_specs=[
            whole((DEC_SEQ, DEC_BATCH, EVEN_IN)),
            _const_spec((A_CONV - 1, DEC_BATCH, W_A), e),
            _const_spec((B_CONV - 1, DEC_BATCH, W_B), e),
            _const_spec((DEC_BATCH, W_A), e),
        ] + _even_param_specs(e),
        out_specs=[
            whole((DEC_SEQ, DEC_BATCH, D_MODEL)),
            whole((DEC_BATCH, W_A)),
            whole((A_CONV - 1, DEC_BATCH, W_A)),
            whole((B_CONV - 1, DEC_BATCH, W_B)),
        ],
        out_shape=[
            jax.ShapeDtypeStruct((DEC_SEQ, DEC_BATCH, D_MODEL), BF16),
            jax.ShapeDtypeStruct((DEC_BATCH, W_A), F32),
            jax.ShapeDtypeStruct((A_CONV - 1, DEC_BATCH, W_A), F32),
            jax.ShapeDtypeStruct((B_CONV - 1, DEC_BATCH, W_B), F32),
        ],
        compiler_params=_params(("arbitrary",), 40),
        name="even_sample",
    )(proj_tm, ca_tm, sc_tm, h0_tm, *params)


def _rope(x, cos, sin_signed):
    half = C_DK // 2
    ax = x.ndim - 1
    pieces = []
    for p in range(W_CK // LANES):
        xv = x[..., p * LANES:(p + 1) * LANES]
        lane = lax.broadcasted_iota(jnp.int32, xv.shape, ax)
        partner = jnp.where((lane % C_DK) < half, pltpu.roll(xv, LANES - half, axis=ax), pltpu.roll(xv, half, axis=ax))
        pieces.append(xv * cos[..., p * LANES:(p + 1) * LANES] + partner * sin_signed[..., p * LANES:(p + 1) * LANES])
    return jnp.concatenate(pieces, axis=ax)


def _group_norm(o):
    mu = jnp.mean(o, -1, keepdims=True)
    oc = o - mu
    var = jnp.mean(oc * oc, -1, keepdims=True)
    return oc * lax.rsqrt(var + LN_EPS)


def _odd_chunk(p_ref, b, cos_ref, sin_ref, dec_ref, cs_ref, kd_ref, sdec_ref, gn_ref,
               spw_ref, spb_ref, lng_ref, lnb_ref, y_ref, s_c):
    q = _rope(p_ref[:, 0:W_CK], cos_ref[...], sin_ref[...])
    k = _rope(p_ref[:, W_CK:2 * W_CK], cos_ref[...], sin_ref[...]) * (C_DK ** -0.5)
    kd = k * kd_ref[...]
    v0 = 2 * W_CK
    g0 = v0 + W_CV
    for h in range(C_HEADS):
        qh = q[:, h * C_DK:(h + 1) * C_DK].astype(BF16)
        kh = k[:, h * C_DK:(h + 1) * C_DK].astype(BF16)
        kdh = kd[:, h * C_DK:(h + 1) * C_DK].astype(BF16)
        vh = p_ref[:, v0 + h * C_DV:v0 + (h + 1) * C_DV].astype(BF16)
        s_prev = s_c[b, h]
        scores = lax.dot_general(qh, kh, (((1,), (1,)), ((), ())), preferred_element_type=F32) * dec_ref[h]
        o = _dot(scores.astype(BF16), vh) + _dot(qh, s_prev.astype(BF16)) * cs_ref[:, h * C_DV:(h + 1) * C_DV]
        s_c[b, h] = sdec_ref[h] * s_prev + lax.dot_general(kdh, vh, (((0,), (0,)), ((), ())), preferred_element_type=F32)
        gate = jax.nn.silu(p_ref[:, g0 + h * C_DV:g0 + (h + 1) * C_DV])
        y_ref[b, :, h * C_DV:(h + 1) * C_DV] = (gate * (_group_norm(o) * gn_ref[:, h * C_DV:(h + 1) * C_DV])).astype(BF16)

    u0 = g0 + W_CV
    u = jax.nn.gelu(p_ref[:, u0:u0 + W_D])
    vd = _layer_norm(jax.nn.gelu(p_ref[:, u0 + W_D:]), lng_ref[...], lnb_ref[...]).astype(BF16)
    ri = lax.broadcasted_iota(jnp.int32, (D_CHUNK, D_CHUNK), 0)
    ci = lax.broadcasted_iota(jnp.int32, (D_CHUNK, D_CHUNK), 1)
    for gi in range(D_GROUPS):
        w = jnp.where(ri >= ci, spw_ref[gi], 0.0).astype(BF16)
        s = _dot(w, vd[:, gi * D_GROUP:(gi + 1) * D_GROUP]) + spb_ref[:, gi:gi + 1]
        y_ref[b, :, W_CV + gi * D_GROUP:W_CV + (gi + 1) * D_GROUP] = (u[:, gi * D_GROUP:(gi + 1) * D_GROUP] * s).astype(BF16)


def _odd_prompt_kernel(x_ref, w_ref, cos_ref, sin_ref, dec_ref, cs_ref, kd_ref, sdec_ref, gn_ref,
                       spw_ref, spb_ref, lng_ref, lnb_ref, *rest, ncast, layer):
    gn_ref, lng_ref, lnb_ref = _layer_rows(layer, gn_ref, lng_ref, lnb_ref)
    cast_src, rest = rest[:ncast], rest[ncast:]
    y_ref, s_out_ref = rest[:2]
    cast_dst, rest = rest[2:2 + ncast], rest[2 + ncast:]
    p0, p1, s_c = rest
    c = pl.program_id(0)
    _cast_slabs(cast_src, cast_dst)

    @pl.when(c == 0)
    def _():
        s_c[...] = jnp.zeros(s_c.shape, F32)

    bufs = (p0, p1)
    group = p0.shape[0] // RET_CHUNK

    def project(g):
        xg = x_ref[g * group:(g + 1) * group].reshape(group * RET_CHUNK, D_MODEL)
        bufs[g % 2][...] = _dot(xg.astype(BF16), w_ref[...])

    project(0)
    for g in range(BATCH // group):
        if g + 1 < BATCH // group:
            project(g + 1)
        for r in range(group):
            _odd_chunk(bufs[g % 2].at[r * RET_CHUNK:(r + 1) * RET_CHUNK], g * group + r,
                       cos_ref, sin_ref, dec_ref, cs_ref, kd_ref, sdec_ref, gn_ref,
                       spw_ref, spb_ref, lng_ref, lnb_ref, y_ref, s_c)

    @pl.when(c == pl.num_programs(0) - 1)
    def _():
        s_out_ref[...] = s_c[...]


def _odd_prompt(x3d, w, o, cos, sin, tabs, gn, spw, spb_t, lng, lnb, casts):
    nchunk = SEQ // RET_CHUNK
    dec, cs, kd, sdec = tabs
    cast_in, cast_out, cast_shapes = _cast_specs(casts, nchunk)
    return pl.pallas_call(
        functools.partial(_odd_prompt_kernel, ncast=len(casts), layer=o),
        grid=(nchunk,),
        in_specs=[
            pl.BlockSpec((BATCH, RET_CHUNK, D_MODEL), lambda c: (0, c, 0)),
            _const_spec((D_MODEL, ODD_IN)),
            pl.BlockSpec((RET_CHUNK, W_CK), lambda c: (c, 0)),
            pl.BlockSpec((RET_CHUNK, W_CK), lambda c: (c, 0)),
            _const_spec((C_HEADS, RET_CHUNK, RET_CHUNK)),
            _const_spec((RET_CHUNK, W_CV)),
            _const_spec((RET_CHUNK, W_CK)),
            _const_spec((C_HEADS, C_DK, C_DV)),
            _const_spec((N_ODD, W_CV)),
            _const_spec((D_GROUPS, D_CHUNK, D_CHUNK), o),
            _const_spec((D_CHUNK, D_GROUPS), o),
            _const_spec((N_ODD, W_D)),
            _const_spec((N_ODD, W_D)),
        ] + cast_in,
        out_specs=[
            pl.BlockSpec((BATCH, RET_CHUNK, D_MODEL), lambda c: (0, c, 0)),
            pl.BlockSpec((BATCH, C_HEADS, C_DK, C_DV), lambda c: (0, 0, 0, 0)),
        ] + cast_out,
        out_shape=[
            jax.ShapeDtypeStruct((BATCH, SEQ, D_MODEL), BF16),
            jax.ShapeDtypeStruct((BATCH, C_HEADS, C_DK, C_DV), F32),
        ] + cast_shapes,
        scratch_shapes=[
            pltpu.VMEM((ODD_GROUP * RET_CHUNK, ODD_IN), F32),
            pltpu.VMEM((ODD_GROUP * RET_CHUNK, ODD_IN), F32),
            pltpu.VMEM((BATCH, C_HEADS, C_DK, C_DV), F32),
        ],
        compiler_params=_params(("arbitrary",), 52),
        name="odd_prompt",
    )(x3d, w, cos, sin, dec, cs, kd, sdec, gn, spw, spb_t, lng, lnb, *[stack for stack, _ in casts])


def _odd_sample_kernel(proj_ref, s0_ref, cos_ref, sin_ref, dec_ref, cs_ref, kd_ref, sdec_ref, gn_ref,
                       spw_ref, spb_ref, lng_ref, lnb_ref, *rest, layer):
    gn_ref, lng_ref, lnb_ref = _layer_rows(layer, gn_ref, lng_ref, lnb_ref)
    y_ref, s_out_ref, vd_ref = rest[-3:]
    if len(rest) > 3:
        s_out_ref[:, 0] = rest[0][...]
        s_out_ref = s_out_ref.at[:, 1]
    nb = s0_ref.shape[0]

    def cols(a, b):
        return proj_ref[:, a:b].reshape(nb, DEC_SEQ, b - a)

    def put(a, b, val):
        y_ref[:, a:b] = val.reshape(nb * DEC_SEQ, b - a).astype(BF16)

    q = _rope(cols(0, W_CK), cos_ref[...], sin_ref[...])
    k = _rope(cols(W_CK, 2 * W_CK), cos_ref[...], sin_ref[...]) * (C_DK ** -0.5)
    kd = k * kd_ref[...]
    v0 = 2 * W_CK
    g0 = v0 + W_CV
    for h in range(C_HEADS):
        qh = q[:, :, h * C_DK:(h + 1) * C_DK].astype(BF16)
        kh = k[:, :, h * C_DK:(h + 1) * C_DK].astype(BF16)
        kdh = kd[:, :, h * C_DK:(h + 1) * C_DK].astype(BF16)
        vh = cols(v0 + h * C_DV, v0 + (h + 1) * C_DV).astype(BF16)
        s_prev = s0_ref[:, h]
        scores = jnp.einsum('nld,nmd->nlm', qh, kh, preferred_element_type=F32) * dec_ref[h]
        o = (jnp.einsum('nlm,nme->nle', scores.astype(BF16), vh, preferred_element_type=F32)
             + jnp.einsum('nld,nde->nle', qh, s_prev.astype(BF16), preferred_element_type=F32)
             * cs_ref[:, h * C_DV:(h + 1) * C_DV])
        s_out_ref[:, h] = sdec_ref[h] * s_prev + jnp.einsum('nld,nle->nde', kdh, vh, preferred_element_type=F32)
        gate = jax.nn.silu(cols(g0 + h * C_DV, g0 + (h + 1) * C_DV))
        put(h * C_DV, (h + 1) * C_DV, gate * (_group_norm(o) * gn_ref[:, h * C_DV:(h + 1) * C_DV]))

    u0 = g0 + W_CV
    u = jax.nn.gelu(cols(u0, u0 + W_D))
    vd = _layer_norm(jax.nn.gelu(cols(u0 + W_D, ODD_IN)), lng_ref[...], lnb_ref[...])
    vd_ref[...] = vd
    s = spb_ref[...] + spw_ref[0] * vd[:, 0:1, :]
    for m in range(1, DEC_SEQ):
        s = s + spw_ref[m] * vd[:, m:m + 1, :]
    put(W_CV, D_MODEL, u * s)


def _odd_sample(proj, state_ret, o, cos, sin, tabs, gn, spw_rows, spb_rows, lng, lnb, prev_states):
    nb = SAMPLE_NB
    dec, cs, kd, sdec = tabs
    seq3 = lambda i: (i, 0, 0)
    one_state = pl.BlockSpec((nb, C_HEADS, C_DK, C_DV), lambda i: (i, 0, 0, 0))
    in_specs = [
        pl.BlockSpec((nb * DEC_SEQ, ODD_IN), lambda i: (i, 0)),
        pl.BlockSpec((nb, None, C_HEADS, C_DK, C_DV), lambda i: (i, o, 0, 0, 0)),
        _const_spec((DEC_SEQ, W_CK)),
        _const_spec((DEC_SEQ, W_CK)),
        _const_spec((C_HEADS, DEC_SEQ, DEC_SEQ)),
        _const_spec((DEC_SEQ, W_CV)),
        _const_spec((DEC_SEQ, W_CK)),
        _const_spec((C_HEADS, C_DK, C_DV)),
        _const_spec((N_ODD, W_CV)),
        _const_spec((DEC_SEQ, DEC_SEQ, W_D), o),
        _const_spec((DEC_SEQ, W_D), o),
        _const_spec((N_ODD, W_D)),
        _const_spec((N_ODD, W_D)),
    ]
    args = [proj, state_ret, cos, sin, dec, cs, kd, sdec, gn, spw_rows, spb_rows, lng, lnb]
    if prev_states is None:
        state_spec = one_state
        state_shape = (DEC_BATCH, C_HEADS, C_DK, C_DV)
    else:
        in_specs.append(one_state)
        args.append(prev_states)
        state_spec = pl.BlockSpec((nb, N_ODD, C_HEADS, C_DK, C_DV), lambda i: (i, 0, 0, 0, 0))
        state_shape = (DEC_BATCH, N_ODD, C_HEADS, C_DK, C_DV)
    return pl.pallas_call(
        functools.partial(_odd_sample_kernel, layer=o),
        grid=(DEC_BATCH // nb,),
        in_specs=in_specs,
        out_specs=[pl.BlockSpec((nb * DEC_SEQ, D_MODEL), lambda i: (i, 0)), state_spec,
                   pl.BlockSpec((nb, DEC_SEQ, W_D), seq3)],
        out_shape=[
            jax.ShapeDtypeStruct((DEC_BATCH * DEC_SEQ, D_MODEL), BF16),
            jax.ShapeDtypeStruct(state_shape, F32),
            jax.ShapeDtypeStruct((DEC_BATCH, DEC_SEQ, W_D), F32),
        ],
        compiler_params=_params(("parallel",), 48),
        name="odd_sample",
    )(*args)


def _rope_tables(pos):
    half = C_DK // 2
    freq = ROPE_BASE ** (-np.arange(half, dtype=np.float64) / half)
    ang = np.asarray(pos, np.float64)[:, None] * freq
    cos, sin = np.cos(ang), np.sin(ang)
    cos_l = np.tile(np.concatenate([cos, cos], axis=-1), (1, C_HEADS))
    sin_l = np.tile(np.concatenate([-sin, sin], axis=-1), (1, C_HEADS))
    return cos_l.astype(np.float32), sin_l.astype(np.float32)


def _retention_tables(length):
    log_g = np.log1p(-np.exp2(-5.0 - np.arange(C_HEADS, dtype=np.float64)))
    idx = np.arange(length, dtype=np.float64)
    rel = idx[:, None] - idx[None, :]
    decay = np.where(rel >= 0, np.exp(log_g[:, None, None] * np.maximum(rel, 0.0)), 0.0)
    cross = np.exp(log_g[None, :] * (idx[:, None] + 1.0))
    kdec = np.exp(log_g[None, :] * (length - 1.0 - idx[:, None]))
    sdec = np.exp(log_g * length)
    tabs = (decay,
            np.repeat(cross, C_DV, axis=1),
            np.repeat(kdec, C_DK, axis=1),
            np.broadcast_to(sdec[:, None, None], (C_HEADS, C_DK, C_DV)))
    return tuple(np.ascontiguousarray(t, dtype=np.float32) for t in tabs)


def _block_diag_halves(wa, wx):
    nb = A_BLOCKS // 2
    width = nb * A_BLOCK

    def bd(w, hh):
        rows = [jnp.pad(w[:, hh * nb + g], ((0, 0), (0, 0), (g * A_BLOCK, width - (g + 1) * A_BLOCK)))
                for g in range(nb)]
        return jnp.concatenate(rows, axis=1)

    halves = [jnp.concatenate([bd(wa, hh), bd(wx, hh)], axis=-1) for hh in range(2)]
    return jnp.stack(halves, axis=1).astype(BF16)


def kernel(x_prompt, x_sample, state_lru_h, state_lru_conv, state_sconv, state_ret, w_in_even, conv_a_w, conv_a_b, lru_wa, lru_ba, lru_wx, lru_bx, lru_lam, conv_b_w, w_out_even, w_in_odd, ret_gn_g, sp_w, sp_b, gm_ln_g, gm_ln_b, w_out_odd, ffn_w1, ffn_w3, ffn_w2, ln1_g, ln1_b, ln2_g, ln2_b):
    xp = x_prompt.reshape(BATCH * SEQ, D_MODEL)
    xs = jnp.swapaxes(x_sample, 0, 1).reshape(DEC_SEQ * DEC_BATCH, D_MODEL)

    cos_p, sin_p = _rope_tables(np.arange(SEQ))
    cos_s, sin_s = _rope_tables(PAST_LEN + np.arange(DEC_SEQ))
    tab_p = _retention_tables(RET_CHUNK)
    tab_s = _retention_tables(DEC_SEQ)

    w_in = w_in_even[0].astype(BF16)
    ln1, ln2 = (ln1_g, ln1_b), (ln2_g, ln2_b)
    even_p = (conv_a_w, conv_a_b, _block_diag_halves(lru_wa, lru_wx), lru_ba, lru_bx, lru_lam, conv_b_w)
    ca_tm = jnp.transpose(state_lru_conv, (1, 2, 0, 3))
    sc_tm = jnp.transpose(state_sconv, (1, 2, 0, 3))
    h0_tm = jnp.swapaxes(state_lru_h, 0, 1)
    tril = jnp.tril(sp_w[:, :, :DEC_SEQ, :DEC_SEQ])
    spw_rows = jnp.repeat(jnp.transpose(tril, (0, 3, 2, 1)), D_GROUP, axis=3)
    spb_rows = jnp.repeat(jnp.swapaxes(sp_b[:, :, :DEC_SEQ], 1, 2), D_GROUP, axis=2)
    spb_t = jnp.swapaxes(sp_b, 1, 2)
    odd_small = (ret_gn_g,)
    odd_ln = (gm_ln_g, gm_ln_b)

    h_p, h_s, ca_p, ca_s, sc_p, sc_s, r_p, v_s = [], [], [], [], [], [], [], []
    ret_sample = None
    for l in range(DEPTH):
        if l % 2 == 0:
            e = l // 2
            casts = [(w_out_even, e), (ffn_w1, l), (ffn_w3, l), (ffn_w2, l), (w_in_odd, e)]
            yp, hlp, cap, scp, wo, w1, w3, w2, w_in_next = _even_prompt(xp.reshape(BATCH, SEQ, D_MODEL), w_in, e,
                                                                        even_p, casts)
            proj_s = _in_proj(xs, w_in).reshape(DEC_SEQ, DEC_BATCH, EVEN_IN)
            ys, hls, cas, scs = _even_sample(proj_s, ca_tm, sc_tm, h0_tm, e, even_p)
            h_p.append(hlp)
            h_s.append(hls)
            ca_p.append(cap[:, SUBLANES - (A_CONV - 1):])
            ca_s.append(jnp.swapaxes(cas, 0, 1))
            sc_p.append(scp[:, SUBLANES - (B_CONV - 1):])
            sc_s.append(jnp.swapaxes(scs, 0, 1))
            yp = yp.reshape(BATCH * SEQ, D_MODEL)
            ys = ys.reshape(DEC_SEQ * DEC_BATCH, D_MODEL)
        else:
            o = l // 2
            casts = [(w_out_odd, o), (ffn_w1, l), (ffn_w3, l), (ffn_w2, l)]
            if l + 1 < DEPTH:
                casts.append((w_in_even, o + 1))
            yp, rp, wo, w1, w3, w2, *w_in_next = _odd_prompt(xp.reshape(BATCH, SEQ, D_MODEL), w_in, o, cos_p, sin_p,
                                                             tab_p, *odd_small, sp_w, spb_t, *odd_ln, casts)
            w_in_next = w_in_next[0] if w_in_next else None
            proj_s = _in_proj(xs, w_in)
            ys, ret_sample, vs = _odd_sample(proj_s, state_ret, o, cos_s, sin_s, tab_s, *odd_small,
                                             spw_rows, spb_rows, *odd_ln, ret_sample)
            r_p.append(rp)
            v_s.append(vs)
            yp = yp.reshape(BATCH * SEQ, D_MODEL)
        xp, xs = _dense_block(yp, xp, ys, xs, l, wo, *ln1, w1, w3, w2, *ln2)
        w_in = w_in_next
        if l + 1 < DEPTH:
            if l % 2 == 0:
                xs = jnp.swapaxes(xs.reshape(DEC_SEQ, DEC_BATCH, D_MODEL), 0, 1).reshape(DEC_BATCH * DEC_SEQ, D_MODEL)
            else:
                xs = jnp.swapaxes(xs.reshape(DEC_BATCH, DEC_SEQ, D_MODEL), 0, 1).reshape(DEC_SEQ * DEC_BATCH, D_MODEL)

    return (xp.reshape(BATCH, SEQ, D_MODEL), xs.reshape(DEC_BATCH, DEC_SEQ, D_MODEL),
            jnp.stack(h_p, axis=1), jnp.stack(h_s, axis=1),
            jnp.stack(ca_p, axis=1), jnp.stack(ca_s, axis=1),
            jnp.stack(sc_p, axis=1), jnp.stack(sc_s, axis=1),
            jnp.stack(r_p, axis=1), ret_sample, jnp.stack(v_s, axis=1))
```

```python
import functools

import jax
import jax.numpy as jnp
import numpy as np
from jax import lax
from jax.experimental import pallas as pl
from jax.experimental.pallas import tpu as pltpu

F32 = jnp.float32
BF16 = jnp.bfloat16

D_MODEL = 1024
BATCH = 8
SEQ = 2048
DEPTH = 4
DEC_BATCH = 128
DEC_SEQ = 4
PAST_LEN = 16384
N_EVEN = (DEPTH + 1) // 2
N_ODD = DEPTH // 2
W_A = 512
A_BLOCKS = 8
A_BLOCK = 64
A_CONV = 4
LRU_C = 8.0
W_B = 512
B_CONV = 3
C_HEADS = 4
C_DK = 64
C_DV = 128
W_CK = C_HEADS * C_DK
W_CV = C_HEADS * C_DV
RET_CHUNK = 128
ROPE_BASE = 10000.0
D_GROUPS = 4
D_CHUNK = 128
W_D = 512
D_GROUP = W_D // D_GROUPS
EVEN_IN = 2 * W_A + 3 * W_B
ODD_IN = 2 * W_CK + 2 * W_CV + 2 * W_D
D_FF = 2816
ALPHA = (2 * DEPTH) ** 0.25
LN_EPS = 1e-5

SUBLANES = 8
LANES = 128
MIB = 1024 * 1024

ROW_TILE = 512
DENSE_TILE = 512
DENSE_GROUP = 256
DENSE_COLS = 256
EVEN_TT = 128
EVEN_GROUP = 1
ODD_GROUP = 2
EVEN_PITCH = EVEN_TT + SUBLANES
SAMPLE_NB = 32


def _params(sem, vmem_mib):
    return pltpu.CompilerParams(dimension_semantics=sem, vmem_limit_bytes=vmem_mib * MIB)


def _const_spec(shape, layer=None):
    nd = len(shape)
    if layer is None:
        return pl.BlockSpec(shape, lambda *_: (0,) * nd, pipeline_mode=pl.Buffered(1))
    return pl.BlockSpec((None,) + tuple(shape), lambda *_: (layer,) + (0,) * nd, pipeline_mode=pl.Buffered(1))


def _layer_rows(layer, *refs):
    return [r.at[layer:layer + 1] for r in refs]


def _layer_norm(x, g, b):
    mu = jnp.mean(x, -1, keepdims=True)
    xc = x - mu
    var = jnp.mean(xc * xc, -1, keepdims=True)
    return xc * lax.rsqrt(var + LN_EPS) * g + b


def _softplus(x):
    return jnp.maximum(x, 0.0) + jnp.log1p(jnp.exp(-jnp.abs(x)))


def _dot(a, b):
    return jnp.dot(a, b, preferred_element_type=F32)


def _cast_specs(stacks_and_layers, nsteps):
    in_specs, out_specs, out_shapes = [], [], []
    for w, layer in stacks_and_layers:
        nrow, ncol = w.shape[1:]
        rb = nrow // nsteps
        assert nrow % nsteps == 0 and rb % (2 * SUBLANES) == 0
        in_specs.append(pl.BlockSpec((None, rb, ncol), lambda j, layer=layer: (layer, j, 0)))
        out_specs.append(pl.BlockSpec((rb, ncol), lambda j: (j, 0)))
        out_shapes.append(jax.ShapeDtypeStruct((nrow, ncol), BF16))
    return in_specs, out_specs, out_shapes


def _cast_slabs(src_refs, dst_refs):
    for src, dst in zip(src_refs, dst_refs):
        dst[...] = src[...].astype(BF16)


def _proj_kernel(x_ref, w_ref, o_ref):
    o_ref[...] = _dot(x_ref[...].astype(BF16), w_ref[...])


def _in_proj(x2d, w):
    m, k = x2d.shape
    n = w.shape[1]
    tm = min(ROW_TILE, m)
    return pl.pallas_call(
        _proj_kernel,
        grid=(m // tm,),
        in_specs=[pl.BlockSpec((tm, k), lambda i: (i, 0)), _const_spec((k, n))],
        out_specs=pl.BlockSpec((tm, n), lambda i: (i, 0)),
        out_shape=jax.ShapeDtypeStruct((m, n), F32),
        compiler_params=_params(("parallel",), 40),
        name="in_proj",
    )(x2d, w)


def _dense_kernel(yp_ref, xp_ref, ys_ref, xs_ref, wo_ref, g1_ref, b1_ref, w1_ref, w3_ref, w2_ref, g2_ref, b2_ref,
                  op_ref, os_ref, h_s, *, layer):
    n_prompt = pl.num_programs(0) - 1
    g1_ref, b1_ref, g2_ref, b2_ref = _layer_rows(layer, g1_ref, b1_ref, g2_ref, b2_ref)
    weights = (wo_ref, g1_ref, b1_ref, w1_ref, w3_ref, w2_ref, g2_ref, b2_ref)

    @pl.when(pl.program_id(0) < n_prompt)
    def _():
        _dense_rows(yp_ref, xp_ref, *weights, op_ref, h_s)

    @pl.when(pl.program_id(0) == n_prompt)
    def _():
        _dense_rows(ys_ref, xs_ref, *weights, os_ref, h_s)


def _dense_rows(y_ref, x_ref, wo_ref, g1_ref, b1_ref, w1_ref, w3_ref, w2_ref, g2_ref, b2_ref, o_ref, h_s):
    rows, ncols = DENSE_GROUP, DENSE_COLS
    nparts = y_ref.shape[0] // rows
    sls = [slice(p * rows, (p + 1) * rows) for p in range(nparts)]
    mix, x1, xb = {}, {}, {}

    def out_proj(p):
        mix[p] = _dot(y_ref[sls[p], :], wo_ref[...])

    def norm1(p):
        x1[p] = _layer_norm(ALPHA * x_ref[sls[p], :] + mix[p], g1_ref[...], b1_ref[...])
        xb[p] = x1[p].astype(BF16)

    def gate_up(p):
        for c in range(0, D_FF, ncols):
            cols = slice(c, c + ncols)
            h_s[sls[p], cols] = (jax.nn.silu(_dot(xb[p], w1_ref[:, cols]))
                                 * _dot(xb[p], w3_ref[:, cols])).astype(BF16)

    def down_norm2(p):
        f = _dot(h_s[sls[p], :], w2_ref[...])
        o_ref[sls[p], :] = _layer_norm(ALPHA * x1[p] + f, g2_ref[...], b2_ref[...])

    out_proj(0)
    for p in range(nparts):
        norm1(p)
        if p + 1 < nparts:
            out_proj(p + 1)
    for p in range(nparts):
        gate_up(p)
    for p in range(nparts):
        down_norm2(p)


def _dense_block(yp, xp, ys, xs, layer, wo, g1, b1, w1, w3, w2, g2, b2):
    tm = DENSE_TILE
    n_prompt = xp.shape[0] // tm
    assert xp.shape[0] % tm == 0 and xs.shape[0] == tm and tm % DENSE_GROUP == 0
    prompt_row = lambda i: (jnp.minimum(i, n_prompt - 1), 0)
    sample_row = lambda i: (0, 0)
    return pl.pallas_call(
        functools.partial(_dense_kernel, layer=layer),
        grid=(n_prompt + 1,),
        in_specs=[
            pl.BlockSpec((tm, D_MODEL), prompt_row),
            pl.BlockSpec((tm, D_MODEL), prompt_row),
            pl.BlockSpec((tm, D_MODEL), sample_row),
            pl.BlockSpec((tm, D_MODEL), sample_row),
            _const_spec((D_MODEL, D_MODEL)),
            _const_spec((DEPTH, D_MODEL)),
            _const_spec((DEPTH, D_MODEL)),
            _const_spec((D_MODEL, D_FF)),
            _const_spec((D_MODEL, D_FF)),
            _const_spec((D_FF, D_MODEL)),
            _const_spec((DEPTH, D_MODEL)),
            _const_spec((DEPTH, D_MODEL)),
        ],
        out_specs=[pl.BlockSpec((tm, D_MODEL), prompt_row), pl.BlockSpec((tm, D_MODEL), sample_row)],
        out_shape=[jax.ShapeDtypeStruct(xp.shape, F32), jax.ShapeDtypeStruct(xs.shape, F32)],
        scratch_shapes=[pltpu.VMEM((tm, D_FF), BF16)],
        compiler_params=_params(("arbitrary",), 58),
        name="dense_block",
    )(yp, xp, ys, xs, wo, g1, b1, w1, w3, w2, g2, b2)


def _lru_gates(xc, wblk_ref, ba, bx, sp):
    half = W_A // 2
    xb = xc.astype(BF16)
    pre0 = _dot(xb[:, :half], wblk_ref[0])
    pre1 = _dot(xb[:, half:], wblk_ref[1])
    r = jax.nn.sigmoid(jnp.concatenate([pre0[:, :half], pre1[:, :half]], axis=1) + ba)
    i = jax.nn.sigmoid(jnp.concatenate([pre0[:, half:], pre1[:, half:]], axis=1) + bx)
    log_a = (-LRU_C) * r * sp
    a = jnp.exp(log_a)
    mult = jnp.sqrt((1.0 + a * a) * jnp.tanh(-log_a))
    return a, mult, i


def _causal_taps(x_slabs, ext, b, w_ref, nrows):
    ktaps = w_ref.shape[0]
    out = []
    for s, x in enumerate(x_slabs):
        lanes = slice(s * LANES, (s + 1) * LANES)
        ext[s, b, SUBLANES:SUBLANES + nrows, :] = x
        acc = w_ref[ktaps - 1:ktaps, lanes] * x
        for k in range(ktaps - 1):
            off = SUBLANES - (ktaps - 1) + k
            acc = acc + w_ref[k:k + 1, lanes] * ext[s, b, off:off + nrows, :]
        out.append(acc)
    return jnp.concatenate(out, axis=1)


def _even_prompt_kernel(x_ref, w_ref, caw_ref, cab_ref, wblk_ref, ba_ref, bx_ref, lam_ref, cbw_ref, *rest,
                        ncast, layer):
    cab_ref, ba_ref, bx_ref, lam_ref = _layer_rows(layer, cab_ref, ba_ref, bx_ref, lam_ref)
    cast_src, rest = rest[:ncast], rest[ncast:]
    y_ref, hlast_ref, ca_out_ref, sc_out_ref = rest[:4]
    cast_dst, rest = rest[4:4 + ncast], rest[4 + ncast:]
    p0, p1, xa_ext, cb_ext, a_s, b_s, g_s, h_c = rest
    tt, pitch = EVEN_TT, EVEN_PITCH
    nslab = W_A // LANES
    j = pl.program_id(0)
    _cast_slabs(cast_src, cast_dst)
    bufs = (p0, p1)
    group = p0.shape[0] // tt

    def project(g):
        xg = x_ref[g * group:(g + 1) * group].reshape(group * tt, D_MODEL)
        bufs[g % 2][...] = _dot(xg.astype(BF16), w_ref[...])

    @pl.when(j == 0)
    def _():
        xa_ext[:, :, 0:SUBLANES, :] = jnp.zeros((nslab, BATCH, SUBLANES, LANES), F32)
        cb_ext[:, :, 0:SUBLANES, :] = jnp.zeros((nslab, BATCH, SUBLANES, LANES), F32)
        h_c[...] = jnp.zeros(h_c.shape, F32)

    @pl.when(j > 0)
    def _():
        xa_ext[:, :, 0:SUBLANES, :] = xa_ext[:, :, tt:tt + SUBLANES, :]
        cb_ext[:, :, 0:SUBLANES, :] = cb_ext[:, :, tt:tt + SUBLANES, :]

    sp = _softplus(-lam_ref[...])
    ba = ba_ref[...]
    bx = bx_ref[...]
    cab = cab_ref[...]
    first_row = (lax.broadcasted_iota(jnp.int32, (SUBLANES, 1), 0) == 0) & (j == 0)
    cg0 = 2 * W_A + W_B
    xb0 = 2 * W_A + 2 * W_B

    def stage(b):
        p_ref = bufs[(b // group) % 2].at[(b % group) * tt:(b % group + 1) * tt]
        xc = cab + _causal_taps([p_ref[:, s * LANES:(s + 1) * LANES] for s in range(nslab)], xa_ext, b, caw_ref, tt)
        a, mult, gate = _lru_gates(xc, wblk_ref, ba, bx, sp)
        mult = jnp.concatenate([jnp.where(first_row, 1.0, mult[:SUBLANES]), mult[SUBLANES:]], axis=0)
        bb = mult * (gate * xc)
        for s in range(nslab):
            a_s[s, b * pitch:b * pitch + tt, :] = a[:, s * LANES:(s + 1) * LANES]
            b_s[s, b * pitch:b * pitch + tt, :] = bb[:, s * LANES:(s + 1) * LANES]
        g_s[b] = jax.nn.gelu(p_ref[:, W_A:2 * W_A])
        cb = [p_ref[:, cg0 + s * LANES:cg0 + (s + 1) * LANES] * p_ref[:, xb0 + s * LANES:xb0 + (s + 1) * LANES]
              for s in range(nslab)]
        zb = _causal_taps(cb, cb_ext, b, cbw_ref, tt)
        y_ref[b, :, W_A:] = (p_ref[:, 2 * W_A:2 * W_A + W_B] * zb).astype(BF16)

    project(0)
    for g in range(BATCH // group):
        if g + 1 < BATCH // group:
            project(g + 1)
        for b in range(g * group, (g + 1) * group):
            stage(b)

    def step(t, hs):
        out = []
        for s in range(nslab):
            rows = pl.ds(t, BATCH, stride=pitch)
            hn = a_s[s, rows, :] * hs[s] + b_s[s, rows, :]
            b_s[s, rows, :] = hn
            out.append(hn)
        return tuple(out)

    hs = lax.fori_loop(0, tt, step, tuple(h_c[s] for s in range(nslab)), unroll=4)
    for s in range(nslab):
        h_c[s] = hs[s]

    def emit(b, carry):
        row0 = pl.multiple_of(b * pitch, SUBLANES)
        h = jnp.concatenate([b_s[s, pl.ds(row0, tt), :] for s in range(nslab)], axis=1)
        y_ref[b, :, 0:W_A] = (h * g_s[b]).astype(BF16)
        return carry

    lax.fori_loop(0, BATCH, emit, 0)

    @pl.when(j == pl.num_programs(0) - 1)
    def _():
        hlast_ref[...] = jnp.concatenate(list(hs), axis=1)
        for s in range(nslab):
            ca_out_ref[:, :, s * LANES:(s + 1) * LANES] = xa_ext[s, :, tt:tt + SUBLANES, :]
            sc_out_ref[:, :, s * LANES:(s + 1) * LANES] = cb_ext[s, :, tt:tt + SUBLANES, :]


def _even_param_specs(e):
    rows = _const_spec((N_EVEN, W_A))
    return [
        _const_spec((A_CONV, W_A), e),
        rows,
        _const_spec((2, W_A // 2, W_A), e),
        rows,
        rows,
        rows,
        _const_spec((B_CONV, W_B), e),
    ]


def _even_prompt(x3d, w, e, params, casts):
    tt, pitch = EVEN_TT, EVEN_PITCH
    nslab = W_A // LANES
    nblock = SEQ // tt
    assert W_A == W_B
    cast_in, cast_out, cast_shapes = _cast_specs(casts, nblock)
    return pl.pallas_call(
        functools.partial(_even_prompt_kernel, ncast=len(casts), layer=e),
        grid=(nblock,),
        in_specs=[pl.BlockSpec((BATCH, tt, D_MODEL), lambda j: (0, j, 0)),
                  _const_spec((D_MODEL, EVEN_IN))] + _even_param_specs(e) + cast_in,
        out_specs=[
            pl.BlockSpec((BATCH, tt, D_MODEL), lambda j: (0, j, 0)),
            pl.BlockSpec((BATCH, W_A), lambda j: (0, 0)),
            pl.BlockSpec((BATCH, SUBLANES, W_A), lambda j: (0, 0, 0)),
            pl.BlockSpec((BATCH, SUBLANES, W_B), lambda j: (0, 0, 0)),
        ] + cast_out,
        out_shape=[
            jax.ShapeDtypeStruct((BATCH, SEQ, D_MODEL), BF16),
            jax.ShapeDtypeStruct((BATCH, W_A), F32),
            jax.ShapeDtypeStruct((BATCH, SUBLANES, W_A), F32),
            jax.ShapeDtypeStruct((BATCH, SUBLANES, W_B), F32),
        ] + cast_shapes,
        scratch_shapes=[
            pltpu.VMEM((EVEN_GROUP * tt, EVEN_IN), F32),
            pltpu.VMEM((EVEN_GROUP * tt, EVEN_IN), F32),
            pltpu.VMEM((nslab, BATCH, tt + SUBLANES, LANES), F32),
            pltpu.VMEM((nslab, BATCH, tt + SUBLANES, LANES), F32),
            pltpu.VMEM((nslab, BATCH * pitch, LANES), F32),
            pltpu.VMEM((nslab, BATCH * pitch, LANES), F32),
            pltpu.VMEM((BATCH, tt, W_A), F32),
            pltpu.VMEM((nslab, BATCH, LANES), F32),
        ],
        compiler_params=_params(("arbitrary",), 52),
        name="even_prompt",
    )(x3d, w, *params, *[stack for stack, _ in casts])


def _even_sample_kernel(proj_ref, ca_ref, sc_ref, h0_ref, caw_ref, cab_ref, wblk_ref, ba_ref, bx_ref,
                        lam_ref, cbw_ref, y_ref, hlast_ref, ca_out_ref, sc_out_ref, *, layer):
    cab_ref, ba_ref, bx_ref, lam_ref = _layer_rows(layer, cab_ref, ba_ref, bx_ref, lam_ref)
    sp = _softplus(-lam_ref[...])
    rows_a = [ca_ref[k] for k in range(A_CONV - 1)] + [proj_ref[l, :, 0:W_A] for l in range(DEC_SEQ)]
    xc = []
    for l in range(DEC_SEQ):
        acc = cab_ref[...] + caw_ref[0:1, :] * rows_a[l]
        for k in range(1, A_CONV):
            acc = acc + caw_ref[k:k + 1, :] * rows_a[l + k]
        xc.append(acc)
    xc_all = jnp.concatenate(xc, axis=0)
    a, mult, gate = _lru_gates(xc_all, wblk_ref, ba_ref[...], bx_ref[...], sp)
    bb = mult * (gate * xc_all)
    h = h0_ref[...]
    rows_b = [sc_ref[k] for k in range(B_CONV - 1)]
    for l in range(DEC_SEQ):
        sl = slice(l * DEC_BATCH, (l + 1) * DEC_BATCH)
        h = a[sl] * h + bb[sl]
        y_ref[l, :, 0:W_A] = (h * jax.nn.gelu(proj_ref[l, :, W_A:2 * W_A])).astype(BF16)
        rows_b.append(proj_ref[l, :, 2 * W_A + W_B:2 * W_A + 2 * W_B] * proj_ref[l, :, 2 * W_A + 2 * W_B:])
    hlast_ref[...] = h
    for l in range(DEC_SEQ):
        zb = cbw_ref[0:1, :] * rows_b[l]
        for k in range(1, B_CONV):
            zb = zb + cbw_ref[k:k + 1, :] * rows_b[l + k]
        y_ref[l, :, W_A:] = (proj_ref[l, :, 2 * W_A:2 * W_A + W_B] * zb).astype(BF16)
    for k in range(A_CONV - 1):
        ca_out_ref[k] = rows_a[DEC_SEQ + k]
    for k in range(B_CONV - 1):
        sc_out_ref[k] = rows_b[DEC_SEQ + k]


def _even_sample(proj_tm, ca_tm, sc_tm, h0_tm, e, params):
    whole = lambda shape: pl.BlockSpec(shape, lambda i: (0,) * len(shape))
    return pl.pallas_call(
        functools.partial(_even_sample_kernel, layer=e),
        grid=(1,),
        in_specs=[
            whole((DEC_SEQ, DEC_BATCH, EVEN_IN)),
            _const_spec((A_CONV - 1, DEC_BATCH, W_A), e),
            _const_spec((B_CONV - 1, DEC_BATCH, W_B), e),
            _const_spec((DEC_BATCH, W_A), e),
        ] + _even_param_specs(e),
        out_specs=[
            whole((DEC_SEQ, DEC_BATCH, D_MODEL)),
            whole((DEC_BATCH, W_A)),
            whole((A_CONV - 1, DEC_BATCH, W_A)),
            whole((B_CONV - 1, DEC_BATCH, W_B)),
        ],
        out_shape=[
            jax.ShapeDtypeStruct((DEC_SEQ, DEC_BATCH, D_MODEL), BF16),
            jax.ShapeDtypeStruct((DEC_BATCH, W_A), F32),
            jax.ShapeDtypeStruct((A_CONV - 1, DEC_BATCH, W_A), F32),
            jax.ShapeDtypeStruct((B_CONV - 1, DEC_BATCH, W_B), F32),
        ],
        compiler_params=_params(("arbitrary",), 40),
        name="even_sample",
    )(proj_tm, ca_tm, sc_tm, h0_tm, *params)


def _rope(x, cos, sin_signed):
    half = C_DK // 2
    ax = x.ndim - 1
    pieces = []
    for p in range(W_CK // LANES):
        xv = x[..., p * LANES:(p + 1) * LANES]
        lane = lax.broadcasted_iota(jnp.int32, xv.shape, ax)
        partner = jnp.where((lane % C_DK) < half, pltpu.roll(xv, LANES - half, axis=ax), pltpu.roll(xv, half, axis=ax))
        pieces.append(xv * cos[..., p * LANES:(p + 1) * LANES] + partner * sin_signed[..., p * LANES:(p + 1) * LANES])
    return jnp.concatenate(pieces, axis=ax)


def _group_norm(o):
    mu = jnp.mean(o, -1, keepdims=True)
    oc = o - mu
    var = jnp.mean(oc * oc, -1, keepdims=True)
    return oc * lax.rsqrt(var + LN_EPS)


def _odd_chunk(p_ref, b, cos_ref, sin_ref, dec_ref, cs_ref, kd_ref, sdec_ref, gn_ref,
               spw_ref, spb_ref, lng_ref, lnb_ref, y_ref, s_c, s_bd):
    q = _rope(p_ref[:, 0:W_CK], cos_ref[...], sin_ref[...])
    k = _rope(p_ref[:, W_CK:2 * W_CK], cos_ref[...], sin_ref[...]) * (C_DK ** -0.5)
    kd = k * kd_ref[...]
    v0 = 2 * W_CK
    g0 = v0 + W_CV
    cross = _dot(q.astype(BF16), s_bd[b]) * cs_ref[...]
    for h in range(C_HEADS):
        qh = q[:, h * C_DK:(h + 1) * C_DK].astype(BF16)
        kh = k[:, h * C_DK:(h + 1) * C_DK].astype(BF16)
        kdh = kd[:, h * C_DK:(h + 1) * C_DK].astype(BF16)
        vh = p_ref[:, v0 + h * C_DV:v0 + (h + 1) * C_DV].astype(BF16)
        scores = lax.dot_general(qh, kh, (((1,), (1,)), ((), ())), preferred_element_type=F32) * dec_ref[h]
        o = _dot(scores.astype(BF16), vh) + cross[:, h * C_DV:(h + 1) * C_DV]
        s_new = sdec_ref[h] * s_c[b, h] + lax.dot_general(kdh, vh, (((0,), (0,)), ((), ())), preferred_element_type=F32)
        s_c[b, h] = s_new
        s_bd[b, h * C_DK:(h + 1) * C_DK, h * C_DV:(h + 1) * C_DV] = s_new.astype(BF16)
        gate = jax.nn.silu(p_ref[:, g0 + h * C_DV:g0 + (h + 1) * C_DV])
        y_ref[b, :, h * C_DV:(h + 1) * C_DV] = (gate * (_group_norm(o) * gn_ref[:, h * C_DV:(h + 1) * C_DV])).astype(BF16)

    u0 = g0 + W_CV
    u = jax.nn.gelu(p_ref[:, u0:u0 + W_D])
    vd = _layer_norm(jax.nn.gelu(p_ref[:, u0 + W_D:]), lng_ref[...], lnb_ref[...]).astype(BF16)
    ri = lax.broadcasted_iota(jnp.int32, (D_CHUNK, D_CHUNK), 0)
    ci = lax.broadcasted_iota(jnp.int32, (D_CHUNK, D_CHUNK), 1)
    for gi in range(D_GROUPS):
        w = jnp.where(ri >= ci, spw_ref[gi], 0.0).astype(BF16)
        s = _dot(w, vd[:, gi * D_GROUP:(gi + 1) * D_GROUP]) + spb_ref[:, gi:gi + 1]
        y_ref[b, :, W_CV + gi * D_GROUP:W_CV + (gi + 1) * D_GROUP] = (u[:, gi * D_GROUP:(gi + 1) * D_GROUP] * s).astype(BF16)


def _odd_prompt_kernel(x_ref, w_ref, cos_ref, sin_ref, dec_ref, cs_ref, kd_ref, sdec_ref, gn_ref,
                       spw_ref, spb_ref, lng_ref, lnb_ref, *rest, ncast, layer):
    gn_ref, lng_ref, lnb_ref = _layer_rows(layer, gn_ref, lng_ref, lnb_ref)
    cast_src, rest = rest[:ncast], rest[ncast:]
    y_ref, s_out_ref = rest[:2]
    cast_dst, rest = rest[2:2 + ncast], rest[2 + ncast:]
    p0, p1, s_c, s_bd = rest
    c = pl.program_id(0)
    _cast_slabs(cast_src, cast_dst)

    @pl.when(c == 0)
    def _():
        s_c[...] = jnp.zeros(s_c.shape, F32)
        s_bd[...] = jnp.zeros(s_bd.shape, BF16)

    bufs = (p0, p1)
    group = p0.shape[0] // RET_CHUNK

    def project(g):
        xg = x_ref[g * group:(g + 1) * group].reshape(group * RET_CHUNK, D_MODEL)
        bufs[g % 2][...] = _dot(xg.astype(BF16), w_ref[...])

    project(0)
    for g in range(BATCH // group):
        if g + 1 < BATCH // group:
            project(g + 1)
        for r in range(group):
            _odd_chunk(bufs[g % 2].at[r * RET_CHUNK:(r + 1) * RET_CHUNK], g * group + r,
                       cos_ref, sin_ref, dec_ref, cs_ref, kd_ref, sdec_ref, gn_ref,
                       spw_ref, spb_ref, lng_ref, lnb_ref, y_ref, s_c, s_bd)

    @pl.when(c == pl.num_programs(0) - 1)
    def _():
        s_out_ref[...] = s_c[...]


def _odd_prompt(x3d, w, o, cos, sin, tabs, gn, spw, spb_t, lng, lnb, casts):
    nchunk = SEQ // RET_CHUNK
    dec, cs, kd, sdec = tabs
    cast_in, cast_out, cast_shapes = _cast_specs(casts, nchunk)
    return pl.pallas_call(
        functools.partial(_odd_prompt_kernel, ncast=len(casts), layer=o),
        grid=(nchunk,),
        in_specs=[
            pl.BlockSpec((BATCH, RET_CHUNK, D_MODEL), lambda c: (0, c, 0)),
            _const_spec((D_MODEL, ODD_IN)),
            pl.BlockSpec((RET_CHUNK, W_CK), lambda c: (c, 0)),
            pl.BlockSpec((RET_CHUNK, W_CK), lambda c: (c, 0)),
            _const_spec((C_HEADS, RET_CHUNK, RET_CHUNK)),
            _const_spec((RET_CHUNK, W_CV)),
            _const_spec((RET_CHUNK, W_CK)),
            _const_spec((C_HEADS, C_DK, C_DV)),
            _const_spec((N_ODD, W_CV)),
            _const_spec((D_GROUPS, D_CHUNK, D_CHUNK), o),
            _const_spec((D_CHUNK, D_GROUPS), o),
            _const_spec((N_ODD, W_D)),
            _const_spec((N_ODD, W_D)),
        ] + cast_in,
        out_specs=[
            pl.BlockSpec((BATCH, RET_CHUNK, D_MODEL), lambda c: (0, c, 0)),
            pl.BlockSpec((BATCH, C_HEADS, C_DK, C_DV), lambda c: (0, 0, 0, 0)),
        ] + cast_out,
        out_shape=[
            jax.ShapeDtypeStruct((BATCH, SEQ, D_MODEL), BF16),
            jax.ShapeDtypeStruct((BATCH, C_HEADS, C_DK, C_DV), F32),
        ] + cast_shapes,
        scratch_shapes=[
            pltpu.VMEM((ODD_GROUP * RET_CHUNK, ODD_IN), F32),
            pltpu.VMEM((ODD_GROUP * RET_CHUNK, ODD_IN), F32),
            pltpu.VMEM((BATCH, C_HEADS, C_DK, C_DV), F32),
            pltpu.VMEM((BATCH, W_CK, W_CV), BF16),
        ],
        compiler_params=_params(("arbitrary",), 52),
        name="odd_prompt",
    )(x3d, w, cos, sin, dec, cs, kd, sdec, gn, spw, spb_t, lng, lnb, *[stack for stack, _ in casts])


def _odd_sample_kernel(proj_ref, s0_ref, cos_ref, sin_ref, dec_ref, cs_ref, kd_ref, sdec_ref, gn_ref,
                       spw_ref, spb_ref, lng_ref, lnb_ref, *rest, layer):
    gn_ref, lng_ref, lnb_ref = _layer_rows(layer, gn_ref, lng_ref, lnb_ref)
    y_ref, s_out_ref, vd_ref = rest[-3:]
    if len(rest) > 3:
        s_out_ref[:, 0] = rest[0][...]
        s_out_ref = s_out_ref.at[:, 1]
    nb = s0_ref.shape[0]

    def cols(a, b):
        return proj_ref[:, a:b].reshape(nb, DEC_SEQ, b - a)

    def put(a, b, val):
        y_ref[:, a:b] = val.reshape(nb * DEC_SEQ, b - a).astype(BF16)

    q = _rope(cols(0, W_CK), cos_ref[...], sin_ref[...])
    k = _rope(cols(W_CK, 2 * W_CK), cos_ref[...], sin_ref[...]) * (C_DK ** -0.5)
    kd = k * kd_ref[...]
    v0 = 2 * W_CK
    g0 = v0 + W_CV
    for h in range(C_HEADS):
        qh = q[:, :, h * C_DK:(h + 1) * C_DK].astype(BF16)
        kh = k[:, :, h * C_DK:(h + 1) * C_DK].astype(BF16)
        kdh = kd[:, :, h * C_DK:(h + 1) * C_DK].astype(BF16)
        vh = cols(v0 + h * C_DV, v0 + (h + 1) * C_DV).astype(BF16)
        s_prev = s0_ref[:, h]
        scores = jnp.einsum('nld,nmd->nlm', qh, kh, preferred_element_type=F32) * dec_ref[h]
        o = (jnp.einsum('nlm,nme->nle', scores.astype(BF16), vh, preferred_element_type=F32)
             + jnp.einsum('nld,nde->nle', qh, s_prev.astype(BF16), preferred_element_type=F32)
             * cs_ref[:, h * C_DV:(h + 1) * C_DV])
        s_out_ref[:, h] = sdec_ref[h] * s_prev + jnp.einsum('nld,nle->nde', kdh, vh, preferred_element_type=F32)
        gate = jax.nn.silu(cols(g0 + h * C_DV, g0 + (h + 1) * C_DV))
        put(h * C_DV, (h + 1) * C_DV, gate * (_group_norm(o) * gn_ref[:, h * C_DV:(h + 1) * C_DV]))

    u0 = g0 + W_CV
    u = jax.nn.gelu(cols(u0, u0 + W_D))
    vd = _layer_norm(jax.nn.gelu(cols(u0 + W_D, ODD_IN)), lng_ref[...], lnb_ref[...])
    vd_ref[...] = vd
    s = spb_ref[...] + spw_ref[0] * vd[:, 0:1, :]
    for m in range(1, DEC_SEQ):
        s = s + spw_ref[m] * vd[:, m:m + 1, :]
    put(W_CV, D_MODEL, u * s)


def _odd_sample(proj, state_ret, o, cos, sin, tabs, gn, spw_rows, spb_rows, lng, lnb, prev_states):
    nb = SAMPLE_NB
    dec, cs, kd, sdec = tabs
    seq3 = lambda i: (i, 0, 0)
    one_state = pl.BlockSpec((nb, C_HEADS, C_DK, C_DV), lambda i: (i, 0, 0, 0))
    in_specs = [
        pl.BlockSpec((nb * DEC_SEQ, ODD_IN), lambda i: (i, 0)),
        pl.BlockSpec((nb, None, C_HEADS, C_DK, C_DV), lambda i: (i, o, 0, 0, 0)),
        _const_spec((DEC_SEQ, W_CK)),
        _const_spec((DEC_SEQ, W_CK)),
        _const_spec((C_HEADS, DEC_SEQ, DEC_SEQ)),
        _const_spec((DEC_SEQ, W_CV)),
        _const_spec((DEC_SEQ, W_CK)),
        _const_spec((C_HEADS, C_DK, C_DV)),
        _const_spec((N_ODD, W_CV)),
        _const_spec((DEC_SEQ, DEC_SEQ, W_D), o),
        _const_spec((DEC_SEQ, W_D), o),
        _const_spec((N_ODD, W_D)),
        _const_spec((N_ODD, W_D)),
    ]
    args = [proj, state_ret, cos, sin, dec, cs, kd, sdec, gn, spw_rows, spb_rows, lng, lnb]
    if prev_states is None:
        state_spec = one_state
        state_shape = (DEC_BATCH, C_HEADS, C_DK, C_DV)
    else:
        in_specs.append(one_state)
        args.append(prev_states)
        state_spec = pl.BlockSpec((nb, N_ODD, C_HEADS, C_DK, C_DV), lambda i: (i, 0, 0, 0, 0))
        state_shape = (DEC_BATCH, N_ODD, C_HEADS, C_DK, C_DV)
    return pl.pallas_call(
        functools.partial(_odd_sample_kernel, layer=o),
        grid=(DEC_BATCH // nb,),
        in_specs=in_specs,
        out_specs=[pl.BlockSpec((nb * DEC_SEQ, D_MODEL), lambda i: (i, 0)), state_spec,
                   pl.BlockSpec((nb, DEC_SEQ, W_D), seq3)],
        out_shape=[
            jax.ShapeDtypeStruct((DEC_BATCH * DEC_SEQ, D_MODEL), BF16),
            jax.ShapeDtypeStruct(state_shape, F32),
            jax.ShapeDtypeStruct((DEC_BATCH, DEC_SEQ, W_D), F32),
        ],
        compiler_params=_params(("parallel",), 48),
        name="odd_sample",
    )(*args)


def _rope_tables(pos):
    half = C_DK // 2
    freq = ROPE_BASE ** (-np.arange(half, dtype=np.float64) / half)
    ang = np.asarray(pos, np.float64)[:, None] * freq
    cos, sin = np.cos(ang), np.sin(ang)
    cos_l = np.tile(np.concatenate([cos, cos], axis=-1), (1, C_HEADS))
    sin_l = np.tile(np.concatenate([-sin, sin], axis=-1), (1, C_HEADS))
    return cos_l.astype(np.float32), sin_l.astype(np.float32)


def _retention_tables(length):
    log_g = np.log1p(-np.exp2(-5.0 - np.arange(C_HEADS, dtype=np.float64)))
    idx = np.arange(length, dtype=np.float64)
    rel = idx[:, None] - idx[None, :]
    decay = np.where(rel >= 0, np.exp(log_g[:, None, None] * np.maximum(rel, 0.0)), 0.0)
    cross = np.exp(log_g[None, :] * (idx[:, None] + 1.0))
    kdec = np.exp(log_g[None, :] * (length - 1.0 - idx[:, None]))
    sdec = np.exp(log_g * length)
    tabs = (decay,
            np.repeat(cross, C_DV, axis=1),
            np.repeat(kdec, C_DK, axis=1),
            np.broadcast_to(sdec[:, None, None], (C_HEADS, C_DK, C_DV)))
    return tuple(np.ascontiguousarray(t, dtype=np.float32) for t in tabs)


def _block_diag_halves(wa, wx):
    nb = A_BLOCKS // 2
    width = nb * A_BLOCK

    def bd(w, hh):
        rows = [jnp.pad(w[:, hh * nb + g], ((0, 0), (0, 0), (g * A_BLOCK, width - (g + 1) * A_BLOCK)))
                for g in range(nb)]
        return jnp.concatenate(rows, axis=1)

    halves = [jnp.concatenate([bd(wa, hh), bd(wx, hh)], axis=-1) for hh in range(2)]
    return jnp.stack(halves, axis=1).astype(BF16)


def kernel(x_prompt, x_sample, state_lru_h, state_lru_conv, state_sconv, state_ret, w_in_even, conv_a_w, conv_a_b, lru_wa, lru_ba, lru_wx, lru_bx, lru_lam, conv_b_w, w_out_even, w_in_odd, ret_gn_g, sp_w, sp_b, gm_ln_g, gm_ln_b, w_out_odd, ffn_w1, ffn_w3, ffn_w2, ln1_g, ln1_b, ln2_g, ln2_b):
    xp = x_prompt.reshape(BATCH * SEQ, D_MODEL)
    xs = jnp.swapaxes(x_sample, 0, 1).reshape(DEC_SEQ * DEC_BATCH, D_MODEL)

    cos_p, sin_p = _rope_tables(np.arange(SEQ))
    cos_s, sin_s = _rope_tables(PAST_LEN + np.arange(DEC_SEQ))
    tab_p = _retention_tables(RET_CHUNK)
    tab_s = _retention_tables(DEC_SEQ)

    w_in = w_in_even[0].astype(BF16)
    ln1, ln2 = (ln1_g, ln1_b), (ln2_g, ln2_b)
    even_p = (conv_a_w, conv_a_b, _block_diag_halves(lru_wa, lru_wx), lru_ba, lru_bx, lru_lam, conv_b_w)
    ca_tm = jnp.transpose(state_lru_conv, (1, 2, 0, 3))
    sc_tm = jnp.transpose(state_sconv, (1, 2, 0, 3))
    h0_tm = jnp.swapaxes(state_lru_h, 0, 1)
    tril = jnp.tril(sp_w[:, :, :DEC_SEQ, :DEC_SEQ])
    spw_rows = jnp.repeat(jnp.transpose(tril, (0, 3, 2, 1)), D_GROUP, axis=3)
    spb_rows = jnp.repeat(jnp.swapaxes(sp_b[:, :, :DEC_SEQ], 1, 2), D_GROUP, axis=2)
    spb_t = jnp.swapaxes(sp_b, 1, 2)
    odd_small = (ret_gn_g,)
    odd_ln = (gm_ln_g, gm_ln_b)

    h_p, h_s, ca_p, ca_s, sc_p, sc_s, r_p, v_s = [], [], [], [], [], [], [], []
    ret_sample = None
    for l in range(DEPTH):
        if l % 2 == 0:
            e = l // 2
            casts = [(w_out_even, e), (ffn_w1, l), (ffn_w3, l), (ffn_w2, l), (w_in_odd, e)]
            yp, hlp, cap, scp, wo, w1, w3, w2, w_in_next = _even_prompt(xp.reshape(BATCH, SEQ, D_MODEL), w_in, e,
                                                                        even_p, casts)
            proj_s = _in_proj(xs, w_in).reshape(DEC_SEQ, DEC_BATCH, EVEN_IN)
            ys, hls, cas, scs = _even_sample(proj_s, ca_tm, sc_tm, h0_tm, e, even_p)
            h_p.append(hlp)
            h_s.append(hls)
            ca_p.append(cap[:, SUBLANES - (A_CONV - 1):])
            ca_s.append(jnp.swapaxes(cas, 0, 1))
            sc_p.append(scp[:, SUBLANES - (B_CONV - 1):])
            sc_s.append(jnp.swapaxes(scs, 0, 1))
            yp = yp.reshape(BATCH * SEQ, D_MODEL)
            ys = ys.reshape(DEC_SEQ * DEC_BATCH, D_MODEL)
        else:
            o = l // 2
            casts = [(w_out_odd, o), (ffn_w1, l), (ffn_w3, l), (ffn_w2, l)]
            if l + 1 < DEPTH:
                casts.append((w_in_even, o + 1))
            yp, rp, wo, w1, w3, w2, *w_in_next = _odd_prompt(xp.reshape(BATCH, SEQ, D_MODEL), w_in, o, cos_p, sin_p,
                                                             tab_p, *odd_small, sp_w, spb_t, *odd_ln, casts)
            w_in_next = w_in_next[0] if w_in_next else None
            proj_s = _in_proj(xs, w_in)
            ys, ret_sample, vs = _odd_sample(proj_s, state_ret, o, cos_s, sin_s, tab_s, *odd_small,
                                             spw_rows, spb_rows, *odd_ln, ret_sample)
            r_p.append(rp)
            v_s.append(vs)
            yp = yp.reshape(BATCH * SEQ, D_MODEL)
        xp, xs = _dense_block(yp, xp, ys, xs, l, wo, *ln1, w1, w3, w2, *ln2)
        w_in = w_in_next
        if l + 1 < DEPTH:
            if l % 2 == 0:
                xs = jnp.swapaxes(xs.reshape(DEC_SEQ, DEC_BATCH, D_MODEL), 0, 1).reshape(DEC_BATCH * DEC_SEQ, D_MODEL)
            else:
                xs = jnp.swapaxes(xs.reshape(DEC_BATCH, DEC_SEQ, D_MODEL), 0, 1).reshape(DEC_SEQ * DEC_BATCH, D_MODEL)

    return (xp.reshape(BATCH, SEQ, D_MODEL), xs.reshape(DEC_BATCH, DEC_SEQ, D_MODEL),
            jnp.stack(h_p, axis=1), jnp.stack(h_s, axis=1),
            jnp.stack(ca_p, axis=1), jnp.stack(ca_s, axis=1),
            jnp.stack(sc_p, axis=1), jnp.stack(sc_s, axis=1),
            jnp.stack(r_p, axis=1), ret_sample, jnp.stack(v_s, axis=1))
```

```python
import functools

import jax
import jax.numpy as jnp
import numpy as np
from jax import lax
from jax.experimental import pallas as pl
from jax.experimental.pallas import tpu as pltpu

F32 = jnp.float32
BF16 = jnp.bfloat16

D_MODEL = 1024
BATCH = 8
SEQ = 2048
DEPTH = 4
DEC_BATCH = 128
DEC_SEQ = 4
PAST_LEN = 16384
N_EVEN = (DEPTH + 1) // 2
N_ODD = DEPTH // 2
W_A = 512
A_BLOCKS = 8
A_BLOCK = 64
A_CONV = 4
LRU_C = 8.0
W_B = 512
B_CONV = 3
C_HEADS = 4
C_DK = 64
C_DV = 128
W_CK = C_HEADS * C_DK
W_CV = C_HEADS * C_DV
RET_CHUNK = 128
ROPE_BASE = 10000.0
D_GROUPS = 4
D_CHUNK = 128
W_D = 512
D_GROUP = W_D // D_GROUPS
EVEN_IN = 2 * W_A + 3 * W_B
ODD_IN = 2 * W_CK + 2 * W_CV + 2 * W_D
D_FF = 2816
ALPHA = (2 * DEPTH) ** 0.25
LN_EPS = 1e-5

SUBLANES = 8
LANES = 128
MIB = 1024 * 1024

ROW_TILE = 512
DENSE_TILE = 512
DENSE_GROUP = 256
DENSE_COLS = 256
EVEN_TT = 128
EVEN_GROUP = 1
ODD_GROUP = 2
EVEN_PITCH = EVEN_TT + SUBLANES
SAMPLE_NB = 32


def _params(sem, vmem_mib):
    return pltpu.CompilerParams(dimension_semantics=sem, vmem_limit_bytes=vmem_mib * MIB)


def _const_spec(shape, layer=None):
    nd = len(shape)
    if layer is None:
        return pl.BlockSpec(shape, lambda *_: (0,) * nd, pipeline_mode=pl.Buffered(1))
    return pl.BlockSpec((None,) + tuple(shape), lambda *_: (layer,) + (0,) * nd, pipeline_mode=pl.Buffered(1))


def _layer_rows(layer, *refs):
    return [r.at[layer:layer + 1] for r in refs]


def _layer_norm(x, g, b):
    mu = jnp.mean(x, -1, keepdims=True)
    xc = x - mu
    var = jnp.mean(xc * xc, -1, keepdims=True)
    return xc * lax.rsqrt(var + LN_EPS) * g + b


def _softplus(x):
    return jnp.maximum(x, 0.0) + jnp.log1p(jnp.exp(-jnp.abs(x)))


def _dot(a, b):
    return jnp.dot(a, b, preferred_element_type=F32)


def _cast_specs(stacks_and_layers, nsteps):
    in_specs, out_specs, out_shapes = [], [], []
    for w, layer in stacks_and_layers:
        nrow, ncol = w.shape[1:]
        rb = nrow // nsteps
        assert nrow % nsteps == 0 and rb % (2 * SUBLANES) == 0
        in_specs.append(pl.BlockSpec((None, rb, ncol), lambda j, layer=layer: (layer, j, 0)))
        out_specs.append(pl.BlockSpec((rb, ncol), lambda j: (j, 0)))
        out_shapes.append(jax.ShapeDtypeStruct((nrow, ncol), BF16))
    return in_specs, out_specs, out_shapes


def _cast_slabs(src_refs, dst_refs):
    for src, dst in zip(src_refs, dst_refs):
        dst[...] = src[...].astype(BF16)


def _proj_kernel(x_ref, w_ref, o_ref):
    o_ref[...] = _dot(x_ref[...].astype(BF16), w_ref[...])


def _in_proj(x2d, w):
    m, k = x2d.shape
    n = w.shape[1]
    tm = min(ROW_TILE, m)
    return pl.pallas_call(
        _proj_kernel,
        grid=(m // tm,),
        in_specs=[pl.BlockSpec((tm, k), lambda i: (i, 0)), _const_spec((k, n))],
        out_specs=pl.BlockSpec((tm, n), lambda i: (i, 0)),
        out_shape=jax.ShapeDtypeStruct((m, n), F32),
        compiler_params=_params(("parallel",), 40),
        name="in_proj",
    )(x2d, w)


def _dense_kernel(yp_ref, xp_ref, ys_ref, xs_ref, wo_ref, g1_ref, b1_ref, w1_ref, w3_ref, w2_ref, g2_ref, b2_ref,
                  op_ref, os_ref, h_s, *, layer):
    n_prompt = pl.num_programs(0) - 1
    g1_ref, b1_ref, g2_ref, b2_ref = _layer_rows(layer, g1_ref, b1_ref, g2_ref, b2_ref)
    weights = (wo_ref, g1_ref, b1_ref, w1_ref, w3_ref, w2_ref, g2_ref, b2_ref)

    @pl.when(pl.program_id(0) < n_prompt)
    def _():
        _dense_rows(yp_ref, xp_ref, *weights, op_ref, h_s)

    @pl.when(pl.program_id(0) == n_prompt)
    def _():
        _dense_rows(ys_ref, xs_ref, *weights, os_ref, h_s)


def _dense_rows(y_ref, x_ref, wo_ref, g1_ref, b1_ref, w1_ref, w3_ref, w2_ref, g2_ref, b2_ref, o_ref, h_s):
    rows, ncols = DENSE_GROUP, DENSE_COLS
    nparts = y_ref.shape[0] // rows
    sls = [slice(p * rows, (p + 1) * rows) for p in range(nparts)]
    mix, x1, xb = {}, {}, {}

    def out_proj(p):
        mix[p] = _dot(y_ref[sls[p], :], wo_ref[...])

    def norm1(p):
        x1[p] = _layer_norm(ALPHA * x_ref[sls[p], :] + mix[p], g1_ref[...], b1_ref[...])
        xb[p] = x1[p].astype(BF16)

    def gate_up(p):
        for c in range(0, D_FF, ncols):
            cols = slice(c, c + ncols)
            h_s[sls[p], cols] = (jax.nn.silu(_dot(xb[p], w1_ref[:, cols]))
                                 * _dot(xb[p], w3_ref[:, cols])).astype(BF16)

    def down_norm2(p):
        f = _dot(h_s[sls[p], :], w2_ref[...])
        o_ref[sls[p], :] = _layer_norm(ALPHA * x1[p] + f, g2_ref[...], b2_ref[...])

    out_proj(0)
    for p in range(nparts):
        norm1(p)
        if p + 1 < nparts:
            out_proj(p + 1)
    for p in range(nparts):
        gate_up(p)
    for p in range(nparts):
        down_norm2(p)


def _dense_block(yp, xp, ys, xs, layer, wo, g1, b1, w1, w3, w2, g2, b2):
    tm = DENSE_TILE
    n_prompt = xp.shape[0] // tm
    assert xp.shape[0] % tm == 0 and xs.shape[0] == tm and tm % DENSE_GROUP == 0
    prompt_row = lambda i: (jnp.minimum(i, n_prompt - 1), 0)
    sample_row = lambda i: (0, 0)
    return pl.pallas_call(
        functools.partial(_dense_kernel, layer=layer),
        grid=(n_prompt + 1,),
        in_specs=[
            pl.BlockSpec((tm, D_MODEL), prompt_row),
            pl.BlockSpec((tm, D_MODEL), prompt_row),
            pl.BlockSpec((tm, D_MODEL), sample_row),
            pl.BlockSpec((tm, D_MODEL), sample_row),
            _const_spec((D_MODEL, D_MODEL)),
            _const_spec((DEPTH, D_MODEL)),
            _const_spec((DEPTH, D_MODEL)),
            _const_spec((D_MODEL, D_FF)),
            _const_spec((D_MODEL, D_FF)),
            _const_spec((D_FF, D_MODEL)),
            _const_spec((DEPTH, D_MODEL)),
            _const_spec((DEPTH, D_MODEL)),
        ],
        out_specs=[pl.BlockSpec((tm, D_MODEL), prompt_row), pl.BlockSpec((tm, D_MODEL), sample_row)],
        out_shape=[jax.ShapeDtypeStruct(xp.shape, F32), jax.ShapeDtypeStruct(xs.shape, F32)],
        scratch_shapes=[pltpu.VMEM((tm, D_FF), BF16)],
        compiler_params=_params(("arbitrary",), 58),
        name="dense_block",
    )(yp, xp, ys, xs, wo, g1, b1, w1, w3, w2, g2, b2)


def _lru_gates(xc, wblk_ref, ba, bx, sp):
    half = W_A // 2
    xb = xc.astype(BF16)
    pre0 = _dot(xb[:, :half], wblk_ref[0])
    pre1 = _dot(xb[:, half:], wblk_ref[1])
    r = jax.nn.sigmoid(jnp.concatenate([pre0[:, :half], pre1[:, :half]], axis=1) + ba)
    i = jax.nn.sigmoid(jnp.concatenate([pre0[:, half:], pre1[:, half:]], axis=1) + bx)
    log_a = (-LRU_C) * r * sp
    a = jnp.exp(log_a)
    mult = jnp.sqrt((1.0 + a * a) * jnp.tanh(-log_a))
    return a, mult, i


def _causal_taps(x_slabs, ext, b, w_ref, nrows):
    ktaps = w_ref.shape[0]
    out = []
    for s, x in enumerate(x_slabs):
        lanes = slice(s * LANES, (s + 1) * LANES)
        ext[s, b, SUBLANES:SUBLANES + nrows, :] = x
        acc = w_ref[ktaps - 1:ktaps, lanes] * x
        for k in range(ktaps - 1):
            off = SUBLANES - (ktaps - 1) + k
            acc = acc + w_ref[k:k + 1, lanes] * ext[s, b, off:off + nrows, :]
        out.append(acc)
    return jnp.concatenate(out, axis=1)


def _even_prompt_kernel(x_ref, w_ref, caw_ref, cab_ref, wblk_ref, ba_ref, bx_ref, lam_ref, cbw_ref, *rest,
                        ncast, layer):
    cab_ref, ba_ref, bx_ref, lam_ref = _layer_rows(layer, cab_ref, ba_ref, bx_ref, lam_ref)
    cast_src, rest = rest[:ncast], rest[ncast:]
    y_ref, hlast_ref, ca_out_ref, sc_out_ref = rest[:4]
    cast_dst, rest = rest[4:4 + ncast], rest[4 + ncast:]
    p0, p1, xa_ext, cb_ext, a_s, b_s, g_s, h_c = rest
    tt, pitch = EVEN_TT, EVEN_PITCH
    nslab = W_A // LANES
    j = pl.program_id(0)
    _cast_slabs(cast_src, cast_dst)
    bufs = (p0, p1)
    group = p0.shape[0] // tt

    def project(g):
        xg = x_ref[g * group:(g + 1) * group].reshape(group * tt, D_MODEL)
        bufs[g % 2][...] = _dot(xg.astype(BF16), w_ref[...])

    @pl.when(j == 0)
    def _():
        xa_ext[:, :, 0:SUBLANES, :] = jnp.zeros((nslab, BATCH, SUBLANES, LANES), F32)
        cb_ext[:, :, 0:SUBLANES, :] = jnp.zeros((nslab, BATCH, SUBLANES, LANES), F32)
        h_c[...] = jnp.zeros(h_c.shape, F32)

    @pl.when(j > 0)
    def _():
        xa_ext[:, :, 0:SUBLANES, :] = xa_ext[:, :, tt:tt + SUBLANES, :]
        cb_ext[:, :, 0:SUBLANES, :] = cb_ext[:, :, tt:tt + SUBLANES, :]

    sp = _softplus(-lam_ref[...])
    ba = ba_ref[...]
    bx = bx_ref[...]
    cab = cab_ref[...]
    first_row = (lax.broadcasted_iota(jnp.int32, (SUBLANES, 1), 0) == 0) & (j == 0)
    cg0 = 2 * W_A + W_B
    xb0 = 2 * W_A + 2 * W_B

    def stage(b):
        p_ref = bufs[(b // group) % 2].at[(b % group) * tt:(b % group + 1) * tt]
        xc = cab + _causal_taps([p_ref[:, s * LANES:(s + 1) * LANES] for s in range(nslab)], xa_ext, b, caw_ref, tt)
        a, mult, gate = _lru_gates(xc, wblk_ref, ba, bx, sp)
        mult = jnp.concatenate([jnp.where(first_row, 1.0, mult[:SUBLANES]), mult[SUBLANES:]], axis=0)
        bb = mult * (gate * xc)
        for s in range(nslab):
            a_s[s, b * pitch:b * pitch + tt, :] = a[:, s * LANES:(s + 1) * LANES]
            b_s[s, b * pitch:b * pitch + tt, :] = bb[:, s * LANES:(s + 1) * LANES]
        g_s[b] = jax.nn.gelu(p_ref[:, W_A:2 * W_A])
        cb = [p_ref[:, cg0 + s * LANES:cg0 + (s + 1) * LANES] * p_ref[:, xb0 + s * LANES:xb0 + (s + 1) * LANES]
              for s in range(nslab)]
        zb = _causal_taps(cb, cb_ext, b, cbw_ref, tt)
        y_ref[b, :, W_A:] = (p_ref[:, 2 * W_A:2 * W_A + W_B] * zb).astype(BF16)

    project(0)
    for g in range(BATCH // group):
        if g + 1 < BATCH // group:
            project(g + 1)
        for b in range(g * group, (g + 1) * group):
            stage(b)

    def step(t, hs):
        out = []
        for s in range(nslab):
            rows = pl.ds(t, BATCH, stride=pitch)
            hn = a_s[s, rows, :] * hs[s] + b_s[s, rows, :]
            b_s[s, rows, :] = hn
            out.append(hn)
        return tuple(out)

    hs = lax.fori_loop(0, tt, step, tuple(h_c[s] for s in range(nslab)), unroll=4)
    for s in range(nslab):
        h_c[s] = hs[s]

    def emit(b, carry):
        row0 = pl.multiple_of(b * pitch, SUBLANES)
        h = jnp.concatenate([b_s[s, pl.ds(row0, tt), :] for s in range(nslab)], axis=1)
        y_ref[b, :, 0:W_A] = (h * g_s[b]).astype(BF16)
        return carry

    lax.fori_loop(0, BATCH, emit, 0)

    @pl.when(j == pl.num_programs(0) - 1)
    def _():
        hlast_ref[...] = jnp.concatenate(list(hs), axis=1)
        for s in range(nslab):
            ca_out_ref[:, :, s * LANES:(s + 1) * LANES] = xa_ext[s, :, tt:tt + SUBLANES, :]
            sc_out_ref[:, :, s * LANES:(s + 1) * LANES] = cb_ext[s, :, tt:tt + SUBLANES, :]


def _even_param_specs(e):
    rows = _const_spec((N_EVEN, W_A))
    return [
        _const_spec((A_CONV, W_A), e),
        rows,
        _const_spec((2, W_A // 2, W_A), e),
        rows,
        rows,
        rows,
        _const_spec((B_CONV, W_B), e),
    ]


def _even_prompt(x3d, w, e, params, casts):
    tt, pitch = EVEN_TT, EVEN_PITCH
    nslab = W_A // LANES
    nblock = SEQ // tt
    assert W_A == W_B
    cast_in, cast_out, cast_shapes = _cast_specs(casts, nblock)
    return pl.pallas_call(
        functools.partial(_even_prompt_kernel, ncast=len(casts), layer=e),
        grid=(nblock,),
        in_specs=[pl.BlockSpec((BATCH, tt, D_MODEL), lambda j: (0, j, 0)),
                  _const_spec((D_MODEL, EVEN_IN))] + _even_param_specs(e) + cast_in,
        out_specs=[
            pl.BlockSpec((BATCH, tt, D_MODEL), lambda j: (0, j, 0)),
            pl.BlockSpec((BATCH, W_A), lambda j: (0, 0)),
            pl.BlockSpec((BATCH, SUBLANES, W_A), lambda j: (0, 0, 0)),
            pl.BlockSpec((BATCH, SUBLANES, W_B), lambda j: (0, 0, 0)),
        ] + cast_out,
        out_shape=[
            jax.ShapeDtypeStruct((BATCH, SEQ, D_MODEL), BF16),
            jax.ShapeDtypeStruct((BATCH, W_A), F32),
            jax.ShapeDtypeStruct((BATCH, SUBLANES, W_A), F32),
            jax.ShapeDtypeStruct((BATCH, SUBLANES, W_B), F32),
        ] + cast_shapes,
        scratch_shapes=[
            pltpu.VMEM((EVEN_GROUP * tt, EVEN_IN), F32),
            pltpu.VMEM((EVEN_GROUP * tt, EVEN_IN), F32),
            pltpu.VMEM((nslab, BATCH, tt + SUBLANES, LANES), F32),
            pltpu.VMEM((nslab, BATCH, tt + SUBLANES, LANES), F32),
            pltpu.VMEM((nslab, BATCH * pitch, LANES), F32),
            pltpu.VMEM((nslab, BATCH * pitch, LANES), F32),
            pltpu.VMEM((BATCH, tt, W_A), F32),
            pltpu.VMEM((nslab, BATCH, LANES), F32),
        ],
        compiler_params=_params(("arbitrary",), 52),
        name="even_prompt",
    )(x3d, w, *params, *[stack for stack, _ in casts])


def _even_sample_kernel(proj_ref, ca_ref, sc_ref, h0_ref, caw_ref, cab_ref, wblk_ref, ba_ref, bx_ref,
                        lam_ref, cbw_ref, y_ref, hlast_ref, ca_out_ref, sc_out_ref, *, layer):
    cab_ref, ba_ref, bx_ref, lam_ref = _layer_rows(layer, cab_ref, ba_ref, bx_ref, lam_ref)
    sp = _softplus(-lam_ref[...])
    rows_a = [ca_ref[k] for k in range(A_CONV - 1)] + [proj_ref[l, :, 0:W_A] for l in range(DEC_SEQ)]
    xc = []
    for l in range(DEC_SEQ):
        acc = cab_ref[...] + caw_ref[0:1, :] * rows_a[l]
        for k in range(1, A_CONV):
            acc = acc + caw_ref[k:k + 1, :] * rows_a[l + k]
        xc.append(acc)
    xc_all = jnp.concatenate(xc, axis=0)
    a, mult, gate = _lru_gates(xc_all, wblk_ref, ba_ref[...], bx_ref[...], sp)
    bb = mult * (gate * xc_all)
    h = h0_ref[...]
    rows_b = [sc_ref[k] for k in range(B_CONV - 1)]
    for l in range(DEC_SEQ):
        sl = slice(l * DEC_BATCH, (l + 1) * DEC_BATCH)
        h = a[sl] * h + bb[sl]
        y_ref[l, :, 0:W_A] = (h * jax.nn.gelu(proj_ref[l, :, W_A:2 * W_A])).astype(BF16)
        rows_b.append(proj_ref[l, :, 2 * W_A + W_B:2 * W_A + 2 * W_B] * proj_ref[l, :, 2 * W_A + 2 * W_B:])
    hlast_ref[...] = h
    for l in range(DEC_SEQ):
        zb = cbw_ref[0:1, :] * rows_b[l]
        for k in range(1, B_CONV):
            zb = zb + cbw_ref[k:k + 1, :] * rows_b[l + k]
        y_ref[l, :, W_A:] = (proj_ref[l, :, 2 * W_A:2 * W_A + W_B] * zb).astype(BF16)
    for k in range(A_CONV - 1):
        ca_out_ref[k] = rows_a[DEC_SEQ + k]
    for k in range(B_CONV - 1):
        sc_out_ref[k] = rows_b[DEC_SEQ + k]


def _even_sample(proj_tm, ca_tm, sc_tm, h0_tm, e, params):
    whole = lambda shape: pl.BlockSpec(shape, lambda i: (0,) * len(shape))
    return pl.pallas_call(
        functools.partial(_even_sample_kernel, layer=e),
        grid=(1,),
        in_specs=[
            whole((DEC_SEQ, DEC_BATCH, EVEN_IN)),
            _const_spec((A_CONV - 1, DEC_BATCH, W_A), e),
            _const_spec((B_CONV - 1, DEC_BATCH, W_B), e),
            _const_spec((DEC_BATCH, W_A), e),
        ] + _even_param_specs(e),
        out_specs=[
            whole((DEC_SEQ, DEC_BATCH, D_MODEL)),
            whole((DEC_BATCH, W_A)),
            whole((A_CONV - 1, DEC_BATCH, W_A)),
            whole((B_CONV - 1, DEC_BATCH, W_B)),
        ],
        out_shape=[
            jax.ShapeDtypeStruct((DEC_SEQ, DEC_BATCH, D_MODEL), BF16),
            jax.ShapeDtypeStruct((DEC_BATCH, W_A), F32),
            jax.ShapeDtypeStruct((A_CONV - 1, DEC_BATCH, W_A), F32),
            jax.ShapeDtypeStruct((B_CONV - 1, DEC_BATCH, W_B), F32),
        ],
        compiler_params=_params(("arbitrary",), 40),
        name="even_sample",
    )(proj_tm, ca_tm, sc_tm, h0_tm, *params)


def _rope(x, cos, sin_signed):
    half = C_DK // 2
    ax = x.ndim - 1
    pieces = []
    for p in range(W_CK // LANES):
        xv = x[..., p * LANES:(p + 1) * LANES]
        lane = lax.broadcasted_iota(jnp.int32, xv.shape, ax)
        partner = jnp.where((lane % C_DK) < half, pltpu.roll(xv, LANES - half, axis=ax), pltpu.roll(xv, half, axis=ax))
        pieces.append(xv * cos[..., p * LANES:(p + 1) * LANES] + partner * sin_signed[..., p * LANES:(p + 1) * LANES])
    return jnp.concatenate(pieces, axis=ax)


def _group_norm(o):
    mu = jnp.mean(o, -1, keepdims=True)
    oc = o - mu
    var = jnp.mean(oc * oc, -1, keepdims=True)
    return oc * lax.rsqrt(var + LN_EPS)


def _odd_chunk(p_ref, b, cos_ref, sin_ref, dec_ref, cs_ref, kd_ref, sdec_ref, gn_ref,
               spw_ref, spb_ref, lng_ref, lnb_ref, y_ref, s_c, s_bd):
    q = _rope(p_ref[:, 0:W_CK], cos_ref[...], sin_ref[...])
    k = _rope(p_ref[:, W_CK:2 * W_CK], cos_ref[...], sin_ref[...]) * (C_DK ** -0.5)
    kd = k * kd_ref[...]
    v0 = 2 * W_CK
    g0 = v0 + W_CV
    cross = _dot(q.astype(BF16), s_bd[b]) * cs_ref[...]
    vb = p_ref[:, v0:g0].astype(BF16)
    kv = lax.dot_general(kd.astype(BF16), vb, (((0,), (0,)), ((), ())), preferred_element_type=F32)
    for h in range(C_HEADS):
        qh = q[:, h * C_DK:(h + 1) * C_DK].astype(BF16)
        kh = k[:, h * C_DK:(h + 1) * C_DK].astype(BF16)
        vh = vb[:, h * C_DV:(h + 1) * C_DV]
        scores = lax.dot_general(qh, kh, (((1,), (1,)), ((), ())), preferred_element_type=F32) * dec_ref[h]
        o = _dot(scores.astype(BF16), vh) + cross[:, h * C_DV:(h + 1) * C_DV]
        s_new = sdec_ref[h] * s_c[b, h] + kv[h * C_DK:(h + 1) * C_DK, h * C_DV:(h + 1) * C_DV]
        s_c[b, h] = s_new
        s_bd[b, h * C_DK:(h + 1) * C_DK, h * C_DV:(h + 1) * C_DV] = s_new.astype(BF16)
        gate = jax.nn.silu(p_ref[:, g0 + h * C_DV:g0 + (h + 1) * C_DV])
        y_ref[b, :, h * C_DV:(h + 1) * C_DV] = (gate * (_group_norm(o) * gn_ref[:, h * C_DV:(h + 1) * C_DV])).astype(BF16)

    u0 = g0 + W_CV
    u = jax.nn.gelu(p_ref[:, u0:u0 + W_D])
    vd = _layer_norm(jax.nn.gelu(p_ref[:, u0 + W_D:]), lng_ref[...], lnb_ref[...]).astype(BF16)
    ri = lax.broadcasted_iota(jnp.int32, (D_CHUNK, D_CHUNK), 0)
    ci = lax.broadcasted_iota(jnp.int32, (D_CHUNK, D_CHUNK), 1)
    for gi in range(D_GROUPS):
        w = jnp.where(ri >= ci, spw_ref[gi], 0.0).astype(BF16)
        s = _dot(w, vd[:, gi * D_GROUP:(gi + 1) * D_GROUP]) + spb_ref[:, gi:gi + 1]
        y_ref[b, :, W_CV + gi * D_GROUP:W_CV + (gi + 1) * D_GROUP] = (u[:, gi * D_GROUP:(gi + 1) * D_GROUP] * s).astype(BF16)


def _odd_prompt_kernel(x_ref, w_ref, cos_ref, sin_ref, dec_ref, cs_ref, kd_ref, sdec_ref, gn_ref,
                       spw_ref, spb_ref, lng_ref, lnb_ref, *rest, ncast, layer):
    gn_ref, lng_ref, lnb_ref = _layer_rows(layer, gn_ref, lng_ref, lnb_ref)
    cast_src, rest = rest[:ncast], rest[ncast:]
    y_ref, s_out_ref = rest[:2]
    cast_dst, rest = rest[2:2 + ncast], rest[2 + ncast:]
    p0, p1, s_c, s_bd = rest
    c = pl.program_id(0)
    _cast_slabs(cast_src, cast_dst)

    @pl.when(c == 0)
    def _():
        s_c[...] = jnp.zeros(s_c.shape, F32)
        s_bd[...] = jnp.zeros(s_bd.shape, BF16)

    bufs = (p0, p1)
    group = p0.shape[0] // RET_CHUNK

    def project(g):
        xg = x_ref[g * group:(g + 1) * group].reshape(group * RET_CHUNK, D_MODEL)
        bufs[g % 2][...] = _dot(xg.astype(BF16), w_ref[...])

    project(0)
    for g in range(BATCH // group):
        if g + 1 < BATCH // group:
            project(g + 1)
        for r in range(group):
            _odd_chunk(bufs[g % 2].at[r * RET_CHUNK:(r + 1) * RET_CHUNK], g * group + r,
                       cos_ref, sin_ref, dec_ref, cs_ref, kd_ref, sdec_ref, gn_ref,
                       spw_ref, spb_ref, lng_ref, lnb_ref, y_ref, s_c, s_bd)

    @pl.when(c == pl.num_programs(0) - 1)
    def _():
        s_out_ref[...] = s_c[...]


def _odd_prompt(x3d, w, o, cos, sin, tabs, gn, spw, spb_t, lng, lnb, casts):
    nchunk = SEQ // RET_CHUNK
    dec, cs, kd, sdec = tabs
    cast_in, cast_out, cast_shapes = _cast_specs(casts, nchunk)
    return pl.pallas_call(
        functools.partial(_odd_prompt_kernel, ncast=len(casts), layer=o),
        grid=(nchunk,),
        in_specs=[
            pl.BlockSpec((BATCH, RET_CHUNK, D_MODEL), lambda c: (0, c, 0)),
            _const_spec((D_MODEL, ODD_IN)),
            pl.BlockSpec((RET_CHUNK, W_CK), lambda c: (c, 0)),
            pl.BlockSpec((RET_CHUNK, W_CK), lambda c: (c, 0)),
            _const_spec((C_HEADS, RET_CHUNK, RET_CHUNK)),
            _const_spec((RET_CHUNK, W_CV)),
            _const_spec((RET_CHUNK, W_CK)),
            _const_spec((C_HEADS, C_DK, C_DV)),
            _const_spec((N_ODD, W_CV)),
            _const_spec((D_GROUPS, D_CHUNK, D_CHUNK), o),
            _const_spec((D_CHUNK, D_GROUPS), o),
            _const_spec((N_ODD, W_D)),
            _const_spec((N_ODD, W_D)),
        ] + cast_in,
        out_specs=[
            pl.BlockSpec((BATCH, RET_CHUNK, D_MODEL), lambda c: (0, c, 0)),
            pl.BlockSpec((BATCH, C_HEADS, C_DK, C_DV), lambda c: (0, 0, 0, 0)),
        ] + cast_out,
        out_shape=[
            jax.ShapeDtypeStruct((BATCH, SEQ, D_MODEL), BF16),
            jax.ShapeDtypeStruct((BATCH, C_HEADS, C_DK, C_DV), F32),
        ] + cast_shapes,
        scratch_shapes=[
            pltpu.VMEM((ODD_GROUP * RET_CHUNK, ODD_IN), F32),
            pltpu.VMEM((ODD_GROUP * RET_CHUNK, ODD_IN), F32),
            pltpu.VMEM((BATCH, C_HEADS, C_DK, C_DV), F32),
            pltpu.VMEM((BATCH, W_CK, W_CV), BF16),
        ],
        compiler_params=_params(("arbitrary",), 52),
        name="odd_prompt",
    )(x3d, w, cos, sin, dec, cs, kd, sdec, gn, spw, spb_t, lng, lnb, *[stack for stack, _ in casts])


def _odd_sample_kernel(proj_ref, s0_ref, cos_ref, sin_ref, dec_ref, cs_ref, kd_ref, sdec_ref, gn_ref,
                       spw_ref, spb_ref, lng_ref, lnb_ref, *rest, layer):
    gn_ref, lng_ref, lnb_ref = _layer_rows(layer, gn_ref, lng_ref, lnb_ref)
    y_ref, s_out_ref, vd_ref = rest[-3:]
    if len(rest) > 3:
        s_out_ref[:, 0] = rest[0][...]
        s_out_ref = s_out_ref.at[:, 1]
    nb = s0_ref.shape[0]

    def cols(a, b):
        return proj_ref[:, a:b].reshape(nb, DEC_SEQ, b - a)

    def put(a, b, val):
        y_ref[:, a:b] = val.reshape(nb * DEC_SEQ, b - a).astype(BF16)

    q = _rope(cols(0, W_CK), cos_ref[...], sin_ref[...])
    k = _rope(cols(W_CK, 2 * W_CK), cos_ref[...], sin_ref[...]) * (C_DK ** -0.5)
    kd = k * kd_ref[...]
    v0 = 2 * W_CK
    g0 = v0 + W_CV
    for h in range(C_HEADS):
        qh = q[:, :, h * C_DK:(h + 1) * C_DK].astype(BF16)
        kh = k[:, :, h * C_DK:(h + 1) * C_DK].astype(BF16)
        kdh = kd[:, :, h * C_DK:(h + 1) * C_DK].astype(BF16)
        vh = cols(v0 + h * C_DV, v0 + (h + 1) * C_DV).astype(BF16)
        s_prev = s0_ref[:, h]
        scores = jnp.einsum('nld,nmd->nlm', qh, kh, preferred_element_type=F32) * dec_ref[h]
        o = (jnp.einsum('nlm,nme->nle', scores.astype(BF16), vh, preferred_element_type=F32)
             + jnp.einsum('nld,nde->nle', qh, s_prev.astype(BF16), preferred_element_type=F32)
             * cs_ref[:, h * C_DV:(h + 1) * C_DV])
        s_out_ref[:, h] = sdec_ref[h] * s_prev + jnp.einsum('nld,nle->nde', kdh, vh, preferred_element_type=F32)
        gate = jax.nn.silu(cols(g0 + h * C_DV, g0 + (h + 1) * C_DV))
        put(h * C_DV, (h + 1) * C_DV, gate * (_group_norm(o) * gn_ref[:, h * C_DV:(h + 1) * C_DV]))

    u0 = g0 + W_CV
    u = jax.nn.gelu(cols(u0, u0 + W_D))
    vd = _layer_norm(jax.nn.gelu(cols(u0 + W_D, ODD_IN)), lng_ref[...], lnb_ref[...])
    vd_ref[...] = vd
    s = spb_ref[...] + spw_ref[0] * vd[:, 0:1, :]
    for m in range(1, DEC_SEQ):
        s = s + spw_ref[m] * vd[:, m:m + 1, :]
    put(W_CV, D_MODEL, u * s)


def _odd_sample(proj, state_ret, o, cos, sin, tabs, gn, spw_rows, spb_rows, lng, lnb, prev_states):
    nb = SAMPLE_NB
    dec, cs, kd, sdec = tabs
    seq3 = lambda i: (i, 0, 0)
    one_state = pl.BlockSpec((nb, C_HEADS, C_DK, C_DV), lambda i: (i, 0, 0, 0))
    in_specs = [
        pl.BlockSpec((nb * DEC_SEQ, ODD_IN), lambda i: (i, 0)),
        pl.BlockSpec((nb, None, C_HEADS, C_DK, C_DV), lambda i: (i, o, 0, 0, 0)),
        _const_spec((DEC_SEQ, W_CK)),
        _const_spec((DEC_SEQ, W_CK)),
        _const_spec((C_HEADS, DEC_SEQ, DEC_SEQ)),
        _const_spec((DEC_SEQ, W_CV)),
        _const_spec((DEC_SEQ, W_CK)),
        _const_spec((C_HEADS, C_DK, C_DV)),
        _const_spec((N_ODD, W_CV)),
        _const_spec((DEC_SEQ, DEC_SEQ, W_D), o),
        _const_spec((DEC_SEQ, W_D), o),
        _const_spec((N_ODD, W_D)),
        _const_spec((N_ODD, W_D)),
    ]
    args = [proj, state_ret, cos, sin, dec, cs, kd, sdec, gn, spw_rows, spb_rows, lng, lnb]
    if prev_states is None:
        state_spec = one_state
        state_shape = (DEC_BATCH, C_HEADS, C_DK, C_DV)
    else:
        in_specs.append(one_state)
        args.append(prev_states)
        state_spec = pl.BlockSpec((nb, N_ODD, C_HEADS, C_DK, C_DV), lambda i: (i, 0, 0, 0, 0))
        state_shape = (DEC_BATCH, N_ODD, C_HEADS, C_DK, C_DV)
    return pl.pallas_call(
        functools.partial(_odd_sample_kernel, layer=o),
        grid=(DEC_BATCH // nb,),
        in_specs=in_specs,
        out_specs=[pl.BlockSpec((nb * DEC_SEQ, D_MODEL), lambda i: (i, 0)), state_spec,
                   pl.BlockSpec((nb, DEC_SEQ, W_D), seq3)],
        out_shape=[
            jax.ShapeDtypeStruct((DEC_BATCH * DEC_SEQ, D_MODEL), BF16),
            jax.ShapeDtypeStruct(state_shape, F32),
            jax.ShapeDtypeStruct((DEC_BATCH, DEC_SEQ, W_D), F32),
        ],
        compiler_params=_params(("parallel",), 48),
        name="odd_sample",
    )(*args)


def _rope_tables(pos):
    half = C_DK // 2
    freq = ROPE_BASE ** (-np.arange(half, dtype=np.float64) / half)
    ang = np.asarray(pos, np.float64)[:, None] * freq
    cos, sin = np.cos(ang), np.sin(ang)
    cos_l = np.tile(np.concatenate([cos, cos], axis=-1), (1, C_HEADS))
    sin_l = np.tile(np.concatenate([-sin, sin], axis=-1), (1, C_HEADS))
    return cos_l.astype(np.float32), sin_l.astype(np.float32)


def _retention_tables(length):
    log_g = np.log1p(-np.exp2(-5.0 - np.arange(C_HEADS, dtype=np.float64)))
    idx = np.arange(length, dtype=np.float64)
    rel = idx[:, None] - idx[None, :]
    decay = np.where(rel >= 0, np.exp(log_g[:, None, None] * np.maximum(rel, 0.0)), 0.0)
    cross = np.exp(log_g[None, :] * (idx[:, None] + 1.0))
    kdec = np.exp(log_g[None, :] * (length - 1.0 - idx[:, None]))
    sdec = np.exp(log_g * length)
    tabs = (decay,
            np.repeat(cross, C_DV, axis=1),
            np.repeat(kdec, C_DK, axis=1),
            np.broadcast_to(sdec[:, None, None], (C_HEADS, C_DK, C_DV)))
    return tuple(np.ascontiguousarray(t, dtype=np.float32) for t in tabs)


def _block_diag_halves(wa, wx):
    nb = A_BLOCKS // 2
    width = nb * A_BLOCK

    def bd(w, hh):
        rows = [jnp.pad(w[:, hh * nb + g], ((0, 0), (0, 0), (g * A_BLOCK, width - (g + 1) * A_BLOCK)))
                for g in range(nb)]
        return jnp.concatenate(rows, axis=1)

    halves = [jnp.concatenate([bd(wa, hh), bd(wx, hh)], axis=-1) for hh in range(2)]
    return jnp.stack(halves, axis=1).astype(BF16)


def kernel(x_prompt, x_sample, state_lru_h, state_lru_conv, state_sconv, state_ret, w_in_even, conv_a_w, conv_a_b, lru_wa, lru_ba, lru_wx, lru_bx, lru_lam, conv_b_w, w_out_even, w_in_odd, ret_gn_g, sp_w, sp_b, gm_ln_g, gm_ln_b, w_out_odd, ffn_w1, ffn_w3, ffn_w2, ln1_g, ln1_b, ln2_g, ln2_b):
    xp = x_prompt.reshape(BATCH * SEQ, D_MODEL)
    xs = jnp.swapaxes(x_sample, 0, 1).reshape(DEC_SEQ * DEC_BATCH, D_MODEL)

    cos_p, sin_p = _rope_tables(np.arange(SEQ))
    cos_s, sin_s = _rope_tables(PAST_LEN + np.arange(DEC_SEQ))
    tab_p = _retention_tables(RET_CHUNK)
    tab_s = _retention_tables(DEC_SEQ)

    w_in = w_in_even[0].astype(BF16)
    ln1, ln2 = (ln1_g, ln1_b), (ln2_g, ln2_b)
    even_p = (conv_a_w, conv_a_b, _block_diag_halves(lru_wa, lru_wx), lru_ba, lru_bx, lru_lam, conv_b_w)
    ca_tm = jnp.transpose(state_lru_conv, (1, 2, 0, 3))
    sc_tm = jnp.transpose(state_sconv, (1, 2, 0, 3))
    h0_tm = jnp.swapaxes(state_lru_h, 0, 1)
    tril = jnp.tril(sp_w[:, :, :DEC_SEQ, :DEC_SEQ])
    spw_rows = jnp.repeat(jnp.transpose(tril, (0, 3, 2, 1)), D_GROUP, axis=3)
    spb_rows = jnp.repeat(jnp.swapaxes(sp_b[:, :, :DEC_SEQ], 1, 2), D_GROUP, axis=2)
    spb_t = jnp.swapaxes(sp_b, 1, 2)
    odd_small = (ret_gn_g,)
    odd_ln = (gm_ln_g, gm_ln_b)

    h_p, h_s, ca_p, ca_s, sc_p, sc_s, r_p, v_s = [], [], [], [], [], [], [], []
    ret_sample = None
    for l in range(DEPTH):
        if l % 2 == 0:
            e = l // 2
            casts = [(w_out_even, e), (ffn_w1, l), (ffn_w3, l), (ffn_w2, l), (w_in_odd, e)]
            yp, hlp, cap, scp, wo, w1, w3, w2, w_in_next = _even_prompt(xp.reshape(BATCH, SEQ, D_MODEL), w_in, e,
                                                                        even_p, casts)
            proj_s = _in_proj(xs, w_in).reshape(DEC_SEQ, DEC_BATCH, EVEN_IN)
            ys, hls, cas, scs = _even_sample(proj_s, ca_tm, sc_tm, h0_tm, e, even_p)
            h_p.append(hlp)
            h_s.append(hls)
            ca_p.append(cap[:, SUBLANES - (A_CONV - 1):])
            ca_s.append(jnp.swapaxes(cas, 0, 1))
            sc_p.append(scp[:, SUBLANES - (B_CONV - 1):])
            sc_s.append(jnp.swapaxes(scs, 0, 1))
            yp = yp.reshape(BATCH * SEQ, D_MODEL)
            ys = ys.reshape(DEC_SEQ * DEC_BATCH, D_MODEL)
        else:
            o = l // 2
            casts = [(w_out_odd, o), (ffn_w1, l), (ffn_w3, l), (ffn_w2, l)]
            if l + 1 < DEPTH:
                casts.append((w_in_even, o + 1))
            yp, rp, wo, w1, w3, w2, *w_in_next = _odd_prompt(xp.reshape(BATCH, SEQ, D_MODEL), w_in, o, cos_p, sin_p,
                                                             tab_p, *odd_small, sp_w, spb_t, *odd_ln, casts)
            w_in_next = w_in_next[0] if w_in_next else None
            proj_s = _in_proj(xs, w_in)
            ys, ret_sample, vs = _odd_sample(proj_s, state_ret, o, cos_s, sin_s, tab_s, *odd_small,
                                             spw_rows, spb_rows, *odd_ln, ret_sample)
            r_p.append(rp)
            v_s.append(vs)
            yp = yp.reshape(BATCH * SEQ, D_MODEL)
        xp, xs = _dense_block(yp, xp, ys, xs, l, wo, *ln1, w1, w3, w2, *ln2)
        w_in = w_in_next
        if l + 1 < DEPTH:
            if l % 2 == 0:
                xs = jnp.swapaxes(xs.reshape(DEC_SEQ, DEC_BATCH, D_MODEL), 0, 1).reshape(DEC_BATCH * DEC_SEQ, D_MODEL)
            else:
                xs = jnp.swapaxes(xs.reshape(DEC_BATCH, DEC_SEQ, D_MODEL), 0, 1).reshape(DEC_SEQ * DEC_BATCH, D_MODEL)

    return (xp.reshape(BATCH, SEQ, D_MODEL), xs.reshape(DEC_BATCH, DEC_SEQ, D_MODEL),
            jnp.stack(h_p, axis=1), jnp.stack(h_s, axis=1),
            jnp.stack(ca_p, axis=1), jnp.stack(ca_s, axis=1),
            jnp.stack(sc_p, axis=1), jnp.stack(sc_s, axis=1),
            jnp.stack(r_p, axis=1), ret_sample, jnp.stack(v_s, axis=1))
```

```python
import functools

import jax
import jax.numpy as jnp
import numpy as np
from jax import lax
from jax.experimental import pallas as pl
from jax.experimental.pallas import tpu as pltpu

F32 = jnp.float32
BF16 = jnp.bfloat16

D_MODEL = 1024
BATCH = 8
SEQ = 2048
DEPTH = 4
DEC_BATCH = 128
DEC_SEQ = 4
PAST_LEN = 16384
N_EVEN = (DEPTH + 1) // 2
N_ODD = DEPTH // 2
W_A = 512
A_BLOCKS = 8
A_BLOCK = 64
A_CONV = 4
LRU_C = 8.0
W_B = 512
B_CONV = 3
C_HEADS = 4
C_DK = 64
C_DV = 128
W_CK = C_HEADS * C_DK
W_CV = C_HEADS * C_DV
RET_CHUNK = 128
ROPE_BASE = 10000.0
D_GROUPS = 4
D_CHUNK = 128
W_D = 512
D_GROUP = W_D // D_GROUPS
EVEN_IN = 2 * W_A + 3 * W_B
ODD_IN = 2 * W_CK + 2 * W_CV + 2 * W_D
D_FF = 2816
ALPHA = (2 * DEPTH) ** 0.25
LN_EPS = 1e-5

SUBLANES = 8
LANES = 128
MIB = 1024 * 1024

ROW_TILE = 512
DENSE_TILE = 512
DENSE_GROUP = 256
DENSE_COLS = 256
EVEN_TT = 128
EVEN_GROUP = 1
ODD_GROUP = 2
EVEN_PITCH = EVEN_TT + SUBLANES
SAMPLE_NB = 32


def _params(sem, vmem_mib):
    return pltpu.CompilerParams(dimension_semantics=sem, vmem_limit_bytes=vmem_mib * MIB)


def _const_spec(shape, layer=None):
    nd = len(shape)
    if layer is None:
        return pl.BlockSpec(shape, lambda *_: (0,) * nd, pipeline_mode=pl.Buffered(1))
    return pl.BlockSpec((None,) + tuple(shape), lambda *_: (layer,) + (0,) * nd, pipeline_mode=pl.Buffered(1))


def _layer_rows(layer, *refs):
    return [r.at[layer:layer + 1] for r in refs]


def _layer_norm(x, g, b):
    mu = jnp.mean(x, -1, keepdims=True)
    xc = x - mu
    var = jnp.mean(xc * xc, -1, keepdims=True)
    return xc * lax.rsqrt(var + LN_EPS) * g + b


def _softplus(x):
    return jnp.maximum(x, 0.0) + jnp.log1p(jnp.exp(-jnp.abs(x)))


def _dot(a, b):
    return jnp.dot(a, b, preferred_element_type=F32)


def _cast_specs(stacks_and_layers, nsteps):
    in_specs, out_specs, out_shapes = [], [], []
    for w, layer in stacks_and_layers:
        nrow, ncol = w.shape[1:]
        rb = nrow // nsteps
        assert nrow % nsteps == 0 and rb % (2 * SUBLANES) == 0
        in_specs.append(pl.BlockSpec((None, rb, ncol), lambda j, layer=layer: (layer, j, 0)))
        out_specs.append(pl.BlockSpec((rb, ncol), lambda j: (j, 0)))
        out_shapes.append(jax.ShapeDtypeStruct((nrow, ncol), BF16))
    return in_specs, out_specs, out_shapes


def _cast_slabs(src_refs, dst_refs):
    for src, dst in zip(src_refs, dst_refs):
        dst[...] = src[...].astype(BF16)


def _proj_kernel(x_ref, w_ref, o_ref):
    o_ref[...] = _dot(x_ref[...].astype(BF16), w_ref[...])


def _in_proj(x2d, w):
    m, k = x2d.shape
    n = w.shape[1]
    tm = min(ROW_TILE, m)
    return pl.pallas_call(
        _proj_kernel,
        grid=(m // tm,),
        in_specs=[pl.BlockSpec((tm, k), lambda i: (i, 0)), _const_spec((k, n))],
        out_specs=pl.BlockSpec((tm, n), lambda i: (i, 0)),
        out_shape=jax.ShapeDtypeStruct((m, n), F32),
        compiler_params=_params(("parallel",), 40),
        name="in_proj",
    )(x2d, w)


def _dense_kernel(yp_ref, xp_ref, ys_ref, xs_ref, wo_ref, g1_ref, b1_ref, w1_ref, w3_ref, w2_ref, g2_ref, b2_ref,
                  op_ref, os_ref, h_s, *, layer):
    n_prompt = pl.num_programs(0) - 1
    g1_ref, b1_ref, g2_ref, b2_ref = _layer_rows(layer, g1_ref, b1_ref, g2_ref, b2_ref)
    weights = (wo_ref, g1_ref, b1_ref, w1_ref, w3_ref, w2_ref, g2_ref, b2_ref)

    @pl.when(pl.program_id(0) < n_prompt)
    def _():
        _dense_rows(yp_ref, xp_ref, *weights, op_ref, h_s)

    @pl.when(pl.program_id(0) == n_prompt)
    def _():
        _dense_rows(ys_ref, xs_ref, *weights, os_ref, h_s)


def _dense_rows(y_ref, x_ref, wo_ref, g1_ref, b1_ref, w1_ref, w3_ref, w2_ref, g2_ref, b2_ref, o_ref, h_s):
    rows, ncols = DENSE_GROUP, DENSE_COLS
    nparts = y_ref.shape[0] // rows
    sls = [slice(p * rows, (p + 1) * rows) for p in range(nparts)]
    mix, x1, xb = {}, {}, {}

    def out_proj(p):
        mix[p] = _dot(y_ref[sls[p], :], wo_ref[...])

    def norm1(p):
        x1[p] = _layer_norm(ALPHA * x_ref[sls[p], :] + mix[p], g1_ref[...], b1_ref[...])
        xb[p] = x1[p].astype(BF16)

    def gate_up(p):
        for c in range(0, D_FF, ncols):
            cols = slice(c, c + ncols)
            h_s[sls[p], cols] = (jax.nn.silu(_dot(xb[p], w1_ref[:, cols]))
                                 * _dot(xb[p], w3_ref[:, cols])).astype(BF16)

    def down_norm2(p):
        f = _dot(h_s[sls[p], :], w2_ref[...])
        o_ref[sls[p], :] = _layer_norm(ALPHA * x1[p] + f, g2_ref[...], b2_ref[...])

    out_proj(0)
    for p in range(nparts):
        norm1(p)
        if p + 1 < nparts:
            out_proj(p + 1)
    for p in range(nparts):
        gate_up(p)
    for p in range(nparts):
        down_norm2(p)


def _dense_block(yp, xp, ys, xs, layer, wo, g1, b1, w1, w3, w2, g2, b2):
    tm = DENSE_TILE
    n_prompt = xp.shape[0] // tm
    assert xp.shape[0] % tm == 0 and xs.shape[0] == tm and tm % DENSE_GROUP == 0
    prompt_row = lambda i: (jnp.minimum(i, n_prompt - 1), 0)
    sample_row = lambda i: (0, 0)
    return pl.pallas_call(
        functools.partial(_dense_kernel, layer=layer),
        grid=(n_prompt + 1,),
        in_specs=[
            pl.BlockSpec((tm, D_MODEL), prompt_row),
            pl.BlockSpec((tm, D_MODEL), prompt_row),
            pl.BlockSpec((tm, D_MODEL), sample_row),
            pl.BlockSpec((tm, D_MODEL), sample_row),
            _const_spec((D_MODEL, D_MODEL)),
            _const_spec((DEPTH, D_MODEL)),
            _const_spec((DEPTH, D_MODEL)),
            _const_spec((D_MODEL, D_FF)),
            _const_spec((D_MODEL, D_FF)),
            _const_spec((D_FF, D_MODEL)),
            _const_spec((DEPTH, D_MODEL)),
            _const_spec((DEPTH, D_MODEL)),
        ],
        out_specs=[pl.BlockSpec((tm, D_MODEL), prompt_row), pl.BlockSpec((tm, D_MODEL), sample_row)],
        out_shape=[jax.ShapeDtypeStruct(xp.shape, F32), jax.ShapeDtypeStruct(xs.shape, F32)],
        scratch_shapes=[pltpu.VMEM((tm, D_FF), BF16)],
        compiler_params=_params(("arbitrary",), 58),
        name="dense_block",
    )(yp, xp, ys, xs, wo, g1, b1, w1, w3, w2, g2, b2)


def _lru_gates(xc, wblk_ref, ba, bx, sp):
    half = W_A // 2
    xb = xc.astype(BF16)
    pre0 = _dot(xb[:, :half], wblk_ref[0])
    pre1 = _dot(xb[:, half:], wblk_ref[1])
    r = jax.nn.sigmoid(jnp.concatenate([pre0[:, :half], pre1[:, :half]], axis=1) + ba)
    i = jax.nn.sigmoid(jnp.concatenate([pre0[:, half:], pre1[:, half:]], axis=1) + bx)
    log_a = (-LRU_C) * r * sp
    a = jnp.exp(log_a)
    mult = jnp.sqrt((1.0 + a * a) * jnp.tanh(-log_a))
    return a, mult, i


def _causal_taps(x_slabs, ext, b, w_ref, nrows):
    ktaps = w_ref.shape[0]
    out = []
    for s, x in enumerate(x_slabs):
        lanes = slice(s * LANES, (s + 1) * LANES)
        ext[s, b, SUBLANES:SUBLANES + nrows, :] = x
        acc = w_ref[ktaps - 1:ktaps, lanes] * x
        for k in range(ktaps - 1):
            off = SUBLANES - (ktaps - 1) + k
            acc = acc + w_ref[k:k + 1, lanes] * ext[s, b, off:off + nrows, :]
        out.append(acc)
    return jnp.concatenate(out, axis=1)


def _even_prompt_kernel(x_ref, w_ref, caw_ref, cab_ref, wblk_ref, ba_ref, bx_ref, lam_ref, cbw_ref, *rest,
                        ncast, layer):
    cab_ref, ba_ref, bx_ref, lam_ref = _layer_rows(layer, cab_ref, ba_ref, bx_ref, lam_ref)
    cast_src, rest = rest[:ncast], rest[ncast:]
    y_ref, hlast_ref, ca_out_ref, sc_out_ref = rest[:4]
    cast_dst, rest = rest[4:4 + ncast], rest[4 + ncast:]
    p0, p1, xa_ext, cb_ext, a_s, b_s, g_s, h_c = rest
    tt, pitch = EVEN_TT, EVEN_PITCH
    nslab = W_A // LANES
    j = pl.program_id(0)
    _cast_slabs(cast_src, cast_dst)
    bufs = (p0, p1)
    group = p0.shape[0] // tt

    def project(g):
        xg = x_ref[g * group:(g + 1) * group].reshape(group * tt, D_MODEL)
        bufs[g % 2][...] = _dot(xg.astype(BF16), w_ref[...])

    @pl.when(j == 0)
    def _():
        xa_ext[:, :, 0:SUBLANES, :] = jnp.zeros((nslab, BATCH, SUBLANES, LANES), F32)
        cb_ext[:, :, 0:SUBLANES, :] = jnp.zeros((nslab, BATCH, SUBLANES, LANES), F32)
        h_c[...] = jnp.zeros(h_c.shape, F32)

    @pl.when(j > 0)
    def _():
        xa_ext[:, :, 0:SUBLANES, :] = xa_ext[:, :, tt:tt + SUBLANES, :]
        cb_ext[:, :, 0:SUBLANES, :] = cb_ext[:, :, tt:tt + SUBLANES, :]

    sp = _softplus(-lam_ref[...])
    ba = ba_ref[...]
    bx = bx_ref[...]
    cab = cab_ref[...]
    first_row = (lax.broadcasted_iota(jnp.int32, (SUBLANES, 1), 0) == 0) & (j == 0)
    cg0 = 2 * W_A + W_B
    xb0 = 2 * W_A + 2 * W_B

    def stage(b):
        p_ref = bufs[(b // group) % 2].at[(b % group) * tt:(b % group + 1) * tt]
        xc = cab + _causal_taps([p_ref[:, s * LANES:(s + 1) * LANES] for s in range(nslab)], xa_ext, b, caw_ref, tt)
        a, mult, gate = _lru_gates(xc, wblk_ref, ba, bx, sp)
        mult = jnp.concatenate([jnp.where(first_row, 1.0, mult[:SUBLANES]), mult[SUBLANES:]], axis=0)
        bb = mult * (gate * xc)
        for s in range(nslab):
            a_s[s, b * pitch:b * pitch + tt, :] = a[:, s * LANES:(s + 1) * LANES]
            b_s[s, b * pitch:b * pitch + tt, :] = bb[:, s * LANES:(s + 1) * LANES]
        g_s[b] = jax.nn.gelu(p_ref[:, W_A:2 * W_A])
        cb = [p_ref[:, cg0 + s * LANES:cg0 + (s + 1) * LANES] * p_ref[:, xb0 + s * LANES:xb0 + (s + 1) * LANES]
              for s in range(nslab)]
        zb = _causal_taps(cb, cb_ext, b, cbw_ref, tt)
        y_ref[b, :, W_A:] = (p_ref[:, 2 * W_A:2 * W_A + W_B] * zb).astype(BF16)

    project(0)
    for g in range(BATCH // group):
        if g + 1 < BATCH // group:
            project(g + 1)
        for b in range(g * group, (g + 1) * group):
            stage(b)

    def step(t, hs):
        out = []
        for s in range(nslab):
            rows = pl.ds(t, BATCH, stride=pitch)
            hn = a_s[s, rows, :] * hs[s] + b_s[s, rows, :]
            b_s[s, rows, :] = hn
            out.append(hn)
        return tuple(out)

    hs = lax.fori_loop(0, tt, step, tuple(h_c[s] for s in range(nslab)), unroll=4)
    for s in range(nslab):
        h_c[s] = hs[s]

    def emit(b, carry):
        row0 = pl.multiple_of(b * pitch, SUBLANES)
        h = jnp.concatenate([b_s[s, pl.ds(row0, tt), :] for s in range(nslab)], axis=1)
        y_ref[b, :, 0:W_A] = (h * g_s[b]).astype(BF16)
        return carry

    lax.fori_loop(0, BATCH, emit, 0)

    @pl.when(j == pl.num_programs(0) - 1)
    def _():
        hlast_ref[...] = jnp.concatenate(list(hs), axis=1)
        for s in range(nslab):
            ca_out_ref[:, :, s * LANES:(s + 1) * LANES] = xa_ext[s, :, tt:tt + SUBLANES, :]
            sc_out_ref[:, :, s * LANES:(s + 1) * LANES] = cb_ext[s, :, tt:tt + SUBLANES, :]


def _even_param_specs(e):
    rows = _const_spec((N_EVEN, W_A))
    return [
        _const_spec((A_CONV, W_A), e),
        rows,
        _const_spec((2, W_A // 2, W_A), e),
        rows,
        rows,
        rows,
        _const_spec((B_CONV, W_B), e),
    ]


def _even_prompt(x3d, w, e, params, casts):
    tt, pitch = EVEN_TT, EVEN_PITCH
    nslab = W_A // LANES
    nblock = SEQ // tt
    assert W_A == W_B
    cast_in, cast_out, cast_shapes = _cast_specs(casts, nblock)
    return pl.pallas_call(
        functools.partial(_even_prompt_kernel, ncast=len(casts), layer=e),
        grid=(nblock,),
        in_specs=[pl.BlockSpec((BATCH, tt, D_MODEL), lambda j: (0, j, 0)),
                  _const_spec((D_MODEL, EVEN_IN))] + _even_param_specs(e) + cast_in,
        out_specs=[
            pl.BlockSpec((BATCH, tt, D_MODEL), lambda j: (0, j, 0)),
            pl.BlockSpec((BATCH, W_A), lambda j: (0, 0)),
            pl.BlockSpec((BATCH, SUBLANES, W_A), lambda j: (0, 0, 0)),
            pl.BlockSpec((BATCH, SUBLANES, W_B), lambda j: (0, 0, 0)),
        ] + cast_out,
        out_shape=[
            jax.ShapeDtypeStruct((BATCH, SEQ, D_MODEL), BF16),
            jax.ShapeDtypeStruct((BATCH, W_A), F32),
            jax.ShapeDtypeStruct((BATCH, SUBLANES, W_A), F32),
            jax.ShapeDtypeStruct((BATCH, SUBLANES, W_B), F32),
        ] + cast_shapes,
        scratch_shapes=[
            pltpu.VMEM((EVEN_GROUP * tt, EVEN_IN), F32),
            pltpu.VMEM((EVEN_GROUP * tt, EVEN_IN), F32),
            pltpu.VMEM((nslab, BATCH, tt + SUBLANES, LANES), F32),
            pltpu.VMEM((nslab, BATCH, tt + SUBLANES, LANES), F32),
            pltpu.VMEM((nslab, BATCH * pitch, LANES), F32),
            pltpu.VMEM((nslab, BATCH * pitch, LANES), F32),
            pltpu.VMEM((BATCH, tt, W_A), F32),
            pltpu.VMEM((nslab, BATCH, LANES), F32),
        ],
        compiler_params=_params(("arbitrary",), 52),
        name="even_prompt",
    )(x3d, w, *params, *[stack for stack, _ in casts])


def _even_sample_kernel(proj_ref, ca_ref, sc_ref, h0_ref, caw_ref, cab_ref, wblk_ref, ba_ref, bx_ref,
                        lam_ref, cbw_ref, y_ref, hlast_ref, ca_out_ref, sc_out_ref, *, layer):
    cab_ref, ba_ref, bx_ref, lam_ref = _layer_rows(layer, cab_ref, ba_ref, bx_ref, lam_ref)
    sp = _softplus(-lam_ref[...])
    rows_a = [ca_ref[k] for k in range(A_CONV - 1)] + [proj_ref[l, :, 0:W_A] for l in range(DEC_SEQ)]
    xc = []
    for l in range(DEC_SEQ):
        acc = cab_ref[...] + caw_ref[0:1, :] * rows_a[l]
        for k in range(1, A_CONV):
            acc = acc + caw_ref[k:k + 1, :] * rows_a[l + k]
        xc.append(acc)
    xc_all = jnp.concatenate(xc, axis=0)
    a, mult, gate = _lru_gates(xc_all, wblk_ref, ba_ref[...], bx_ref[...], sp)
    bb = mult * (gate * xc_all)
    h = h0_ref[...]
    rows_b = [sc_ref[k] for k in range(B_CONV - 1)]
    for l in range(DEC_SEQ):
        sl = slice(l * DEC_BATCH, (l + 1) * DEC_BATCH)
        h = a[sl] * h + bb[sl]
        y_ref[l, :, 0:W_A] = (h * jax.nn.gelu(proj_ref[l, :, W_A:2 * W_A])).astype(BF16)
        rows_b.append(proj_ref[l, :, 2 * W_A + W_B:2 * W_A + 2 * W_B] * proj_ref[l, :, 2 * W_A + 2 * W_B:])
    hlast_ref[...] = h
    for l in range(DEC_SEQ):
        zb = cbw_ref[0:1, :] * rows_b[l]
        for k in range(1, B_CONV):
            zb = zb + cbw_ref[k:k + 1, :] * rows_b[l + k]
        y_ref[l, :, W_A:] = (proj_ref[l, :, 2 * W_A:2 * W_A + W_B] * zb).astype(BF16)
    for k in range(A_CONV - 1):
        ca_out_ref[k] = rows_a[DEC_SEQ + k]
    for k in range(B_CONV - 1):
        sc_out_ref[k] = rows_b[DEC_SEQ + k]


def _even_sample(proj_tm, ca_tm, sc_tm, h0_tm, e, params):
    whole = lambda shape: pl.BlockSpec(shape, lambda i: (0,) * len(shape))
    return pl.pallas_call(
        functools.partial(_even_sample_kernel, layer=e),
        grid=(1,),
        in_specs=[
            whole((DEC_SEQ, DEC_BATCH, EVEN_IN)),
            _const_spec((A_CONV - 1, DEC_BATCH, W_A), e),
            _const_spec((B_CONV - 1, DEC_BATCH, W_B), e),
            _const_spec((DEC_BATCH, W_A), e),
        ] + _even_param_specs(e),
        out_specs=[
            whole((DEC_SEQ, DEC_BATCH, D_MODEL)),
            whole((DEC_BATCH, W_A)),
            whole((A_CONV - 1, DEC_BATCH, W_A)),
            whole((B_CONV - 1, DEC_BATCH, W_B)),
        ],
        out_shape=[
            jax.ShapeDtypeStruct((DEC_SEQ, DEC_BATCH, D_MODEL), BF16),
            jax.ShapeDtypeStruct((DEC_BATCH, W_A), F32),
            jax.ShapeDtypeStruct((A_CONV - 1, DEC_BATCH, W_A), F32),
            jax.ShapeDtypeStruct((B_CONV - 1, DEC_BATCH, W_B), F32),
        ],
        compiler_params=_params(("arbitrary",), 40),
        name="even_sample",
    )(proj_tm, ca_tm, sc_tm, h0_tm, *params)


def _rope(x, cos, sin_signed):
    half = C_DK // 2
    ax = x.ndim - 1
    pieces = []
    for p in range(W_CK // LANES):
        xv = x[..., p * LANES:(p + 1) * LANES]
        lane = lax.broadcasted_iota(jnp.int32, xv.shape, ax)
        partner = jnp.where((lane % C_DK) < half, pltpu.roll(xv, LANES - half, axis=ax), pltpu.roll(xv, half, axis=ax))
        pieces.append(xv * cos[..., p * LANES:(p + 1) * LANES] + partner * sin_signed[..., p * LANES:(p + 1) * LANES])
    return jnp.concatenate(pieces, axis=ax)


def _group_norm(o):
    mu = jnp.mean(o, -1, keepdims=True)
    oc = o - mu
    var = jnp.mean(oc * oc, -1, keepdims=True)
    return oc * lax.rsqrt(var + LN_EPS)


def _odd_chunk(p_ref, b, cos_ref, sin_ref, dec_ref, cs_ref, kd_ref, sdec_ref, gn_ref,
               spw_ref, spb_ref, lng_ref, lnb_ref, y_ref, s_c, s_bd):
    q = _rope(p_ref[:, 0:W_CK], cos_ref[...], sin_ref[...])
    k = _rope(p_ref[:, W_CK:2 * W_CK], cos_ref[...], sin_ref[...]) * (C_DK ** -0.5)
    kd = k * kd_ref[...]
    v0 = 2 * W_CK
    g0 = v0 + W_CV
    cross = _dot(q.astype(BF16), s_bd[b]) * cs_ref[...]
    for h in range(C_HEADS):
        qh = q[:, h * C_DK:(h + 1) * C_DK].astype(BF16)
        kh = k[:, h * C_DK:(h + 1) * C_DK].astype(BF16)
        kdh = kd[:, h * C_DK:(h + 1) * C_DK].astype(BF16)
        vh = p_ref[:, v0 + h * C_DV:v0 + (h + 1) * C_DV].astype(BF16)
        scores = lax.dot_general(qh, kh, (((1,), (1,)), ((), ())), preferred_element_type=F32) * dec_ref[h]
        o = _dot(scores.astype(BF16), vh) + cross[:, h * C_DV:(h + 1) * C_DV]
        s_new = sdec_ref[h] * s_c[b, h] + lax.dot_general(kdh, vh, (((0,), (0,)), ((), ())), preferred_element_type=F32)
        s_c[b, h] = s_new
        s_bd[b, h * C_DK:(h + 1) * C_DK, h * C_DV:(h + 1) * C_DV] = s_new.astype(BF16)
        gate = jax.nn.silu(p_ref[:, g0 + h * C_DV:g0 + (h + 1) * C_DV])
        y_ref[b, :, h * C_DV:(h + 1) * C_DV] = (gate * (_group_norm(o) * gn_ref[:, h * C_DV:(h + 1) * C_DV])).astype(BF16)

    u0 = g0 + W_CV
    u = jax.nn.gelu(p_ref[:, u0:u0 + W_D])
    vd = _layer_norm(jax.nn.gelu(p_ref[:, u0 + W_D:]), lng_ref[...], lnb_ref[...]).astype(BF16)
    ri = lax.broadcasted_iota(jnp.int32, (D_CHUNK, D_CHUNK), 0)
    ci = lax.broadcasted_iota(jnp.int32, (D_CHUNK, D_CHUNK), 1)
    zero = jnp.zeros((D_CHUNK, D_GROUP), BF16)
    w_cat = jnp.concatenate([jnp.where(ri >= ci, spw_ref[gi], 0.0).astype(BF16) for gi in range(D_GROUPS)], axis=1)
    vd_bd = jnp.concatenate(
        [jnp.concatenate([vd[:, gi * D_GROUP:(gi + 1) * D_GROUP] if gj == gi else zero for gj in range(D_GROUPS)], axis=1)
         for gi in range(D_GROUPS)], axis=0)
    bias = jnp.concatenate([jnp.broadcast_to(spb_ref[:, gi:gi + 1], (D_CHUNK, D_GROUP)) for gi in range(D_GROUPS)], axis=1)
    y_ref[b, :, W_CV:] = (u * (_dot(w_cat, vd_bd) + bias)).astype(BF16)


def _odd_prompt_kernel(x_ref, w_ref, cos_ref, sin_ref, dec_ref, cs_ref, kd_ref, sdec_ref, gn_ref,
                       spw_ref, spb_ref, lng_ref, lnb_ref, *rest, ncast, layer):
    gn_ref, lng_ref, lnb_ref = _layer_rows(layer, gn_ref, lng_ref, lnb_ref)
    cast_src, rest = rest[:ncast], rest[ncast:]
    y_ref, s_out_ref = rest[:2]
    cast_dst, rest = rest[2:2 + ncast], rest[2 + ncast:]
    p0, p1, s_c, s_bd = rest
    c = pl.program_id(0)
    _cast_slabs(cast_src, cast_dst)

    @pl.when(c == 0)
    def _():
        s_c[...] = jnp.zeros(s_c.shape, F32)
        s_bd[...] = jnp.zeros(s_bd.shape, BF16)

    bufs = (p0, p1)
    group = p0.shape[0] // RET_CHUNK

    def project(g):
        xg = x_ref[g * group:(g + 1) * group].reshape(group * RET_CHUNK, D_MODEL)
        bufs[g % 2][...] = _dot(xg.astype(BF16), w_ref[...])

    project(0)
    for g in range(BATCH // group):
        if g + 1 < BATCH // group:
            project(g + 1)
        for r in range(group):
            _odd_chunk(bufs[g % 2].at[r * RET_CHUNK:(r + 1) * RET_CHUNK], g * group + r,
                       cos_ref, sin_ref, dec_ref, cs_ref, kd_ref, sdec_ref, gn_ref,
                       spw_ref, spb_ref, lng_ref, lnb_ref, y_ref, s_c, s_bd)

    @pl.when(c == pl.num_programs(0) - 1)
    def _():
        s_out_ref[...] = s_c[...]


def _odd_prompt(x3d, w, o, cos, sin, tabs, gn, spw, spb_t, lng, lnb, casts):
    nchunk = SEQ // RET_CHUNK
    dec, cs, kd, sdec = tabs
    cast_in, cast_out, cast_shapes = _cast_specs(casts, nchunk)
    return pl.pallas_call(
        functools.partial(_odd_prompt_kernel, ncast=len(casts), layer=o),
        grid=(nchunk,),
        in_specs=[
            pl.BlockSpec((BATCH, RET_CHUNK, D_MODEL), lambda c: (0, c, 0)),
            _const_spec((D_MODEL, ODD_IN)),
            pl.BlockSpec((RET_CHUNK, W_CK), lambda c: (c, 0)),
            pl.BlockSpec((RET_CHUNK, W_CK), lambda c: (c, 0)),
            _const_spec((C_HEADS, RET_CHUNK, RET_CHUNK)),
            _const_spec((RET_CHUNK, W_CV)),
            _const_spec((RET_CHUNK, W_CK)),
            _const_spec((C_HEADS, C_DK, C_DV)),
            _const_spec((N_ODD, W_CV)),
            _const_spec((D_GROUPS, D_CHUNK, D_CHUNK), o),
            _const_spec((D_CHUNK, D_GROUPS), o),
            _const_spec((N_ODD, W_D)),
            _const_spec((N_ODD, W_D)),
        ] + cast_in,
        out_specs=[
            pl.BlockSpec((BATCH, RET_CHUNK, D_MODEL), lambda c: (0, c, 0)),
            pl.BlockSpec((BATCH, C_HEADS, C_DK, C_DV), lambda c: (0, 0, 0, 0)),
        ] + cast_out,
        out_shape=[
            jax.ShapeDtypeStruct((BATCH, SEQ, D_MODEL), BF16),
            jax.ShapeDtypeStruct((BATCH, C_HEADS, C_DK, C_DV), F32),
        ] + cast_shapes,
        scratch_shapes=[
            pltpu.VMEM((ODD_GROUP * RET_CHUNK, ODD_IN), F32),
            pltpu.VMEM((ODD_GROUP * RET_CHUNK, ODD_IN), F32),
            pltpu.VMEM((BATCH, C_HEADS, C_DK, C_DV), F32),
            pltpu.VMEM((BATCH, W_CK, W_CV), BF16),
        ],
        compiler_params=_params(("arbitrary",), 52),
        name="odd_prompt",
    )(x3d, w, cos, sin, dec, cs, kd, sdec, gn, spw, spb_t, lng, lnb, *[stack for stack, _ in casts])


def _odd_sample_kernel(proj_ref, s0_ref, cos_ref, sin_ref, dec_ref, cs_ref, kd_ref, sdec_ref, gn_ref,
                       spw_ref, spb_ref, lng_ref, lnb_ref, *rest, layer):
    gn_ref, lng_ref, lnb_ref = _layer_rows(layer, gn_ref, lng_ref, lnb_ref)
    y_ref, s_out_ref, vd_ref = rest[-3:]
    if len(rest) > 3:
        s_out_ref[:, 0] = rest[0][...]
        s_out_ref = s_out_ref.at[:, 1]
    nb = s0_ref.shape[0]

    def cols(a, b):
        return proj_ref[:, a:b].reshape(nb, DEC_SEQ, b - a)

    def put(a, b, val):
        y_ref[:, a:b] = val.reshape(nb * DEC_SEQ, b - a).astype(BF16)

    q = _rope(cols(0, W_CK), cos_ref[...], sin_ref[...])
    k = _rope(cols(W_CK, 2 * W_CK), cos_ref[...], sin_ref[...]) * (C_DK ** -0.5)
    kd = k * kd_ref[...]
    v0 = 2 * W_CK
    g0 = v0 + W_CV
    for h in range(C_HEADS):
        qh = q[:, :, h * C_DK:(h + 1) * C_DK].astype(BF16)
        kh = k[:, :, h * C_DK:(h + 1) * C_DK].astype(BF16)
        kdh = kd[:, :, h * C_DK:(h + 1) * C_DK].astype(BF16)
        vh = cols(v0 + h * C_DV, v0 + (h + 1) * C_DV).astype(BF16)
        s_prev = s0_ref[:, h]
        scores = jnp.einsum('nld,nmd->nlm', qh, kh, preferred_element_type=F32) * dec_ref[h]
        o = (jnp.einsum('nlm,nme->nle', scores.astype(BF16), vh, preferred_element_type=F32)
             + jnp.einsum('nld,nde->nle', qh, s_prev.astype(BF16), preferred_element_type=F32)
             * cs_ref[:, h * C_DV:(h + 1) * C_DV])
        s_out_ref[:, h] = sdec_ref[h] * s_prev + jnp.einsum('nld,nle->nde', kdh, vh, preferred_element_type=F32)
        gate = jax.nn.silu(cols(g0 + h * C_DV, g0 + (h + 1) * C_DV))
        put(h * C_DV, (h + 1) * C_DV, gate * (_group_norm(o) * gn_ref[:, h * C_DV:(h + 1) * C_DV]))

    u0 = g0 + W_CV
    u = jax.nn.gelu(cols(u0, u0 + W_D))
    vd = _layer_norm(jax.nn.gelu(cols(u0 + W_D, ODD_IN)), lng_ref[...], lnb_ref[...])
    vd_ref[...] = vd
    s = spb_ref[...] + spw_ref[0] * vd[:, 0:1, :]
    for m in range(1, DEC_SEQ):
        s = s + spw_ref[m] * vd[:, m:m + 1, :]
    put(W_CV, D_MODEL, u * s)


def _odd_sample(proj, state_ret, o, cos, sin, tabs, gn, spw_rows, spb_rows, lng, lnb, prev_states):
    nb = SAMPLE_NB
    dec, cs, kd, sdec = tabs
    seq3 = lambda i: (i, 0, 0)
    one_state = pl.BlockSpec((nb, C_HEADS, C_DK, C_DV), lambda i: (i, 0, 0, 0))
    in_specs = [
        pl.BlockSpec((nb * DEC_SEQ, ODD_IN), lambda i: (i, 0)),
        pl.BlockSpec((nb, None, C_HEADS, C_DK, C_DV), lambda i: (i, o, 0, 0, 0)),
        _const_spec((DEC_SEQ, W_CK)),
        _const_spec((DEC_SEQ, W_CK)),
        _const_spec((C_HEADS, DEC_SEQ, DEC_SEQ)),
        _const_spec((DEC_SEQ, W_CV)),
        _const_spec((DEC_SEQ, W_CK)),
        _const_spec((C_HEADS, C_DK, C_DV)),
        _const_spec((N_ODD, W_CV)),
        _const_spec((DEC_SEQ, DEC_SEQ, W_D), o),
        _const_spec((DEC_SEQ, W_D), o),
        _const_spec((N_ODD, W_D)),
        _const_spec((N_ODD, W_D)),
    ]
    args = [proj, state_ret, cos, sin, dec, cs, kd, sdec, gn, spw_rows, spb_rows, lng, lnb]
    if prev_states is None:
        state_spec = one_state
        state_shape = (DEC_BATCH, C_HEADS, C_DK, C_DV)
    else:
        in_specs.append(one_state)
        args.append(prev_states)
        state_spec = pl.BlockSpec((nb, N_ODD, C_HEADS, C_DK, C_DV), lambda i: (i, 0, 0, 0, 0))
        state_shape = (DEC_BATCH, N_ODD, C_HEADS, C_DK, C_DV)
    return pl.pallas_call(
        functools.partial(_odd_sample_kernel, layer=o),
        grid=(DEC_BATCH // nb,),
        in_specs=in_specs,
        out_specs=[pl.BlockSpec((nb * DEC_SEQ, D_MODEL), lambda i: (i, 0)), state_spec,
                   pl.BlockSpec((nb, DEC_SEQ, W_D), seq3)],
        out_shape=[
            jax.ShapeDtypeStruct((DEC_BATCH * DEC_SEQ, D_MODEL), BF16),
            jax.ShapeDtypeStruct(state_shape, F32),
            jax.ShapeDtypeStruct((DEC_BATCH, DEC_SEQ, W_D), F32),
        ],
        compiler_params=_params(("parallel",), 48),
        name="odd_sample",
    )(*args)


def _rope_tables(pos):
    half = C_DK // 2
    freq = ROPE_BASE ** (-np.arange(half, dtype=np.float64) / half)
    ang = np.asarray(pos, np.float64)[:, None] * freq
    cos, sin = np.cos(ang), np.sin(ang)
    cos_l = np.tile(np.concatenate([cos, cos], axis=-1), (1, C_HEADS))
    sin_l = np.tile(np.concatenate([-sin, sin], axis=-1), (1, C_HEADS))
    return cos_l.astype(np.float32), sin_l.astype(np.float32)


def _retention_tables(length):
    log_g = np.log1p(-np.exp2(-5.0 - np.arange(C_HEADS, dtype=np.float64)))
    idx = np.arange(length, dtype=np.float64)
    rel = idx[:, None] - idx[None, :]
    decay = np.where(rel >= 0, np.exp(log_g[:, None, None] * np.maximum(rel, 0.0)), 0.0)
    cross = np.exp(log_g[None, :] * (idx[:, None] + 1.0))
    kdec = np.exp(log_g[None, :] * (length - 1.0 - idx[:, None]))
    sdec = np.exp(log_g * length)
    tabs = (decay,
            np.repeat(cross, C_DV, axis=1),
            np.repeat(kdec, C_DK, axis=1),
            np.broadcast_to(sdec[:, None, None], (C_HEADS, C_DK, C_DV)))
    return tuple(np.ascontiguousarray(t, dtype=np.float32) for t in tabs)


def _block_diag_halves(wa, wx):
    nb = A_BLOCKS // 2
    width = nb * A_BLOCK

    def bd(w, hh):
        rows = [jnp.pad(w[:, hh * nb + g], ((0, 0), (0, 0), (g * A_BLOCK, width - (g + 1) * A_BLOCK)))
                for g in range(nb)]
        return jnp.concatenate(rows, axis=1)

    halves = [jnp.concatenate([bd(wa, hh), bd(wx, hh)], axis=-1) for hh in range(2)]
    return jnp.stack(halves, axis=1).astype(BF16)


def kernel(x_prompt, x_sample, state_lru_h, state_lru_conv, state_sconv, state_ret, w_in_even, conv_a_w, conv_a_b, lru_wa, lru_ba, lru_wx, lru_bx, lru_lam, conv_b_w, w_out_even, w_in_odd, ret_gn_g, sp_w, sp_b, gm_ln_g, gm_ln_b, w_out_odd, ffn_w1, ffn_w3, ffn_w2, ln1_g, ln1_b, ln2_g, ln2_b):
    xp = x_prompt.reshape(BATCH * SEQ, D_MODEL)
    xs = jnp.swapaxes(x_sample, 0, 1).reshape(DEC_SEQ * DEC_BATCH, D_MODEL)

    cos_p, sin_p = _rope_tables(np.arange(SEQ))
    cos_s, sin_s = _rope_tables(PAST_LEN + np.arange(DEC_SEQ))
    tab_p = _retention_tables(RET_CHUNK)
    tab_s = _retention_tables(DEC_SEQ)

    w_in = w_in_even[0].astype(BF16)
    ln1, ln2 = (ln1_g, ln1_b), (ln2_g, ln2_b)
    even_p = (conv_a_w, conv_a_b, _block_diag_halves(lru_wa, lru_wx), lru_ba, lru_bx, lru_lam, conv_b_w)
    ca_tm = jnp.transpose(state_lru_conv, (1, 2, 0, 3))
    sc_tm = jnp.transpose(state_sconv, (1, 2, 0, 3))
    h0_tm = jnp.swapaxes(state_lru_h, 0, 1)
    tril = jnp.tril(sp_w[:, :, :DEC_SEQ, :DEC_SEQ])
    spw_rows = jnp.repeat(jnp.transpose(tril, (0, 3, 2, 1)), D_GROUP, axis=3)
    spb_rows = jnp.repeat(jnp.swapaxes(sp_b[:, :, :DEC_SEQ], 1, 2), D_GROUP, axis=2)
    spb_t = jnp.swapaxes(sp_b, 1, 2)
    odd_small = (ret_gn_g,)
    odd_ln = (gm_ln_g, gm_ln_b)

    h_p, h_s, ca_p, ca_s, sc_p, sc_s, r_p, v_s = [], [], [], [], [], [], [], []
    ret_sample = None
    for l in range(DEPTH):
        if l % 2 == 0:
            e = l // 2
            casts = [(w_out_even, e), (ffn_w1, l), (ffn_w3, l), (ffn_w2, l), (w_in_odd, e)]
            yp, hlp, cap, scp, wo, w1, w3, w2, w_in_next = _even_prompt(xp.reshape(BATCH, SEQ, D_MODEL), w_in, e,
                                                                        even_p, casts)
            proj_s = _in_proj(xs, w_in).reshape(DEC_SEQ, DEC_BATCH, EVEN_IN)
            ys, hls, cas, scs = _even_sample(proj_s, ca_tm, sc_tm, h0_tm, e, even_p)
            h_p.append(hlp)
            h_s.append(hls)
            ca_p.append(cap[:, SUBLANES - (A_CONV - 1):])
            ca_s.append(jnp.swapaxes(cas, 0, 1))
            sc_p.append(scp[:, SUBLANES - (B_CONV - 1):])
            sc_s.append(jnp.swapaxes(scs, 0, 1))
            yp = yp.reshape(BATCH * SEQ, D_MODEL)
            ys = ys.reshape(DEC_SEQ * DEC_BATCH, D_MODEL)
        else:
            o = l // 2
            casts = [(w_out_odd, o), (ffn_w1, l), (ffn_w3, l), (ffn_w2, l)]
            if l + 1 < DEPTH:
                casts.append((w_in_even, o + 1))
            yp, rp, wo, w1, w3, w2, *w_in_next = _odd_prompt(xp.reshape(BATCH, SEQ, D_MODEL), w_in, o, cos_p, sin_p,
                                                             tab_p, *odd_small, sp_w, spb_t, *odd_ln, casts)
            w_in_next = w_in_next[0] if w_in_next else None
            proj_s = _in_proj(xs, w_in)
            ys, ret_sample, vs = _odd_sample(proj_s, state_ret, o, cos_s, sin_s, tab_s, *odd_small,
                                             spw_rows, spb_rows, *odd_ln, ret_sample)
            r_p.append(rp)
            v_s.append(vs)
            yp = yp.reshape(BATCH * SEQ, D_MODEL)
        xp, xs = _dense_block(yp, xp, ys, xs, l, wo, *ln1, w1, w3, w2, *ln2)
        w_in = w_in_next
        if l + 1 < DEPTH:
            if l % 2 == 0:
                xs = jnp.swapaxes(xs.reshape(DEC_SEQ, DEC_BATCH, D_MODEL), 0, 1).reshape(DEC_BATCH * DEC_SEQ, D_MODEL)
            else:
                xs = jnp.swapaxes(xs.reshape(DEC_BATCH, DEC_SEQ, D_MODEL), 0, 1).reshape(DEC_SEQ * DEC_BATCH, D_MODEL)

    return (xp.reshape(BATCH, SEQ, D_MODEL), xs.reshape(DEC_BATCH, DEC_SEQ, D_MODEL),
            jnp.stack(h_p, axis=1), jnp.stack(h_s, axis=1),
            jnp.stack(ca_p, axis=1), jnp.stack(ca_s, axis=1),
            jnp.stack(sc_p, axis=1), jnp.stack(sc_s, axis=1),
            jnp.stack(r_p, axis=1), ret_sample, jnp.stack(v_s, axis=1))
```

```python
import functools

import jax
import jax.numpy as jnp
import numpy as np
from jax import lax
from jax.experimental import pallas as pl
from jax.experimental.pallas import tpu as pltpu

F32 = jnp.float32
BF16 = jnp.bfloat16

D_MODEL = 1024
BATCH = 8
SEQ = 2048
DEPTH = 4
DEC_BATCH = 128
DEC_SEQ = 4
PAST_LEN = 16384
N_EVEN = (DEPTH + 1) // 2
N_ODD = DEPTH // 2
W_A = 512
A_BLOCKS = 8
A_BLOCK = 64
A_CONV = 4
LRU_C = 8.0
W_B = 512
B_CONV = 3
C_HEADS = 4
C_DK = 64
C_DV = 128
W_CK = C_HEADS * C_DK
W_CV = C_HEADS * C_DV
RET_CHUNK = 128
ROPE_BASE = 10000.0
D_GROUPS = 4
D_CHUNK = 128
W_D = 512
D_GROUP = W_D // D_GROUPS
EVEN_IN = 2 * W_A + 3 * W_B
ODD_IN = 2 * W_CK + 2 * W_CV + 2 * W_D
D_FF = 2816
ALPHA = (2 * DEPTH) ** 0.25
LN_EPS = 1e-5

SUBLANES = 8
LANES = 128
MIB = 1024 * 1024

ROW_TILE = 512
DENSE_TILE = 512
DENSE_GROUP = 256
DENSE_COLS = 256
EVEN_TT = 128
EVEN_GROUP = 1
ODD_GROUP = 2
EVEN_PITCH = EVEN_TT + SUBLANES
SAMPLE_NB = 32


def _params(sem, vmem_mib):
    return pltpu.CompilerParams(dimension_semantics=sem, vmem_limit_bytes=vmem_mib * MIB)


def _const_spec(shape, layer=None):
    nd = len(shape)
    if layer is None:
        return pl.BlockSpec(shape, lambda *_: (0,) * nd, pipeline_mode=pl.Buffered(1))
    return pl.BlockSpec((None,) + tuple(shape), lambda *_: (layer,) + (0,) * nd, pipeline_mode=pl.Buffered(1))


def _layer_rows(layer, *refs):
    return [r.at[layer:layer + 1] for r in refs]


def _layer_norm(x, g, b):
    mu = jnp.mean(x, -1, keepdims=True)
    xc = x - mu
    var = jnp.mean(xc * xc, -1, keepdims=True)
    return xc * lax.rsqrt(var + LN_EPS) * g + b


def _softplus(x):
    return jnp.maximum(x, 0.0) + jnp.log1p(jnp.exp(-jnp.abs(x)))


def _dot(a, b):
    return jnp.dot(a, b, preferred_element_type=F32)


def _cast_specs(stacks_and_layers, nsteps):
    in_specs, out_specs, out_shapes = [], [], []
    for w, layer in stacks_and_layers:
        nrow, ncol = w.shape[1:]
        rb = nrow // nsteps
        assert nrow % nsteps == 0 and rb % (2 * SUBLANES) == 0
        in_specs.append(pl.BlockSpec((None, rb, ncol), lambda j, layer=layer: (layer, j, 0)))
        out_specs.append(pl.BlockSpec((rb, ncol), lambda j: (j, 0)))
        out_shapes.append(jax.ShapeDtypeStruct((nrow, ncol), BF16))
    return in_specs, out_specs, out_shapes


def _cast_slabs(src_refs, dst_refs):
    for src, dst in zip(src_refs, dst_refs):
        dst[...] = src[...].astype(BF16)


def _proj_kernel(x_ref, w_ref, o_ref):
    o_ref[...] = _dot(x_ref[...].astype(BF16), w_ref[...])


def _in_proj(x2d, w):
    m, k = x2d.shape
    n = w.shape[1]
    tm = min(ROW_TILE, m)
    return pl.pallas_call(
        _proj_kernel,
        grid=(m // tm,),
        in_specs=[pl.BlockSpec((tm, k), lambda i: (i, 0)), _const_spec((k, n))],
        out_specs=pl.BlockSpec((tm, n), lambda i: (i, 0)),
        out_shape=jax.ShapeDtypeStruct((m, n), F32),
        compiler_params=_params(("parallel",), 40),
        name="in_proj",
    )(x2d, w)


def _dense_kernel(yp_ref, xp_ref, ys_ref, xs_ref, wo_ref, g1_ref, b1_ref, w1_ref, w3_ref, w2_ref, g2_ref, b2_ref,
                  op_ref, os_ref, h_s, *, layer):
    n_prompt = pl.num_programs(0) - 1
    g1_ref, b1_ref, g2_ref, b2_ref = _layer_rows(layer, g1_ref, b1_ref, g2_ref, b2_ref)
    weights = (wo_ref, g1_ref, b1_ref, w1_ref, w3_ref, w2_ref, g2_ref, b2_ref)

    @pl.when(pl.program_id(0) < n_prompt)
    def _():
        _dense_rows(yp_ref, xp_ref, *weights, op_ref, h_s)

    @pl.when(pl.program_id(0) == n_prompt)
    def _():
        _dense_rows(ys_ref, xs_ref, *weights, os_ref, h_s)


def _dense_rows(y_ref, x_ref, wo_ref, g1_ref, b1_ref, w1_ref, w3_ref, w2_ref, g2_ref, b2_ref, o_ref, h_s):
    rows, ncols = DENSE_GROUP, DENSE_COLS
    nparts = y_ref.shape[0] // rows
    sls = [slice(p * rows, (p + 1) * rows) for p in range(nparts)]
    mix, x1, xb = {}, {}, {}

    def out_proj(p):
        mix[p] = _dot(y_ref[sls[p], :], wo_ref[...])

    def norm1(p):
        x1[p] = _layer_norm(ALPHA * x_ref[sls[p], :] + mix[p], g1_ref[...], b1_ref[...])
        xb[p] = x1[p].astype(BF16)

    def gate_up(p):
        for c in range(0, D_FF, ncols):
            cols = slice(c, c + ncols)
            h_s[sls[p], cols] = (jax.nn.silu(_dot(xb[p], w1_ref[:, cols]))
                                 * _dot(xb[p], w3_ref[:, cols])).astype(BF16)

    def down_norm2(p):
        f = _dot(h_s[sls[p], :], w2_ref[...])
        o_ref[sls[p], :] = _layer_norm(ALPHA * x1[p] + f, g2_ref[...], b2_ref[...])

    out_proj(0)
    for p in range(nparts):
        norm1(p)
        if p + 1 < nparts:
            out_proj(p + 1)
    for p in range(nparts):
        gate_up(p)
    for p in range(nparts):
        down_norm2(p)


def _dense_block(yp, xp, ys, xs, layer, wo, g1, b1, w1, w3, w2, g2, b2):
    tm = DENSE_TILE
    n_prompt = xp.shape[0] // tm
    assert xp.shape[0] % tm == 0 and xs.shape[0] == tm and tm % DENSE_GROUP == 0
    prompt_row = lambda i: (jnp.minimum(i, n_prompt - 1), 0)
    sample_row = lambda i: (0, 0)
    return pl.pallas_call(
        functools.partial(_dense_kernel, layer=layer),
        grid=(n_prompt + 1,),
        in_specs=[
            pl.BlockSpec((tm, D_MODEL), prompt_row),
            pl.BlockSpec((tm, D_MODEL), prompt_row),
            pl.BlockSpec((tm, D_MODEL), sample_row),
            pl.BlockSpec((tm, D_MODEL), sample_row),
            _const_spec((D_MODEL, D_MODEL)),
            _const_spec((DEPTH, D_MODEL)),
            _const_spec((DEPTH, D_MODEL)),
            _const_spec((D_MODEL, D_FF)),
            _const_spec((D_MODEL, D_FF)),
            _const_spec((D_FF, D_MODEL)),
            _const_spec((DEPTH, D_MODEL)),
            _const_spec((DEPTH, D_MODEL)),
        ],
        out_specs=[pl.BlockSpec((tm, D_MODEL), prompt_row), pl.BlockSpec((tm, D_MODEL), sample_row)],
        out_shape=[jax.ShapeDtypeStruct(xp.shape, F32), jax.ShapeDtypeStruct(xs.shape, F32)],
        scratch_shapes=[pltpu.VMEM((tm, D_FF), BF16)],
        compiler_params=_params(("arbitrary",), 58),
        name="dense_block",
    )(yp, xp, ys, xs, wo, g1, b1, w1, w3, w2, g2, b2)


def _lru_gates(xc, wblk_ref, ba, bx, sp):
    pre = _dot(xc.astype(BF16), wblk_ref[...])
    r = jax.nn.sigmoid(pre[:, :W_A] + ba)
    i = jax.nn.sigmoid(pre[:, W_A:] + bx)
    log_a = (-LRU_C) * r * sp
    a = jnp.exp(log_a)
    mult = jnp.sqrt((1.0 + a * a) * jnp.tanh(-log_a))
    return a, mult, i


def _causal_taps(x_slabs, ext, b, w_ref, nrows):
    ktaps = w_ref.shape[0]
    out = []
    for s, x in enumerate(x_slabs):
        lanes = slice(s * LANES, (s + 1) * LANES)
        ext[s, b, SUBLANES:SUBLANES + nrows, :] = x
        acc = w_ref[ktaps - 1:ktaps, lanes] * x
        for k in range(ktaps - 1):
            off = SUBLANES - (ktaps - 1) + k
            acc = acc + w_ref[k:k + 1, lanes] * ext[s, b, off:off + nrows, :]
        out.append(acc)
    return jnp.concatenate(out, axis=1)


def _even_prompt_kernel(x_ref, w_ref, caw_ref, cab_ref, wblk_ref, ba_ref, bx_ref, lam_ref, cbw_ref, *rest,
                        ncast, layer):
    cab_ref, ba_ref, bx_ref, lam_ref = _layer_rows(layer, cab_ref, ba_ref, bx_ref, lam_ref)
    cast_src, rest = rest[:ncast], rest[ncast:]
    y_ref, hlast_ref, ca_out_ref, sc_out_ref = rest[:4]
    cast_dst, rest = rest[4:4 + ncast], rest[4 + ncast:]
    p0, p1, xa_ext, cb_ext, a_s, b_s, g_s, h_c = rest
    tt, pitch = EVEN_TT, EVEN_PITCH
    nslab = W_A // LANES
    j = pl.program_id(0)
    _cast_slabs(cast_src, cast_dst)
    bufs = (p0, p1)
    group = p0.shape[0] // tt

    def project(g):
        xg = x_ref[g * group:(g + 1) * group].reshape(group * tt, D_MODEL)
        bufs[g % 2][...] = _dot(xg.astype(BF16), w_ref[...])

    @pl.when(j == 0)
    def _():
        xa_ext[:, :, 0:SUBLANES, :] = jnp.zeros((nslab, BATCH, SUBLANES, LANES), F32)
        cb_ext[:, :, 0:SUBLANES, :] = jnp.zeros((nslab, BATCH, SUBLANES, LANES), F32)
        h_c[...] = jnp.zeros(h_c.shape, F32)

    @pl.when(j > 0)
    def _():
        xa_ext[:, :, 0:SUBLANES, :] = xa_ext[:, :, tt:tt + SUBLANES, :]
        cb_ext[:, :, 0:SUBLANES, :] = cb_ext[:, :, tt:tt + SUBLANES, :]

    sp = _softplus(-lam_ref[...])
    ba = ba_ref[...]
    bx = bx_ref[...]
    cab = cab_ref[...]
    first_row = (lax.broadcasted_iota(jnp.int32, (SUBLANES, 1), 0) == 0) & (j == 0)
    cg0 = 2 * W_A + W_B
    xb0 = 2 * W_A + 2 * W_B

    def stage(b):
        p_ref = bufs[(b // group) % 2].at[(b % group) * tt:(b % group + 1) * tt]
        xc = cab + _causal_taps([p_ref[:, s * LANES:(s + 1) * LANES] for s in range(nslab)], xa_ext, b, caw_ref, tt)
        a, mult, gate = _lru_gates(xc, wblk_ref, ba, bx, sp)
        mult = jnp.concatenate([jnp.where(first_row, 1.0, mult[:SUBLANES]), mult[SUBLANES:]], axis=0)
        bb = mult * (gate * xc)
        for s in range(nslab):
            a_s[s, b * pitch:b * pitch + tt, :] = a[:, s * LANES:(s + 1) * LANES]
            b_s[s, b * pitch:b * pitch + tt, :] = bb[:, s * LANES:(s + 1) * LANES]
        g_s[b] = jax.nn.gelu(p_ref[:, W_A:2 * W_A])
        cb = [p_ref[:, cg0 + s * LANES:cg0 + (s + 1) * LANES] * p_ref[:, xb0 + s * LANES:xb0 + (s + 1) * LANES]
              for s in range(nslab)]
        zb = _causal_taps(cb, cb_ext, b, cbw_ref, tt)
        y_ref[b, :, W_A:] = (p_ref[:, 2 * W_A:2 * W_A + W_B] * zb).astype(BF16)

    project(0)
    for g in range(BATCH // group):
        if g + 1 < BATCH // group:
            project(g + 1)
        for b in range(g * group, (g + 1) * group):
            stage(b)

    def step(t, hs):
        out = []
        for s in range(nslab):
            rows = pl.ds(t, BATCH, stride=pitch)
            hn = a_s[s, rows, :] * hs[s] + b_s[s, rows, :]
            b_s[s, rows, :] = hn
            out.append(hn)
        return tuple(out)

    hs = lax.fori_loop(0, tt, step, tuple(h_c[s] for s in range(nslab)), unroll=4)
    for s in range(nslab):
        h_c[s] = hs[s]

    def emit(b, carry):
        row0 = pl.multiple_of(b * pitch, SUBLANES)
        h = jnp.concatenate([b_s[s, pl.ds(row0, tt), :] for s in range(nslab)], axis=1)
        y_ref[b, :, 0:W_A] = (h * g_s[b]).astype(BF16)
        return carry

    lax.fori_loop(0, BATCH, emit, 0)

    @pl.when(j == pl.num_programs(0) - 1)
    def _():
        hlast_ref[...] = jnp.concatenate(list(hs), axis=1)
        for s in range(nslab):
            ca_out_ref[:, :, s * LANES:(s + 1) * LANES] = xa_ext[s, :, tt:tt + SUBLANES, :]
            sc_out_ref[:, :, s * LANES:(s + 1) * LANES] = cb_ext[s, :, tt:tt + SUBLANES, :]


def _even_param_specs(e):
    rows = _const_spec((N_EVEN, W_A))
    return [
        _const_spec((A_CONV, W_A), e),
        rows,
        _const_spec((W_A, 2 * W_A), e),
        rows,
        rows,
        rows,
        _const_spec((B_CONV, W_B), e),
    ]


def _even_prompt(x3d, w, e, params, casts):
    tt, pitch = EVEN_TT, EVEN_PITCH
    nslab = W_A // LANES
    nblock = SEQ // tt
    assert W_A == W_B
    cast_in, cast_out, cast_shapes = _cast_specs(casts, nblock)
    return pl.pallas_call(
        functools.partial(_even_prompt_kernel, ncast=len(casts), layer=e),
        grid=(nblock,),
        in_specs=[pl.BlockSpec((BATCH, tt, D_MODEL), lambda j: (0, j, 0)),
                  _const_spec((D_MODEL, EVEN_IN))] + _even_param_specs(e) + cast_in,
        out_specs=[
            pl.BlockSpec((BATCH, tt, D_MODEL), lambda j: (0, j, 0)),
            pl.BlockSpec((BATCH, W_A), lambda j: (0, 0)),
            pl.BlockSpec((BATCH, SUBLANES, W_A), lambda j: (0, 0, 0)),
            pl.BlockSpec((BATCH, SUBLANES, W_B), lambda j: (0, 0, 0)),
        ] + cast_out,
        out_shape=[
            jax.ShapeDtypeStruct((BATCH, SEQ, D_MODEL), BF16),
            jax.ShapeDtypeStruct((BATCH, W_A), F32),
            jax.ShapeDtypeStruct((BATCH, SUBLANES, W_A), F32),
            jax.ShapeDtypeStruct((BATCH, SUBLANES, W_B), F32),
        ] + cast_shapes,
        scratch_shapes=[
            pltpu.VMEM((EVEN_GROUP * tt, EVEN_IN), F32),
            pltpu.VMEM((EVEN_GROUP * tt, EVEN_IN), F32),
            pltpu.VMEM((nslab, BATCH, tt + SUBLANES, LANES), F32),
            pltpu.VMEM((nslab, BATCH, tt + SUBLANES, LANES), F32),
            pltpu.VMEM((nslab, BATCH * pitch, LANES), F32),
            pltpu.VMEM((nslab, BATCH * pitch, LANES), F32),
            pltpu.VMEM((BATCH, tt, W_A), F32),
            pltpu.VMEM((nslab, BATCH, LANES), F32),
        ],
        compiler_params=_params(("arbitrary",), 52),
        name="even_prompt",
    )(x3d, w, *params, *[stack for stack, _ in casts])


def _even_sample_kernel(proj_ref, ca_ref, sc_ref, h0_ref, caw_ref, cab_ref, wblk_ref, ba_ref, bx_ref,
                        lam_ref, cbw_ref, y_ref, hlast_ref, ca_out_ref, sc_out_ref, *, layer):
    cab_ref, ba_ref, bx_ref, lam_ref = _layer_rows(layer, cab_ref, ba_ref, bx_ref, lam_ref)
    sp = _softplus(-lam_ref[...])
    rows_a = [ca_ref[k] for k in range(A_CONV - 1)] + [proj_ref[l, :, 0:W_A] for l in range(DEC_SEQ)]
    xc = []
    for l in range(DEC_SEQ):
        acc = cab_ref[...] + caw_ref[0:1, :] * rows_a[l]
        for k in range(1, A_CONV):
            acc = acc + caw_ref[k:k + 1, :] * rows_a[l + k]
        xc.append(acc)
    xc_all = jnp.concatenate(xc, axis=0)
    a, mult, gate = _lru_gates(xc_all, wblk_ref, ba_ref[...], bx_ref[...], sp)
    bb = mult * (gate * xc_all)
    h = h0_ref[...]
    rows_b = [sc_ref[k] for k in range(B_CONV - 1)]
    for l in range(DEC_SEQ):
        sl = slice(l * DEC_BATCH, (l + 1) * DEC_BATCH)
        h = a[sl] * h + bb[sl]
        y_ref[l, :, 0:W_A] = (h * jax.nn.gelu(proj_ref[l, :, W_A:2 * W_A])).astype(BF16)
        rows_b.append(proj_ref[l, :, 2 * W_A + W_B:2 * W_A + 2 * W_B] * proj_ref[l, :, 2 * W_A + 2 * W_B:])
    hlast_ref[...] = h
    for l in range(DEC_SEQ):
        zb = cbw_ref[0:1, :] * rows_b[l]
        for k in range(1, B_CONV):
            zb = zb + cbw_ref[k:k + 1, :] * rows_b[l + k]
        y_ref[l, :, W_A:] = (proj_ref[l, :, 2 * W_A:2 * W_A + W_B] * zb).astype(BF16)
    for k in range(A_CONV - 1):
        ca_out_ref[k] = rows_a[DEC_SEQ + k]
    for k in range(B_CONV - 1):
        sc_out_ref[k] = rows_b[DEC_SEQ + k]


def _even_sample(proj_tm, ca_tm, sc_tm, h0_tm, e, params):
    whole = lambda shape: pl.BlockSpec(shape, lambda i: (0,) * len(shape))
    return pl.pallas_call(
        functools.partial(_even_sample_kernel, layer=e),
        grid=(1,),
        in_specs=[
            whole((DEC_SEQ, DEC_BATCH, EVEN_IN)),
            _const_spec((A_CONV - 1, DEC_BATCH, W_A), e),
            _const_spec((B_CONV - 1, DEC_BATCH, W_B), e),
            _const_spec((DEC_BATCH, W_A), e),
        ] + _even_param_specs(e),
        out_specs=[
            whole((DEC_SEQ, DEC_BATCH, D_MODEL)),
            whole((DEC_BATCH, W_A)),
            whole((A_CONV - 1, DEC_BATCH, W_A)),
            whole((B_CONV - 1, DEC_BATCH, W_B)),
        ],
        out_shape=[
            jax.ShapeDtypeStruct((DEC_SEQ, DEC_BATCH, D_MODEL), BF16),
            jax.ShapeDtypeStruct((DEC_BATCH, W_A), F32),
            jax.ShapeDtypeStruct((A_CONV - 1, DEC_BATCH, W_A), F32),
            jax.ShapeDtypeStruct((B_CONV - 1, DEC_BATCH, W_B), F32),
        ],
        compiler_params=_params(("arbitrary",), 40),
        name="even_sample",
    )(proj_tm, ca_tm, sc_tm, h0_tm, *params)


def _rope(x, cos, sin_signed):
    half = C_DK // 2
    ax = x.ndim - 1
    pieces = []
    for p in range(W_CK // LANES):
        xv = x[..., p * LANES:(p + 1) * LANES]
        lane = lax.broadcasted_iota(jnp.int32, xv.shape, ax)
        partner = jnp.where((lane % C_DK) < half, pltpu.roll(xv, LANES - half, axis=ax), pltpu.roll(xv, half, axis=ax))
        pieces.append(xv * cos[..., p * LANES:(p + 1) * LANES] + partner * sin_signed[..., p * LANES:(p + 1) * LANES])
    return jnp.concatenate(pieces, axis=ax)


def _group_norm(o):
    mu = jnp.mean(o, -1, keepdims=True)
    oc = o - mu
    var = jnp.mean(oc * oc, -1, keepdims=True)
    return oc * lax.rsqrt(var + LN_EPS)


def _odd_chunk(p_ref, b, cos_ref, sin_ref, dec_ref, cs_ref, kd_ref, sdec_ref, gn_ref,
               spw_ref, spb_ref, lng_ref, lnb_ref, y_ref, s_c, s_bd):
    q = _rope(p_ref[:, 0:W_CK], cos_ref[...], sin_ref[...])
    k = _rope(p_ref[:, W_CK:2 * W_CK], cos_ref[...], sin_ref[...]) * (C_DK ** -0.5)
    kd = k * kd_ref[...]
    v0 = 2 * W_CK
    g0 = v0 + W_CV
    cross = _dot(q.astype(BF16), s_bd[b]) * cs_ref[...]
    for h in range(C_HEADS):
        qh = q[:, h * C_DK:(h + 1) * C_DK].astype(BF16)
        kh = k[:, h * C_DK:(h + 1) * C_DK].astype(BF16)
        kdh = kd[:, h * C_DK:(h + 1) * C_DK].astype(BF16)
        vh = p_ref[:, v0 + h * C_DV:v0 + (h + 1) * C_DV].astype(BF16)
        scores = lax.dot_general(qh, kh, (((1,), (1,)), ((), ())), preferred_element_type=F32) * dec_ref[h]
        o = _dot(scores.astype(BF16), vh) + cross[:, h * C_DV:(h + 1) * C_DV]
        s_new = sdec_ref[h] * s_c[b, h] + lax.dot_general(kdh, vh, (((0,), (0,)), ((), ())), preferred_element_type=F32)
        s_c[b, h] = s_new
        s_bd[b, h * C_DK:(h + 1) * C_DK, h * C_DV:(h + 1) * C_DV] = s_new.astype(BF16)
        gate = jax.nn.silu(p_ref[:, g0 + h * C_DV:g0 + (h + 1) * C_DV])
        y_ref[b, :, h * C_DV:(h + 1) * C_DV] = (gate * (_group_norm(o) * gn_ref[:, h * C_DV:(h + 1) * C_DV])).astype(BF16)

    u0 = g0 + W_CV
    u = jax.nn.gelu(p_ref[:, u0:u0 + W_D])
    vd = _layer_norm(jax.nn.gelu(p_ref[:, u0 + W_D:]), lng_ref[...], lnb_ref[...]).astype(BF16)
    ri = lax.broadcasted_iota(jnp.int32, (D_CHUNK, D_CHUNK), 0)
    ci = lax.broadcasted_iota(jnp.int32, (D_CHUNK, D_CHUNK), 1)
    for gi in range(D_GROUPS):
        w = jnp.where(ri >= ci, spw_ref[gi], 0.0).astype(BF16)
        s = _dot(w, vd[:, gi * D_GROUP:(gi + 1) * D_GROUP]) + spb_ref[:, gi:gi + 1]
        y_ref[b, :, W_CV + gi * D_GROUP:W_CV + (gi + 1) * D_GROUP] = (u[:, gi * D_GROUP:(gi + 1) * D_GROUP] * s).astype(BF16)


def _odd_prompt_kernel(x_ref, w_ref, cos_ref, sin_ref, dec_ref, cs_ref, kd_ref, sdec_ref, gn_ref,
                       spw_ref, spb_ref, lng_ref, lnb_ref, *rest, ncast, layer):
    gn_ref, lng_ref, lnb_ref = _layer_rows(layer, gn_ref, lng_ref, lnb_ref)
    cast_src, rest = rest[:ncast], rest[ncast:]
    y_ref, s_out_ref = rest[:2]
    cast_dst, rest = rest[2:2 + ncast], rest[2 + ncast:]
    p0, p1, s_c, s_bd = rest
    c = pl.program_id(0)
    _cast_slabs(cast_src, cast_dst)

    @pl.when(c == 0)
    def _():
        s_c[...] = jnp.zeros(s_c.shape, F32)
        s_bd[...] = jnp.zeros(s_bd.shape, BF16)

    bufs = (p0, p1)
    group = p0.shape[0] // RET_CHUNK

    def project(g):
        xg = x_ref[g * group:(g + 1) * group].reshape(group * RET_CHUNK, D_MODEL)
        bufs[g % 2][...] = _dot(xg.astype(BF16), w_ref[...])

    project(0)
    for g in range(BATCH // group):
        if g + 1 < BATCH // group:
            project(g + 1)
        for r in range(group):
            _odd_chunk(bufs[g % 2].at[r * RET_CHUNK:(r + 1) * RET_CHUNK], g * group + r,
                       cos_ref, sin_ref, dec_ref, cs_ref, kd_ref, sdec_ref, gn_ref,
                       spw_ref, spb_ref, lng_ref, lnb_ref, y_ref, s_c, s_bd)

    @pl.when(c == pl.num_programs(0) - 1)
    def _():
        s_out_ref[...] = s_c[...]


def _odd_prompt(x3d, w, o, cos, sin, tabs, gn, spw, spb_t, lng, lnb, casts):
    nchunk = SEQ // RET_CHUNK
    dec, cs, kd, sdec = tabs
    cast_in, cast_out, cast_shapes = _cast_specs(casts, nchunk)
    return pl.pallas_call(
        functools.partial(_odd_prompt_kernel, ncast=len(casts), layer=o),
        grid=(nchunk,),
        in_specs=[
            pl.BlockSpec((BATCH, RET_CHUNK, D_MODEL), lambda c: (0, c, 0)),
            _const_spec((D_MODEL, ODD_IN)),
            pl.BlockSpec((RET_CHUNK, W_CK), lambda c: (c, 0)),
            pl.BlockSpec((RET_CHUNK, W_CK), lambda c: (c, 0)),
            _const_spec((C_HEADS, RET_CHUNK, RET_CHUNK)),
            _const_spec((RET_CHUNK, W_CV)),
            _const_spec((RET_CHUNK, W_CK)),
            _const_spec((C_HEADS, C_DK, C_DV)),
            _const_spec((N_ODD, W_CV)),
            _const_spec((D_GROUPS, D_CHUNK, D_CHUNK), o),
            _const_spec((D_CHUNK, D_GROUPS), o),
            _const_spec((N_ODD, W_D)),
            _const_spec((N_ODD, W_D)),
        ] + cast_in,
        out_specs=[
            pl.BlockSpec((BATCH, RET_CHUNK, D_MODEL), lambda c: (0, c, 0)),
            pl.BlockSpec((BATCH, C_HEADS, C_DK, C_DV), lambda c: (0, 0, 0, 0)),
        ] + cast_out,
        out_shape=[
            jax.ShapeDtypeStruct((BATCH, SEQ, D_MODEL), BF16),
            jax.ShapeDtypeStruct((BATCH, C_HEADS, C_DK, C_DV), F32),
        ] + cast_shapes,
        scratch_shapes=[
            pltpu.VMEM((ODD_GROUP * RET_CHUNK, ODD_IN), F32),
            pltpu.VMEM((ODD_GROUP * RET_CHUNK, ODD_IN), F32),
            pltpu.VMEM((BATCH, C_HEADS, C_DK, C_DV), F32),
            pltpu.VMEM((BATCH, W_CK, W_CV), BF16),
        ],
        compiler_params=_params(("arbitrary",), 52),
        name="odd_prompt",
    )(x3d, w, cos, sin, dec, cs, kd, sdec, gn, spw, spb_t, lng, lnb, *[stack for stack, _ in casts])


def _odd_sample_kernel(proj_ref, s0_ref, cos_ref, sin_ref, dec_ref, cs_ref, kd_ref, sdec_ref, gn_ref,
                       spw_ref, spb_ref, lng_ref, lnb_ref, *rest, layer):
    gn_ref, lng_ref, lnb_ref = _layer_rows(layer, gn_ref, lng_ref, lnb_ref)
    y_ref, s_out_ref, vd_ref = rest[-3:]
    if len(rest) > 3:
        s_out_ref[:, 0] = rest[0][...]
        s_out_ref = s_out_ref.at[:, 1]
    nb = s0_ref.shape[0]

    def cols(a, b):
        return proj_ref[:, a:b].reshape(nb, DEC_SEQ, b - a)

    def put(a, b, val):
        y_ref[:, a:b] = val.reshape(nb * DEC_SEQ, b - a).astype(BF16)

    q = _rope(cols(0, W_CK), cos_ref[...], sin_ref[...])
    k = _rope(cols(W_CK, 2 * W_CK), cos_ref[...], sin_ref[...]) * (C_DK ** -0.5)
    kd = k * kd_ref[...]
    v0 = 2 * W_CK
    g0 = v0 + W_CV
    for h in range(C_HEADS):
        qh = q[:, :, h * C_DK:(h + 1) * C_DK].astype(BF16)
        kh = k[:, :, h * C_DK:(h + 1) * C_DK].astype(BF16)
        kdh = kd[:, :, h * C_DK:(h + 1) * C_DK].astype(BF16)
        vh = cols(v0 + h * C_DV, v0 + (h + 1) * C_DV).astype(BF16)
        s_prev = s0_ref[:, h]
        scores = jnp.einsum('nld,nmd->nlm', qh, kh, preferred_element_type=F32) * dec_ref[h]
        o = (jnp.einsum('nlm,nme->nle', scores.astype(BF16), vh, preferred_element_type=F32)
             + jnp.einsum('nld,nde->nle', qh, s_prev.astype(BF16), preferred_element_type=F32)
             * cs_ref[:, h * C_DV:(h + 1) * C_DV])
        s_out_ref[:, h] = sdec_ref[h] * s_prev + jnp.einsum('nld,nle->nde', kdh, vh, preferred_element_type=F32)
        gate = jax.nn.silu(cols(g0 + h * C_DV, g0 + (h + 1) * C_DV))
        put(h * C_DV, (h + 1) * C_DV, gate * (_group_norm(o) * gn_ref[:, h * C_DV:(h + 1) * C_DV]))

    u0 = g0 + W_CV
    u = jax.nn.gelu(cols(u0, u0 + W_D))
    vd = _layer_norm(jax.nn.gelu(cols(u0 + W_D, ODD_IN)), lng_ref[...], lnb_ref[...])
    vd_ref[...] = vd
    s = spb_ref[...] + spw_ref[0] * vd[:, 0:1, :]
    for m in range(1, DEC_SEQ):
        s = s + spw_ref[m] * vd[:, m:m + 1, :]
    put(W_CV, D_MODEL, u * s)


def _odd_sample(proj, state_ret, o, cos, sin, tabs, gn, spw_rows, spb_rows, lng, lnb, prev_states):
    nb = SAMPLE_NB
    dec, cs, kd, sdec = tabs
    seq3 = lambda i: (i, 0, 0)
    one_state = pl.BlockSpec((nb, C_HEADS, C_DK, C_DV), lambda i: (i, 0, 0, 0))
    in_specs = [
        pl.BlockSpec((nb * DEC_SEQ, ODD_IN), lambda i: (i, 0)),
        pl.BlockSpec((nb, None, C_HEADS, C_DK, C_DV), lambda i: (i, o, 0, 0, 0)),
        _const_spec((DEC_SEQ, W_CK)),
        _const_spec((DEC_SEQ, W_CK)),
        _const_spec((C_HEADS, DEC_SEQ, DEC_SEQ)),
        _const_spec((DEC_SEQ, W_CV)),
        _const_spec((DEC_SEQ, W_CK)),
        _const_spec((C_HEADS, C_DK, C_DV)),
        _const_spec((N_ODD, W_CV)),
        _const_spec((DEC_SEQ, DEC_SEQ, W_D), o),
        _const_spec((DEC_SEQ, W_D), o),
        _const_spec((N_ODD, W_D)),
        _const_spec((N_ODD, W_D)),
    ]
    args = [proj, state_ret, cos, sin, dec, cs, kd, sdec, gn, spw_rows, spb_rows, lng, lnb]
    if prev_states is None:
        state_spec = one_state
        state_shape = (DEC_BATCH, C_HEADS, C_DK, C_DV)
    else:
        in_specs.append(one_state)
        args.append(prev_states)
        state_spec = pl.BlockSpec((nb, N_ODD, C_HEADS, C_DK, C_DV), lambda i: (i, 0, 0, 0, 0))
        state_shape = (DEC_BATCH, N_ODD, C_HEADS, C_DK, C_DV)
    return pl.pallas_call(
        functools.partial(_odd_sample_kernel, layer=o),
        grid=(DEC_BATCH // nb,),
        in_specs=in_specs,
        out_specs=[pl.BlockSpec((nb * DEC_SEQ, D_MODEL), lambda i: (i, 0)), state_spec,
                   pl.BlockSpec((nb, DEC_SEQ, W_D), seq3)],
        out_shape=[
            jax.ShapeDtypeStruct((DEC_BATCH * DEC_SEQ, D_MODEL), BF16),
            jax.ShapeDtypeStruct(state_shape, F32),
            jax.ShapeDtypeStruct((DEC_BATCH, DEC_SEQ, W_D), F32),
        ],
        compiler_params=_params(("parallel",), 48),
        name="odd_sample",
    )(*args)


def _rope_tables(pos):
    half = C_DK // 2
    freq = ROPE_BASE ** (-np.arange(half, dtype=np.float64) / half)
    ang = np.asarray(pos, np.float64)[:, None] * freq
    cos, sin = np.cos(ang), np.sin(ang)
    cos_l = np.tile(np.concatenate([cos, cos], axis=-1), (1, C_HEADS))
    sin_l = np.tile(np.concatenate([-sin, sin], axis=-1), (1, C_HEADS))
    return cos_l.astype(np.float32), sin_l.astype(np.float32)


def _retention_tables(length):
    log_g = np.log1p(-np.exp2(-5.0 - np.arange(C_HEADS, dtype=np.float64)))
    idx = np.arange(length, dtype=np.float64)
    rel = idx[:, None] - idx[None, :]
    decay = np.where(rel >= 0, np.exp(log_g[:, None, None] * np.maximum(rel, 0.0)), 0.0)
    cross = np.exp(log_g[None, :] * (idx[:, None] + 1.0))
    kdec = np.exp(log_g[None, :] * (length - 1.0 - idx[:, None]))
    sdec = np.exp(log_g * length)
    tabs = (decay,
            np.repeat(cross, C_DV, axis=1),
            np.repeat(kdec, C_DK, axis=1),
            np.broadcast_to(sdec[:, None, None], (C_HEADS, C_DK, C_DV)))
    return tuple(np.ascontiguousarray(t, dtype=np.float32) for t in tabs)


def _block_diag_halves(wa, wx):
    def bd(w):
        rows = [jnp.pad(w[:, g], ((0, 0), (0, 0), (g * A_BLOCK, W_A - (g + 1) * A_BLOCK))) for g in range(A_BLOCKS)]
        return jnp.concatenate(rows, axis=1)

    return jnp.concatenate([bd(wa), bd(wx)], axis=-1).astype(BF16)


def kernel(x_prompt, x_sample, state_lru_h, state_lru_conv, state_sconv, state_ret, w_in_even, conv_a_w, conv_a_b, lru_wa, lru_ba, lru_wx, lru_bx, lru_lam, conv_b_w, w_out_even, w_in_odd, ret_gn_g, sp_w, sp_b, gm_ln_g, gm_ln_b, w_out_odd, ffn_w1, ffn_w3, ffn_w2, ln1_g, ln1_b, ln2_g, ln2_b):
    xp = x_prompt.reshape(BATCH * SEQ, D_MODEL)
    xs = jnp.swapaxes(x_sample, 0, 1).reshape(DEC_SEQ * DEC_BATCH, D_MODEL)

    cos_p, sin_p = _rope_tables(np.arange(SEQ))
    cos_s, sin_s = _rope_tables(PAST_LEN + np.arange(DEC_SEQ))
    tab_p = _retention_tables(RET_CHUNK)
    tab_s = _retention_tables(DEC_SEQ)

    w_in = w_in_even[0].astype(BF16)
    ln1, ln2 = (ln1_g, ln1_b), (ln2_g, ln2_b)
    even_p = (conv_a_w, conv_a_b, _block_diag_halves(lru_wa, lru_wx), lru_ba, lru_bx, lru_lam, conv_b_w)
    ca_tm = jnp.transpose(state_lru_conv, (1, 2, 0, 3))
    sc_tm = jnp.transpose(state_sconv, (1, 2, 0, 3))
    h0_tm = jnp.swapaxes(state_lru_h, 0, 1)
    tril = jnp.tril(sp_w[:, :, :DEC_SEQ, :DEC_SEQ])
    spw_rows = jnp.repeat(jnp.transpose(tril, (0, 3, 2, 1)), D_GROUP, axis=3)
    spb_rows = jnp.repeat(jnp.swapaxes(sp_b[:, :, :DEC_SEQ], 1, 2), D_GROUP, axis=2)
    spb_t = jnp.swapaxes(sp_b, 1, 2)
    odd_small = (ret_gn_g,)
    odd_ln = (gm_ln_g, gm_ln_b)

    h_p, h_s, ca_p, ca_s, sc_p, sc_s, r_p, v_s = [], [], [], [], [], [], [], []
    ret_sample = None
    for l in range(DEPTH):
        if l % 2 == 0:
            e = l // 2
            casts = [(w_out_even, e), (ffn_w1, l), (ffn_w3, l), (ffn_w2, l), (w_in_odd, e)]
            yp, hlp, cap, scp, wo, w1, w3, w2, w_in_next = _even_prompt(xp.reshape(BATCH, SEQ, D_MODEL), w_in, e,
                                                                        even_p, casts)
            proj_s = _in_proj(xs, w_in).reshape(DEC_SEQ, DEC_BATCH, EVEN_IN)
            ys, hls, cas, scs = _even_sample(proj_s, ca_tm, sc_tm, h0_tm, e, even_p)
            h_p.append(hlp)
            h_s.append(hls)
            ca_p.append(cap[:, SUBLANES - (A_CONV - 1):])
            ca_s.append(jnp.swapaxes(cas, 0, 1))
            sc_p.append(scp[:, SUBLANES - (B_CONV - 1):])
            sc_s.append(jnp.swapaxes(scs, 0, 1))
            yp = yp.reshape(BATCH * SEQ, D_MODEL)
            ys = ys.reshape(DEC_SEQ * DEC_BATCH, D_MODEL)
        else:
            o = l // 2
            casts = [(w_out_odd, o), (ffn_w1, l), (ffn_w3, l), (ffn_w2, l)]
            if l + 1 < DEPTH:
                casts.append((w_in_even, o + 1))
            yp, rp, wo, w1, w3, w2, *w_in_next = _odd_prompt(xp.reshape(BATCH, SEQ, D_MODEL), w_in, o, cos_p, sin_p,
                                                             tab_p, *odd_small, sp_w, spb_t, *odd_ln, casts)
            w_in_next = w_in_next[0] if w_in_next else None
            proj_s = _in_proj(xs, w_in)
            ys, ret_sample, vs = _odd_sample(proj_s, state_ret, o, cos_s, sin_s, tab_s, *odd_small,
                                             spw_rows, spb_rows, *odd_ln, ret_sample)
            r_p.append(rp)
            v_s.append(vs)
            yp = yp.reshape(BATCH * SEQ, D_MODEL)
        xp, xs = _dense_block(yp, xp, ys, xs, l, wo, *ln1, w1, w3, w2, *ln2)
        w_in = w_in_next
        if l + 1 < DEPTH:
            if l % 2 == 0:
                xs = jnp.swapaxes(xs.reshape(DEC_SEQ, DEC_BATCH, D_MODEL), 0, 1).reshape(DEC_BATCH * DEC_SEQ, D_MODEL)
            else:
                xs = jnp.swapaxes(xs.reshape(DEC_BATCH, DEC_SEQ, D_MODEL), 0, 1).reshape(DEC_SEQ * DEC_BATCH, D_MODEL)

    return (xp.reshape(BATCH, SEQ, D_MODEL), xs.reshape(DEC_BATCH, DEC_SEQ, D_MODEL),
            jnp.stack(h_p, axis=1), jnp.stack(h_s, axis=1),
            jnp.stack(ca_p, axis=1), jnp.stack(ca_s, axis=1),
            jnp.stack(sc_p, axis=1), jnp.stack(sc_s, axis=1),
            jnp.stack(r_p, axis=1), ret_sample, jnp.stack(v_s, axis=1))
```
